```python
import math
import jax
import jax.numpy as jnp
from jax import lax
import numpy as np

D_MODEL = 1024
BATCH = 2
SEQ = 16384
DEPTH = 2

ML_HEADS = 4
ML_DIM = 64
ML_WIDTH = ML_HEADS * ML_DIM
ML_CHUNK = 64
CONV_W = 4
DA_HEADS = 4
DA_QK_DIM = 32
DA_V_DIM = 2 * DA_QK_DIM
DA_WIDTH = DA_HEADS * DA_V_DIM
NSA_HEADS = 8
NSA_GROUPS = 2
NSA_REP = NSA_HEADS // NSA_GROUPS
NSA_DIM = 64
NSA_WIDTH = NSA_HEADS * NSA_DIM
NSA_KV = NSA_GROUPS * NSA_DIM
CMP_BLOCK = 32
CMP_STRIDE = 16
CMP_HIDDEN = 4 * NSA_DIM
SEL_BLOCK = 64
SEL_TOPK = 16
WINDOW = 512
Q_BLOCK = 128
D_MIX = ML_WIDTH + DA_WIDTH + NSA_WIDTH
D_FF = 4 * D_MODEL
EPS = 1e-6
FORCE_SCORE = 1e4
IN_SIZES = (ML_WIDTH, ML_WIDTH, ML_WIDTH, ML_WIDTH, ML_HEADS, ML_HEADS,
            2 * DA_HEADS * DA_QK_DIM, 2 * DA_HEADS * DA_QK_DIM, DA_WIDTH,
            NSA_WIDTH, NSA_KV, NSA_KV, NSA_KV, NSA_KV, NSA_KV, NSA_KV, 3 * NSA_HEADS)
IN_COLS = sum(IN_SIZES)

kernel_name = 'hybrid_mlstm_diffattn_nsa_block'

F32 = jnp.float32


def rmsnorm(x, g):
    xf = x.astype(F32)
    y = xf * lax.rsqrt(jnp.mean(xf * xf, axis=-1, keepdims=True) + EPS)
    return (y * g).astype(x.dtype)


def head_rmsnorm(x, g, heads):
    B, S, W = x.shape
    xf = x.astype(F32).reshape(B, S, heads, W // heads)
    y = xf * lax.rsqrt(jnp.mean(xf * xf, axis=-1, keepdims=True) + EPS)
    return y.reshape(B, S, W) * g


def masked_softmax(s, mask):
    s = jnp.where(mask, s.astype(F32), -jnp.inf)
    m = jnp.max(s, axis=-1, keepdims=True)
    m = jnp.where(jnp.isfinite(m), m, 0.0)
    p = jnp.exp(s - m)
    den = jnp.sum(p, axis=-1, keepdims=True)
    return p / jnp.where(den > 0, den, 1.0)


def causal_dwconv(x, w):
    return lax.conv_general_dilated(
        x, w[:, None, :].astype(x.dtype), window_strides=(1,), padding=[(CONV_W - 1, 0)],
        dimension_numbers=('NWC', 'WIO', 'NWC'), feature_group_count=x.shape[-1])


def mlstm(q, k, v, i_pre, f_pre):
    B, S, H, d = q.shape
    L = ML_CHUNK
    NC = S // L

    def chunk(t):
        return t.astype(F32).reshape(B, NC, L, H, -1).transpose(0, 3, 1, 2, 4)

    q = chunk(q)
    k = chunk(k) * (d ** -0.5)
    v = chunk(v)
    ig = i_pre.astype(F32).reshape(B, NC, L, H).transpose(0, 3, 1, 2)
    logf = jax.nn.log_sigmoid(f_pre.astype(F32)).reshape(B, NC, L, H).transpose(0, 3, 1, 2)
    b = jnp.cumsum(logf, axis=-1)
    g = b[..., -1]
    a = g[..., None] - b + ig
    a_max = jnp.max(a, axis=-1)
    w = jnp.exp(a - a_max[..., None])
    c_loc = jnp.einsum('bhcl,bhclv,bhclk->bhcvk', w, v, k)
    n_loc = jnp.einsum('bhcl,bhclk->bhck', w, k)

    def step(carry, xs):
        C, n, m = carry
        gc, amc, Cl, nl = xs
        m_new = jnp.maximum(gc + m, amc)
        s_prev = jnp.exp(gc + m - m_new)
        s_loc = jnp.exp(amc - m_new)
        C_new = s_prev[..., None, None] * C + s_loc[..., None, None] * Cl
        n_new = s_prev[..., None] * n + s_loc[..., None] * nl
        return (C_new, n_new, m_new), (C, n, m)

    init = (jnp.zeros((B, H, d, d), F32), jnp.zeros((B, H, d), F32), jnp.zeros((B, H), F32))
    xs = (jnp.moveaxis(g, 2, 0), jnp.moveaxis(a_max, 2, 0),
          jnp.moveaxis(c_loc, 2, 0), jnp.moveaxis(n_loc, 2, 0))
    _, (C_prev, n_prev, m_prev) = lax.scan(step, init, xs)
    C_prev = jnp.moveaxis(C_prev, 0, 2)
    n_prev = jnp.moveaxis(n_prev, 0, 2)
    m_prev = jnp.moveaxis(m_prev, 0, 2)

    causal = jnp.tril(jnp.ones((L, L), dtype=bool))
    D = jnp.where(causal, b[..., :, None] - b[..., None, :] + ig[..., None, :], -jnp.inf)
    inter_log = b + m_prev[..., None]
    m_t = jnp.maximum(inter_log, jnp.max(D, axis=-1))
    wts = jnp.exp(D - m_t[..., None]) * jnp.einsum('bhctd,bhcsd->bhcts', q, k)
    s_inter = jnp.exp(inter_log - m_t)
    num = (jnp.einsum('bhcts,bhcsv->bhctv', wts, v)
           + s_inter[..., None] * jnp.einsum('bhcvk,bhctk->bhctv', C_prev, q))
    den = jnp.sum(wts, axis=-1) + s_inter * jnp.einsum('bhck,bhctk->bhct', n_prev, q)
    h = num / jnp.maximum(jnp.abs(den), jnp.exp(-m_t))[..., None]
    return h.transpose(0, 2, 3, 1, 4).reshape(B, S, H * d)


def diff_attention(q, k, v, lam, lam_init, gain):
    B, S, _ = q.shape
    H, d = DA_HEADS, DA_QK_DIM
    q = q.reshape(B, S, H, 2, d).transpose(0, 2, 3, 1, 4) * (d ** -0.5)
    k = k.reshape(B, S, H, 2, d).transpose(0, 2, 3, 1, 4)
    v = v.reshape(B, S, H, DA_V_DIM).transpose(0, 2, 1, 3)
    kpos = jnp.arange(S)

    def block(qb):
        q0 = qb * Q_BLOCK
        qblk = lax.dynamic_slice_in_dim(q, q0, Q_BLOCK, axis=3)
        s = jnp.einsum('bhmqd,bhmkd->bhmqk', qblk, k, preferred_element_type=F32)
        qpos = q0 + jnp.arange(Q_BLOCK)
        s = jnp.where(kpos[None, :] <= qpos[:, None], s, -jnp.inf)
        p = jax.nn.softmax(s, axis=-1)
        att = p[:, :, 0] - lam * p[:, :, 1]
        return jnp.einsum('bhqk,bhkv->bhqv', att, v.astype(F32))

    o = lax.map(block, jnp.arange(S // Q_BLOCK))
    o = o.transpose(1, 0, 3, 2, 4).reshape(B, S, DA_WIDTH)
    return head_rmsnorm(o, gain, DA_HEADS) * (1.0 - lam_init)


def nsa_attention(q, kc_raw, vc_raw, ks_raw, vs_raw, kw_raw, vw_raw, gate_pre, pe, w1, w2):
    B, S, _ = q.shape
    G, R, d = NSA_GROUPS, NSA_REP, NSA_DIM
    q = q.reshape(B, S, G, R, d).transpose(0, 2, 3, 1, 4) * (d ** -0.5)
    gates = jax.nn.sigmoid(gate_pre.reshape(B, S, G, R, 3).transpose(0, 2, 3, 1, 4))

    def to_g(t):
        return t.reshape(B, S, G, d).transpose(0, 2, 1, 3)

    ncb = (S - CMP_BLOCK) // CMP_STRIDE + 1
    tok = (jnp.arange(ncb) * CMP_STRIDE)[:, None] + jnp.arange(CMP_BLOCK)[None, :]

    def compress(t, pe_, w1_, w2_):
        blk = to_g(t)[:, :, tok] + pe_
        hid = jax.nn.gelu(blk.reshape(B, G, ncb, CMP_BLOCK * d) @ w1_)
        return hid @ w2_

    k_cmp = compress(kc_raw, pe[0], w1[0], w2[0])
    v_cmp = compress(vc_raw, pe[1], w1[1], w2[1])
    cmp_end = jnp.arange(ncb) * CMP_STRIDE + CMP_BLOCK - 1

    nsb = S // SEL_BLOCK
    n_sel = min(SEL_TOPK, nsb)
    k_sel = to_g(ks_raw).reshape(B, G, nsb, SEL_BLOCK, d)
    v_sel = to_g(vs_raw).reshape(B, G, nsb, SEL_BLOCK, d)
    ratio = SEL_BLOCK // CMP_STRIDE
    lead = CMP_BLOCK // CMP_STRIDE - 1
    blk_ids = jnp.arange(nsb)
    bi = jnp.arange(B)[:, None, None, None]
    gi = jnp.arange(G)[None, :, None, None]

    pad = ((0, 0), (0, 0), (WINDOW, 0), (0, 0))
    k_win = jnp.pad(to_g(kw_raw), pad)
    v_win = jnp.pad(to_g(vw_raw), pad)

    def block(qb):
        q0 = qb * Q_BLOCK
        qblk = lax.dynamic_slice_in_dim(q, q0, Q_BLOCK, axis=3)
        gblk = lax.dynamic_slice_in_dim(gates, q0, Q_BLOCK, axis=3)
        qpos = q0 + jnp.arange(Q_BLOCK)

        s = jnp.einsum('bgrqd,bgcd->bgrqc', qblk, k_cmp, preferred_element_type=F32)
        p_cmp = masked_softmax(s, cmp_end[None, :] <= qpos[:, None])
        o_cmp = jnp.einsum('bgrqc,bgcd->bgrqd', p_cmp, v_cmp.astype(F32))

        imp = jnp.pad(jnp.sum(p_cmp, axis=2), ((0, 0), (0, 0), (0, 0), (lead, ratio + lead)))
        p_slc = jnp.zeros((B, G, Q_BLOCK, nsb), F32)
        for o in range(-lead, ratio):
            st = lead + o
            p_slc = p_slc + imp[..., st:st + ratio * nsb:ratio]
        cur = qpos // SEL_BLOCK
        forced = ((blk_ids[None, :] == 0) | (blk_ids[None, :] == cur[:, None])
                  | (blk_ids[None, :] == cur[:, None] - 1))
        causal_blk = blk_ids[None, :] * SEL_BLOCK <= qpos[:, None]
        score = jnp.where(forced, FORCE_SCORE, jnp.where(causal_blk, p_slc, -1.0))
        _, idx = lax.top_k(score, n_sel)
        ksb = k_sel[bi, gi, idx]
        vsb = v_sel[bi, gi, idx]
        pos = idx[..., None] * SEL_BLOCK + jnp.arange(SEL_BLOCK)
        smask = (pos <= qpos[:, None, None]).reshape(B, G, 1, Q_BLOCK, n_sel * SEL_BLOCK)
        s = jnp.einsum('bgrqd,bgqnld->bgrqnl', qblk, ksb, preferred_element_type=F32)
        p = masked_softmax(s.reshape(B, G, R, Q_BLOCK, n_sel * SEL_BLOCK), smask)
        o_sel = jnp.einsum('bgrqm,bgqmd->bgrqd', p,
                           vsb.reshape(B, G, Q_BLOCK, n_sel * SEL_BLOCK, d).astype(F32))

        kwb = lax.dynamic_slice_in_dim(k_win, q0, Q_BLOCK + WINDOW, axis=2)
        vwb = lax.dynamic_slice_in_dim(v_win, q0, Q_BLOCK + WINDOW, axis=2)
        wpos = q0 - WINDOW + jnp.arange(Q_BLOCK + WINDOW)
        rel = qpos[:, None] - wpos[None, :]
        wmask = (rel >= 0) & (rel < WINDOW) & (wpos[None, :] >= 0)
        s = jnp.einsum('bgrqd,bgkd->bgrqk', qblk, kwb, preferred_element_type=F32)
        o_win = jnp.einsum('bgrqk,bgkd->bgrqd', masked_softmax(s, wmask), vwb.astype(F32))

        gf = gblk.astype(F32)
        return gf[..., 0:1] * o_cmp + gf[..., 1:2] * o_sel + gf[..., 2:3] * o_win

    o = lax.map(block, jnp.arange(S // Q_BLOCK))
    return o.transpose(1, 0, 4, 2, 3, 5).reshape(B, S, NSA_WIDTH)


def token_mixer(h, layer_idx, w_in, ml_conv, ml_gate_bias, ml_norm, da_lambda, da_norm,
                nsa_pe, nsa_w1, nsa_w2, w_out):
    B, S, _ = h.shape
    z = h @ w_in
    splits = [int(c) for c in np.cumsum(IN_SIZES)[:-1]]
    (ml_q, ml_k, ml_v, ml_o, ml_i, ml_f, da_q, da_k, da_v,
     ns_q, ns_kc, ns_vc, ns_ks, ns_vs, ns_kw, ns_vw, ns_g) = jnp.split(z, splits, axis=-1)

    qk = jax.nn.silu(causal_dwconv(jnp.concatenate([ml_q, ml_k], axis=-1), ml_conv))
    ml_q, ml_k = jnp.split(qk, 2, axis=-1)
    i_pre = ml_i + ml_gate_bias[:ML_HEADS]
    f_pre = ml_f + ml_gate_bias[ML_HEADS:]
    hd = (B, S, ML_HEADS, ML_DIM)
    y_ml = mlstm(ml_q.reshape(hd), ml_k.reshape(hd), ml_v.reshape(hd), i_pre, f_pre)
    y_ml = head_rmsnorm(jax.nn.sigmoid(ml_o.astype(F32)) * y_ml, ml_norm, ML_HEADS)

    lam_init = 0.8 - 0.6 * math.exp(-0.3 * layer_idx)
    lf = da_lambda.astype(F32)
    lam = jnp.exp(jnp.sum(lf[0] * lf[1])) - jnp.exp(jnp.sum(lf[2] * lf[3])) + lam_init
    y_da = diff_attention(da_q, da_k, da_v, lam, lam_init, da_norm)

    y_ns = nsa_attention(ns_q, ns_kc, ns_vc, ns_ks, ns_vs, ns_kw, ns_vw, ns_g, nsa_pe, nsa_w1, nsa_w2)

    mixed = jnp.concatenate([y_ml.astype(h.dtype), y_da.astype(h.dtype), y_ns.astype(h.dtype)], axis=-1)
    return mixed @ w_out


def squared_relu_mlp(h, w1, w2):
    return jnp.square(jax.nn.relu(h @ w1)) @ w2


def setup_inputs(seed: int = 0) -> dict:
    key = jax.random.key(seed)
    ks = jax.random.split(key, 18)

    def nrm(k, shape, scale):
        return jax.random.normal(k, shape, F32) * scale

    x = nrm(ks[0], (BATCH, SEQ, D_MODEL), 1.0)
    norm1 = 1.0 + nrm(ks[1], (DEPTH, D_MODEL), 0.02)
    w_in = nrm(ks[2], (DEPTH, D_MODEL, IN_COLS), D_MODEL ** -0.5)
    ml_conv = nrm(ks[3], (DEPTH, CONV_W, 2 * ML_WIDTH), CONV_W ** -0.5)
    ig_bias = nrm(ks[4], (DEPTH, ML_HEADS), 0.1)
    fg_bias = jnp.linspace(3.0, 6.0, ML_HEADS, dtype=F32)[None, :] + nrm(ks[5], (DEPTH, ML_HEADS), 0.01)
    ml_gate_bias = jnp.concatenate([ig_bias, fg_bias], axis=-1)
    ml_norm = 1.0 + nrm(ks[6], (DEPTH, ML_WIDTH), 0.02)
    da_lambda = nrm(ks[7], (DEPTH, 4, DA_QK_DIM), 0.1)
    da_norm = 1.0 + nrm(ks[8], (DEPTH, DA_WIDTH), 0.02)
    nsa_pe = nrm(ks[9], (DEPTH, 2, CMP_BLOCK, NSA_DIM), 0.1)
    nsa_w1 = nrm(ks[10], (DEPTH, 2, CMP_BLOCK * NSA_DIM, CMP_HIDDEN), (CMP_BLOCK * NSA_DIM) ** -0.5)
    nsa_w2 = nrm(ks[11], (DEPTH, 2, CMP_HIDDEN, NSA_DIM), CMP_HIDDEN ** -0.5)
    w_out = nrm(ks[12], (DEPTH, D_MIX, D_MODEL), D_MIX ** -0.5)
    norm2 = 1.0 + nrm(ks[13], (DEPTH, D_MODEL), 0.02)
    w_ff1 = nrm(ks[14], (DEPTH, D_MODEL, D_FF), D_MODEL ** -0.5)
    w_ff2 = nrm(ks[15], (DEPTH, D_FF, D_MODEL), D_FF ** -0.5)
    final_norm = 1.0 + nrm(ks[16], (D_MODEL,), 0.02)
    return {'x': x, 'norm1': norm1, 'w_in': w_in, 'ml_conv': ml_conv, 'ml_gate_bias': ml_gate_bias,
            'ml_norm': ml_norm, 'da_lambda': da_lambda, 'da_norm': da_norm, 'nsa_pe': nsa_pe,
            'nsa_w1': nsa_w1, 'nsa_w2': nsa_w2, 'w_out': w_out, 'norm2': norm2,
            'w_ff1': w_ff1, 'w_ff2': w_ff2, 'final_norm': final_norm}


def reference(x, norm1, w_in, ml_conv, ml_gate_bias, ml_norm, da_lambda, da_norm, nsa_pe,
              nsa_w1, nsa_w2, w_out, norm2, w_ff1, w_ff2, final_norm):
    for l in range(DEPTH):
        h = rmsnorm(x, norm1[l])
        x = x + token_mixer(h, l, w_in[l], ml_conv[l], ml_gate_bias[l], ml_norm[l], da_lambda[l],
                            da_norm[l], nsa_pe[l], nsa_w1[l], nsa_w2[l], w_out[l])
        x = x + squared_relu_mlp(rmsnorm(x, norm2[l]), w_ff1[l], w_ff2[l])
    return rmsnorm(x, final_norm)
```

```python
import functools
import math

import numpy as np
import jax
import jax.numpy as jnp
from jax import lax
from jax.experimental import pallas as pl
from jax.experimental.pallas import tpu as pltpu

F32 = jnp.float32
BF16 = jnp.bfloat16

D_MODEL = 1024
ML_HEADS = 4
ML_DIM = 64
ML_WIDTH = ML_HEADS * ML_DIM
ML_CHUNK = 64
CONV_W = 4
DA_HEADS = 4
DA_QK_DIM = 32
DA_V_DIM = 64
DA_WIDTH = DA_HEADS * DA_V_DIM
NSA_HEADS = 8
NSA_GROUPS = 2
NSA_REP = NSA_HEADS // NSA_GROUPS
NSA_DIM = 64
NSA_WIDTH = NSA_HEADS * NSA_DIM
NSA_KV = NSA_GROUPS * NSA_DIM
CMP_BLOCK = 32
CMP_STRIDE = 16
CMP_HIDDEN = 4 * NSA_DIM
SEL_BLOCK = 64
SEL_TOPK = 16
WINDOW = 512
D_FF = 4 * D_MODEL
EPS = 1e-6
FORCE_SCORE = 1e4
IN_SIZES = (ML_WIDTH, ML_WIDTH, ML_WIDTH, ML_WIDTH, ML_HEADS, ML_HEADS,
            2 * DA_HEADS * DA_QK_DIM, 2 * DA_HEADS * DA_QK_DIM, DA_WIDTH,
            NSA_WIDTH, NSA_KV, NSA_KV, NSA_KV, NSA_KV, NSA_KV, NSA_KV, 3 * NSA_HEADS)

LANES = 128
GATE_PAD = 16
NEG = -1e30
VMEM_LIMIT = 56 * 1024 * 1024

_T_ML = (0, 1024)
_T_MLG = (1024, 1152)
_T_DAK = (1152, 1408)
_T_NSK = (1408, 1664)
_T_NSC = (1664, 1920)
_T_COLS = 1920
_F_DAQ = (0, 256)
_F_DAV = (256, 512)
_F_NSQ = (512, 1024)
_F_NSV = (1024, 1280)
_F_NSG = (1280, 1280 + NSA_GROUPS * GATE_PAD)
_F_ROWS = _F_NSG[1]

_NT = (((1,), (1,)), ((), ()))
_TN = (((0,), (0,)), ((), ()))


def _cparams(sem):
    return pltpu.CompilerParams(dimension_semantics=sem, vmem_limit_bytes=VMEM_LIMIT)


def _rms(x, g):
    return x * lax.rsqrt(jnp.mean(x * x, axis=-1, keepdims=True) + EPS) * g


def _inproj_kernel(x_ref, g_ref, wt_ref, wf_ref,
                   zml_ref, zg_ref, dak_ref, nsk_ref, nsc_ref,
                   daqT_ref, davT_ref, nsqT_ref, nsvT_ref, nsgT_ref):
    hb = _rms(x_ref[0], g_ref[...]).astype(BF16)

    def tdot(span):
        return jnp.dot(hb, wt_ref[:, span[0]:span[1]], preferred_element_type=F32)

    def fdot(span):
        return lax.dot_general(wf_ref[span[0]:span[1], :], hb, _NT, preferred_element_type=F32)

    zml_ref[0] = tdot(_T_ML)
    zg_ref[0] = tdot(_T_MLG)
    dak_ref[0] = tdot(_T_DAK).astype(BF16)
    nsk_ref[0] = tdot(_T_NSK).astype(BF16)
    nsc_ref[0] = tdot(_T_NSC)
    daqT_ref[0] = (fdot(_F_DAQ) * (DA_QK_DIM ** -0.5)).astype(BF16)
    davT_ref[0] = fdot(_F_DAV).astype(BF16)
    nsqT_ref[0] = (fdot(_F_NSQ) * (NSA_DIM ** -0.5)).astype(BF16)
    nsvT_ref[0] = fdot(_F_NSV).astype(BF16)
    nsgT_ref[0] = fdot(_F_NSG)


def _inproj(x, g, wt, wfT, tm=512):
    B, S, D = x.shape
    tok = lambda w, dt: jax.ShapeDtypeStruct((B, S, w), dt)
    feat = lambda r, dt: jax.ShapeDtypeStruct((B, r, S), dt)
    tspec = lambda w: pl.BlockSpec((1, tm, w), lambda b, i: (b, i, 0))
    fspec = lambda r: pl.BlockSpec((1, r, tm), lambda b, i: (b, 0, i))
    return pl.pallas_call(
        _inproj_kernel,
        grid=(B, S // tm),
        in_specs=[tspec(D),
                  pl.BlockSpec((1, D), lambda b, i: (0, 0)),
                  pl.BlockSpec((D, _T_COLS), lambda b, i: (0, 0)),
                  pl.BlockSpec((_F_ROWS, D), lambda b, i: (0, 0))],
        out_specs=[tspec(1024), tspec(128), tspec(256), tspec(256), tspec(256),
                   fspec(256), fspec(256), fspec(512), fspec(256), fspec(NSA_GROUPS * GATE_PAD)],
        out_shape=[tok(1024, F32), tok(128, F32), tok(256, BF16), tok(256, BF16), tok(256, F32),
                   feat(256, BF16), feat(256, BF16), feat(512, BF16), feat(256, BF16),
                   feat(NSA_GROUPS * GATE_PAD, F32)],
        compiler_params=_cparams(("parallel", "parallel")),
        name="inproj",
    )(x, g, wt, wfT)


def _log_sigmoid(x):
    return jnp.minimum(x, 0.0) - jnp.log1p(jnp.exp(-jnp.abs(x)))


def _sigmoid(x):
    return 1.0 / (1.0 + jnp.exp(-x))


def _mlstm_kernel(zml_ref, zg_ref, cw_ref, gb_ref, nrm_ref, y_ref,
                  pad_ref, c_ref, n_ref, m_ref, *, T):
    L = ML_CHUNK
    d = ML_DIM
    t = pl.program_id(1)

    @pl.when(t == 0)
    def _():
        pad_ref[0:8, :] = jnp.zeros((8, 2 * ML_WIDTH), F32)
        c_ref[...] = jnp.zeros_like(c_ref)
        n_ref[...] = jnp.zeros_like(n_ref)
        m_ref[...] = jnp.zeros_like(m_ref)

    @pl.when(t > 0)
    def _():
        pad_ref[0:8, :] = pad_ref[T:T + 8, :]

    pad_ref[8:8 + T, :] = zml_ref[0, :, 0:2 * ML_WIDTH]
    conv = cw_ref[0:1, :] * pad_ref[5:5 + T, :]
    for j in range(1, CONV_W):
        conv = conv + cw_ref[j:j + 1, :] * pad_ref[5 + j:5 + j + T, :]
    qk = conv * _sigmoid(conv)
    gates = zg_ref[0] + gb_ref[...]
    logf = _log_sigmoid(gates)

    row = lax.broadcasted_iota(jnp.int32, (L, L), 0)
    col = lax.broadcasted_iota(jnp.int32, (L, L), 1)
    causal = col <= row
    tril = causal.astype(F32)
    triu = (row <= col).astype(F32)

    for c in range(T // L):
        r0 = c * L
        g_c = gates[r0:r0 + L, :]
        lf_c = logf[r0:r0 + L, :]
        b_cols = jnp.dot(tril, lf_c, preferred_element_type=F32,
                         precision=lax.Precision.HIGHEST)
        b_rows = jnp.dot(lf_c.T, triu, preferred_element_type=F32,
                         precision=lax.Precision.HIGHEST)
        g_rows = g_c.T
        for h in range(ML_HEADS):
            b_col = b_cols[:, ML_HEADS + h:ML_HEADS + h + 1]
            b_row = b_rows[ML_HEADS + h:ML_HEADS + h + 1, :]
            ig_col = g_c[:, h:h + 1]
            ig_row = g_rows[h:h + 1, :]
            qh = qk[r0:r0 + L, h * d:(h + 1) * d]
            kh = qk[r0:r0 + L, ML_WIDTH + h * d:ML_WIDTH + (h + 1) * d] * (d ** -0.5)
            vh = zml_ref[0, r0:r0 + L, 2 * ML_WIDTH + h * d:2 * ML_WIDTH + (h + 1) * d]
            oh = zml_ref[0, r0:r0 + L, 3 * ML_WIDTH + h * d:3 * ML_WIDTH + (h + 1) * d]
            c_prev = c_ref[h]
            n_prev = n_ref[h, 0:1, :]
            m_prev = m_ref[h, 0:1, 0:1]

            g_tot = b_col[L - 1:L, :]
            a_row = g_tot - b_row + ig_row
            a_max = jnp.max(a_row, axis=1, keepdims=True)
            w_col = jnp.exp(g_tot - b_col + ig_col - a_max)
            c_loc = lax.dot_general(vh * w_col, kh, _TN, preferred_element_type=F32)
            n_loc = jnp.sum(kh * w_col, axis=0, keepdims=True)

            dmat = jnp.where(causal, b_col - b_row + ig_row, NEG)
            inter_log = b_col + m_prev
            m_t = jnp.maximum(inter_log, jnp.max(dmat, axis=1, keepdims=True))
            qkt = lax.dot_general(qh, kh, _NT, preferred_element_type=F32)
            wts = jnp.exp(dmat - m_t) * qkt
            s_inter = jnp.exp(inter_log - m_t)
            num = (jnp.dot(wts, vh, preferred_element_type=F32)
                   + s_inter * lax.dot_general(qh, c_prev, _NT, preferred_element_type=F32))
            den = (jnp.sum(wts, axis=1, keepdims=True)
                   + s_inter * jnp.sum(qh * n_prev, axis=1, keepdims=True))
            hh = num / jnp.maximum(jnp.abs(den), jnp.exp(-m_t))

            m_new = jnp.maximum(g_tot + m_prev, a_max)
            s_prev = jnp.exp(g_tot + m_prev - m_new)
            s_loc = jnp.exp(a_max - m_new)
            c_ref[h] = s_prev * c_prev + s_loc * c_loc
            n_ref[h] = jnp.broadcast_to(s_prev * n_prev + s_loc * n_loc, (8, d))
            m_ref[h] = jnp.broadcast_to(m_new, (8, LANES))

            yh = _sigmoid(oh) * hh
            y_ref[0, r0:r0 + L, h * d:(h + 1) * d] = _rms(yh, nrm_ref[0:1, h * d:(h + 1) * d])


def _mlstm(zml, zg, conv_w, gate_bias, norm_g, T=256):
    B, S, _ = zml.shape
    return pl.pallas_call(
        functools.partial(_mlstm_kernel, T=T),
        grid=(B, S // T),
        in_specs=[pl.BlockSpec((1, T, 1024), lambda b, t: (b, t, 0)),
                  pl.BlockSpec((1, T, 128), lambda b, t: (b, t, 0)),
                  pl.BlockSpec((CONV_W, 2 * ML_WIDTH), lambda b, t: (0, 0)),
                  pl.BlockSpec((1, 128), lambda b, t: (0, 0)),
                  pl.BlockSpec((1, ML_WIDTH), lambda b, t: (0, 0))],
        out_specs=pl.BlockSpec((1, T, ML_WIDTH), lambda b, t: (b, t, 0)),
        out_shape=jax.ShapeDtypeStruct((B, S, ML_WIDTH), F32),
        scratch_shapes=[pltpu.VMEM((T + 8, 2 * ML_WIDTH), F32),
                        pltpu.VMEM((ML_HEADS, ML_DIM, ML_DIM), F32),
                        pltpu.VMEM((ML_HEADS, 8, ML_DIM), F32),
                        pltpu.VMEM((ML_HEADS, 8, LANES), F32)],
        compiler_params=_cparams(("parallel", "arbitrary")),
        name="mlstm",
    )(zml, zg, conv_w, gate_bias, norm_g)


def _softmax_step(s, valid, m_ref, l_ref, acc_ref, p_ref, cols):
    m_old = m_ref[:, cols]
    if valid is None:
        m_new = jnp.maximum(m_old, jnp.max(s, axis=0, keepdims=True))
        p = jnp.exp(s - m_new)
    else:
        m_new = jnp.maximum(m_old, jnp.max(jnp.where(valid, s, NEG), axis=0, keepdims=True))
        p = jnp.exp(jnp.where(valid, s - m_new, NEG))
    alpha = jnp.exp(m_old - m_new)
    m_ref[:, cols] = m_new
    l_ref[:, cols] = alpha * l_ref[:, cols] + jnp.sum(p, axis=0, keepdims=True)
    acc_ref[:, cols] = alpha * acc_ref[:, cols]
    p_ref[:, cols] = p.astype(BF16)


def _da_kernel(lam_ref, qT_ref, k_ref, vT_ref, gain_ref, o_ref,
               qz_ref, p_ref, m_ref, l_ref, acc_ref, *, tq, lam_init):
    h = pl.program_id(1)
    qi = pl.program_id(2)
    d = DA_QK_DIM
    tk = tq

    qz_ref[...] = jnp.zeros_like(qz_ref)
    for hh in range(2):
        @pl.when(h % 2 == hh)
        def _():
            qz_ref[hh * 2 * d:hh * 2 * d + d, 0:tq] = qT_ref[0, 0:d, :]
            qz_ref[hh * 2 * d + d:(hh + 1) * 2 * d, tq:2 * tq] = qT_ref[0, d:2 * d, :]
    m_ref[...] = jnp.full_like(m_ref, NEG)
    l_ref[...] = jnp.zeros_like(l_ref)
    acc_ref[...] = jnp.zeros_like(acc_ref)

    def step(j, diag):
        start = pl.multiple_of(j * tk, tk)
        s = jnp.dot(k_ref[0, pl.ds(start, tk), :], qz_ref[...], preferred_element_type=F32)
        for mp in range(2):
            cols = slice(mp * tq, (mp + 1) * tq)
            valid = None
            if diag:
                kpos = lax.broadcasted_iota(jnp.int32, (tk, tq), 0)
                qpos = lax.broadcasted_iota(jnp.int32, (tk, tq), 1)
                valid = kpos <= qpos
            _softmax_step(s[:, cols], valid, m_ref, l_ref, acc_ref, p_ref, cols)
        acc_ref[...] += jnp.dot(vT_ref[0, :, pl.ds(start, tk)], p_ref[...], preferred_element_type=F32)

    def body(j, carry):
        step(j, False)
        return carry

    lax.fori_loop(0, qi, body, 0)
    step(qi, True)

    lp = lam_ref[...]
    lam = (jnp.exp(jnp.sum(lp[0:1] * lp[1:2], axis=1, keepdims=True))
           - jnp.exp(jnp.sum(lp[2:3] * lp[3:4], axis=1, keepdims=True)) + lam_init)
    o1 = acc_ref[:, 0:tq] / l_ref[:, 0:tq]
    o2 = acc_ref[:, tq:2 * tq] / l_ref[:, tq:2 * tq]
    o = o1 - lam * o2
    y = o * lax.rsqrt(jnp.mean(o * o, axis=0, keepdims=True) + EPS) * gain_ref[...]
    o_ref[0] = (y * (1.0 - lam_init)).astype(o_ref.dtype)


def _diff_attn(da_lambda, daqT, dak, davT, gain_col, lam_init, tq=256):
    B, _, S = daqT.shape
    return pl.pallas_call(
        functools.partial(_da_kernel, tq=tq, lam_init=lam_init),
        grid=(B, DA_HEADS, S // tq),
        in_specs=[pl.BlockSpec((4, DA_QK_DIM), lambda b, h, i: (0, 0)),
                  pl.BlockSpec((1, DA_V_DIM, tq), lambda b, h, i: (b, h, i)),
                  pl.BlockSpec((1, S, LANES), lambda b, h, i: (b, 0, h // 2)),
                  pl.BlockSpec((1, DA_V_DIM, S), lambda b, h, i: (b, h, 0)),
                  pl.BlockSpec((DA_V_DIM, 1), lambda b, h, i: (h, 0))],
        out_specs=pl.BlockSpec((1, DA_V_DIM, tq), lambda b, h, i: (b, h, i)),
        out_shape=jax.ShapeDtypeStruct((B, DA_WIDTH, S), BF16),
        scratch_shapes=[pltpu.VMEM((LANES, 2 * tq), BF16),
                        pltpu.VMEM((tq, 2 * tq), BF16),
                        pltpu.VMEM((1, 2 * tq), F32),
                        pltpu.VMEM((1, 2 * tq), F32),
                        pltpu.VMEM((DA_V_DIM, 2 * tq), F32)],
        compiler_params=_cparams(("parallel", "parallel", "arbitrary")),
        name="diff_attn",
    )(da_lambda, daqT, dak, davT, gain_col)


def _gelu_tanh(x):
    return x * (0.5 * (1.0 + jnp.tanh(math.sqrt(2.0 / math.pi) * (x + 0.044715 * (x * x * x)))))


def _compress_kernel(x_ref, pe_ref, w1_ref, w2_ref, o_ref, b_ref, *, feature_major):
    n = x_ref.shape[2]
    half = CMP_STRIDE * NSA_DIM
    x = x_ref[0, 0]
    a = jnp.dot((x + pe_ref[0, 0:1, :]).astype(BF16), w1_ref[0, 0:half, :], preferred_element_type=F32)
    b_ref[0:n, :] = jnp.dot((x + pe_ref[0, 1:2, :]).astype(BF16), w1_ref[0, half:2 * half, :],
                            preferred_element_type=F32)
    b_ref[n:n + 8, :] = jnp.zeros((8, CMP_HIDDEN), F32)
    hid = _gelu_tanh(a + b_ref[1:n + 1, :]).astype(BF16)
    if feature_major:
        o_ref[0, 0] = lax.dot_general(w2_ref[0], hid, _NT, preferred_element_type=F32).astype(o_ref.dtype)
    else:
        o_ref[0, 0] = jnp.dot(hid, w2_ref[0], preferred_element_type=F32).astype(o_ref.dtype)


def _compress(x16, pe2, w1, w2, which, feature_major):
    B, _, n, half = x16.shape
    G = NSA_GROUPS
    if feature_major:
        out_shape, out_block = (B, G, NSA_DIM, n), (1, 1, NSA_DIM, n)
        w2_arr, w2_block = jnp.swapaxes(w2, 1, 2), (1, NSA_DIM, CMP_HIDDEN)
    else:
        out_shape, out_block = (B, G, n, NSA_DIM), (1, 1, n, NSA_DIM)
        w2_arr, w2_block = w2, (1, CMP_HIDDEN, NSA_DIM)
    return pl.pallas_call(
        functools.partial(_compress_kernel, feature_major=feature_major),
        grid=(B, G),
        in_specs=[pl.BlockSpec((1, 1, n, half), lambda b, g: (b, which * G + g, 0, 0)),
                  pl.BlockSpec((1, 2, half), lambda b, g: (which, 0, 0)),
                  pl.BlockSpec((1, 2 * half, CMP_HIDDEN), lambda b, g: (which, 0, 0)),
                  pl.BlockSpec(w2_block, lambda b, g: (which, 0, 0))],
        out_specs=pl.BlockSpec(out_block, lambda b, g: (b, g, 0, 0)),
        out_shape=jax.ShapeDtypeStruct(out_shape, BF16),
        scratch_shapes=[pltpu.VMEM((n + 8, CMP_HIDDEN), F32)],
        compiler_params=_cparams(("parallel", "parallel")),
        name="compress_v" if feature_major else "compress_k",
    )(x16, pe2, w1, w2_arr.astype(BF16))


def _nsa_kernel(qT_ref, gT_ref, kc_ref, vcT_ref, ks_ref, vsT_ref, kw_ref, vwT_ref, o_ref,
                qg_ref, qz_ref, imp_ref, sel_ref, pc_ref, p_ref, m_ref, l_ref, acc_ref, out_ref,
                *, tq, tk, S):
    g = pl.program_id(1)
    qi = pl.program_id(2)
    d = NSA_DIM
    R = NSA_REP
    ncb = S // CMP_STRIDE
    nsb = S // SEL_BLOCK
    q0 = qi * tq
    jd = q0 // tk
    ratio = SEL_BLOCK // CMP_STRIDE
    per_tile = tk // SEL_BLOCK

    for r in range(R):
        qg_ref[:, r * tq:(r + 1) * tq] = qT_ref[0, r * d:(r + 1) * d, :]
    qz_ref[...] = jnp.zeros_like(qz_ref)
    for gg in range(NSA_GROUPS):
        @pl.when(g == gg)
        def _():
            qz_ref[gg * d:(gg + 1) * d, :] = qg_ref[...]

    def gate(r, br):
        return _sigmoid(gT_ref[0, r * 3 + br:r * 3 + br + 1, :])

    qpos = q0 + lax.broadcasted_iota(jnp.int32, (1, tq), 1)

    s_all = jnp.dot(kc_ref[0, 0], qg_ref[...], preferred_element_type=F32)
    cend = lax.broadcasted_iota(jnp.int32, (ncb, tq), 0) * CMP_STRIDE + (CMP_BLOCK - 1)
    cvalid = cend <= qpos
    imp = jnp.zeros((ncb, tq), F32)
    for r in range(R):
        cols = slice(r * tq, (r + 1) * tq)
        s = s_all[:, cols]
        mx = jnp.max(jnp.where(cvalid, s, NEG), axis=0, keepdims=True)
        p = jnp.exp(jnp.where(cvalid, s - mx, NEG))
        den = jnp.sum(p, axis=0, keepdims=True)
        p = p / jnp.where(den > 0, den, 1.0)
        imp = imp + p
        pc_ref[:, cols] = p.astype(BF16)
    o_cmp = jnp.dot(vcT_ref[0, 0], pc_ref[...], preferred_element_type=F32)
    for r in range(R):
        cols = slice(r * tq, (r + 1) * tq)
        out_ref[:, cols] = gate(r, 0) * o_cmp[:, cols]

    imp_ref[0:8, :] = jnp.zeros((8, tq), F32)
    imp_ref[8:8 + ncb, :] = imp
    imp_ref[8 + ncb:16 + ncb, :] = jnp.zeros((8, tq), F32)
    p_slc = jnp.zeros((nsb, tq), F32)
    for o in range(-1, ratio):
        p_slc = p_slc + imp_ref[pl.ds(8 + o, nsb, stride=ratio), :]
    blk = lax.broadcasted_iota(jnp.int32, (nsb, tq), 0)
    cur = lax.shift_right_logical(qpos, int(math.log2(SEL_BLOCK)))
    forced = (blk == 0) | (blk == cur) | (blk == cur - 1)
    causal_blk = blk * SEL_BLOCK <= qpos
    score = jnp.where(forced, FORCE_SCORE, jnp.where(causal_blk, p_slc, -1.0))
    blk_f = blk.astype(F32)
    sel = jnp.zeros((nsb, tq), F32)
    for _ in range(SEL_TOPK):
        mx = jnp.max(score, axis=0, keepdims=True)
        first = jnp.min(jnp.where(score == mx, blk_f, float(nsb)), axis=0, keepdims=True)
        hit = blk_f == first
        sel = jnp.where(hit, 1.0, sel)
        score = jnp.where(hit, -3e38, score)
    sel_ref[...] = sel

    def reset():
        m_ref[...] = jnp.full_like(m_ref, NEG)
        l_ref[...] = jnp.zeros_like(l_ref)
        acc_ref[...] = jnp.zeros_like(acc_ref)

    def attend(k_ref_, vT_ref_, j, valid):
        start = pl.multiple_of(j * tk, tk)
        s = jnp.dot(k_ref_[0, pl.ds(start, tk), :], qz_ref[...], preferred_element_type=F32)
        for r in range(R):
            cols = slice(r * tq, (r + 1) * tq)
            _softmax_step(s[:, cols], valid, m_ref, l_ref, acc_ref, p_ref, cols)
        acc_ref[...] += jnp.dot(vT_ref_[0, :, pl.ds(start, tk)], p_ref[...], preferred_element_type=F32)

    def flush(br):
        for r in range(R):
            cols = slice(r * tq, (r + 1) * tq)
            den = l_ref[:, cols]
            out_ref[:, cols] += gate(r, br) * (acc_ref[:, cols] / jnp.where(den > 0, den, 1.0))

    def kpos_of(j):
        return j * tk + lax.broadcasted_iota(jnp.int32, (tk, tq), 0)

    def sel_valid(j):
        rows = [jnp.broadcast_to(sel_ref[pl.ds(j * per_tile + i, 1), :], (SEL_BLOCK, tq))
                for i in range(per_tile)]
        return jnp.concatenate(rows, axis=0) > 0.5

    reset()

    def sel_body(j, carry):
        attend(ks_ref, vsT_ref, j, sel_valid(j))
        return carry

    lax.fori_loop(0, jd, sel_body, 0)
    attend(ks_ref, vsT_ref, jd, sel_valid(jd) & (kpos_of(jd) <= qpos))
    flush(1)

    reset()
    for back in range(WINDOW // tk, -1, -1):
        @pl.when(jd - back >= 0)
        def _():
            rel = qpos - kpos_of(jd - back)
            attend(kw_ref, vwT_ref, jd - back, (rel >= 0) & (rel < WINDOW))
    flush(2)

    for r in range(R):
        o_ref[0, r * d:(r + 1) * d, :] = out_ref[:, r * tq:(r + 1) * tq].astype(o_ref.dtype)


def _nsa(nsqT, nsgT, kcmp, vcmpT, nsk, nsvT, tq=128, tk=256):
    B, _, S = nsqT.shape
    G, R, d = NSA_GROUPS, NSA_REP, NSA_DIM
    ncb = S // CMP_STRIDE
    nsb = S // SEL_BLOCK
    return pl.pallas_call(
        functools.partial(_nsa_kernel, tq=tq, tk=tk, S=S),
        grid=(B, G, S // tq),
        in_specs=[pl.BlockSpec((1, R * d, tq), lambda b, g, i: (b, g, i)),
                  pl.BlockSpec((1, GATE_PAD, tq), lambda b, g, i: (b, g, i)),
                  pl.BlockSpec((1, 1, ncb, d), lambda b, g, i: (b, g, 0, 0)),
                  pl.BlockSpec((1, 1, d, ncb), lambda b, g, i: (b, g, 0, 0)),
                  pl.BlockSpec((1, S, LANES), lambda b, g, i: (b, 0, 0)),
                  pl.BlockSpec((1, d, S), lambda b, g, i: (b, g, 0)),
                  pl.BlockSpec((1, S, LANES), lambda b, g, i: (b, 0, 1)),
                  pl.BlockSpec((1, d, S), lambda b, g, i: (b, G + g, 0))],
        out_specs=pl.BlockSpec((1, R * d, tq), lambda b, g, i: (b, g, i)),
        out_shape=jax.ShapeDtypeStruct((B, NSA_WIDTH, S), BF16),
        scratch_shapes=[pltpu.VMEM((d, R * tq), BF16),
                        pltpu.VMEM((LANES, R * tq), BF16),
                        pltpu.VMEM((ncb + 16, tq), F32),
                        pltpu.VMEM((nsb, tq), F32),
                        pltpu.VMEM((ncb, R * tq), BF16),
                        pltpu.VMEM((tk, R * tq), BF16),
                        pltpu.VMEM((1, R * tq), F32),
                        pltpu.VMEM((1, R * tq), F32),
                        pltpu.VMEM((d, R * tq), F32),
                        pltpu.VMEM((d, R * tq), F32)],
        compiler_params=_cparams(("parallel", "parallel", "arbitrary")),
        name="nsa",
    )(nsqT, nsgT, kcmp, vcmpT, nsk, nsvT, nsk, nsvT)


def _outproj_kernel(x_ref, yml_ref, ydaT_ref, ynsT_ref, wo_ref, o_ref):
    acc = x_ref[0] + jnp.dot(yml_ref[0].astype(BF16), wo_ref[0:ML_WIDTH, :], preferred_element_type=F32)
    acc = acc + lax.dot_general(ydaT_ref[0], wo_ref[ML_WIDTH:ML_WIDTH + DA_WIDTH, :], _TN,
                                preferred_element_type=F32)
    acc = acc + lax.dot_general(ynsT_ref[0], wo_ref[ML_WIDTH + DA_WIDTH:, :], _TN,
                                preferred_element_type=F32)
    o_ref[0] = acc


def _outproj(x, yml, ydaT, ynsT, wo, tm=512):
    B, S, D = x.shape
    return pl.pallas_call(
        _outproj_kernel,
        grid=(B, S // tm),
        in_specs=[pl.BlockSpec((1, tm, D), lambda b, i: (b, i, 0)),
                  pl.BlockSpec((1, tm, ML_WIDTH), lambda b, i: (b, i, 0)),
                  pl.BlockSpec((1, DA_WIDTH, tm), lambda b, i: (b, 0, i)),
                  pl.BlockSpec((1, NSA_WIDTH, tm), lambda b, i: (b, 0, i)),
                  pl.BlockSpec((D, D), lambda b, i: (0, 0))],
        out_specs=pl.BlockSpec((1, tm, D), lambda b, i: (b, i, 0)),
        out_shape=jax.ShapeDtypeStruct((B, S, D), F32),
        compiler_params=_cparams(("parallel", "parallel")),
        name="outproj",
    )(x, yml, ydaT, ynsT, wo)


def _ffn_kernel(x_ref, g_ref, w1_ref, w2_ref, gf_ref, o_ref, hb_ref, acc_ref, *, final):
    j = pl.program_id(1)

    @pl.when(j == 0)
    def _():
        x = x_ref[...]
        hb_ref[...] = _rms(x, g_ref[...]).astype(BF16)
        acc_ref[...] = x

    u = jnp.dot(hb_ref[...], w1_ref[...], preferred_element_type=F32)
    a = jnp.square(jnp.maximum(u, 0.0)).astype(BF16)
    acc_ref[...] += jnp.dot(a, w2_ref[...], preferred_element_type=F32)

    @pl.when(j == pl.num_programs(1) - 1)
    def _():
        y = acc_ref[...]
        if final:
            y = _rms(y, gf_ref[...])
        o_ref[...] = y


def _ffn(x2d, g, w1, w2, gf, final, tm=1024, tf=512):
    N, D = x2d.shape
    return pl.pallas_call(
        functools.partial(_ffn_kernel, final=final),
        grid=(N // tm, D_FF // tf),
        in_specs=[pl.BlockSpec((tm, D), lambda i, j: (i, 0)),
                  pl.BlockSpec((1, D), lambda i, j: (0, 0)),
                  pl.BlockSpec((D, tf), lambda i, j: (0, j)),
                  pl.BlockSpec((tf, D), lambda i, j: (j, 0)),
                  pl.BlockSpec((1, D), lambda i, j: (0, 0))],
        out_specs=pl.BlockSpec((tm, D), lambda i, j: (i, 0)),
        out_shape=jax.ShapeDtypeStruct((N, D), F32),
        scratch_shapes=[pltpu.VMEM((tm, D), BF16), pltpu.VMEM((tm, D), F32)],
        compiler_params=_cparams(("parallel", "arbitrary")),
        name="ffn",
    )(x2d, g, w1, w2, gf)


def _split_w_in(w_in_l):
    edges = np.concatenate([[0], np.cumsum(IN_SIZES)])
    return [w_in_l[:, int(edges[i]):int(edges[i + 1])] for i in range(len(IN_SIZES))]


def _pad_cols(w, n):
    return jnp.pad(w, ((0, 0), (0, n - w.shape[1])))


def _inproj_weights(w_in_l):
    (ml_q, ml_k, ml_v, ml_o, ml_i, ml_f, da_q, da_k, da_v,
     ns_q, ns_kc, ns_vc, ns_ks, ns_vs, ns_kw, ns_vw, ns_g) = _split_w_in(w_in_l)
    wt = jnp.concatenate([ml_q, ml_k, ml_v, ml_o, _pad_cols(jnp.concatenate([ml_i, ml_f], 1), LANES),
                          da_k, ns_ks, ns_kw, ns_kc, ns_vc], axis=1)
    per_group = NSA_REP * 3
    ns_g_pad = jnp.concatenate(
        [_pad_cols(ns_g[:, gi * per_group:(gi + 1) * per_group], GATE_PAD) for gi in range(NSA_GROUPS)], axis=1)
    wf = jnp.concatenate([da_q, da_v, ns_q, ns_vs, ns_vw, ns_g_pad], axis=1)
    return wt.astype(BF16), wf.T.astype(BF16)


def _half_blocks(nsc):
    B, S, _ = nsc.shape
    n = S // CMP_STRIDE
    t = nsc.reshape(B, n, CMP_STRIDE, 2 * NSA_GROUPS, NSA_DIM)
    return t.transpose(0, 3, 1, 2, 4).reshape(B, 2 * NSA_GROUPS, n, CMP_STRIDE * NSA_DIM)


def kernel(x, norm1, w_in, ml_conv, ml_gate_bias, ml_norm, da_lambda, da_norm, nsa_pe,
           nsa_w1, nsa_w2, w_out, norm2, w_ff1, w_ff2, final_norm):
    B, S, D = x.shape
    depth = norm1.shape[0]
    for l in range(depth):
        wt, wfT = _inproj_weights(w_in[l])
        (zml, zg, dak, nsk, nsc, daqT, davT, nsqT, nsvT, nsgT) = _inproj(x, norm1[l][None, :], wt, wfT)

        yml = _mlstm(zml, zg, ml_conv[l], _pad_cols(ml_gate_bias[l][None, :], LANES), ml_norm[l][None, :])

        lam_init = 0.8 - 0.6 * math.exp(-0.3 * l)
        ydaT = _diff_attn(da_lambda[l], daqT, dak, davT, da_norm[l][:, None], lam_init)

        x16 = _half_blocks(nsc)
        pe2 = nsa_pe[l].reshape(2, 2, CMP_STRIDE * NSA_DIM)
        w1b = nsa_w1[l].astype(BF16)
        kcmp = _compress(x16, pe2, w1b, nsa_w2[l], 0, False)
        vcmpT = _compress(x16, pe2, w1b, nsa_w2[l], 1, True)
        ynsT = _nsa(nsqT, nsgT, kcmp, vcmpT, nsk, nsvT)

        x = _outproj(x, yml, ydaT, ynsT, w_out[l].astype(BF16))
        x = _ffn(x.reshape(B * S, D), norm2[l][None, :], w_ff1[l].astype(BF16), w_ff2[l].astype(BF16),
                 final_norm[None, :], final=(l == depth - 1)).reshape(B, S, D)
    return x
```

```python
import functools
import math

import numpy as np
import jax
import jax.numpy as jnp
from jax import lax
from jax.experimental import pallas as pl
from jax.experimental.pallas import tpu as pltpu

F32 = jnp.float32
BF16 = jnp.bfloat16

D_MODEL = 1024
ML_HEADS = 4
ML_DIM = 64
ML_WIDTH = ML_HEADS * ML_DIM
ML_CHUNK = 64
CONV_W = 4
DA_HEADS = 4
DA_QK_DIM = 32
DA_V_DIM = 64
DA_WIDTH = DA_HEADS * DA_V_DIM
NSA_HEADS = 8
NSA_GROUPS = 2
NSA_REP = NSA_HEADS // NSA_GROUPS
NSA_DIM = 64
NSA_WIDTH = NSA_HEADS * NSA_DIM
NSA_KV = NSA_GROUPS * NSA_DIM
CMP_BLOCK = 32
CMP_STRIDE = 16
CMP_HIDDEN = 4 * NSA_DIM
SEL_BLOCK = 64
SEL_TOPK = 16
WINDOW = 512
D_FF = 4 * D_MODEL
EPS = 1e-6
FORCE_SCORE = 1e4
IN_SIZES = (ML_WIDTH, ML_WIDTH, ML_WIDTH, ML_WIDTH, ML_HEADS, ML_HEADS,
            2 * DA_HEADS * DA_QK_DIM, 2 * DA_HEADS * DA_QK_DIM, DA_WIDTH,
            NSA_WIDTH, NSA_KV, NSA_KV, NSA_KV, NSA_KV, NSA_KV, NSA_KV, 3 * NSA_HEADS)

LANES = 128
GATE_PAD = 16
NEG = -1e30
M_INIT = -1e29
LOG2E = 1.4426950408889634
V_DIM = 64
V_EXT = 80
VMEM_LIMIT = 56 * 1024 * 1024

_T_ML = (0, 1024)
_T_MLG = (1024, 1152)
_T_DAK = (1152, 1408)
_T_NSK = (1408, 1664)
_T_NSC = (1664, 1920)
_T_COLS = 1920
_F_DAQ = (0, 256)
_F_DAV = (256, 512)
_F_NSQ = (512, 1024)
_F_NSV = (1024, 1280)
_F_NSG = (1280, 1280 + NSA_GROUPS * GATE_PAD)
_F_ROWS = _F_NSG[1]

_NT = (((1,), (1,)), ((), ()))
_TN = (((0,), (0,)), ((), ()))


def _cparams(sem):
    return pltpu.CompilerParams(dimension_semantics=sem, vmem_limit_bytes=VMEM_LIMIT)


def _rms(x, g):
    return x * lax.rsqrt(jnp.mean(x * x, axis=-1, keepdims=True) + EPS) * g


def _inproj_kernel(x_ref, g_ref, wt_ref, wf_ref,
                   zml_ref, zg_ref, dak_ref, nsk_ref, nsc_ref,
                   daqT_ref, davT_ref, nsqT_ref, nsvT_ref, nsgT_ref):
    hb = _rms(x_ref[0], g_ref[...]).astype(BF16)

    def tdot(span):
        return jnp.dot(hb, wt_ref[:, span[0]:span[1]], preferred_element_type=F32)

    def fdot(span):
        return lax.dot_general(wf_ref[span[0]:span[1], :], hb, _NT, preferred_element_type=F32)

    zml_ref[0] = tdot(_T_ML)
    zg_ref[0] = tdot(_T_MLG)
    dak_ref[0] = tdot(_T_DAK).astype(BF16)
    nsk_ref[0] = tdot(_T_NSK).astype(BF16)
    nsc_ref[0] = tdot(_T_NSC)
    daqT_ref[0] = (fdot(_F_DAQ) * (DA_QK_DIM ** -0.5 * LOG2E)).astype(BF16)
    nsqT_ref[0] = (fdot(_F_NSQ) * (NSA_DIM ** -0.5 * LOG2E)).astype(BF16)
    nsgT_ref[0] = fdot(_F_NSG)
    tm = hb.shape[0]
    for v_ref, span in ((davT_ref, _F_DAV), (nsvT_ref, _F_NSV)):
        v = fdot(span).astype(BF16)
        for h in range((span[1] - span[0]) // V_DIM):
            v_ref[0, h * V_EXT:h * V_EXT + V_DIM, :] = v[h * V_DIM:(h + 1) * V_DIM, :]
            v_ref[0, h * V_EXT + V_DIM:(h + 1) * V_EXT, :] = jnp.ones((V_EXT - V_DIM, tm), BF16)


def _inproj(x, g, wt, wfT, tm=512):
    B, S, D = x.shape
    tok = lambda w, dt: jax.ShapeDtypeStruct((B, S, w), dt)
    feat = lambda r, dt: jax.ShapeDtypeStruct((B, r, S), dt)
    tspec = lambda w: pl.BlockSpec((1, tm, w), lambda b, i: (b, i, 0))
    fspec = lambda r: pl.BlockSpec((1, r, tm), lambda b, i: (b, 0, i))
    return pl.pallas_call(
        _inproj_kernel,
        grid=(B, S // tm),
        in_specs=[tspec(D),
                  pl.BlockSpec((1, D), lambda b, i: (0, 0)),
                  pl.BlockSpec((D, _T_COLS), lambda b, i: (0, 0)),
                  pl.BlockSpec((_F_ROWS, D), lambda b, i: (0, 0))],
        out_specs=[tspec(1024), tspec(128), tspec(256), tspec(256), tspec(256),
                   fspec(256), fspec(4 * V_EXT), fspec(512), fspec(4 * V_EXT), fspec(NSA_GROUPS * GATE_PAD)],
        out_shape=[tok(1024, F32), tok(128, F32), tok(256, BF16), tok(256, BF16), tok(256, F32),
                   feat(256, BF16), feat(4 * V_EXT, BF16), feat(512, BF16), feat(4 * V_EXT, BF16),
                   feat(NSA_GROUPS * GATE_PAD, F32)],
        compiler_params=_cparams(("parallel", "parallel")),
        name="inproj",
    )(x, g, wt, wfT)


def _log_sigmoid(x):
    return jnp.minimum(x, 0.0) - jnp.log1p(jnp.exp(-jnp.abs(x)))


def _sigmoid(x):
    return 1.0 / (1.0 + jnp.exp(-x))


def _mlstm_kernel(zml_ref, zg_ref, cw_ref, gb_ref, nrm_ref, y_ref,
                  pad_ref, c_ref, n_ref, m_ref, *, T):
    L = ML_CHUNK
    d = ML_DIM
    t = pl.program_id(1)

    @pl.when(t == 0)
    def _():
        pad_ref[0:8, :] = jnp.zeros((8, 2 * ML_WIDTH), F32)
        c_ref[...] = jnp.zeros_like(c_ref)
        n_ref[...] = jnp.zeros_like(n_ref)
        m_ref[...] = jnp.zeros_like(m_ref)

    @pl.when(t > 0)
    def _():
        pad_ref[0:8, :] = pad_ref[T:T + 8, :]

    pad_ref[8:8 + T, :] = zml_ref[0, :, 0:2 * ML_WIDTH]
    conv = cw_ref[0:1, :] * pad_ref[5:5 + T, :]
    for j in range(1, CONV_W):
        conv = conv + cw_ref[j:j + 1, :] * pad_ref[5 + j:5 + j + T, :]
    qk = conv * _sigmoid(conv)
    gates = zg_ref[0] + gb_ref[...]
    logf = _log_sigmoid(gates)

    row = lax.broadcasted_iota(jnp.int32, (L, L), 0)
    col = lax.broadcasted_iota(jnp.int32, (L, L), 1)
    causal = col <= row
    tril = causal.astype(F32)
    triu = (row <= col).astype(F32)

    for c in range(T // L):
        r0 = c * L
        g_c = gates[r0:r0 + L, :]
        lf_c = logf[r0:r0 + L, :]
        b_cols = jnp.dot(tril, lf_c, preferred_element_type=F32,
                         precision=lax.Precision.HIGHEST)
        b_rows = jnp.dot(lf_c.T, triu, preferred_element_type=F32,
                         precision=lax.Precision.HIGHEST)
        g_rows = g_c.T
        for h in range(ML_HEADS):
            b_col = b_cols[:, ML_HEADS + h:ML_HEADS + h + 1]
            b_row = b_rows[ML_HEADS + h:ML_HEADS + h + 1, :]
            ig_col = g_c[:, h:h + 1]
            ig_row = g_rows[h:h + 1, :]
            qh = qk[r0:r0 + L, h * d:(h + 1) * d]
            kh = qk[r0:r0 + L, ML_WIDTH + h * d:ML_WIDTH + (h + 1) * d] * (d ** -0.5)
            vh = zml_ref[0, r0:r0 + L, 2 * ML_WIDTH + h * d:2 * ML_WIDTH + (h + 1) * d]
            oh = zml_ref[0, r0:r0 + L, 3 * ML_WIDTH + h * d:3 * ML_WIDTH + (h + 1) * d]
            c_prev = c_ref[h]
            n_prev = n_ref[h, 0:1, :]
            m_prev = m_ref[h, 0:1, 0:1]

            g_tot = b_col[L - 1:L, :]
            a_row = g_tot - b_row + ig_row
            a_max = jnp.max(a_row, axis=1, keepdims=True)
            w_col = jnp.exp(g_tot - b_col + ig_col - a_max)
            c_loc = lax.dot_general(vh * w_col, kh, _TN, preferred_element_type=F32)
            n_loc = jnp.sum(kh * w_col, axis=0, keepdims=True)

            dmat = jnp.where(causal, b_col - b_row + ig_row, NEG)
            inter_log = b_col + m_prev
            m_t = jnp.maximum(inter_log, jnp.max(dmat, axis=1, keepdims=True))
            qkt = lax.dot_general(qh, kh, _NT, preferred_element_type=F32)
            wts = jnp.exp(dmat - m_t) * qkt
            s_inter = jnp.exp(inter_log - m_t)
            num = (jnp.dot(wts, vh, preferred_element_type=F32)
                   + s_inter * lax.dot_general(qh, c_prev, _NT, preferred_element_type=F32))
            den = (jnp.sum(wts, axis=1, keepdims=True)
                   + s_inter * jnp.sum(qh * n_prev, axis=1, keepdims=True))
            hh = num / jnp.maximum(jnp.abs(den), jnp.exp(-m_t))

            m_new = jnp.maximum(g_tot + m_prev, a_max)
            s_prev = jnp.exp(g_tot + m_prev - m_new)
            s_loc = jnp.exp(a_max - m_new)
            c_ref[h] = s_prev * c_prev + s_loc * c_loc
            n_ref[h] = jnp.broadcast_to(s_prev * n_prev + s_loc * n_loc, (8, d))
            m_ref[h] = jnp.broadcast_to(m_new, (8, LANES))

            yh = _sigmoid(oh) * hh
            y_ref[0, r0:r0 + L, h * d:(h + 1) * d] = _rms(yh, nrm_ref[0:1, h * d:(h + 1) * d])


def _mlstm(zml, zg, conv_w, gate_bias, norm_g, T=256):
    B, S, _ = zml.shape
    return pl.pallas_call(
        functools.partial(_mlstm_kernel, T=T),
        grid=(B, S // T),
        in_specs=[pl.BlockSpec((1, T, 1024), lambda b, t: (b, t, 0)),
                  pl.BlockSpec((1, T, 128), lambda b, t: (b, t, 0)),
                  pl.BlockSpec((CONV_W, 2 * ML_WIDTH), lambda b, t: (0, 0)),
                  pl.BlockSpec((1, 128), lambda b, t: (0, 0)),
                  pl.BlockSpec((1, ML_WIDTH), lambda b, t: (0, 0))],
        out_specs=pl.BlockSpec((1, T, ML_WIDTH), lambda b, t: (b, t, 0)),
        out_shape=jax.ShapeDtypeStruct((B, S, ML_WIDTH), F32),
        scratch_shapes=[pltpu.VMEM((T + 8, 2 * ML_WIDTH), F32),
                        pltpu.VMEM((ML_HEADS, ML_DIM, ML_DIM), F32),
                        pltpu.VMEM((ML_HEADS, 8, ML_DIM), F32),
                        pltpu.VMEM((ML_HEADS, 8, LANES), F32)],
        compiler_params=_cparams(("parallel", "arbitrary")),
        name="mlstm",
    )(zml, zg, conv_w, gate_bias, norm_g)


def _softmax_cols(s, m_ref, p_ref, cols):
    m_old = m_ref[:, cols]
    m_new = jnp.maximum(m_old, jnp.max(s, axis=0, keepdims=True))
    m_ref[:, cols] = m_new
    p_ref[:, cols] = jnp.exp2(s - m_new).astype(BF16)
    return jnp.exp2(m_old - m_new)


def _pipelined_attention(k_tile, vT_tile, qz_ref, s_refs, p_refs, m_ref, acc_ref, softmax,
                         first, n_plain_pairs, n_tail, max_tile, col_groups):
    sa, sb = s_refs
    pa, pb = p_refs

    def load(j):
        return jnp.clip(j, 0, max_tile)

    def half(j, s_cur, s_nxt, p_cur, p_prev, tail):
        s_nxt[...] = jnp.dot(k_tile(load(j + 1)), qz_ref[...], preferred_element_type=F32)
        pv = jnp.dot(vT_tile(load(j - 1)), p_prev[...], preferred_element_type=F32)
        alphas = softmax(s_cur, p_cur, j, tail)
        for cols, alpha in zip(col_groups, alphas):
            acc_ref[:, cols] = alpha * (acc_ref[:, cols] + pv[:, cols])

    def pair(j, tail):
        half(j, sa, sb, pa, pb, tail)
        half(j + 1, sb, sa, pb, pa, tail)

    m_ref[...] = jnp.full_like(m_ref, M_INIT)
    acc_ref[...] = jnp.zeros_like(acc_ref)
    pb[...] = jnp.zeros_like(pb)
    sa[...] = jnp.dot(k_tile(load(first)), qz_ref[...], preferred_element_type=F32)

    def body(i, carry):
        pair(first + 2 * i, False)
        return carry

    lax.fori_loop(0, n_plain_pairs, body, 0)
    j = first + 2 * n_plain_pairs
    bufs = ((sa, sb, pa, pb), (sb, sa, pb, pa))
    for t in range(n_tail):
        half(j + t, *bufs[t % 2], True)
    p_last = bufs[(n_tail - 1) % 2][2]
    acc_ref[...] += jnp.dot(vT_tile(load(j + n_tail - 1)), p_last[...], preferred_element_type=F32)


def _da_kernel(lam_ref, qT_ref, k_ref, vT_ref, gain_ref, o_ref,
               qz_ref, sa_ref, sb_ref, pa_ref, pb_ref, m_ref, acc_ref, *, tq, lam_init, S):
    h = pl.program_id(1)
    qi = pl.program_id(2)
    d = DA_QK_DIM
    tk = tq
    col_groups = [slice(mp * tq, (mp + 1) * tq) for mp in range(2)]

    qz_ref[...] = jnp.zeros_like(qz_ref)
    for hh in range(2):
        @pl.when(h % 2 == hh)
        def _():
            qz_ref[hh * 2 * d:hh * 2 * d + d, 0:tq] = qT_ref[0, 0:d, :]
            qz_ref[hh * 2 * d + d:(hh + 1) * 2 * d, tq:2 * tq] = qT_ref[0, d:2 * d, :]

    def k_tile(j):
        return k_ref[0, pl.ds(pl.multiple_of(j * tk, tk), tk), :]

    def vT_tile(j):
        return vT_ref[0, :, pl.ds(pl.multiple_of(j * tk, tk), tk)]

    def softmax(s_ref, p_ref, j, tail):
        alphas = []
        for cols in col_groups:
            s = s_ref[:, cols]
            if tail:
                kpos = j * tk + lax.broadcasted_iota(jnp.int32, (tk, tq), 0)
                qpos = qi * tq + lax.broadcasted_iota(jnp.int32, (tk, tq), 1)
                s = jnp.where(kpos <= qpos, s, NEG)
            alphas.append(_softmax_cols(s, m_ref, p_ref, cols))
        return alphas

    _pipelined_attention(k_tile, vT_tile, qz_ref, (sa_ref, sb_ref), (pa_ref, pb_ref), m_ref, acc_ref, softmax,
                         first=0, n_plain_pairs=qi // 2, n_tail=2, max_tile=S // tk - 1, col_groups=col_groups)

    lp = lam_ref[...]
    lam = (jnp.exp(jnp.sum(lp[0:1] * lp[1:2], axis=1, keepdims=True))
           - jnp.exp(jnp.sum(lp[2:3] * lp[3:4], axis=1, keepdims=True)) + lam_init)
    o1 = acc_ref[0:V_DIM, 0:tq] / acc_ref[V_DIM:V_DIM + 1, 0:tq]
    o2 = acc_ref[0:V_DIM, tq:2 * tq] / acc_ref[V_DIM:V_DIM + 1, tq:2 * tq]
    o = o1 - lam * o2
    y = o * lax.rsqrt(jnp.mean(o * o, axis=0, keepdims=True) + EPS) * gain_ref[...]
    o_ref[0] = (y * (1.0 - lam_init)).astype(o_ref.dtype)


def _diff_attn(da_lambda, daqT, dak, davT, gain_col, lam_init, tq=256):
    B, _, S = daqT.shape
    return pl.pallas_call(
        functools.partial(_da_kernel, tq=tq, lam_init=lam_init, S=S),
        grid=(B, DA_HEADS, S // tq),
        in_specs=[pl.BlockSpec((4, DA_QK_DIM), lambda b, h, i: (0, 0)),
                  pl.BlockSpec((1, DA_V_DIM, tq), lambda b, h, i: (b, h, i)),
                  pl.BlockSpec((1, S, LANES), lambda b, h, i: (b, 0, h // 2)),
                  pl.BlockSpec((1, V_EXT, S), lambda b, h, i: (b, h, 0)),
                  pl.BlockSpec((DA_V_DIM, 1), lambda b, h, i: (h, 0))],
        out_specs=pl.BlockSpec((1, DA_V_DIM, tq), lambda b, h, i: (b, h, i)),
        out_shape=jax.ShapeDtypeStruct((B, DA_WIDTH, S), BF16),
        scratch_shapes=[pltpu.VMEM((LANES, 2 * tq), BF16),
                        pltpu.VMEM((tq, 2 * tq), F32),
                        pltpu.VMEM((tq, 2 * tq), F32),
                        pltpu.VMEM((tq, 2 * tq), BF16),
                        pltpu.VMEM((tq, 2 * tq), BF16),
                        pltpu.VMEM((1, 2 * tq), F32),
                        pltpu.VMEM((V_EXT, 2 * tq), F32)],
        compiler_params=_cparams(("parallel", "parallel", "arbitrary")),
        name="diff_attn",
    )(da_lambda, daqT, dak, davT, gain_col)


def _gelu_tanh(x):
    return x * (0.5 * (1.0 + jnp.tanh(math.sqrt(2.0 / math.pi) * (x + 0.044715 * (x * x * x)))))


def _compress_kernel(x_ref, pe_ref, w1_ref, w2_ref, o_ref, b_ref, *, feature_major):
    n = x_ref.shape[2]
    half = CMP_STRIDE * NSA_DIM
    x = x_ref[0, 0]
    a = jnp.dot((x + pe_ref[0, 0:1, :]).astype(BF16), w1_ref[0, 0:half, :], preferred_element_type=F32)
    b_ref[0:n, :] = jnp.dot((x + pe_ref[0, 1:2, :]).astype(BF16), w1_ref[0, half:2 * half, :],
                            preferred_element_type=F32)
    b_ref[n:n + 8, :] = jnp.zeros((8, CMP_HIDDEN), F32)
    hid = _gelu_tanh(a + b_ref[1:n + 1, :]).astype(BF16)
    if feature_major:
        o_ref[0, 0] = lax.dot_general(w2_ref[0], hid, _NT, preferred_element_type=F32).astype(o_ref.dtype)
    else:
        o_ref[0, 0] = jnp.dot(hid, w2_ref[0], preferred_element_type=F32).astype(o_ref.dtype)


def _compress(x16, pe2, w1, w2, which, feature_major):
    B, _, n, half = x16.shape
    G = NSA_GROUPS
    if feature_major:
        out_shape, out_block = (B, G, NSA_DIM, n), (1, 1, NSA_DIM, n)
        w2_arr, w2_block = jnp.swapaxes(w2, 1, 2), (1, NSA_DIM, CMP_HIDDEN)
    else:
        out_shape, out_block = (B, G, n, NSA_DIM), (1, 1, n, NSA_DIM)
        w2_arr, w2_block = w2, (1, CMP_HIDDEN, NSA_DIM)
    return pl.pallas_call(
        functools.partial(_compress_kernel, feature_major=feature_major),
        grid=(B, G),
        in_specs=[pl.BlockSpec((1, 1, n, half), lambda b, g: (b, which * G + g, 0, 0)),
                  pl.BlockSpec((1, 2, half), lambda b, g: (which, 0, 0)),
                  pl.BlockSpec((1, 2 * half, CMP_HIDDEN), lambda b, g: (which, 0, 0)),
                  pl.BlockSpec(w2_block, lambda b, g: (which, 0, 0))],
        out_specs=pl.BlockSpec(out_block, lambda b, g: (b, g, 0, 0)),
        out_shape=jax.ShapeDtypeStruct(out_shape, BF16),
        scratch_shapes=[pltpu.VMEM((n + 8, CMP_HIDDEN), F32)],
        compiler_params=_cparams(("parallel", "parallel")),
        name="compress_v" if feature_major else "compress_k",
    )(x16, pe2, w1, w2_arr.astype(BF16))


def _nsa_kernel(qT_ref, gT_ref, kc_ref, vcT_ref, ks_ref, vsT_ref, kw_ref, vwT_ref, o_ref,
                qg_ref, qz_ref, imp_ref, sel_ref, pc_ref, sa_ref, sb_ref, pa_ref, pb_ref, m_ref, acc_ref, out_ref,
                *, tq, tk, S):
    g = pl.program_id(1)
    qi = pl.program_id(2)
    d = NSA_DIM
    R = NSA_REP
    ncb = S // CMP_STRIDE
    nsb = S // SEL_BLOCK
    q0 = qi * tq
    jd = q0 // tk
    ratio = SEL_BLOCK // CMP_STRIDE
    per_tile = tk // SEL_BLOCK

    for r in range(R):
        qg_ref[:, r * tq:(r + 1) * tq] = qT_ref[0, r * d:(r + 1) * d, :]
    qz_ref[...] = jnp.zeros_like(qz_ref)
    for gg in range(NSA_GROUPS):
        @pl.when(g == gg)
        def _():
            qz_ref[gg * d:(gg + 1) * d, :] = qg_ref[...]

    def gate(r, br):
        return _sigmoid(gT_ref[0, r * 3 + br:r * 3 + br + 1, :])

    qpos = q0 + lax.broadcasted_iota(jnp.int32, (1, tq), 1)

    s_all = jnp.dot(kc_ref[0, 0], qg_ref[...], preferred_element_type=F32)
    cend = lax.broadcasted_iota(jnp.int32, (ncb, tq), 0) * CMP_STRIDE + (CMP_BLOCK - 1)
    cvalid = cend <= qpos
    imp = jnp.zeros((ncb, tq), F32)
    for r in range(R):
        cols = slice(r * tq, (r + 1) * tq)
        s = s_all[:, cols]
        mx = jnp.max(jnp.where(cvalid, s, NEG), axis=0, keepdims=True)
        p = jnp.exp2(jnp.where(cvalid, s - mx, NEG))
        den = jnp.sum(p, axis=0, keepdims=True)
        p = p / jnp.where(den > 0, den, 1.0)
        imp = imp + p
        pc_ref[:, cols] = p.astype(BF16)
    o_cmp = jnp.dot(vcT_ref[0, 0], pc_ref[...], preferred_element_type=F32)
    for r in range(R):
        cols = slice(r * tq, (r + 1) * tq)
        out_ref[:, cols] = gate(r, 0) * o_cmp[:, cols]

    imp_ref[0:8, :] = jnp.zeros((8, tq), F32)
    imp_ref[8:8 + ncb, :] = imp
    imp_ref[8 + ncb:16 + ncb, :] = jnp.zeros((8, tq), F32)
    p_slc = jnp.zeros((nsb, tq), F32)
    for o in range(-1, ratio):
        p_slc = p_slc + imp_ref[pl.ds(8 + o, nsb, stride=ratio), :]
    blk = lax.broadcasted_iota(jnp.int32, (nsb, tq), 0)
    cur = lax.shift_right_logical(qpos, int(math.log2(SEL_BLOCK)))
    forced = (blk == 0) | (blk == cur) | (blk == cur - 1)
    causal_blk = blk * SEL_BLOCK <= qpos
    score = jnp.where(forced, FORCE_SCORE, jnp.where(causal_blk, p_slc, -1.0))
    blk_f = blk.astype(F32)
    sel = jnp.zeros((nsb, tq), F32)
    for _ in range(SEL_TOPK):
        mx = jnp.max(score, axis=0, keepdims=True)
        first = jnp.min(jnp.where(score == mx, blk_f, float(nsb)), axis=0, keepdims=True)
        hit = blk_f == first
        sel = jnp.where(hit, 1.0, sel)
        score = jnp.where(hit, -3e38, score)
    sel_ref[...] = jnp.where(sel > 0.5, 0.0, NEG)

    col_groups = [slice(r * tq, (r + 1) * tq) for r in range(R)]
    max_tile = S // tk - 1

    def tiles_of(k_ref_, vT_ref_):
        def k_tile(j):
            return k_ref_[0, pl.ds(pl.multiple_of(j * tk, tk), tk), :]

        def vT_tile(j):
            return vT_ref_[0, :, pl.ds(pl.multiple_of(j * tk, tk), tk)]
        return k_tile, vT_tile

    def softmax_with(bias_of):
        def softmax(s_ref, p_ref, j, tail):
            bias = bias_of(j, tail)
            return [_softmax_cols(s_ref[:, cols] + bias, m_ref, p_ref, cols) for cols in col_groups]
        return softmax

    def flush(br):
        for r, cols in enumerate(col_groups):
            den = acc_ref[V_DIM:V_DIM + 1, cols]
            out_ref[:, cols] += gate(r, br) * (acc_ref[0:V_DIM, cols] / jnp.where(den > 0, den, 1.0))

    def kpos_of(j):
        return j * tk + lax.broadcasted_iota(jnp.int32, (tk, tq), 0)

    def sel_bias(j, tail):
        first_blk = jnp.clip(j, 0, max_tile) * per_tile
        rows = [jnp.broadcast_to(sel_ref[pl.ds(first_blk + i, 1), :], (SEL_BLOCK, tq)) for i in range(per_tile)]
        bias = jnp.concatenate(rows, axis=0)
        if tail:
            bias = jnp.where(kpos_of(j) <= qpos, bias, NEG)
        return bias

    _pipelined_attention(*tiles_of(ks_ref, vsT_ref), qz_ref, (sa_ref, sb_ref), (pa_ref, pb_ref), m_ref, acc_ref,
                         softmax_with(sel_bias), first=0, n_plain_pairs=jd // 2, n_tail=2,
                         max_tile=max_tile, col_groups=col_groups)
    flush(1)

    def win_bias(j, tail):
        kpos = kpos_of(j)
        rel = qpos - kpos
        return jnp.where((rel >= 0) & (rel < WINDOW) & (kpos >= 0), 0.0, NEG)

    n_win = WINDOW // tk + 1
    _pipelined_attention(*tiles_of(kw_ref, vwT_ref), qz_ref, (sa_ref, sb_ref), (pa_ref, pb_ref), m_ref, acc_ref,
                         softmax_with(win_bias), first=jd - (n_win - 1), n_plain_pairs=0, n_tail=n_win,
                         max_tile=max_tile, col_groups=col_groups)
    flush(2)

    for r in range(R):
        o_ref[0, r * d:(r + 1) * d, :] = out_ref[:, r * tq:(r + 1) * tq].astype(o_ref.dtype)


def _nsa(nsqT, nsgT, kcmp, vcmpT, nsk, nsvT, tq=128, tk=256):
    B, _, S = nsqT.shape
    G, R, d = NSA_GROUPS, NSA_REP, NSA_DIM
    ncb = S // CMP_STRIDE
    nsb = S // SEL_BLOCK
    return pl.pallas_call(
        functools.partial(_nsa_kernel, tq=tq, tk=tk, S=S),
        grid=(B, G, S // tq),
        in_specs=[pl.BlockSpec((1, R * d, tq), lambda b, g, i: (b, g, i)),
                  pl.BlockSpec((1, GATE_PAD, tq), lambda b, g, i: (b, g, i)),
                  pl.BlockSpec((1, 1, ncb, d), lambda b, g, i: (b, g, 0, 0)),
                  pl.BlockSpec((1, 1, d, ncb), lambda b, g, i: (b, g, 0, 0)),
                  pl.BlockSpec((1, S, LANES), lambda b, g, i: (b, 0, 0)),
                  pl.BlockSpec((1, V_EXT, S), lambda b, g, i: (b, g, 0)),
                  pl.BlockSpec((1, S, LANES), lambda b, g, i: (b, 0, 1)),
                  pl.BlockSpec((1, V_EXT, S), lambda b, g, i: (b, G + g, 0))],
        out_specs=pl.BlockSpec((1, R * d, tq), lambda b, g, i: (b, g, i)),
        out_shape=jax.ShapeDtypeStruct((B, NSA_WIDTH, S), BF16),
        scratch_shapes=[pltpu.VMEM((d, R * tq), BF16),
                        pltpu.VMEM((LANES, R * tq), BF16),
                        pltpu.VMEM((ncb + 16, tq), F32),
                        pltpu.VMEM((nsb, tq), F32),
                        pltpu.VMEM((ncb, R * tq), BF16),
                        pltpu.VMEM((tk, R * tq), F32),
                        pltpu.VMEM((tk, R * tq), F32),
                        pltpu.VMEM((tk, R * tq), BF16),
                        pltpu.VMEM((tk, R * tq), BF16),
                        pltpu.VMEM((1, R * tq), F32),
                        pltpu.VMEM((V_EXT, R * tq), F32),
                        pltpu.VMEM((d, R * tq), F32)],
        compiler_params=_cparams(("parallel", "parallel", "arbitrary")),
        name="nsa",
    )(nsqT, nsgT, kcmp, vcmpT, nsk, nsvT, nsk, nsvT)


def _outproj_kernel(x_ref, yml_ref, ydaT_ref, ynsT_ref, wo_ref, o_ref):
    acc = x_ref[0] + jnp.dot(yml_ref[0].astype(BF16), wo_ref[0:ML_WIDTH, :], preferred_element_type=F32)
    acc = acc + lax.dot_general(ydaT_ref[0], wo_ref[ML_WIDTH:ML_WIDTH + DA_WIDTH, :], _TN,
                                preferred_element_type=F32)
    acc = acc + lax.dot_general(ynsT_ref[0], wo_ref[ML_WIDTH + DA_WIDTH:, :], _TN,
                                preferred_element_type=F32)
    o_ref[0] = acc


def _outproj(x, yml, ydaT, ynsT, wo, tm=512):
    B, S, D = x.shape
    return pl.pallas_call(
        _outproj_kernel,
        grid=(B, S // tm),
        in_specs=[pl.BlockSpec((1, tm, D), lambda b, i: (b, i, 0)),
                  pl.BlockSpec((1, tm, ML_WIDTH), lambda b, i: (b, i, 0)),
                  pl.BlockSpec((1, DA_WIDTH, tm), lambda b, i: (b, 0, i)),
                  pl.BlockSpec((1, NSA_WIDTH, tm), lambda b, i: (b, 0, i)),
                  pl.BlockSpec((D, D), lambda b, i: (0, 0))],
        out_specs=pl.BlockSpec((1, tm, D), lambda b, i: (b, i, 0)),
        out_shape=jax.ShapeDtypeStruct((B, S, D), F32),
        compiler_params=_cparams(("parallel", "parallel")),
        name="outproj",
    )(x, yml, ydaT, ynsT, wo)


def _ffn_kernel(x_ref, g_ref, w1_ref, w2_ref, gf_ref, o_ref, hb_ref, acc_ref, *, final):
    j = pl.program_id(1)

    @pl.when(j == 0)
    def _():
        x = x_ref[...]
        hb_ref[...] = _rms(x, g_ref[...]).astype(BF16)
        acc_ref[...] = x

    u = jnp.dot(hb_ref[...], w1_ref[...], preferred_element_type=F32)
    a = jnp.square(jnp.maximum(u, 0.0)).astype(BF16)
    acc_ref[...] += jnp.dot(a, w2_ref[...], preferred_element_type=F32)

    @pl.when(j == pl.num_programs(1) - 1)
    def _():
        y = acc_ref[...]
        if final:
            y = _rms(y, gf_ref[...])
        o_ref[...] = y


def _ffn(x2d, g, w1, w2, gf, final, tm=1024, tf=512):
    N, D = x2d.shape
    return pl.pallas_call(
        functools.partial(_ffn_kernel, final=final),
        grid=(N // tm, D_FF // tf),
        in_specs=[pl.BlockSpec((tm, D), lambda i, j: (i, 0)),
                  pl.BlockSpec((1, D), lambda i, j: (0, 0)),
                  pl.BlockSpec((D, tf), lambda i, j: (0, j)),
                  pl.BlockSpec((tf, D), lambda i, j: (j, 0)),
                  pl.BlockSpec((1, D), lambda i, j: (0, 0))],
        out_specs=pl.BlockSpec((tm, D), lambda i, j: (i, 0)),
        out_shape=jax.ShapeDtypeStruct((N, D), F32),
        scratch_shapes=[pltpu.VMEM((tm, D), BF16), pltpu.VMEM((tm, D), F32)],
        compiler_params=_cparams(("parallel", "arbitrary")),
        name="ffn",
    )(x2d, g, w1, w2, gf)


def _split_w_in(w_in_l):
    edges = np.concatenate([[0], np.cumsum(IN_SIZES)])
    return [w_in_l[:, int(edges[i]):int(edges[i + 1])] for i in range(len(IN_SIZES))]


def _pad_cols(w, n):
    return jnp.pad(w, ((0, 0), (0, n - w.shape[1])))


def _inproj_weights(w_in_l):
    (ml_q, ml_k, ml_v, ml_o, ml_i, ml_f, da_q, da_k, da_v,
     ns_q, ns_kc, ns_vc, ns_ks, ns_vs, ns_kw, ns_vw, ns_g) = _split_w_in(w_in_l)
    wt = jnp.concatenate([ml_q, ml_k, ml_v, ml_o, _pad_cols(jnp.concatenate([ml_i, ml_f], 1), LANES),
                          da_k, ns_ks, ns_kw, ns_kc, ns_vc], axis=1)
    per_group = NSA_REP * 3
    ns_g_pad = jnp.concatenate(
        [_pad_cols(ns_g[:, gi * per_group:(gi + 1) * per_group], GATE_PAD) for gi in range(NSA_GROUPS)], axis=1)
    wf = jnp.concatenate([da_q, da_v, ns_q, ns_vs, ns_vw, ns_g_pad], axis=1)
    return wt.astype(BF16), wf.T.astype(BF16)


def _half_blocks(nsc):
    B, S, _ = nsc.shape
    n = S // CMP_STRIDE
    t = nsc.reshape(B, n, CMP_STRIDE, 2 * NSA_GROUPS, NSA_DIM)
    return t.transpose(0, 3, 1, 2, 4).reshape(B, 2 * NSA_GROUPS, n, CMP_STRIDE * NSA_DIM)


def kernel(x, norm1, w_in, ml_conv, ml_gate_bias, ml_norm, da_lambda, da_norm, nsa_pe,
           nsa_w1, nsa_w2, w_out, norm2, w_ff1, w_ff2, final_norm):
    B, S, D = x.shape
    depth = norm1.shape[0]
    for l in range(depth):
        wt, wfT = _inproj_weights(w_in[l])
        (zml, zg, dak, nsk, nsc, daqT, davT, nsqT, nsvT, nsgT) = _inproj(x, norm1[l][None, :], wt, wfT)

        yml = _mlstm(zml, zg, ml_conv[l], _pad_cols(ml_gate_bias[l][None, :], LANES), ml_norm[l][None, :])

        lam_init = 0.8 - 0.6 * math.exp(-0.3 * l)
        ydaT = _diff_attn(da_lambda[l], daqT, dak, davT, da_norm[l][:, None], lam_init)

        x16 = _half_blocks(nsc)
        pe2 = nsa_pe[l].reshape(2, 2, CMP_STRIDE * NSA_DIM)
        w1b = nsa_w1[l].astype(BF16)
        kcmp = _compress(x16, pe2, w1b, nsa_w2[l], 0, False)
        vcmpT = _compress(x16, pe2, w1b, nsa_w2[l], 1, True)
        ynsT = _nsa(nsqT, nsgT, kcmp, vcmpT, nsk, nsvT)

        x = _outproj(x, yml, ydaT, ynsT, w_out[l].astype(BF16))
        x = _ffn(x.reshape(B * S, D), norm2[l][None, :], w_ff1[l].astype(BF16), w_ff2[l].astype(BF16),
                 final_norm[None, :], final=(l == depth - 1)).reshape(B, S, D)
    return x
```

```python
import functools
import math

import numpy as np
import jax
import jax.numpy as jnp
from jax import lax
from jax.experimental import pallas as pl
from jax.experimental.pallas import tpu as pltpu

F32 = jnp.float32
BF16 = jnp.bfloat16

D_MODEL = 1024
ML_HEADS = 4
ML_DIM = 64
ML_WIDTH = ML_HEADS * ML_DIM
ML_CHUNK = 64
CONV_W = 4
DA_HEADS = 4
DA_QK_DIM = 32
DA_V_DIM = 64
DA_WIDTH = DA_HEADS * DA_V_DIM
NSA_HEADS = 8
NSA_GROUPS = 2
NSA_REP = NSA_HEADS // NSA_GROUPS
NSA_DIM = 64
NSA_WIDTH = NSA_HEADS * NSA_DIM
NSA_KV = NSA_GROUPS * NSA_DIM
CMP_BLOCK = 32
CMP_STRIDE = 16
CMP_HIDDEN = 4 * NSA_DIM
SEL_BLOCK = 64
SEL_TOPK = 16
WINDOW = 512
D_FF = 4 * D_MODEL
EPS = 1e-6
FORCE_SCORE = 1e4
IN_SIZES = (ML_WIDTH, ML_WIDTH, ML_WIDTH, ML_WIDTH, ML_HEADS, ML_HEADS,
            2 * DA_HEADS * DA_QK_DIM, 2 * DA_HEADS * DA_QK_DIM, DA_WIDTH,
            NSA_WIDTH, NSA_KV, NSA_KV, NSA_KV, NSA_KV, NSA_KV, NSA_KV, 3 * NSA_HEADS)

LANES = 128
GATE_PAD = 16
NEG = -1e30
M_INIT = -1e29
TAKEN = -3e38
LOG2E = 1.4426950408889634
V_DIM = 64
V_EXT = 80
VMEM_LIMIT = 56 * 1024 * 1024

_T_ML = (0, 1024)
_T_MLG = (1024, 1152)
_T_DAK = (1152, 1408)
_T_NSK = (1408, 1664)
_T_NSC = (1664, 1920)
_T_COLS = 1920
_F_DAQ = (0, 256)
_F_DAV = (256, 512)
_F_NSQ = (512, 1024)
_F_NSV = (1024, 1280)
_F_NSG = (1280, 1280 + NSA_GROUPS * GATE_PAD)
_F_ROWS = _F_NSG[1]

_NT = (((1,), (1,)), ((), ()))
_TN = (((0,), (0,)), ((), ()))


def _cparams(sem):
    return pltpu.CompilerParams(dimension_semantics=sem, vmem_limit_bytes=VMEM_LIMIT)


def _rms(x, g):
    return x * lax.rsqrt(jnp.mean(x * x, axis=-1, keepdims=True) + EPS) * g


def _inproj_kernel(x_ref, g_ref, wt_ref, wf_ref,
                   zml_ref, zg_ref, dak_ref, nsk_ref, nsc_ref,
                   daqT_ref, davT_ref, nsqT_ref, nsvT_ref, nsgT_ref):
    hb = _rms(x_ref[0], g_ref[...]).astype(BF16)

    def tdot(span):
        return jnp.dot(hb, wt_ref[:, span[0]:span[1]], preferred_element_type=F32)

    def fdot(span):
        return lax.dot_general(wf_ref[span[0]:span[1], :], hb, _NT, preferred_element_type=F32)

    zml_ref[0] = tdot(_T_ML)
    zg_ref[0] = tdot(_T_MLG)
    dak_ref[0] = tdot(_T_DAK).astype(BF16)
    nsk_ref[0] = tdot(_T_NSK).astype(BF16)
    nsc_ref[0] = tdot(_T_NSC)
    daqT_ref[0] = (fdot(_F_DAQ) * (DA_QK_DIM ** -0.5 * LOG2E)).astype(BF16)
    nsqT_ref[0] = (fdot(_F_NSQ) * (NSA_DIM ** -0.5 * LOG2E)).astype(BF16)
    nsgT_ref[0] = fdot(_F_NSG)
    tm = hb.shape[0]
    for v_ref, span in ((davT_ref, _F_DAV), (nsvT_ref, _F_NSV)):
        v = fdot(span).astype(BF16)
        for h in range((span[1] - span[0]) // V_DIM):
            v_ref[0, h * V_EXT:h * V_EXT + V_DIM, :] = v[h * V_DIM:(h + 1) * V_DIM, :]
            v_ref[0, h * V_EXT + V_DIM:(h + 1) * V_EXT, :] = jnp.ones((V_EXT - V_DIM, tm), BF16)


def _inproj(x, g, wt, wfT, tm=512):
    B, S, D = x.shape
    tok = lambda w, dt: jax.ShapeDtypeStruct((B, S, w), dt)
    feat = lambda r, dt: jax.ShapeDtypeStruct((B, r, S), dt)
    tspec = lambda w: pl.BlockSpec((1, tm, w), lambda b, i: (b, i, 0))
    fspec = lambda r: pl.BlockSpec((1, r, tm), lambda b, i: (b, 0, i))
    return pl.pallas_call(
        _inproj_kernel,
        grid=(B, S // tm),
        in_specs=[tspec(D),
                  pl.BlockSpec((1, D), lambda b, i: (0, 0)),
                  pl.BlockSpec((D, _T_COLS), lambda b, i: (0, 0)),
                  pl.BlockSpec((_F_ROWS, D), lambda b, i: (0, 0))],
        out_specs=[tspec(1024), tspec(128), tspec(256), tspec(256), tspec(256),
                   fspec(256), fspec(4 * V_EXT), fspec(512), fspec(4 * V_EXT), fspec(NSA_GROUPS * GATE_PAD)],
        out_shape=[tok(1024, F32), tok(128, F32), tok(256, BF16), tok(256, BF16), tok(256, F32),
                   feat(256, BF16), feat(4 * V_EXT, BF16), feat(512, BF16), feat(4 * V_EXT, BF16),
                   feat(NSA_GROUPS * GATE_PAD, F32)],
        compiler_params=_cparams(("parallel", "parallel")),
        name="inproj",
    )(x, g, wt, wfT)


def _log_sigmoid(x):
    return jnp.minimum(x, 0.0) - jnp.log1p(jnp.exp(-jnp.abs(x)))


def _sigmoid(x):
    return 1.0 / (1.0 + jnp.exp(-x))


def _mlstm_kernel(zml_ref, zg_ref, cw_ref, gb_ref, nrm_ref, y_ref,
                  pad_ref, c_ref, n_ref, m_ref, *, T):
    L = ML_CHUNK
    d = ML_DIM
    t = pl.program_id(1)

    @pl.when(t == 0)
    def _():
        pad_ref[0:8, :] = jnp.zeros((8, 2 * ML_WIDTH), F32)
        c_ref[...] = jnp.zeros_like(c_ref)
        n_ref[...] = jnp.zeros_like(n_ref)
        m_ref[...] = jnp.zeros_like(m_ref)

    @pl.when(t > 0)
    def _():
        pad_ref[0:8, :] = pad_ref[T:T + 8, :]

    pad_ref[8:8 + T, :] = zml_ref[0, :, 0:2 * ML_WIDTH]
    conv = cw_ref[0:1, :] * pad_ref[5:5 + T, :]
    for j in range(1, CONV_W):
        conv = conv + cw_ref[j:j + 1, :] * pad_ref[5 + j:5 + j + T, :]
    qk = conv * _sigmoid(conv)
    gates = zg_ref[0] + gb_ref[...]
    logf = _log_sigmoid(gates)

    row = lax.broadcasted_iota(jnp.int32, (L, L), 0)
    col = lax.broadcasted_iota(jnp.int32, (L, L), 1)
    causal = col <= row
    tril = causal.astype(F32)
    triu = (row <= col).astype(F32)

    for c in range(T // L):
        r0 = c * L
        g_c = gates[r0:r0 + L, :]
        lf_c = logf[r0:r0 + L, :]
        b_cols = jnp.dot(tril, lf_c, preferred_element_type=F32,
                         precision=lax.Precision.HIGHEST)
        b_rows = jnp.dot(lf_c.T, triu, preferred_element_type=F32,
                         precision=lax.Precision.HIGHEST)
        g_rows = g_c.T
        for h in range(ML_HEADS):
            b_col = b_cols[:, ML_HEADS + h:ML_HEADS + h + 1]
            b_row = b_rows[ML_HEADS + h:ML_HEADS + h + 1, :]
            ig_col = g_c[:, h:h + 1]
            ig_row = g_rows[h:h + 1, :]
            qh = qk[r0:r0 + L, h * d:(h + 1) * d]
            kh = qk[r0:r0 + L, ML_WIDTH + h * d:ML_WIDTH + (h + 1) * d] * (d ** -0.5)
            vh = zml_ref[0, r0:r0 + L, 2 * ML_WIDTH + h * d:2 * ML_WIDTH + (h + 1) * d]
            oh = zml_ref[0, r0:r0 + L, 3 * ML_WIDTH + h * d:3 * ML_WIDTH + (h + 1) * d]
            c_prev = c_ref[h]
            n_prev = n_ref[h, 0:1, :]
            m_prev = m_ref[h, 0:1, 0:1]

            g_tot = b_col[L - 1:L, :]
            a_row = g_tot - b_row + ig_row
            a_max = jnp.max(a_row, axis=1, keepdims=True)
            w_col = jnp.exp(g_tot - b_col + ig_col - a_max)
            c_loc = lax.dot_general(vh * w_col, kh, _TN, preferred_element_type=F32)
            n_loc = jnp.sum(kh * w_col, axis=0, keepdims=True)

            dmat = jnp.where(causal, b_col - b_row + ig_row, NEG)
            inter_log = b_col + m_prev
            m_t = jnp.maximum(inter_log, jnp.max(dmat, axis=1, keepdims=True))
            qkt = lax.dot_general(qh, kh, _NT, preferred_element_type=F32)
            wts = jnp.exp(dmat - m_t) * qkt
            s_inter = jnp.exp(inter_log - m_t)
            num = (jnp.dot(wts, vh, preferred_element_type=F32)
                   + s_inter * lax.dot_general(qh, c_prev, _NT, preferred_element_type=F32))
            den = (jnp.sum(wts, axis=1, keepdims=True)
                   + s_inter * jnp.sum(qh * n_prev, axis=1, keepdims=True))
            hh = num / jnp.maximum(jnp.abs(den), jnp.exp(-m_t))

            m_new = jnp.maximum(g_tot + m_prev, a_max)
            s_prev = jnp.exp(g_tot + m_prev - m_new)
            s_loc = jnp.exp(a_max - m_new)
            c_ref[h] = s_prev * c_prev + s_loc * c_loc
            n_ref[h] = jnp.broadcast_to(s_prev * n_prev + s_loc * n_loc, (8, d))
            m_ref[h] = jnp.broadcast_to(m_new, (8, LANES))

            yh = _sigmoid(oh) * hh
            y_ref[0, r0:r0 + L, h * d:(h + 1) * d] = _rms(yh, nrm_ref[0:1, h * d:(h + 1) * d])


def _mlstm(zml, zg, conv_w, gate_bias, norm_g, T=256):
    B, S, _ = zml.shape
    return pl.pallas_call(
        functools.partial(_mlstm_kernel, T=T),
        grid=(B, S // T),
        in_specs=[pl.BlockSpec((1, T, 1024), lambda b, t: (b, t, 0)),
                  pl.BlockSpec((1, T, 128), lambda b, t: (b, t, 0)),
                  pl.BlockSpec((CONV_W, 2 * ML_WIDTH), lambda b, t: (0, 0)),
                  pl.BlockSpec((1, 128), lambda b, t: (0, 0)),
                  pl.BlockSpec((1, ML_WIDTH), lambda b, t: (0, 0))],
        out_specs=pl.BlockSpec((1, T, ML_WIDTH), lambda b, t: (b, t, 0)),
        out_shape=jax.ShapeDtypeStruct((B, S, ML_WIDTH), F32),
        scratch_shapes=[pltpu.VMEM((T + 8, 2 * ML_WIDTH), F32),
                        pltpu.VMEM((ML_HEADS, ML_DIM, ML_DIM), F32),
                        pltpu.VMEM((ML_HEADS, 8, ML_DIM), F32),
                        pltpu.VMEM((ML_HEADS, 8, LANES), F32)],
        compiler_params=_cparams(("parallel", "arbitrary")),
        name="mlstm",
    )(zml, zg, conv_w, gate_bias, norm_g)


def _softmax_cols(s, m_ref, p_ref, cols):
    m_old = m_ref[:, cols]
    m_new = jnp.maximum(m_old, jnp.max(s, axis=0, keepdims=True))
    m_ref[:, cols] = m_new
    p_ref[:, cols] = jnp.exp2(s - m_new).astype(BF16)
    return jnp.exp2(m_old - m_new)


def _pipelined_attention(k_tile, vT_tile, qz_ref, s_refs, p_refs, m_ref, acc_ref, softmax,
                         first, n_plain_pairs, n_tail, max_tile, col_groups):
    sa, sb = s_refs
    pa, pb = p_refs

    def load(j):
        return jnp.clip(j, 0, max_tile)

    def half(j, s_cur, s_nxt, p_cur, p_prev, tail):
        s_nxt[...] = jnp.dot(k_tile(load(j + 1)), qz_ref[...], preferred_element_type=F32)
        pv = jnp.dot(vT_tile(load(j - 1)), p_prev[...], preferred_element_type=F32)
        alphas = softmax(s_cur, p_cur, j, tail)
        for cols, alpha in zip(col_groups, alphas):
            acc_ref[:, cols] = alpha * (acc_ref[:, cols] + pv[:, cols])

    def pair(j, tail):
        half(j, sa, sb, pa, pb, tail)
        half(j + 1, sb, sa, pb, pa, tail)

    m_ref[...] = jnp.full_like(m_ref, M_INIT)
    acc_ref[...] = jnp.zeros_like(acc_ref)
    pb[...] = jnp.zeros_like(pb)
    sa[...] = jnp.dot(k_tile(load(first)), qz_ref[...], preferred_element_type=F32)

    def body(i, carry):
        pair(first + 2 * i, False)
        return carry

    lax.fori_loop(0, n_plain_pairs, body, 0)
    j = first + 2 * n_plain_pairs
    bufs = ((sa, sb, pa, pb), (sb, sa, pb, pa))
    for t in range(n_tail):
        half(j + t, *bufs[t % 2], True)
    p_last = bufs[(n_tail - 1) % 2][2]
    acc_ref[...] += jnp.dot(vT_tile(load(j + n_tail - 1)), p_last[...], preferred_element_type=F32)


def _da_kernel(lam_ref, qT_ref, k_ref, vT_ref, gain_ref, o_ref,
               qz_ref, sa_ref, sb_ref, pa_ref, pb_ref, m_ref, acc_ref, *, tq, tk, lam_init, S):
    h = pl.program_id(1)
    qi = pl.program_id(2)
    d = DA_QK_DIM
    jd = (qi * tq) // tk
    col_groups = [slice(mp * tq, (mp + 1) * tq) for mp in range(2)]

    qz_ref[...] = jnp.zeros_like(qz_ref)
    for hh in range(2):
        @pl.when(h % 2 == hh)
        def _():
            qz_ref[hh * 2 * d:hh * 2 * d + d, 0:tq] = qT_ref[0, 0:d, :]
            qz_ref[hh * 2 * d + d:(hh + 1) * 2 * d, tq:2 * tq] = qT_ref[0, d:2 * d, :]

    def k_tile(j):
        return k_ref[0, pl.ds(pl.multiple_of(j * tk, tk), tk), :]

    def vT_tile(j):
        return vT_ref[0, :, pl.ds(pl.multiple_of(j * tk, tk), tk)]

    def softmax(s_ref, p_ref, j, tail):
        alphas = []
        for cols in col_groups:
            s = s_ref[:, cols]
            if tail:
                kpos = j * tk + lax.broadcasted_iota(jnp.int32, (tk, tq), 0)
                qpos = qi * tq + lax.broadcasted_iota(jnp.int32, (tk, tq), 1)
                s = jnp.where(kpos <= qpos, s, NEG)
            alphas.append(_softmax_cols(s, m_ref, p_ref, cols))
        return alphas

    _pipelined_attention(k_tile, vT_tile, qz_ref, (sa_ref, sb_ref), (pa_ref, pb_ref), m_ref, acc_ref, softmax,
                         first=0, n_plain_pairs=jd // 2, n_tail=2, max_tile=S // tk - 1, col_groups=col_groups)

    lp = lam_ref[...]
    lam = (jnp.exp(jnp.sum(lp[0:1] * lp[1:2], axis=1, keepdims=True))
           - jnp.exp(jnp.sum(lp[2:3] * lp[3:4], axis=1, keepdims=True)) + lam_init)
    o1 = acc_ref[0:V_DIM, 0:tq] / acc_ref[V_DIM:V_DIM + 1, 0:tq]
    o2 = acc_ref[0:V_DIM, tq:2 * tq] / acc_ref[V_DIM:V_DIM + 1, tq:2 * tq]
    o = o1 - lam * o2
    y = o * lax.rsqrt(jnp.mean(o * o, axis=0, keepdims=True) + EPS) * gain_ref[...]
    o_ref[0] = (y * (1.0 - lam_init)).astype(o_ref.dtype)


def _diff_attn(da_lambda, daqT, dak, davT, gain_col, lam_init, tq=256, tk=512):
    B, _, S = daqT.shape
    return pl.pallas_call(
        functools.partial(_da_kernel, tq=tq, tk=tk, lam_init=lam_init, S=S),
        grid=(B, DA_HEADS, S // tq),
        in_specs=[pl.BlockSpec((4, DA_QK_DIM), lambda b, h, i: (0, 0)),
                  pl.BlockSpec((1, DA_V_DIM, tq), lambda b, h, i: (b, h, i)),
                  pl.BlockSpec((1, S, LANES), lambda b, h, i: (b, 0, h // 2)),
                  pl.BlockSpec((1, V_EXT, S), lambda b, h, i: (b, h, 0)),
                  pl.BlockSpec((DA_V_DIM, 1), lambda b, h, i: (h, 0))],
        out_specs=pl.BlockSpec((1, DA_V_DIM, tq), lambda b, h, i: (b, h, i)),
        out_shape=jax.ShapeDtypeStruct((B, DA_WIDTH, S), BF16),
        scratch_shapes=[pltpu.VMEM((LANES, 2 * tq), BF16),
                        pltpu.VMEM((tk, 2 * tq), F32),
                        pltpu.VMEM((tk, 2 * tq), F32),
                        pltpu.VMEM((tk, 2 * tq), BF16),
                        pltpu.VMEM((tk, 2 * tq), BF16),
                        pltpu.VMEM((1, 2 * tq), F32),
                        pltpu.VMEM((V_EXT, 2 * tq), F32)],
        compiler_params=_cparams(("parallel", "parallel", "arbitrary")),
        name="diff_attn",
    )(da_lambda, daqT, dak, davT, gain_col)


def _gelu_tanh(x):
    return x * (0.5 * (1.0 + jnp.tanh(math.sqrt(2.0 / math.pi) * (x + 0.044715 * (x * x * x)))))


def _compress_kernel(x_ref, pe_ref, w1_ref, w2_ref, o_ref, b_ref, *, feature_major):
    n = x_ref.shape[2]
    half = CMP_STRIDE * NSA_DIM
    x = x_ref[0, 0]
    a = jnp.dot((x + pe_ref[0, 0:1, :]).astype(BF16), w1_ref[0, 0:half, :], preferred_element_type=F32)
    b_ref[0:n, :] = jnp.dot((x + pe_ref[0, 1:2, :]).astype(BF16), w1_ref[0, half:2 * half, :],
                            preferred_element_type=F32)
    b_ref[n:n + 8, :] = jnp.zeros((8, CMP_HIDDEN), F32)
    hid = _gelu_tanh(a + b_ref[1:n + 1, :]).astype(BF16)
    if feature_major:
        o_ref[0, 0] = lax.dot_general(w2_ref[0], hid, _NT, preferred_element_type=F32).astype(o_ref.dtype)
    else:
        o_ref[0, 0] = jnp.dot(hid, w2_ref[0], preferred_element_type=F32).astype(o_ref.dtype)


def _compress(x16, pe2, w1, w2, which, feature_major):
    B, _, n, half = x16.shape
    G = NSA_GROUPS
    if feature_major:
        out_shape, out_block = (B, G, NSA_DIM, n), (1, 1, NSA_DIM, n)
        w2_arr, w2_block = jnp.swapaxes(w2, 1, 2), (1, NSA_DIM, CMP_HIDDEN)
    else:
        out_shape, out_block = (B, G, n, NSA_DIM), (1, 1, n, NSA_DIM)
        w2_arr, w2_block = w2, (1, CMP_HIDDEN, NSA_DIM)
    return pl.pallas_call(
        functools.partial(_compress_kernel, feature_major=feature_major),
        grid=(B, G),
        in_specs=[pl.BlockSpec((1, 1, n, half), lambda b, g: (b, which * G + g, 0, 0)),
                  pl.BlockSpec((1, 2, half), lambda b, g: (which, 0, 0)),
                  pl.BlockSpec((1, 2 * half, CMP_HIDDEN), lambda b, g: (which, 0, 0)),
                  pl.BlockSpec(w2_block, lambda b, g: (which, 0, 0))],
        out_specs=pl.BlockSpec(out_block, lambda b, g: (b, g, 0, 0)),
        out_shape=jax.ShapeDtypeStruct(out_shape, BF16),
        scratch_shapes=[pltpu.VMEM((n + 8, CMP_HIDDEN), F32)],
        compiler_params=_cparams(("parallel", "parallel")),
        name="compress_v" if feature_major else "compress_k",
    )(x16, pe2, w1, w2_arr.astype(BF16))


def _nsa_kernel(qT_ref, gT_ref, kc_ref, vcT_ref, ks_ref, vsT_ref, kw_ref, vwT_ref, o_ref,
                qg_ref, qz_ref, imp_ref, sel_ref, pc_ref, sa_ref, sb_ref, pa_ref, pb_ref, m_ref, acc_ref, out_ref,
                *, tq, tk, S):
    g = pl.program_id(1)
    qi = pl.program_id(2)
    d = NSA_DIM
    R = NSA_REP
    ncb = S // CMP_STRIDE
    nsb = S // SEL_BLOCK
    q0 = qi * tq
    jd = q0 // tk
    ratio = SEL_BLOCK // CMP_STRIDE
    per_tile = tk // SEL_BLOCK

    for r in range(R):
        qg_ref[:, r * tq:(r + 1) * tq] = qT_ref[0, r * d:(r + 1) * d, :]
    qz_ref[...] = jnp.zeros_like(qz_ref)
    for gg in range(NSA_GROUPS):
        @pl.when(g == gg)
        def _():
            qz_ref[gg * d:(gg + 1) * d, :] = qg_ref[...]

    def gate(r, br):
        return _sigmoid(gT_ref[0, r * 3 + br:r * 3 + br + 1, :])

    qpos = q0 + lax.broadcasted_iota(jnp.int32, (1, tq), 1)

    s_all = jnp.dot(kc_ref[0, 0], qg_ref[...], preferred_element_type=F32)
    cend = lax.broadcasted_iota(jnp.int32, (ncb, tq), 0) * CMP_STRIDE + (CMP_BLOCK - 1)
    cvalid = cend <= qpos
    imp = jnp.zeros((ncb, tq), F32)
    for r in range(R):
        cols = slice(r * tq, (r + 1) * tq)
        s = s_all[:, cols]
        mx = jnp.max(jnp.where(cvalid, s, NEG), axis=0, keepdims=True)
        p = jnp.exp2(jnp.where(cvalid, s - mx, NEG))
        den = jnp.sum(p, axis=0, keepdims=True)
        p = p / jnp.where(den > 0, den, 1.0)
        imp = imp + p
        pc_ref[:, cols] = p.astype(BF16)
    o_cmp = jnp.dot(vcT_ref[0, 0], pc_ref[...], preferred_element_type=F32)
    for r in range(R):
        cols = slice(r * tq, (r + 1) * tq)
        out_ref[:, cols] = gate(r, 0) * o_cmp[:, cols]

    imp_ref[0:8, :] = jnp.zeros((8, tq), F32)
    imp_ref[8:8 + ncb, :] = imp
    imp_ref[8 + ncb:16 + ncb, :] = jnp.zeros((8, tq), F32)
    p_slc = jnp.zeros((nsb, tq), F32)
    for o in range(-1, ratio):
        p_slc = p_slc + imp_ref[pl.ds(8 + o, nsb, stride=ratio), :]
    blk = lax.broadcasted_iota(jnp.int32, (nsb, tq), 0)
    cur = lax.shift_right_logical(qpos, int(math.log2(SEL_BLOCK)))
    forced = (blk == 0) | (blk == cur) | (blk == cur - 1)
    causal_blk = blk * SEL_BLOCK <= qpos
    score = jnp.where(forced, FORCE_SCORE, jnp.where(causal_blk, p_slc, -1.0))
    blk_f = blk.astype(F32)
    for _ in range(SEL_TOPK):
        mx = jnp.max(score, axis=0, keepdims=True)
        first = jnp.min(jnp.where(score == mx, blk_f, float(nsb)), axis=0, keepdims=True)
        score = jnp.where(blk_f == first, TAKEN, score)
    sel_ref[...] = jnp.where(score == TAKEN, 0.0, NEG)

    col_groups = [slice(r * tq, (r + 1) * tq) for r in range(R)]
    max_tile = S // tk - 1

    def tiles_of(k_ref_, vT_ref_):
        def k_tile(j):
            return k_ref_[0, pl.ds(pl.multiple_of(j * tk, tk), tk), :]

        def vT_tile(j):
            return vT_ref_[0, :, pl.ds(pl.multiple_of(j * tk, tk), tk)]
        return k_tile, vT_tile

    def softmax_with(bias_of):
        def softmax(s_ref, p_ref, j, tail):
            bias = bias_of(j, tail)
            return [_softmax_cols(s_ref[:, cols] + bias, m_ref, p_ref, cols) for cols in col_groups]
        return softmax

    def flush(br):
        for r, cols in enumerate(col_groups):
            den = acc_ref[V_DIM:V_DIM + 1, cols]
            out_ref[:, cols] += gate(r, br) * (acc_ref[0:V_DIM, cols] / jnp.where(den > 0, den, 1.0))

    def kpos_of(j):
        return j * tk + lax.broadcasted_iota(jnp.int32, (tk, tq), 0)

    def sel_bias(j, tail):
        first_blk = jnp.clip(j, 0, max_tile) * per_tile
        rows = [jnp.broadcast_to(sel_ref[pl.ds(first_blk + i, 1), :], (SEL_BLOCK, tq)) for i in range(per_tile)]
        bias = jnp.concatenate(rows, axis=0)
        if tail:
            bias = jnp.where(kpos_of(j) <= qpos, bias, NEG)
        return bias

    _pipelined_attention(*tiles_of(ks_ref, vsT_ref), qz_ref, (sa_ref, sb_ref), (pa_ref, pb_ref), m_ref, acc_ref,
                         softmax_with(sel_bias), first=0, n_plain_pairs=jd // 2, n_tail=2,
                         max_tile=max_tile, col_groups=col_groups)
    flush(1)

    def win_bias(j, tail):
        kpos = kpos_of(j)
        rel = qpos - kpos
        return jnp.where((rel >= 0) & (rel < WINDOW) & (kpos >= 0), 0.0, NEG)

    n_win = WINDOW // tk + 1
    _pipelined_attention(*tiles_of(kw_ref, vwT_ref), qz_ref, (sa_ref, sb_ref), (pa_ref, pb_ref), m_ref, acc_ref,
                         softmax_with(win_bias), first=jd - (n_win - 1), n_plain_pairs=0, n_tail=n_win,
                         max_tile=max_tile, col_groups=col_groups)
    flush(2)

    for r in range(R):
        o_ref[0, r * d:(r + 1) * d, :] = out_ref[:, r * tq:(r + 1) * tq].astype(o_ref.dtype)


def _nsa(nsqT, nsgT, kcmp, vcmpT, nsk, nsvT, tq=128, tk=512):
    B, _, S = nsqT.shape
    G, R, d = NSA_GROUPS, NSA_REP, NSA_DIM
    ncb = S // CMP_STRIDE
    nsb = S // SEL_BLOCK
    return pl.pallas_call(
        functools.partial(_nsa_kernel, tq=tq, tk=tk, S=S),
        grid=(B, G, S // tq),
        in_specs=[pl.BlockSpec((1, R * d, tq), lambda b, g, i: (b, g, i)),
                  pl.BlockSpec((1, GATE_PAD, tq), lambda b, g, i: (b, g, i)),
                  pl.BlockSpec((1, 1, ncb, d), lambda b, g, i: (b, g, 0, 0)),
                  pl.BlockSpec((1, 1, d, ncb), lambda b, g, i: (b, g, 0, 0)),
                  pl.BlockSpec((1, S, LANES), lambda b, g, i: (b, 0, 0)),
                  pl.BlockSpec((1, V_EXT, S), lambda b, g, i: (b, g, 0)),
                  pl.BlockSpec((1, S, LANES), lambda b, g, i: (b, 0, 1)),
                  pl.BlockSpec((1, V_EXT, S), lambda b, g, i: (b, G + g, 0))],
        out_specs=pl.BlockSpec((1, R * d, tq), lambda b, g, i: (b, g, i)),
        out_shape=jax.ShapeDtypeStruct((B, NSA_WIDTH, S), BF16),
        scratch_shapes=[pltpu.VMEM((d, R * tq), BF16),
                        pltpu.VMEM((LANES, R * tq), BF16),
                        pltpu.VMEM((ncb + 16, tq), F32),
                        pltpu.VMEM((nsb, tq), F32),
                        pltpu.VMEM((ncb, R * tq), BF16),
                        pltpu.VMEM((tk, R * tq), F32),
                        pltpu.VMEM((tk, R * tq), F32),
                        pltpu.VMEM((tk, R * tq), BF16),
                        pltpu.VMEM((tk, R * tq), BF16),
                        pltpu.VMEM((1, R * tq), F32),
                        pltpu.VMEM((V_EXT, R * tq), F32),
                        pltpu.VMEM((d, R * tq), F32)],
        compiler_params=_cparams(("parallel", "parallel", "arbitrary")),
        name="nsa",
    )(nsqT, nsgT, kcmp, vcmpT, nsk, nsvT, nsk, nsvT)


def _outproj_kernel(x_ref, yml_ref, ydaT_ref, ynsT_ref, wo_ref, o_ref):
    acc = x_ref[0] + jnp.dot(yml_ref[0].astype(BF16), wo_ref[0:ML_WIDTH, :], preferred_element_type=F32)
    acc = acc + lax.dot_general(ydaT_ref[0], wo_ref[ML_WIDTH:ML_WIDTH + DA_WIDTH, :], _TN,
                                preferred_element_type=F32)
    acc = acc + lax.dot_general(ynsT_ref[0], wo_ref[ML_WIDTH + DA_WIDTH:, :], _TN,
                                preferred_element_type=F32)
    o_ref[0] = acc


def _outproj(x, yml, ydaT, ynsT, wo, tm=512):
    B, S, D = x.shape
    return pl.pallas_call(
        _outproj_kernel,
        grid=(B, S // tm),
        in_specs=[pl.BlockSpec((1, tm, D), lambda b, i: (b, i, 0)),
                  pl.BlockSpec((1, tm, ML_WIDTH), lambda b, i: (b, i, 0)),
                  pl.BlockSpec((1, DA_WIDTH, tm), lambda b, i: (b, 0, i)),
                  pl.BlockSpec((1, NSA_WIDTH, tm), lambda b, i: (b, 0, i)),
                  pl.BlockSpec((D, D), lambda b, i: (0, 0))],
        out_specs=pl.BlockSpec((1, tm, D), lambda b, i: (b, i, 0)),
        out_shape=jax.ShapeDtypeStruct((B, S, D), F32),
        compiler_params=_cparams(("parallel", "parallel")),
        name="outproj",
    )(x, yml, ydaT, ynsT, wo)


def _ffn_kernel(x_ref, g_ref, w1_ref, w2_ref, gf_ref, o_ref, hb_ref, acc_ref, *, final):
    j = pl.program_id(1)

    @pl.when(j == 0)
    def _():
        x = x_ref[...]
        hb_ref[...] = _rms(x, g_ref[...]).astype(BF16)
        acc_ref[...] = x

    u = jnp.dot(hb_ref[...], w1_ref[...], preferred_element_type=F32)
    a = jnp.square(jnp.maximum(u, 0.0)).astype(BF16)
    acc_ref[...] += jnp.dot(a, w2_ref[...], preferred_element_type=F32)

    @pl.when(j == pl.num_programs(1) - 1)
    def _():
        y = acc_ref[...]
        if final:
            y = _rms(y, gf_ref[...])
        o_ref[...] = y


def _ffn(x2d, g, w1, w2, gf, final, tm=1024, tf=512):
    N, D = x2d.shape
    return pl.pallas_call(
        functools.partial(_ffn_kernel, final=final),
        grid=(N // tm, D_FF // tf),
        in_specs=[pl.BlockSpec((tm, D), lambda i, j: (i, 0)),
                  pl.BlockSpec((1, D), lambda i, j: (0, 0)),
                  pl.BlockSpec((D, tf), lambda i, j: (0, j)),
                  pl.BlockSpec((tf, D), lambda i, j: (j, 0)),
                  pl.BlockSpec((1, D), lambda i, j: (0, 0))],
        out_specs=pl.BlockSpec((tm, D), lambda i, j: (i, 0)),
        out_shape=jax.ShapeDtypeStruct((N, D), F32),
        scratch_shapes=[pltpu.VMEM((tm, D), BF16), pltpu.VMEM((tm, D), F32)],
        compiler_params=_cparams(("parallel", "arbitrary")),
        name="ffn",
    )(x2d, g, w1, w2, gf)


def _split_w_in(w_in_l):
    edges = np.concatenate([[0], np.cumsum(IN_SIZES)])
    return [w_in_l[:, int(edges[i]):int(edges[i + 1])] for i in range(len(IN_SIZES))]


def _pad_cols(w, n):
    return jnp.pad(w, ((0, 0), (0, n - w.shape[1])))


def _inproj_weights(w_in_l):
    (ml_q, ml_k, ml_v, ml_o, ml_i, ml_f, da_q, da_k, da_v,
     ns_q, ns_kc, ns_vc, ns_ks, ns_vs, ns_kw, ns_vw, ns_g) = _split_w_in(w_in_l)
    wt = jnp.concatenate([ml_q, ml_k, ml_v, ml_o, _pad_cols(jnp.concatenate([ml_i, ml_f], 1), LANES),
                          da_k, ns_ks, ns_kw, ns_kc, ns_vc], axis=1)
    per_group = NSA_REP * 3
    ns_g_pad = jnp.concatenate(
        [_pad_cols(ns_g[:, gi * per_group:(gi + 1) * per_group], GATE_PAD) for gi in range(NSA_GROUPS)], axis=1)
    wf = jnp.concatenate([da_q, da_v, ns_q, ns_vs, ns_vw, ns_g_pad], axis=1)
    return wt.astype(BF16), wf.T.astype(BF16)


def _half_blocks(nsc):
    B, S, _ = nsc.shape
    n = S // CMP_STRIDE
    t = nsc.reshape(B, n, CMP_STRIDE, 2 * NSA_GROUPS, NSA_DIM)
    return t.transpose(0, 3, 1, 2, 4).reshape(B, 2 * NSA_GROUPS, n, CMP_STRIDE * NSA_DIM)


def kernel(x, norm1, w_in, ml_conv, ml_gate_bias, ml_norm, da_lambda, da_norm, nsa_pe,
           nsa_w1, nsa_w2, w_out, norm2, w_ff1, w_ff2, final_norm):
    B, S, D = x.shape
    depth = norm1.shape[0]
    for l in range(depth):
        wt, wfT = _inproj_weights(w_in[l])
        (zml, zg, dak, nsk, nsc, daqT, davT, nsqT, nsvT, nsgT) = _inproj(x, norm1[l][None, :], wt, wfT)

        yml = _mlstm(zml, zg, ml_conv[l], _pad_cols(ml_gate_bias[l][None, :], LANES), ml_norm[l][None, :])

        lam_init = 0.8 - 0.6 * math.exp(-0.3 * l)
        ydaT = _diff_attn(da_lambda[l], daqT, dak, davT, da_norm[l][:, None], lam_init)

        x16 = _half_blocks(nsc)
        pe2 = nsa_pe[l].reshape(2, 2, CMP_STRIDE * NSA_DIM)
        w1b = nsa_w1[l].astype(BF16)
        kcmp = _compress(x16, pe2, w1b, nsa_w2[l], 0, False)
        vcmpT = _compress(x16, pe2, w1b, nsa_w2[l], 1, True)
        ynsT = _nsa(nsqT, nsgT, kcmp, vcmpT, nsk, nsvT)

        x = _outproj(x, yml, ydaT, ynsT, w_out[l].astype(BF16))
        x = _ffn(x.reshape(B * S, D), norm2[l][None, :], w_ff1[l].astype(BF16), w_ff2[l].astype(BF16),
                 final_norm[None, :], final=(l == depth - 1)).reshape(B, S, D)
    return x
```

```python
import functools
import math

import numpy as np
import jax
import jax.numpy as jnp
from jax import lax
from jax.experimental import pallas as pl
from jax.experimental.pallas import tpu as pltpu

F32 = jnp.float32
BF16 = jnp.bfloat16

D_MODEL = 1024
ML_HEADS = 4
ML_DIM = 64
ML_WIDTH = ML_HEADS * ML_DIM
ML_CHUNK = 64
CONV_W = 4
DA_HEADS = 4
DA_QK_DIM = 32
DA_V_DIM = 64
DA_WIDTH = DA_HEADS * DA_V_DIM
NSA_HEADS = 8
NSA_GROUPS = 2
NSA_REP = NSA_HEADS // NSA_GROUPS
NSA_DIM = 64
NSA_WIDTH = NSA_HEADS * NSA_DIM
NSA_KV = NSA_GROUPS * NSA_DIM
CMP_BLOCK = 32
CMP_STRIDE = 16
CMP_HIDDEN = 4 * NSA_DIM
SEL_BLOCK = 64
SEL_TOPK = 16
WINDOW = 512
D_FF = 4 * D_MODEL
EPS = 1e-6
FORCE_SCORE = 1e4
IN_SIZES = (ML_WIDTH, ML_WIDTH, ML_WIDTH, ML_WIDTH, ML_HEADS, ML_HEADS,
            2 * DA_HEADS * DA_QK_DIM, 2 * DA_HEADS * DA_QK_DIM, DA_WIDTH,
            NSA_WIDTH, NSA_KV, NSA_KV, NSA_KV, NSA_KV, NSA_KV, NSA_KV, 3 * NSA_HEADS)

LANES = 128
GATE_PAD = 16
NEG = -1e30
M_INIT = -1e29
TAKEN = -3e38
LOG2E = 1.4426950408889634
V_DIM = 64
V_EXT = 80
VMEM_LIMIT = 56 * 1024 * 1024

_T_ML = (0, 1024)
_T_MLG = (1024, 1152)
_T_DAK = (1152, 1408)
_T_NSK = (1408, 1664)
_T_NSC = (1664, 1920)
_T_COLS = 1920
_F_DAQ = (0, 256)
_F_DAV = (256, 512)
_F_NSQ = (512, 1024)
_F_NSV = (1024, 1280)
_F_NSG = (1280, 1280 + NSA_GROUPS * GATE_PAD)
_F_ROWS = _F_NSG[1]

_NT = (((1,), (1,)), ((), ()))
_TN = (((0,), (0,)), ((), ()))


def _cparams(sem):
    return pltpu.CompilerParams(dimension_semantics=sem, vmem_limit_bytes=VMEM_LIMIT)


def _rms(x, g):
    return x * lax.rsqrt(jnp.mean(x * x, axis=-1, keepdims=True) + EPS) * g


def _inproj_kernel(x_ref, g_ref, wt_ref, wf_ref,
                   zml_ref, zg_ref, dak_ref, nsk_ref, nsc_ref,
                   daqT_ref, davT_ref, nsqT_ref, nsvT_ref, nsgT_ref):
    hb = _rms(x_ref[0], g_ref[...]).astype(BF16)

    def tdot(span):
        return jnp.dot(hb, wt_ref[:, span[0]:span[1]], preferred_element_type=F32)

    def fdot(span):
        return lax.dot_general(wf_ref[span[0]:span[1], :], hb, _NT, preferred_element_type=F32)

    zml_ref[0] = tdot(_T_ML)
    zg_ref[0] = tdot(_T_MLG)
    dak_ref[0] = tdot(_T_DAK).astype(BF16)
    nsk_ref[0] = tdot(_T_NSK).astype(BF16)
    nsc_ref[0] = tdot(_T_NSC)
    daqT_ref[0] = (fdot(_F_DAQ) * (DA_QK_DIM ** -0.5 * LOG2E)).astype(BF16)
    nsqT_ref[0] = (fdot(_F_NSQ) * (NSA_DIM ** -0.5 * LOG2E)).astype(BF16)
    nsgT_ref[0] = fdot(_F_NSG)
    tm = hb.shape[0]
    for v_ref, span in ((davT_ref, _F_DAV), (nsvT_ref, _F_NSV)):
        v = fdot(span).astype(BF16)
        for h in range((span[1] - span[0]) // V_DIM):
            v_ref[0, h * V_EXT:h * V_EXT + V_DIM, :] = v[h * V_DIM:(h + 1) * V_DIM, :]
            v_ref[0, h * V_EXT + V_DIM:(h + 1) * V_EXT, :] = jnp.ones((V_EXT - V_DIM, tm), BF16)


def _inproj(x, g, wt, wfT, tm=512):
    B, S, D = x.shape
    tok = lambda w, dt: jax.ShapeDtypeStruct((B, S, w), dt)
    feat = lambda r, dt: jax.ShapeDtypeStruct((B, r, S), dt)
    tspec = lambda w: pl.BlockSpec((1, tm, w), lambda b, i: (b, i, 0))
    fspec = lambda r: pl.BlockSpec((1, r, tm), lambda b, i: (b, 0, i))
    return pl.pallas_call(
        _inproj_kernel,
        grid=(B, S // tm),
        in_specs=[tspec(D),
                  pl.BlockSpec((1, D), lambda b, i: (0, 0)),
                  pl.BlockSpec((D, _T_COLS), lambda b, i: (0, 0)),
                  pl.BlockSpec((_F_ROWS, D), lambda b, i: (0, 0))],
        out_specs=[tspec(1024), tspec(128), tspec(256), tspec(256), tspec(256),
                   fspec(256), fspec(4 * V_EXT), fspec(512), fspec(4 * V_EXT), fspec(NSA_GROUPS * GATE_PAD)],
        out_shape=[tok(1024, F32), tok(128, F32), tok(256, BF16), tok(256, BF16), tok(256, F32),
                   feat(256, BF16), feat(4 * V_EXT, BF16), feat(512, BF16), feat(4 * V_EXT, BF16),
                   feat(NSA_GROUPS * GATE_PAD, F32)],
        compiler_params=_cparams(("parallel", "parallel")),
        name="inproj",
    )(x, g, wt, wfT)


def _log_sigmoid(x):
    return jnp.minimum(x, 0.0) - jnp.log1p(jnp.exp(-jnp.abs(x)))


def _sigmoid(x):
    return 1.0 / (1.0 + jnp.exp(-x))


def _mlstm_kernel(zml_ref, zg_ref, cw_ref, gb_ref, nrm_ref, y_ref,
                  pad_ref, c_ref, n_ref, m_ref, *, T):
    L = ML_CHUNK
    d = ML_DIM
    t = pl.program_id(1)

    @pl.when(t == 0)
    def _():
        pad_ref[0:8, :] = jnp.zeros((8, 2 * ML_WIDTH), F32)
        c_ref[...] = jnp.zeros_like(c_ref)
        n_ref[...] = jnp.zeros_like(n_ref)
        m_ref[...] = jnp.zeros_like(m_ref)

    @pl.when(t > 0)
    def _():
        pad_ref[0:8, :] = pad_ref[T:T + 8, :]

    pad_ref[8:8 + T, :] = zml_ref[0, :, 0:2 * ML_WIDTH]
    conv = cw_ref[0:1, :] * pad_ref[5:5 + T, :]
    for j in range(1, CONV_W):
        conv = conv + cw_ref[j:j + 1, :] * pad_ref[5 + j:5 + j + T, :]
    qk = conv * _sigmoid(conv)
    gates = zg_ref[0] + gb_ref[...]
    logf = _log_sigmoid(gates)

    row = lax.broadcasted_iota(jnp.int32, (L, L), 0)
    col = lax.broadcasted_iota(jnp.int32, (L, L), 1)
    causal = col <= row
    tril = causal.astype(F32)
    triu = (row <= col).astype(F32)

    for c in range(T // L):
        r0 = c * L
        g_c = gates[r0:r0 + L, :]
        lf_c = logf[r0:r0 + L, :]
        b_cols = jnp.dot(tril, lf_c, preferred_element_type=F32,
                         precision=lax.Precision.HIGHEST)
        b_rows = jnp.dot(lf_c.T, triu, preferred_element_type=F32,
                         precision=lax.Precision.HIGHEST)
        g_rows = g_c.T
        for h in range(ML_HEADS):
            b_col = b_cols[:, ML_HEADS + h:ML_HEADS + h + 1]
            b_row = b_rows[ML_HEADS + h:ML_HEADS + h + 1, :]
            ig_col = g_c[:, h:h + 1]
            ig_row = g_rows[h:h + 1, :]
            qh = qk[r0:r0 + L, h * d:(h + 1) * d]
            kh = qk[r0:r0 + L, ML_WIDTH + h * d:ML_WIDTH + (h + 1) * d] * (d ** -0.5)
            vh = zml_ref[0, r0:r0 + L, 2 * ML_WIDTH + h * d:2 * ML_WIDTH + (h + 1) * d]
            oh = zml_ref[0, r0:r0 + L, 3 * ML_WIDTH + h * d:3 * ML_WIDTH + (h + 1) * d]
            c_prev = c_ref[h]
            n_prev = n_ref[h, 0:1, :]
            m_prev = m_ref[h, 0:1, 0:1]

            g_tot = b_col[L - 1:L, :]
            a_row = g_tot - b_row + ig_row
            a_max = jnp.max(a_row, axis=1, keepdims=True)
            w_col = jnp.exp(g_tot - b_col + ig_col - a_max)
            c_loc = lax.dot_general(vh * w_col, kh, _TN, preferred_element_type=F32)
            n_loc = jnp.sum(kh * w_col, axis=0, keepdims=True)

            dmat = jnp.where(causal, b_col - b_row + ig_row, NEG)
            inter_log = b_col + m_prev
            m_t = jnp.maximum(inter_log, jnp.max(dmat, axis=1, keepdims=True))
            qkt = lax.dot_general(qh, kh, _NT, preferred_element_type=F32)
            wts = jnp.exp(dmat - m_t) * qkt
            s_inter = jnp.exp(inter_log - m_t)
            num = (jnp.dot(wts, vh, preferred_element_type=F32)
                   + s_inter * lax.dot_general(qh, c_prev, _NT, preferred_element_type=F32))
            den = (jnp.sum(wts, axis=1, keepdims=True)
                   + s_inter * jnp.sum(qh * n_prev, axis=1, keepdims=True))
            hh = num / jnp.maximum(jnp.abs(den), jnp.exp(-m_t))

            m_new = jnp.maximum(g_tot + m_prev, a_max)
            s_prev = jnp.exp(g_tot + m_prev - m_new)
            s_loc = jnp.exp(a_max - m_new)
            c_ref[h] = s_prev * c_prev + s_loc * c_loc
            n_ref[h] = jnp.broadcast_to(s_prev * n_prev + s_loc * n_loc, (8, d))
            m_ref[h] = jnp.broadcast_to(m_new, (8, LANES))

            yh = _sigmoid(oh) * hh
            y_ref[0, r0:r0 + L, h * d:(h + 1) * d] = _rms(yh, nrm_ref[0:1, h * d:(h + 1) * d])


def _mlstm(zml, zg, conv_w, gate_bias, norm_g, T=256):
    B, S, _ = zml.shape
    return pl.pallas_call(
        functools.partial(_mlstm_kernel, T=T),
        grid=(B, S // T),
        in_specs=[pl.BlockSpec((1, T, 1024), lambda b, t: (b, t, 0)),
                  pl.BlockSpec((1, T, 128), lambda b, t: (b, t, 0)),
                  pl.BlockSpec((CONV_W, 2 * ML_WIDTH), lambda b, t: (0, 0)),
                  pl.BlockSpec((1, 128), lambda b, t: (0, 0)),
                  pl.BlockSpec((1, ML_WIDTH), lambda b, t: (0, 0))],
        out_specs=pl.BlockSpec((1, T, ML_WIDTH), lambda b, t: (b, t, 0)),
        out_shape=jax.ShapeDtypeStruct((B, S, ML_WIDTH), F32),
        scratch_shapes=[pltpu.VMEM((T + 8, 2 * ML_WIDTH), F32),
                        pltpu.VMEM((ML_HEADS, ML_DIM, ML_DIM), F32),
                        pltpu.VMEM((ML_HEADS, 8, ML_DIM), F32),
                        pltpu.VMEM((ML_HEADS, 8, LANES), F32)],
        compiler_params=_cparams(("parallel", "arbitrary")),
        name="mlstm",
    )(zml, zg, conv_w, gate_bias, norm_g)


def _softmax_cols(s, m_ref, p_ref, cols):
    m_old = m_ref[:, cols]
    m_new = jnp.maximum(m_old, jnp.max(s, axis=0, keepdims=True))
    m_ref[:, cols] = m_new
    p_ref[:, cols] = jnp.exp2(s - m_new).astype(BF16)
    return jnp.exp2(m_old - m_new)


def _pipelined_attention(k_tile, vT_tile, qz_ref, s_refs, p_refs, m_ref, acc_ref, softmax,
                         first, n_plain_pairs, n_tail, max_tile, col_groups):
    sa, sb = s_refs
    pa, pb = p_refs

    def load(j):
        return jnp.clip(j, 0, max_tile)

    def half(j, s_cur, s_nxt, p_cur, p_prev, tail):
        s_nxt[...] = jnp.dot(k_tile(load(j + 1)), qz_ref[...], preferred_element_type=F32)
        pv = jnp.dot(vT_tile(load(j - 1)), p_prev[...], preferred_element_type=F32)
        alphas = softmax(s_cur, p_cur, j, tail)
        for cols, alpha in zip(col_groups, alphas):
            acc_ref[:, cols] = alpha * (acc_ref[:, cols] + pv[:, cols])

    def pair(j, tail):
        half(j, sa, sb, pa, pb, tail)
        half(j + 1, sb, sa, pb, pa, tail)

    m_ref[...] = jnp.full_like(m_ref, M_INIT)
    acc_ref[...] = jnp.zeros_like(acc_ref)
    pb[...] = jnp.zeros_like(pb)
    sa[...] = jnp.dot(k_tile(load(first)), qz_ref[...], preferred_element_type=F32)

    def body(i, carry):
        pair(first + 2 * i, False)
        return carry

    if not (isinstance(n_plain_pairs, int) and n_plain_pairs == 0):
        lax.fori_loop(0, n_plain_pairs, body, 0)
    j = first + 2 * n_plain_pairs
    bufs = ((sa, sb, pa, pb), (sb, sa, pb, pa))
    for t in range(n_tail):
        half(j + t, *bufs[t % 2], True)
    p_last = bufs[(n_tail - 1) % 2][2]
    acc_ref[...] += jnp.dot(vT_tile(load(j + n_tail - 1)), p_last[...], preferred_element_type=F32)


def _da_kernel(lam_ref, qT_ref, k_ref, vT_ref, gain_ref, o_ref,
               qz_ref, sa_ref, sb_ref, pa_ref, pb_ref, m_ref, acc_ref, *, tq, tk, lam_init, S):
    h = pl.program_id(1)
    qi = pl.program_id(2)
    d = DA_QK_DIM
    jd = (qi * tq) // tk
    col_groups = [slice(mp * tq, (mp + 1) * tq) for mp in range(2)]

    qz_ref[...] = jnp.zeros_like(qz_ref)
    for hh in range(2):
        @pl.when(h % 2 == hh)
        def _():
            qz_ref[hh * 2 * d:hh * 2 * d + d, 0:tq] = qT_ref[0, 0:d, :]
            qz_ref[hh * 2 * d + d:(hh + 1) * 2 * d, tq:2 * tq] = qT_ref[0, d:2 * d, :]

    def k_tile(j):
        return k_ref[0, pl.ds(pl.multiple_of(j * tk, tk), tk), :]

    def vT_tile(j):
        return vT_ref[0, :, pl.ds(pl.multiple_of(j * tk, tk), tk)]

    def softmax(s_ref, p_ref, j, tail):
        alphas = []
        for cols in col_groups:
            s = s_ref[:, cols]
            if tail:
                kpos = j * tk + lax.broadcasted_iota(jnp.int32, (tk, tq), 0)
                qpos = qi * tq + lax.broadcasted_iota(jnp.int32, (tk, tq), 1)
                s = jnp.where(kpos <= qpos, s, NEG)
            alphas.append(_softmax_cols(s, m_ref, p_ref, cols))
        return alphas

    _pipelined_attention(k_tile, vT_tile, qz_ref, (sa_ref, sb_ref), (pa_ref, pb_ref), m_ref, acc_ref, softmax,
                         first=0, n_plain_pairs=jd // 2, n_tail=2, max_tile=S // tk - 1, col_groups=col_groups)

    lp = lam_ref[...]
    lam = (jnp.exp(jnp.sum(lp[0:1] * lp[1:2], axis=1, keepdims=True))
           - jnp.exp(jnp.sum(lp[2:3] * lp[3:4], axis=1, keepdims=True)) + lam_init)
    o1 = acc_ref[0:V_DIM, 0:tq] / acc_ref[V_DIM:V_DIM + 1, 0:tq]
    o2 = acc_ref[0:V_DIM, tq:2 * tq] / acc_ref[V_DIM:V_DIM + 1, tq:2 * tq]
    o = o1 - lam * o2
    y = o * lax.rsqrt(jnp.mean(o * o, axis=0, keepdims=True) + EPS) * gain_ref[...]
    o_ref[0] = (y * (1.0 - lam_init)).astype(o_ref.dtype)


def _diff_attn(da_lambda, daqT, dak, davT, gain_col, lam_init, tq=256, tk=256):
    B, _, S = daqT.shape
    return pl.pallas_call(
        functools.partial(_da_kernel, tq=tq, tk=tk, lam_init=lam_init, S=S),
        grid=(B, DA_HEADS, S // tq),
        in_specs=[pl.BlockSpec((4, DA_QK_DIM), lambda b, h, i: (0, 0)),
                  pl.BlockSpec((1, DA_V_DIM, tq), lambda b, h, i: (b, h, i)),
                  pl.BlockSpec((1, S, LANES), lambda b, h, i: (b, 0, h // 2)),
                  pl.BlockSpec((1, V_EXT, S), lambda b, h, i: (b, h, 0)),
                  pl.BlockSpec((DA_V_DIM, 1), lambda b, h, i: (h, 0))],
        out_specs=pl.BlockSpec((1, DA_V_DIM, tq), lambda b, h, i: (b, h, i)),
        out_shape=jax.ShapeDtypeStruct((B, DA_WIDTH, S), BF16),
        scratch_shapes=[pltpu.VMEM((LANES, 2 * tq), BF16),
                        pltpu.VMEM((tk, 2 * tq), F32),
                        pltpu.VMEM((tk, 2 * tq), F32),
                        pltpu.VMEM((tk, 2 * tq), BF16),
                        pltpu.VMEM((tk, 2 * tq), BF16),
                        pltpu.VMEM((1, 2 * tq), F32),
                        pltpu.VMEM((V_EXT, 2 * tq), F32)],
        compiler_params=_cparams(("parallel", "parallel", "arbitrary")),
        name="diff_attn",
    )(da_lambda, daqT, dak, davT, gain_col)


def _gelu_tanh(x):
    return x * (0.5 * (1.0 + jnp.tanh(math.sqrt(2.0 / math.pi) * (x + 0.044715 * (x * x * x)))))


def _compress_kernel(x_ref, pe_ref, w1_ref, w2_ref, o_ref, b_ref, *, feature_major):
    n = x_ref.shape[2]
    half = CMP_STRIDE * NSA_DIM
    x = x_ref[0, 0]
    a = jnp.dot((x + pe_ref[0, 0:1, :]).astype(BF16), w1_ref[0, 0:half, :], preferred_element_type=F32)
    b_ref[0:n, :] = jnp.dot((x + pe_ref[0, 1:2, :]).astype(BF16), w1_ref[0, half:2 * half, :],
                            preferred_element_type=F32)
    b_ref[n:n + 8, :] = jnp.zeros((8, CMP_HIDDEN), F32)
    hid = _gelu_tanh(a + b_ref[1:n + 1, :]).astype(BF16)
    if feature_major:
        o_ref[0, 0] = lax.dot_general(w2_ref[0], hid, _NT, preferred_element_type=F32).astype(o_ref.dtype)
    else:
        o_ref[0, 0] = jnp.dot(hid, w2_ref[0], preferred_element_type=F32).astype(o_ref.dtype)


def _compress(x16, pe2, w1, w2, which, feature_major):
    B, _, n, half = x16.shape
    G = NSA_GROUPS
    if feature_major:
        out_shape, out_block = (B, G, NSA_DIM, n), (1, 1, NSA_DIM, n)
        w2_arr, w2_block = jnp.swapaxes(w2, 1, 2), (1, NSA_DIM, CMP_HIDDEN)
    else:
        out_shape, out_block = (B, G, n, NSA_DIM), (1, 1, n, NSA_DIM)
        w2_arr, w2_block = w2, (1, CMP_HIDDEN, NSA_DIM)
    return pl.pallas_call(
        functools.partial(_compress_kernel, feature_major=feature_major),
        grid=(B, G),
        in_specs=[pl.BlockSpec((1, 1, n, half), lambda b, g: (b, which * G + g, 0, 0)),
                  pl.BlockSpec((1, 2, half), lambda b, g: (which, 0, 0)),
                  pl.BlockSpec((1, 2 * half, CMP_HIDDEN), lambda b, g: (which, 0, 0)),
                  pl.BlockSpec(w2_block, lambda b, g: (which, 0, 0))],
        out_specs=pl.BlockSpec(out_block, lambda b, g: (b, g, 0, 0)),
        out_shape=jax.ShapeDtypeStruct(out_shape, BF16),
        scratch_shapes=[pltpu.VMEM((n + 8, CMP_HIDDEN), F32)],
        compiler_params=_cparams(("parallel", "parallel")),
        name="compress_v" if feature_major else "compress_k",
    )(x16, pe2, w1, w2_arr.astype(BF16))


def _nsa_kernel(qT_ref, gT_ref, kc_ref, vcT_ref, ks_ref, vsT_ref, kw_ref, vwT_ref, o_ref,
                qg_ref, qz_ref, imp_ref, sel_ref, pc_ref, sa_ref, sb_ref, pa_ref, pb_ref, m_ref, acc_ref, out_ref,
                *, tq, tk, S):
    g = pl.program_id(1)
    qi = pl.program_id(2)
    d = NSA_DIM
    R = NSA_REP
    ncb = S // CMP_STRIDE
    nsb = S // SEL_BLOCK
    q0 = qi * tq
    jd = q0 // tk
    ratio = SEL_BLOCK // CMP_STRIDE
    per_tile = tk // SEL_BLOCK

    for r in range(R):
        qg_ref[:, r * tq:(r + 1) * tq] = qT_ref[0, r * d:(r + 1) * d, :]
    qz_ref[...] = jnp.zeros_like(qz_ref)
    for gg in range(NSA_GROUPS):
        @pl.when(g == gg)
        def _():
            qz_ref[gg * d:(gg + 1) * d, :] = qg_ref[...]

    def gate(r, br):
        return _sigmoid(gT_ref[0, r * 3 + br:r * 3 + br + 1, :])

    qpos = q0 + lax.broadcasted_iota(jnp.int32, (1, tq), 1)

    s_all = jnp.dot(kc_ref[0, 0], qg_ref[...], preferred_element_type=F32)
    cend = lax.broadcasted_iota(jnp.int32, (ncb, tq), 0) * CMP_STRIDE + (CMP_BLOCK - 1)
    cvalid = cend <= qpos
    imp = jnp.zeros((ncb, tq), F32)
    for r in range(R):
        cols = slice(r * tq, (r + 1) * tq)
        s = s_all[:, cols]
        mx = jnp.max(jnp.where(cvalid, s, NEG), axis=0, keepdims=True)
        p = jnp.exp2(jnp.where(cvalid, s - mx, NEG))
        den = jnp.sum(p, axis=0, keepdims=True)
        p = p / jnp.where(den > 0, den, 1.0)
        imp = imp + p
        pc_ref[:, cols] = p.astype(BF16)
    o_cmp = jnp.dot(vcT_ref[0, 0], pc_ref[...], preferred_element_type=F32)
    for r in range(R):
        cols = slice(r * tq, (r + 1) * tq)
        out_ref[:, cols] = gate(r, 0) * o_cmp[:, cols]

    slabs = []
    for c in range(tq // LANES):
        imp_ref[c, 0:8, :] = jnp.zeros((8, LANES), F32)
        imp_ref[c, 8:8 + ncb, :] = imp[:, c * LANES:(c + 1) * LANES]
        imp_ref[c, 8 + ncb:16 + ncb, :] = jnp.zeros((8, LANES), F32)
        slab = jnp.zeros((nsb, LANES), F32)
        for o in range(-1, ratio):
            slab = slab + imp_ref[c, pl.ds(8 + o, nsb, stride=ratio), :]
        slabs.append(slab)
    p_slc = slabs[0] if len(slabs) == 1 else jnp.concatenate(slabs, axis=1)
    blk = lax.broadcasted_iota(jnp.int32, (nsb, tq), 0)
    cur = lax.shift_right_logical(qpos, int(math.log2(SEL_BLOCK)))
    forced = (blk == 0) | (blk == cur) | (blk == cur - 1)
    causal_blk = blk * SEL_BLOCK <= qpos
    score = jnp.where(forced, FORCE_SCORE, jnp.where(causal_blk, p_slc, -1.0))
    blk_f = blk.astype(F32)
    for _ in range(SEL_TOPK):
        mx = jnp.max(score, axis=0, keepdims=True)
        first = jnp.min(jnp.where(score == mx, blk_f, float(nsb)), axis=0, keepdims=True)
        score = jnp.where(blk_f == first, TAKEN, score)
    sel_ref[...] = jnp.where(score == TAKEN, 0.0, NEG)

    col_groups = [slice(r * tq, (r + 1) * tq) for r in range(R)]
    max_tile = S // tk - 1

    def tiles_of(k_ref_, vT_ref_):
        def k_tile(j):
            return k_ref_[0, pl.ds(pl.multiple_of(j * tk, tk), tk), :]

        def vT_tile(j):
            return vT_ref_[0, :, pl.ds(pl.multiple_of(j * tk, tk), tk)]
        return k_tile, vT_tile

    def softmax_with(bias_of):
        def softmax(s_ref, p_ref, j, tail):
            bias = bias_of(j, tail)
            return [_softmax_cols(s_ref[:, cols] + bias, m_ref, p_ref, cols) for cols in col_groups]
        return softmax

    def flush(br):
        for r, cols in enumerate(col_groups):
            den = acc_ref[V_DIM:V_DIM + 1, cols]
            out_ref[:, cols] += gate(r, br) * (acc_ref[0:V_DIM, cols] / jnp.where(den > 0, den, 1.0))

    def kpos_of(j):
        return j * tk + lax.broadcasted_iota(jnp.int32, (tk, tq), 0)

    def win_bias(j, tail):
        kpos = kpos_of(j)
        rel = qpos - kpos
        return jnp.where((rel >= 0) & (rel < WINDOW) & (kpos >= 0), 0.0, NEG)

    n_win = WINDOW // tk + 1
    _pipelined_attention(*tiles_of(kw_ref, vwT_ref), qz_ref, (sa_ref, sb_ref), (pa_ref, pb_ref), m_ref, acc_ref,
                         softmax_with(win_bias), first=jd - (n_win - 1), n_plain_pairs=0, n_tail=n_win,
                         max_tile=max_tile, col_groups=col_groups)
    flush(2)

    def sel_bias(j, tail):
        first_blk = jnp.clip(j, 0, max_tile) * per_tile
        rows = [jnp.broadcast_to(sel_ref[pl.ds(first_blk + i, 1), :], (SEL_BLOCK, tq)) for i in range(per_tile)]
        bias = jnp.concatenate(rows, axis=0)
        if tail:
            bias = jnp.where(kpos_of(j) <= qpos, bias, NEG)
        return bias

    _pipelined_attention(*tiles_of(ks_ref, vsT_ref), qz_ref, (sa_ref, sb_ref), (pa_ref, pb_ref), m_ref, acc_ref,
                         softmax_with(sel_bias), first=0, n_plain_pairs=jd // 2, n_tail=2,
                         max_tile=max_tile, col_groups=col_groups)
    flush(1)

    for r in range(R):
        o_ref[0, r * d:(r + 1) * d, :] = out_ref[:, r * tq:(r + 1) * tq].astype(o_ref.dtype)


def _nsa(nsqT, nsgT, kcmp, vcmpT, nsk, nsvT, tq=128, tk=256):
    B, _, S = nsqT.shape
    G, R, d = NSA_GROUPS, NSA_REP, NSA_DIM
    ncb = S // CMP_STRIDE
    nsb = S // SEL_BLOCK
    return pl.pallas_call(
        functools.partial(_nsa_kernel, tq=tq, tk=tk, S=S),
        grid=(B, G, S // tq),
        in_specs=[pl.BlockSpec((1, R * d, tq), lambda b, g, i: (b, g, i)),
                  pl.BlockSpec((1, GATE_PAD, tq), lambda b, g, i: (b, g, i)),
                  pl.BlockSpec((1, 1, ncb, d), lambda b, g, i: (b, g, 0, 0)),
                  pl.BlockSpec((1, 1, d, ncb), lambda b, g, i: (b, g, 0, 0)),
                  pl.BlockSpec((1, S, LANES), lambda b, g, i: (b, 0, 0)),
                  pl.BlockSpec((1, V_EXT, S), lambda b, g, i: (b, g, 0)),
                  pl.BlockSpec((1, S, LANES), lambda b, g, i: (b, 0, 1)),
                  pl.BlockSpec((1, V_EXT, S), lambda b, g, i: (b, G + g, 0))],
        out_specs=pl.BlockSpec((1, R * d, tq), lambda b, g, i: (b, g, i)),
        out_shape=jax.ShapeDtypeStruct((B, NSA_WIDTH, S), BF16),
        scratch_shapes=[pltpu.VMEM((d, R * tq), BF16),
                        pltpu.VMEM((LANES, R * tq), BF16),
                        pltpu.VMEM((tq // LANES, ncb + 16, LANES), F32),
                        pltpu.VMEM((nsb, tq), F32),
                        pltpu.VMEM((ncb, R * tq), BF16),
                        pltpu.VMEM((tk, R * tq), F32),
                        pltpu.VMEM((tk, R * tq), F32),
                        pltpu.VMEM((tk, R * tq), BF16),
                        pltpu.VMEM((tk, R * tq), BF16),
                        pltpu.VMEM((1, R * tq), F32),
                        pltpu.VMEM((V_EXT, R * tq), F32),
                        pltpu.VMEM((d, R * tq), F32)],
        compiler_params=_cparams(("parallel", "parallel", "arbitrary")),
        name="nsa",
    )(nsqT, nsgT, kcmp, vcmpT, nsk, nsvT, nsk, nsvT)


def _outproj_kernel(x_ref, yml_ref, ydaT_ref, ynsT_ref, wo_ref, o_ref):
    acc = x_ref[0] + jnp.dot(yml_ref[0].astype(BF16), wo_ref[0:ML_WIDTH, :], preferred_element_type=F32)
    acc = acc + lax.dot_general(ydaT_ref[0], wo_ref[ML_WIDTH:ML_WIDTH + DA_WIDTH, :], _TN,
                                preferred_element_type=F32)
    acc = acc + lax.dot_general(ynsT_ref[0], wo_ref[ML_WIDTH + DA_WIDTH:, :], _TN,
                                preferred_element_type=F32)
    o_ref[0] = acc


def _outproj(x, yml, ydaT, ynsT, wo, tm=512):
    B, S, D = x.shape
    return pl.pallas_call(
        _outproj_kernel,
        grid=(B, S // tm),
        in_specs=[pl.BlockSpec((1, tm, D), lambda b, i: (b, i, 0)),
                  pl.BlockSpec((1, tm, ML_WIDTH), lambda b, i: (b, i, 0)),
                  pl.BlockSpec((1, DA_WIDTH, tm), lambda b, i: (b, 0, i)),
                  pl.BlockSpec((1, NSA_WIDTH, tm), lambda b, i: (b, 0, i)),
                  pl.BlockSpec((D, D), lambda b, i: (0, 0))],
        out_specs=pl.BlockSpec((1, tm, D), lambda b, i: (b, i, 0)),
        out_shape=jax.ShapeDtypeStruct((B, S, D), F32),
        compiler_params=_cparams(("parallel", "parallel")),
        name="outproj",
    )(x, yml, ydaT, ynsT, wo)


def _ffn_kernel(x_ref, g_ref, w1_ref, w2_ref, gf_ref, o_ref, hb_ref, acc_ref, *, final):
    j = pl.program_id(1)

    @pl.when(j == 0)
    def _():
        x = x_ref[...]
        hb_ref[...] = _rms(x, g_ref[...]).astype(BF16)
        acc_ref[...] = x

    u = jnp.dot(hb_ref[...], w1_ref[...], preferred_element_type=F32)
    a = jnp.square(jnp.maximum(u, 0.0)).astype(BF16)
    acc_ref[...] += jnp.dot(a, w2_ref[...], preferred_element_type=F32)

    @pl.when(j == pl.num_programs(1) - 1)
    def _():
        y = acc_ref[...]
        if final:
            y = _rms(y, gf_ref[...])
        o_ref[...] = y


def _ffn(x2d, g, w1, w2, gf, final, tm=1024, tf=512):
    N, D = x2d.shape
    return pl.pallas_call(
        functools.partial(_ffn_kernel, final=final),
        grid=(N // tm, D_FF // tf),
        in_specs=[pl.BlockSpec((tm, D), lambda i, j: (i, 0)),
                  pl.BlockSpec((1, D), lambda i, j: (0, 0)),
                  pl.BlockSpec((D, tf), lambda i, j: (0, j)),
                  pl.BlockSpec((tf, D), lambda i, j: (j, 0)),
                  pl.BlockSpec((1, D), lambda i, j: (0, 0))],
        out_specs=pl.BlockSpec((tm, D), lambda i, j: (i, 0)),
        out_shape=jax.ShapeDtypeStruct((N, D), F32),
        scratch_shapes=[pltpu.VMEM((tm, D), BF16), pltpu.VMEM((tm, D), F32)],
        compiler_params=_cparams(("parallel", "arbitrary")),
        name="ffn",
    )(x2d, g, w1, w2, gf)


def _split_w_in(w_in_l):
    edges = np.concatenate([[0], np.cumsum(IN_SIZES)])
    return [w_in_l[:, int(edges[i]):int(edges[i + 1])] for i in range(len(IN_SIZES))]


def _pad_cols(w, n):
    return jnp.pad(w, ((0, 0), (0, n - w.shape[1])))


def _inproj_weights(w_in_l):
    (ml_q, ml_k, ml_v, ml_o, ml_i, ml_f, da_q, da_k, da_v,
     ns_q, ns_kc, ns_vc, ns_ks, ns_vs, ns_kw, ns_vw, ns_g) = _split_w_in(w_in_l)
    wt = jnp.concatenate([ml_q, ml_k, ml_v, ml_o, _pad_cols(jnp.concatenate([ml_i, ml_f], 1), LANES),
                          da_k, ns_ks, ns_kw, ns_kc, ns_vc], axis=1)
    per_group = NSA_REP * 3
    ns_g_pad = jnp.concatenate(
        [_pad_cols(ns_g[:, gi * per_group:(gi + 1) * per_group], GATE_PAD) for gi in range(NSA_GROUPS)], axis=1)
    wf = jnp.concatenate([da_q, da_v, ns_q, ns_vs, ns_vw, ns_g_pad], axis=1)
    return wt.astype(BF16), wf.T.astype(BF16)


def _half_blocks(nsc):
    B, S, _ = nsc.shape
    n = S // CMP_STRIDE
    t = nsc.reshape(B, n, CMP_STRIDE, 2 * NSA_GROUPS, NSA_DIM)
    return t.transpose(0, 3, 1, 2, 4).reshape(B, 2 * NSA_GROUPS, n, CMP_STRIDE * NSA_DIM)


def kernel(x, norm1, w_in, ml_conv, ml_gate_bias, ml_norm, da_lambda, da_norm, nsa_pe,
           nsa_w1, nsa_w2, w_out, norm2, w_ff1, w_ff2, final_norm):
    B, S, D = x.shape
    depth = norm1.shape[0]
    for l in range(depth):
        wt, wfT = _inproj_weights(w_in[l])
        (zml, zg, dak, nsk, nsc, daqT, davT, nsqT, nsvT, nsgT) = _inproj(x, norm1[l][None, :], wt, wfT)

        yml = _mlstm(zml, zg, ml_conv[l], _pad_cols(ml_gate_bias[l][None, :], LANES), ml_norm[l][None, :])

        lam_init = 0.8 - 0.6 * math.exp(-0.3 * l)
        ydaT = _diff_attn(da_lambda[l], daqT, dak, davT, da_norm[l][:, None], lam_init)

        x16 = _half_blocks(nsc)
        pe2 = nsa_pe[l].reshape(2, 2, CMP_STRIDE * NSA_DIM)
        w1b = nsa_w1[l].astype(BF16)
        kcmp = _compress(x16, pe2, w1b, nsa_w2[l], 0, False)
        vcmpT = _compress(x16, pe2, w1b, nsa_w2[l], 1, True)
        ynsT = _nsa(nsqT, nsgT, kcmp, vcmpT, nsk, nsvT)

        x = _outproj(x, yml, ydaT, ynsT, w_out[l].astype(BF16))
        x = _ffn(x.reshape(B * S, D), norm2[l][None, :], w_ff1[l].astype(BF16), w_ff2[l].astype(BF16),
                 final_norm[None, :], final=(l == depth - 1)).reshape(B, S, D)
    return x
```

```python
import functools
import math

import numpy as np
import jax
import jax.numpy as jnp
from jax import lax
from jax.experimental import pallas as pl
from jax.experimental.pallas import tpu as pltpu

F32 = jnp.float32
BF16 = jnp.bfloat16

D_MODEL = 1024
ML_HEADS = 4
ML_DIM = 64
ML_WIDTH = ML_HEADS * ML_DIM
ML_CHUNK = 64
CONV_W = 4
DA_HEADS = 4
DA_QK_DIM = 32
DA_V_DIM = 64
DA_WIDTH = DA_HEADS * DA_V_DIM
NSA_HEADS = 8
NSA_GROUPS = 2
NSA_REP = NSA_HEADS // NSA_GROUPS
NSA_DIM = 64
NSA_WIDTH = NSA_HEADS * NSA_DIM
NSA_KV = NSA_GROUPS * NSA_DIM
CMP_BLOCK = 32
CMP_STRIDE = 16
CMP_HIDDEN = 4 * NSA_DIM
SEL_BLOCK = 64
SEL_TOPK = 16
WINDOW = 512
D_FF = 4 * D_MODEL
EPS = 1e-6
FORCE_SCORE = 1e4
IN_SIZES = (ML_WIDTH, ML_WIDTH, ML_WIDTH, ML_WIDTH, ML_HEADS, ML_HEADS,
            2 * DA_HEADS * DA_QK_DIM, 2 * DA_HEADS * DA_QK_DIM, DA_WIDTH,
            NSA_WIDTH, NSA_KV, NSA_KV, NSA_KV, NSA_KV, NSA_KV, NSA_KV, 3 * NSA_HEADS)

LANES = 128
GATE_PAD = 16
NEG = -1e30
M_INIT = -1e29
LOOP_PAIRS = 4
TAKEN = -3e38
LOG2E = 1.4426950408889634
V_DIM = 64
V_EXT = 80
VMEM_LIMIT = 56 * 1024 * 1024

_T_ML = (0, 1024)
_T_MLG = (1024, 1152)
_T_DAK = (1152, 1408)
_T_NSK = (1408, 1664)
_T_NSC = (1664, 1920)
_T_COLS = 1920
_F_DAQ = (0, 256)
_F_DAV = (256, 512)
_F_NSQ = (512, 1024)
_F_NSV = (1024, 1280)
_F_NSG = (1280, 1280 + NSA_GROUPS * GATE_PAD)
_F_ROWS = _F_NSG[1]

_NT = (((1,), (1,)), ((), ()))
_TN = (((0,), (0,)), ((), ()))


def _cparams(sem):
    return pltpu.CompilerParams(dimension_semantics=sem, vmem_limit_bytes=VMEM_LIMIT)


def _rms(x, g):
    return x * lax.rsqrt(jnp.mean(x * x, axis=-1, keepdims=True) + EPS) * g


def _inproj_kernel(x_ref, g_ref, wt_ref, wf_ref,
                   zml_ref, zg_ref, dak_ref, nsk_ref, nsc_ref,
                   daqT_ref, davT_ref, nsqT_ref, nsvT_ref, nsgT_ref):
    hb = _rms(x_ref[0], g_ref[...]).astype(BF16)

    def tdot(span):
        return jnp.dot(hb, wt_ref[:, span[0]:span[1]], preferred_element_type=F32)

    def fdot(span):
        return lax.dot_general(wf_ref[span[0]:span[1], :], hb, _NT, preferred_element_type=F32)

    zml_ref[0] = tdot(_T_ML)
    zg_ref[0] = tdot(_T_MLG)
    dak_ref[0] = tdot(_T_DAK).astype(BF16)
    nsk_ref[0] = tdot(_T_NSK).astype(BF16)
    nsc_ref[0] = tdot(_T_NSC)
    daqT_ref[0] = (fdot(_F_DAQ) * (DA_QK_DIM ** -0.5 * LOG2E)).astype(BF16)
    nsqT_ref[0] = (fdot(_F_NSQ) * (NSA_DIM ** -0.5 * LOG2E)).astype(BF16)
    nsgT_ref[0] = fdot(_F_NSG)
    tm = hb.shape[0]
    for v_ref, span in ((davT_ref, _F_DAV), (nsvT_ref, _F_NSV)):
        v = fdot(span).astype(BF16)
        for h in range((span[1] - span[0]) // V_DIM):
            v_ref[0, h * V_EXT:h * V_EXT + V_DIM, :] = v[h * V_DIM:(h + 1) * V_DIM, :]
            v_ref[0, h * V_EXT + V_DIM:(h + 1) * V_EXT, :] = jnp.ones((V_EXT - V_DIM, tm), BF16)


def _inproj(x, g, wt, wfT, tm=512):
    B, S, D = x.shape
    tok = lambda w, dt: jax.ShapeDtypeStruct((B, S, w), dt)
    feat = lambda r, dt: jax.ShapeDtypeStruct((B, r, S), dt)
    tspec = lambda w: pl.BlockSpec((1, tm, w), lambda b, i: (b, i, 0))
    fspec = lambda r: pl.BlockSpec((1, r, tm), lambda b, i: (b, 0, i))
    return pl.pallas_call(
        _inproj_kernel,
        grid=(B, S // tm),
        in_specs=[tspec(D),
                  pl.BlockSpec((1, D), lambda b, i: (0, 0)),
                  pl.BlockSpec((D, _T_COLS), lambda b, i: (0, 0)),
                  pl.BlockSpec((_F_ROWS, D), lambda b, i: (0, 0))],
        out_specs=[tspec(1024), tspec(128), tspec(256), tspec(256), tspec(256),
                   fspec(256), fspec(4 * V_EXT), fspec(512), fspec(4 * V_EXT), fspec(NSA_GROUPS * GATE_PAD)],
        out_shape=[tok(1024, F32), tok(128, F32), tok(256, BF16), tok(256, BF16), tok(256, F32),
                   feat(256, BF16), feat(4 * V_EXT, BF16), feat(512, BF16), feat(4 * V_EXT, BF16),
                   feat(NSA_GROUPS * GATE_PAD, F32)],
        compiler_params=_cparams(("parallel", "parallel")),
        name="inproj",
    )(x, g, wt, wfT)


def _log_sigmoid(x):
    return jnp.minimum(x, 0.0) - jnp.log1p(jnp.exp(-jnp.abs(x)))


def _sigmoid(x):
    return 1.0 / (1.0 + jnp.exp(-x))


def _mlstm_kernel(zml_ref, zg_ref, cw_ref, gb_ref, nrm_ref, y_ref,
                  pad_ref, c_ref, n_ref, m_ref, *, T):
    L = ML_CHUNK
    d = ML_DIM
    t = pl.program_id(1)

    @pl.when(t == 0)
    def _():
        pad_ref[0:8, :] = jnp.zeros((8, 2 * ML_WIDTH), F32)
        c_ref[...] = jnp.zeros_like(c_ref)
        n_ref[...] = jnp.zeros_like(n_ref)
        m_ref[...] = jnp.zeros_like(m_ref)

    @pl.when(t > 0)
    def _():
        pad_ref[0:8, :] = pad_ref[T:T + 8, :]

    pad_ref[8:8 + T, :] = zml_ref[0, :, 0:2 * ML_WIDTH]
    conv = cw_ref[0:1, :] * pad_ref[5:5 + T, :]
    for j in range(1, CONV_W):
        conv = conv + cw_ref[j:j + 1, :] * pad_ref[5 + j:5 + j + T, :]
    qk = conv * _sigmoid(conv)
    gates = zg_ref[0] + gb_ref[...]
    logf = _log_sigmoid(gates)

    row = lax.broadcasted_iota(jnp.int32, (L, L), 0)
    col = lax.broadcasted_iota(jnp.int32, (L, L), 1)
    causal = col <= row
    tril = causal.astype(F32)
    triu = (row <= col).astype(F32)

    for c in range(T // L):
        r0 = c * L
        g_c = gates[r0:r0 + L, :]
        lf_c = logf[r0:r0 + L, :]
        b_cols = jnp.dot(tril, lf_c, preferred_element_type=F32,
                         precision=lax.Precision.HIGHEST)
        b_rows = jnp.dot(lf_c.T, triu, preferred_element_type=F32,
                         precision=lax.Precision.HIGHEST)
        g_rows = g_c.T
        for h in range(ML_HEADS):
            b_col = b_cols[:, ML_HEADS + h:ML_HEADS + h + 1]
            b_row = b_rows[ML_HEADS + h:ML_HEADS + h + 1, :]
            ig_col = g_c[:, h:h + 1]
            ig_row = g_rows[h:h + 1, :]
            qh = qk[r0:r0 + L, h * d:(h + 1) * d]
            kh = qk[r0:r0 + L, ML_WIDTH + h * d:ML_WIDTH + (h + 1) * d] * (d ** -0.5)
            vh = zml_ref[0, r0:r0 + L, 2 * ML_WIDTH + h * d:2 * ML_WIDTH + (h + 1) * d]
            oh = zml_ref[0, r0:r0 + L, 3 * ML_WIDTH + h * d:3 * ML_WIDTH + (h + 1) * d]
            c_prev = c_ref[h]
            n_prev = n_ref[h, 0:1, :]
            m_prev = m_ref[h, 0:1, 0:1]

            g_tot = b_col[L - 1:L, :]
            a_row = g_tot - b_row + ig_row
            a_max = jnp.max(a_row, axis=1, keepdims=True)
            w_col = jnp.exp(g_tot - b_col + ig_col - a_max)
            c_loc = lax.dot_general(vh * w_col, kh, _TN, preferred_element_type=F32)
            n_loc = jnp.sum(kh * w_col, axis=0, keepdims=True)

            dmat = jnp.where(causal, b_col - b_row + ig_row, NEG)
            inter_log = b_col + m_prev
            m_t = jnp.maximum(inter_log, jnp.max(dmat, axis=1, keepdims=True))
            qkt = lax.dot_general(qh, kh, _NT, preferred_element_type=F32)
            wts = jnp.exp(dmat - m_t) * qkt
            s_inter = jnp.exp(inter_log - m_t)
            num = (jnp.dot(wts, vh, preferred_element_type=F32)
                   + s_inter * lax.dot_general(qh, c_prev, _NT, preferred_element_type=F32))
            den = (jnp.sum(wts, axis=1, keepdims=True)
                   + s_inter * jnp.sum(qh * n_prev, axis=1, keepdims=True))
            hh = num / jnp.maximum(jnp.abs(den), jnp.exp(-m_t))

            m_new = jnp.maximum(g_tot + m_prev, a_max)
            s_prev = jnp.exp(g_tot + m_prev - m_new)
            s_loc = jnp.exp(a_max - m_new)
            c_ref[h] = s_prev * c_prev + s_loc * c_loc
            n_ref[h] = jnp.broadcast_to(s_prev * n_prev + s_loc * n_loc, (8, d))
            m_ref[h] = jnp.broadcast_to(m_new, (8, LANES))

            yh = _sigmoid(oh) * hh
            y_ref[0, r0:r0 + L, h * d:(h + 1) * d] = _rms(yh, nrm_ref[0:1, h * d:(h + 1) * d])


def _mlstm(zml, zg, conv_w, gate_bias, norm_g, T=256):
    B, S, _ = zml.shape
    return pl.pallas_call(
        functools.partial(_mlstm_kernel, T=T),
        grid=(B, S // T),
        in_specs=[pl.BlockSpec((1, T, 1024), lambda b, t: (b, t, 0)),
                  pl.BlockSpec((1, T, 128), lambda b, t: (b, t, 0)),
                  pl.BlockSpec((CONV_W, 2 * ML_WIDTH), lambda b, t: (0, 0)),
                  pl.BlockSpec((1, 128), lambda b, t: (0, 0)),
                  pl.BlockSpec((1, ML_WIDTH), lambda b, t: (0, 0))],
        out_specs=pl.BlockSpec((1, T, ML_WIDTH), lambda b, t: (b, t, 0)),
        out_shape=jax.ShapeDtypeStruct((B, S, ML_WIDTH), F32),
        scratch_shapes=[pltpu.VMEM((T + 8, 2 * ML_WIDTH), F32),
                        pltpu.VMEM((ML_HEADS, ML_DIM, ML_DIM), F32),
                        pltpu.VMEM((ML_HEADS, 8, ML_DIM), F32),
                        pltpu.VMEM((ML_HEADS, 8, LANES), F32)],
        compiler_params=_cparams(("parallel", "arbitrary")),
        name="mlstm",
    )(zml, zg, conv_w, gate_bias, norm_g)


def _softmax_cols(s, m_ref, p_ref, cols):
    m_old = m_ref[:, cols]
    m_new = jnp.maximum(m_old, jnp.max(s, axis=0, keepdims=True))
    m_ref[:, cols] = m_new
    p_ref[:, cols] = jnp.exp2(s - m_new).astype(BF16)
    return jnp.exp2(m_old - m_new)


def _pipelined_attention(k_tile, vT_tile, qz_ref, s_refs, p_refs, m_ref, acc_ref, softmax,
                         first, n_plain_pairs, n_tail, max_tile, col_groups):
    sa, sb = s_refs
    pa, pb = p_refs

    def load(j):
        return jnp.clip(j, 0, max_tile)

    def half(j, s_cur, s_nxt, p_cur, p_prev, tail):
        s_nxt[...] = jnp.dot(k_tile(load(j + 1)), qz_ref[...], preferred_element_type=F32)
        pv = jnp.dot(vT_tile(load(j - 1)), p_prev[...], preferred_element_type=F32)
        alphas = softmax(s_cur, p_cur, j, tail)
        for cols, alpha in zip(col_groups, alphas):
            acc_ref[:, cols] = alpha * (acc_ref[:, cols] + pv[:, cols])

    def pair(j, tail):
        half(j, sa, sb, pa, pb, tail)
        half(j + 1, sb, sa, pb, pa, tail)

    m_ref[...] = jnp.full_like(m_ref, M_INIT)
    acc_ref[...] = jnp.zeros_like(acc_ref)
    pb[...] = jnp.zeros_like(pb)
    sa[...] = jnp.dot(k_tile(load(first)), qz_ref[...], preferred_element_type=F32)

    if not (isinstance(n_plain_pairs, int) and n_plain_pairs == 0):
        n_trips = n_plain_pairs // LOOP_PAIRS

        def body(i, carry):
            for u in range(LOOP_PAIRS):
                pair(first + 2 * (LOOP_PAIRS * i + u), False)
            return carry

        lax.fori_loop(0, n_trips, body, 0)
        rest = n_plain_pairs - n_trips * LOOP_PAIRS
        for u in range(LOOP_PAIRS - 1):
            @pl.when(u < rest)
            def _():
                pair(first + 2 * (LOOP_PAIRS * n_trips + u), False)
    j = first + 2 * n_plain_pairs
    bufs = ((sa, sb, pa, pb), (sb, sa, pb, pa))
    for t in range(n_tail):
        half(j + t, *bufs[t % 2], True)
    p_last = bufs[(n_tail - 1) % 2][2]
    acc_ref[...] += jnp.dot(vT_tile(load(j + n_tail - 1)), p_last[...], preferred_element_type=F32)


def _da_kernel(lam_ref, qT_ref, k_ref, vT_ref, gain_ref, o_ref,
               qz_ref, sa_ref, sb_ref, pa_ref, pb_ref, m_ref, acc_ref, *, tq, tk, lam_init, S):
    h = pl.program_id(1)
    qi = pl.program_id(2)
    d = DA_QK_DIM
    jd = (qi * tq) // tk
    col_groups = [slice(mp * tq, (mp + 1) * tq) for mp in range(2)]

    qz_ref[...] = jnp.zeros_like(qz_ref)
    for hh in range(2):
        @pl.when(h % 2 == hh)
        def _():
            qz_ref[hh * 2 * d:hh * 2 * d + d, 0:tq] = qT_ref[0, 0:d, :]
            qz_ref[hh * 2 * d + d:(hh + 1) * 2 * d, tq:2 * tq] = qT_ref[0, d:2 * d, :]

    def k_tile(j):
        return k_ref[0, pl.ds(pl.multiple_of(j * tk, tk), tk), :]

    def vT_tile(j):
        return vT_ref[0, :, pl.ds(pl.multiple_of(j * tk, tk), tk)]

    def softmax(s_ref, p_ref, j, tail):
        alphas = []
        for cols in col_groups:
            s = s_ref[:, cols]
            if tail:
                kpos = j * tk + lax.broadcasted_iota(jnp.int32, (tk, tq), 0)
                qpos = qi * tq + lax.broadcasted_iota(jnp.int32, (tk, tq), 1)
                s = jnp.where(kpos <= qpos, s, NEG)
            alphas.append(_softmax_cols(s, m_ref, p_ref, cols))
        return alphas

    _pipelined_attention(k_tile, vT_tile, qz_ref, (sa_ref, sb_ref), (pa_ref, pb_ref), m_ref, acc_ref, softmax,
                         first=0, n_plain_pairs=jd // 2, n_tail=2, max_tile=S // tk - 1, col_groups=col_groups)

    lp = lam_ref[...]
    lam = (jnp.exp(jnp.sum(lp[0:1] * lp[1:2], axis=1, keepdims=True))
           - jnp.exp(jnp.sum(lp[2:3] * lp[3:4], axis=1, keepdims=True)) + lam_init)
    o1 = acc_ref[0:V_DIM, 0:tq] / acc_ref[V_DIM:V_DIM + 1, 0:tq]
    o2 = acc_ref[0:V_DIM, tq:2 * tq] / acc_ref[V_DIM:V_DIM + 1, tq:2 * tq]
    o = o1 - lam * o2
    y = o * lax.rsqrt(jnp.mean(o * o, axis=0, keepdims=True) + EPS) * gain_ref[...]
    o_ref[0] = (y * (1.0 - lam_init)).astype(o_ref.dtype)


def _diff_attn(da_lambda, daqT, dak, davT, gain_col, lam_init, tq=256, tk=256):
    B, _, S = daqT.shape
    return pl.pallas_call(
        functools.partial(_da_kernel, tq=tq, tk=tk, lam_init=lam_init, S=S),
        grid=(B, DA_HEADS, S // tq),
        in_specs=[pl.BlockSpec((4, DA_QK_DIM), lambda b, h, i: (0, 0)),
                  pl.BlockSpec((1, DA_V_DIM, tq), lambda b, h, i: (b, h, i)),
                  pl.BlockSpec((1, S, LANES), lambda b, h, i: (b, 0, h // 2)),
                  pl.BlockSpec((1, V_EXT, S), lambda b, h, i: (b, h, 0)),
                  pl.BlockSpec((DA_V_DIM, 1), lambda b, h, i: (h, 0))],
        out_specs=pl.BlockSpec((1, DA_V_DIM, tq), lambda b, h, i: (b, h, i)),
        out_shape=jax.ShapeDtypeStruct((B, DA_WIDTH, S), BF16),
        scratch_shapes=[pltpu.VMEM((LANES, 2 * tq), BF16),
                        pltpu.VMEM((tk, 2 * tq), F32),
                        pltpu.VMEM((tk, 2 * tq), F32),
                        pltpu.VMEM((tk, 2 * tq), BF16),
                        pltpu.VMEM((tk, 2 * tq), BF16),
                        pltpu.VMEM((1, 2 * tq), F32),
                        pltpu.VMEM((V_EXT, 2 * tq), F32)],
        compiler_params=_cparams(("parallel", "parallel", "arbitrary")),
        name="diff_attn",
    )(da_lambda, daqT, dak, davT, gain_col)


def _gelu_tanh(x):
    return x * (0.5 * (1.0 + jnp.tanh(math.sqrt(2.0 / math.pi) * (x + 0.044715 * (x * x * x)))))


def _compress_kernel(x_ref, pe_ref, w1_ref, w2_ref, o_ref, b_ref, *, feature_major):
    n = x_ref.shape[2]
    half = CMP_STRIDE * NSA_DIM
    x = x_ref[0, 0]
    a = jnp.dot((x + pe_ref[0, 0:1, :]).astype(BF16), w1_ref[0, 0:half, :], preferred_element_type=F32)
    b_ref[0:n, :] = jnp.dot((x + pe_ref[0, 1:2, :]).astype(BF16), w1_ref[0, half:2 * half, :],
                            preferred_element_type=F32)
    b_ref[n:n + 8, :] = jnp.zeros((8, CMP_HIDDEN), F32)
    hid = _gelu_tanh(a + b_ref[1:n + 1, :]).astype(BF16)
    if feature_major:
        o_ref[0, 0] = lax.dot_general(w2_ref[0], hid, _NT, preferred_element_type=F32).astype(o_ref.dtype)
    else:
        o_ref[0, 0] = jnp.dot(hid, w2_ref[0], preferred_element_type=F32).astype(o_ref.dtype)


def _compress(x16, pe2, w1, w2, which, feature_major):
    B, _, n, half = x16.shape
    G = NSA_GROUPS
    if feature_major:
        out_shape, out_block = (B, G, NSA_DIM, n), (1, 1, NSA_DIM, n)
        w2_arr, w2_block = jnp.swapaxes(w2, 1, 2), (1, NSA_DIM, CMP_HIDDEN)
    else:
        out_shape, out_block = (B, G, n, NSA_DIM), (1, 1, n, NSA_DIM)
        w2_arr, w2_block = w2, (1, CMP_HIDDEN, NSA_DIM)
    return pl.pallas_call(
        functools.partial(_compress_kernel, feature_major=feature_major),
        grid=(B, G),
        in_specs=[pl.BlockSpec((1, 1, n, half), lambda b, g: (b, which * G + g, 0, 0)),
                  pl.BlockSpec((1, 2, half), lambda b, g: (which, 0, 0)),
                  pl.BlockSpec((1, 2 * half, CMP_HIDDEN), lambda b, g: (which, 0, 0)),
                  pl.BlockSpec(w2_block, lambda b, g: (which, 0, 0))],
        out_specs=pl.BlockSpec(out_block, lambda b, g: (b, g, 0, 0)),
        out_shape=jax.ShapeDtypeStruct(out_shape, BF16),
        scratch_shapes=[pltpu.VMEM((n + 8, CMP_HIDDEN), F32)],
        compiler_params=_cparams(("parallel", "parallel")),
        name="compress_v" if feature_major else "compress_k",
    )(x16, pe2, w1, w2_arr.astype(BF16))


def _nsa_kernel(qT_ref, gT_ref, kc_ref, vcT_ref, ks_ref, vsT_ref, kw_ref, vwT_ref, o_ref,
                qg_ref, qz_ref, imp_ref, sel_ref, pc_ref, sa_ref, sb_ref, pa_ref, pb_ref, m_ref, acc_ref, out_ref,
                *, tq, tk, S):
    g = pl.program_id(1)
    qi = pl.program_id(2)
    d = NSA_DIM
    R = NSA_REP
    ncb = S // CMP_STRIDE
    nsb = S // SEL_BLOCK
    q0 = qi * tq
    jd = q0 // tk
    ratio = SEL_BLOCK // CMP_STRIDE
    per_tile = tk // SEL_BLOCK

    for r in range(R):
        qg_ref[:, r * tq:(r + 1) * tq] = qT_ref[0, r * d:(r + 1) * d, :]
    qz_ref[...] = jnp.zeros_like(qz_ref)
    for gg in range(NSA_GROUPS):
        @pl.when(g == gg)
        def _():
            qz_ref[gg * d:(gg + 1) * d, :] = qg_ref[...]

    def gate(r, br):
        return _sigmoid(gT_ref[0, r * 3 + br:r * 3 + br + 1, :])

    qpos = q0 + lax.broadcasted_iota(jnp.int32, (1, tq), 1)

    s_all = jnp.dot(kc_ref[0, 0], qg_ref[...], preferred_element_type=F32)
    cend = lax.broadcasted_iota(jnp.int32, (ncb, tq), 0) * CMP_STRIDE + (CMP_BLOCK - 1)
    cvalid = cend <= qpos
    imp = jnp.zeros((ncb, tq), F32)
    for r in range(R):
        cols = slice(r * tq, (r + 1) * tq)
        s = s_all[:, cols]
        mx = jnp.max(jnp.where(cvalid, s, NEG), axis=0, keepdims=True)
        p = jnp.exp2(jnp.where(cvalid, s - mx, NEG))
        den = jnp.sum(p, axis=0, keepdims=True)
        p = p / jnp.where(den > 0, den, 1.0)
        imp = imp + p
        pc_ref[:, cols] = p.astype(BF16)
    o_cmp = jnp.dot(vcT_ref[0, 0], pc_ref[...], preferred_element_type=F32)
    for r in range(R):
        cols = slice(r * tq, (r + 1) * tq)
        out_ref[:, cols] = gate(r, 0) * o_cmp[:, cols]

    slabs = []
    for c in range(tq // LANES):
        imp_ref[c, 0:8, :] = jnp.zeros((8, LANES), F32)
        imp_ref[c, 8:8 + ncb, :] = imp[:, c * LANES:(c + 1) * LANES]
        imp_ref[c, 8 + ncb:16 + ncb, :] = jnp.zeros((8, LANES), F32)
        slab = jnp.zeros((nsb, LANES), F32)
        for o in range(-1, ratio):
            slab = slab + imp_ref[c, pl.ds(8 + o, nsb, stride=ratio), :]
        slabs.append(slab)
    p_slc = slabs[0] if len(slabs) == 1 else jnp.concatenate(slabs, axis=1)
    blk = lax.broadcasted_iota(jnp.int32, (nsb, tq), 0)
    cur = lax.shift_right_logical(qpos, int(math.log2(SEL_BLOCK)))
    forced = (blk == 0) | (blk == cur) | (blk == cur - 1)
    causal_blk = blk * SEL_BLOCK <= qpos
    score = jnp.where(forced, FORCE_SCORE, jnp.where(causal_blk, p_slc, -1.0))
    blk_f = blk.astype(F32)
    for _ in range(SEL_TOPK):
        mx = jnp.max(score, axis=0, keepdims=True)
        first = jnp.min(jnp.where(score == mx, blk_f, float(nsb)), axis=0, keepdims=True)
        score = jnp.where(blk_f == first, TAKEN, score)
    sel_ref[...] = jnp.where(score == TAKEN, 0.0, NEG)

    col_groups = [slice(r * tq, (r + 1) * tq) for r in range(R)]
    max_tile = S // tk - 1

    def tiles_of(k_ref_, vT_ref_):
        def k_tile(j):
            return k_ref_[0, pl.ds(pl.multiple_of(j * tk, tk), tk), :]

        def vT_tile(j):
            return vT_ref_[0, :, pl.ds(pl.multiple_of(j * tk, tk), tk)]
        return k_tile, vT_tile

    def softmax_with(bias_of):
        def softmax(s_ref, p_ref, j, tail):
            bias = bias_of(j, tail)
            return [_softmax_cols(s_ref[:, cols] + bias, m_ref, p_ref, cols) for cols in col_groups]
        return softmax

    def flush(br):
        for r, cols in enumerate(col_groups):
            den = acc_ref[V_DIM:V_DIM + 1, cols]
            out_ref[:, cols] += gate(r, br) * (acc_ref[0:V_DIM, cols] / jnp.where(den > 0, den, 1.0))

    def kpos_of(j):
        return j * tk + lax.broadcasted_iota(jnp.int32, (tk, tq), 0)

    def win_bias(j, tail):
        kpos = kpos_of(j)
        rel = qpos - kpos
        return jnp.where((rel >= 0) & (rel < WINDOW) & (kpos >= 0), 0.0, NEG)

    n_win = WINDOW // tk + 1
    _pipelined_attention(*tiles_of(kw_ref, vwT_ref), qz_ref, (sa_ref, sb_ref), (pa_ref, pb_ref), m_ref, acc_ref,
                         softmax_with(win_bias), first=jd - (n_win - 1), n_plain_pairs=0, n_tail=n_win,
                         max_tile=max_tile, col_groups=col_groups)
    flush(2)

    def sel_bias(j, tail):
        first_blk = jnp.clip(j, 0, max_tile) * per_tile
        rows = [jnp.broadcast_to(sel_ref[pl.ds(first_blk + i, 1), :], (SEL_BLOCK, tq)) for i in range(per_tile)]
        bias = jnp.concatenate(rows, axis=0)
        if tail:
            bias = jnp.where(kpos_of(j) <= qpos, bias, NEG)
        return bias

    _pipelined_attention(*tiles_of(ks_ref, vsT_ref), qz_ref, (sa_ref, sb_ref), (pa_ref, pb_ref), m_ref, acc_ref,
                         softmax_with(sel_bias), first=0, n_plain_pairs=jd // 2, n_tail=2,
                         max_tile=max_tile, col_groups=col_groups)
    flush(1)

    for r in range(R):
        o_ref[0, r * d:(r + 1) * d, :] = out_ref[:, r * tq:(r + 1) * tq].astype(o_ref.dtype)


def _nsa(nsqT, nsgT, kcmp, vcmpT, nsk, nsvT, tq=128, tk=256):
    B, _, S = nsqT.shape
    G, R, d = NSA_GROUPS, NSA_REP, NSA_DIM
    ncb = S // CMP_STRIDE
    nsb = S // SEL_BLOCK
    return pl.pallas_call(
        functools.partial(_nsa_kernel, tq=tq, tk=tk, S=S),
        grid=(B, G, S // tq),
        in_specs=[pl.BlockSpec((1, R * d, tq), lambda b, g, i: (b, g, i)),
                  pl.BlockSpec((1, GATE_PAD, tq), lambda b, g, i: (b, g, i)),
                  pl.BlockSpec((1, 1, ncb, d), lambda b, g, i: (b, g, 0, 0)),
                  pl.BlockSpec((1, 1, d, ncb), lambda b, g, i: (b, g, 0, 0)),
                  pl.BlockSpec((1, S, LANES), lambda b, g, i: (b, 0, 0)),
                  pl.BlockSpec((1, V_EXT, S), lambda b, g, i: (b, g, 0)),
                  pl.BlockSpec((1, S, LANES), lambda b, g, i: (b, 0, 1)),
                  pl.BlockSpec((1, V_EXT, S), lambda b, g, i: (b, G + g, 0))],
        out_specs=pl.BlockSpec((1, R * d, tq), lambda b, g, i: (b, g, i)),
        out_shape=jax.ShapeDtypeStruct((B, NSA_WIDTH, S), BF16),
        scratch_shapes=[pltpu.VMEM((d, R * tq), BF16),
                        pltpu.VMEM((LANES, R * tq), BF16),
                        pltpu.VMEM((tq // LANES, ncb + 16, LANES), F32),
                        pltpu.VMEM((nsb, tq), F32),
                        pltpu.VMEM((ncb, R * tq), BF16),
                        pltpu.VMEM((tk, R * tq), F32),
                        pltpu.VMEM((tk, R * tq), F32),
                        pltpu.VMEM((tk, R * tq), BF16),
                        pltpu.VMEM((tk, R * tq), BF16),
                        pltpu.VMEM((1, R * tq), F32),
                        pltpu.VMEM((V_EXT, R * tq), F32),
                        pltpu.VMEM((d, R * tq), F32)],
        compiler_params=_cparams(("parallel", "parallel", "arbitrary")),
        name="nsa",
    )(nsqT, nsgT, kcmp, vcmpT, nsk, nsvT, nsk, nsvT)


def _outproj_kernel(x_ref, yml_ref, ydaT_ref, ynsT_ref, wo_ref, o_ref):
    acc = x_ref[0] + jnp.dot(yml_ref[0].astype(BF16), wo_ref[0:ML_WIDTH, :], preferred_element_type=F32)
    acc = acc + lax.dot_general(ydaT_ref[0], wo_ref[ML_WIDTH:ML_WIDTH + DA_WIDTH, :], _TN,
                                preferred_element_type=F32)
    acc = acc + lax.dot_general(ynsT_ref[0], wo_ref[ML_WIDTH + DA_WIDTH:, :], _TN,
                                preferred_element_type=F32)
    o_ref[0] = acc


def _outproj(x, yml, ydaT, ynsT, wo, tm=512):
    B, S, D = x.shape
    return pl.pallas_call(
        _outproj_kernel,
        grid=(B, S // tm),
        in_specs=[pl.BlockSpec((1, tm, D), lambda b, i: (b, i, 0)),
                  pl.BlockSpec((1, tm, ML_WIDTH), lambda b, i: (b, i, 0)),
                  pl.BlockSpec((1, DA_WIDTH, tm), lambda b, i: (b, 0, i)),
                  pl.BlockSpec((1, NSA_WIDTH, tm), lambda b, i: (b, 0, i)),
                  pl.BlockSpec((D, D), lambda b, i: (0, 0))],
        out_specs=pl.BlockSpec((1, tm, D), lambda b, i: (b, i, 0)),
        out_shape=jax.ShapeDtypeStruct((B, S, D), F32),
        compiler_params=_cparams(("parallel", "parallel")),
        name="outproj",
    )(x, yml, ydaT, ynsT, wo)


def _ffn_kernel(x_ref, g_ref, w1_ref, w2_ref, gf_ref, o_ref, hb_ref, acc_ref, *, final):
    j = pl.program_id(1)

    @pl.when(j == 0)
    def _():
        x = x_ref[...]
        hb_ref[...] = _rms(x, g_ref[...]).astype(BF16)
        acc_ref[...] = x

    u = jnp.dot(hb_ref[...], w1_ref[...], preferred_element_type=F32)
    a = jnp.square(jnp.maximum(u, 0.0)).astype(BF16)
    acc_ref[...] += jnp.dot(a, w2_ref[...], preferred_element_type=F32)

    @pl.when(j == pl.num_programs(1) - 1)
    def _():
        y = acc_ref[...]
        if final:
            y = _rms(y, gf_ref[...])
        o_ref[...] = y


def _ffn(x2d, g, w1, w2, gf, final, tm=1024, tf=512):
    N, D = x2d.shape
    return pl.pallas_call(
        functools.partial(_ffn_kernel, final=final),
        grid=(N // tm, D_FF // tf),
        in_specs=[pl.BlockSpec((tm, D), lambda i, j: (i, 0)),
                  pl.BlockSpec((1, D), lambda i, j: (0, 0)),
                  pl.BlockSpec((D, tf), lambda i, j: (0, j)),
                  pl.BlockSpec((tf, D), lambda i, j: (j, 0)),
                  pl.BlockSpec((1, D), lambda i, j: (0, 0))],
        out_specs=pl.BlockSpec((tm, D), lambda i, j: (i, 0)),
        out_shape=jax.ShapeDtypeStruct((N, D), F32),
        scratch_shapes=[pltpu.VMEM((tm, D), BF16), pltpu.VMEM((tm, D), F32)],
        compiler_params=_cparams(("parallel", "arbitrary")),
        name="ffn",
    )(x2d, g, w1, w2, gf)


def _split_w_in(w_in_l):
    edges = np.concatenate([[0], np.cumsum(IN_SIZES)])
    return [w_in_l[:, int(edges[i]):int(edges[i + 1])] for i in range(len(IN_SIZES))]


def _pad_cols(w, n):
    return jnp.pad(w, ((0, 0), (0, n - w.shape[1])))


def _inproj_weights(w_in_l):
    (ml_q, ml_k, ml_v, ml_o, ml_i, ml_f, da_q, da_k, da_v,
     ns_q, ns_kc, ns_vc, ns_ks, ns_vs, ns_kw, ns_vw, ns_g) = _split_w_in(w_in_l)
    wt = jnp.concatenate([ml_q, ml_k, ml_v, ml_o, _pad_cols(jnp.concatenate([ml_i, ml_f], 1), LANES),
                          da_k, ns_ks, ns_kw, ns_kc, ns_vc], axis=1)
    per_group = NSA_REP * 3
    ns_g_pad = jnp.concatenate(
        [_pad_cols(ns_g[:, gi * per_group:(gi + 1) * per_group], GATE_PAD) for gi in range(NSA_GROUPS)], axis=1)
    wf = jnp.concatenate([da_q, da_v, ns_q, ns_vs, ns_vw, ns_g_pad], axis=1)
    return wt.astype(BF16), wf.T.astype(BF16)


def _half_blocks(nsc):
    B, S, _ = nsc.shape
    n = S // CMP_STRIDE
    t = nsc.reshape(B, n, CMP_STRIDE, 2 * NSA_GROUPS, NSA_DIM)
    return t.transpose(0, 3, 1, 2, 4).reshape(B, 2 * NSA_GROUPS, n, CMP_STRIDE * NSA_DIM)


def kernel(x, norm1, w_in, ml_conv, ml_gate_bias, ml_norm, da_lambda, da_norm, nsa_pe,
           nsa_w1, nsa_w2, w_out, norm2, w_ff1, w_ff2, final_norm):
    B, S, D = x.shape
    depth = norm1.shape[0]
    for l in range(depth):
        wt, wfT = _inproj_weights(w_in[l])
        (zml, zg, dak, nsk, nsc, daqT, davT, nsqT, nsvT, nsgT) = _inproj(x, norm1[l][None, :], wt, wfT)

        yml = _mlstm(zml, zg, ml_conv[l], _pad_cols(ml_gate_bias[l][None, :], LANES), ml_norm[l][None, :])

        lam_init = 0.8 - 0.6 * math.exp(-0.3 * l)
        ydaT = _diff_attn(da_lambda[l], daqT, dak, davT, da_norm[l][:, None], lam_init)

        x16 = _half_blocks(nsc)
        pe2 = nsa_pe[l].reshape(2, 2, CMP_STRIDE * NSA_DIM)
        w1b = nsa_w1[l].astype(BF16)
        kcmp = _compress(x16, pe2, w1b, nsa_w2[l], 0, False)
        vcmpT = _compress(x16, pe2, w1b, nsa_w2[l], 1, True)
        ynsT = _nsa(nsqT, nsgT, kcmp, vcmpT, nsk, nsvT)

        x = _outproj(x, yml, ydaT, ynsT, w_out[l].astype(BF16))
        x = _ffn(x.reshape(B * S, D), norm2[l][None, :], w_ff1[l].astype(BF16), w_ff2[l].astype(BF16),
                 final_norm[None, :], final=(l == depth - 1)).reshape(B, S, D)
    return x
```

```python
import functools
import math

import numpy as np
import jax
import jax.numpy as jnp
from jax import lax
from jax.experimental import pallas as pl
from jax.experimental.pallas import tpu as pltpu

F32 = jnp.float32
BF16 = jnp.bfloat16

D_MODEL = 1024
ML_HEADS = 4
ML_DIM = 64
ML_WIDTH = ML_HEADS * ML_DIM
ML_TILE_CHUNK = 128
CONV_W = 4
DA_HEADS = 4
DA_QK_DIM = 32
DA_V_DIM = 64
DA_WIDTH = DA_HEADS * DA_V_DIM
NSA_HEADS = 8
NSA_GROUPS = 2
NSA_REP = NSA_HEADS // NSA_GROUPS
NSA_DIM = 64
NSA_WIDTH = NSA_HEADS * NSA_DIM
NSA_KV = NSA_GROUPS * NSA_DIM
CMP_BLOCK = 32
CMP_STRIDE = 16
CMP_HIDDEN = 4 * NSA_DIM
SEL_BLOCK = 64
SEL_TOPK = 16
WINDOW = 512
D_FF = 4 * D_MODEL
EPS = 1e-6
FORCE_SCORE = 1e4
IN_SIZES = (ML_WIDTH, ML_WIDTH, ML_WIDTH, ML_WIDTH, ML_HEADS, ML_HEADS,
            2 * DA_HEADS * DA_QK_DIM, 2 * DA_HEADS * DA_QK_DIM, DA_WIDTH,
            NSA_WIDTH, NSA_KV, NSA_KV, NSA_KV, NSA_KV, NSA_KV, NSA_KV, 3 * NSA_HEADS)

LANES = 128
GATE_PAD = 16
NEG = -1e30
M_INIT = -1e29
LOOP_PAIRS = 4
TAKEN = -3e38
LOG2E = 1.4426950408889634
V_DIM = 64
V_EXT = 80
VMEM_LIMIT = 56 * 1024 * 1024

_T_ML = (0, 1024)
_T_MLG = (1024, 1152)
_T_DAK = (1152, 1408)
_T_NSK = (1408, 1664)
_T_NSC = (1664, 1920)
_T_COLS = 1920
_F_DAQ = (0, 256)
_F_DAV = (256, 512)
_F_NSQ = (512, 1024)
_F_NSV = (1024, 1280)
_F_NSG = (1280, 1280 + NSA_GROUPS * GATE_PAD)
_F_ROWS = _F_NSG[1]

_NT = (((1,), (1,)), ((), ()))
_TN = (((0,), (0,)), ((), ()))


def _cparams(sem):
    return pltpu.CompilerParams(dimension_semantics=sem, vmem_limit_bytes=VMEM_LIMIT)


def _rms(x, g):
    return x * lax.rsqrt(jnp.mean(x * x, axis=-1, keepdims=True) + EPS) * g


def _inproj_kernel(x_ref, g_ref, wt_ref, wf_ref,
                   zml_ref, zg_ref, dak_ref, nsk_ref, nsc_ref,
                   daqT_ref, davT_ref, nsqT_ref, nsvT_ref, nsgT_ref):
    hb = _rms(x_ref[0], g_ref[...]).astype(BF16)

    def tdot(span):
        return jnp.dot(hb, wt_ref[:, span[0]:span[1]], preferred_element_type=F32)

    def fdot(span):
        return lax.dot_general(wf_ref[span[0]:span[1], :], hb, _NT, preferred_element_type=F32)

    zml_ref[0] = tdot(_T_ML)
    zg_ref[0] = tdot(_T_MLG)
    dak_ref[0] = tdot(_T_DAK).astype(BF16)
    nsk_ref[0] = tdot(_T_NSK).astype(BF16)
    nsc_ref[0] = tdot(_T_NSC)
    daqT_ref[0] = (fdot(_F_DAQ) * (DA_QK_DIM ** -0.5 * LOG2E)).astype(BF16)
    nsqT_ref[0] = (fdot(_F_NSQ) * (NSA_DIM ** -0.5 * LOG2E)).astype(BF16)
    nsgT_ref[0] = fdot(_F_NSG)
    tm = hb.shape[0]
    for v_ref, span in ((davT_ref, _F_DAV), (nsvT_ref, _F_NSV)):
        v = fdot(span).astype(BF16)
        for h in range((span[1] - span[0]) // V_DIM):
            v_ref[0, h * V_EXT:h * V_EXT + V_DIM, :] = v[h * V_DIM:(h + 1) * V_DIM, :]
            v_ref[0, h * V_EXT + V_DIM:(h + 1) * V_EXT, :] = jnp.ones((V_EXT - V_DIM, tm), BF16)


def _inproj(x, g, wt, wfT, tm=512):
    B, S, D = x.shape
    tok = lambda w, dt: jax.ShapeDtypeStruct((B, S, w), dt)
    feat = lambda r, dt: jax.ShapeDtypeStruct((B, r, S), dt)
    tspec = lambda w: pl.BlockSpec((1, tm, w), lambda b, i: (b, i, 0))
    fspec = lambda r: pl.BlockSpec((1, r, tm), lambda b, i: (b, 0, i))
    return pl.pallas_call(
        _inproj_kernel,
        grid=(B, S // tm),
        in_specs=[tspec(D),
                  pl.BlockSpec((1, D), lambda b, i: (0, 0)),
                  pl.BlockSpec((D, _T_COLS), lambda b, i: (0, 0)),
                  pl.BlockSpec((_F_ROWS, D), lambda b, i: (0, 0))],
        out_specs=[tspec(1024), tspec(128), tspec(256), tspec(256), tspec(256),
                   fspec(256), fspec(4 * V_EXT), fspec(512), fspec(4 * V_EXT), fspec(NSA_GROUPS * GATE_PAD)],
        out_shape=[tok(1024, F32), tok(128, F32), tok(256, BF16), tok(256, BF16), tok(256, F32),
                   feat(256, BF16), feat(4 * V_EXT, BF16), feat(512, BF16), feat(4 * V_EXT, BF16),
                   feat(NSA_GROUPS * GATE_PAD, F32)],
        compiler_params=_cparams(("parallel", "parallel")),
        name="inproj",
    )(x, g, wt, wfT)


def _log_sigmoid(x):
    return jnp.minimum(x, 0.0) - jnp.log1p(jnp.exp(-jnp.abs(x)))


def _sigmoid(x):
    return 1.0 / (1.0 + jnp.exp(-x))


def _mlstm_kernel(zml_ref, zg_ref, cw_ref, gb_ref, nrm_ref, y_ref,
                  pad_ref, c_ref, n_ref, m_ref, *, T, L):
    d = ML_DIM
    W = max(L, d)
    t = pl.program_id(1)

    @pl.when(t == 0)
    def _():
        pad_ref[0:8, :] = jnp.zeros((8, 2 * ML_WIDTH), F32)
        c_ref[...] = jnp.zeros_like(c_ref)
        n_ref[...] = jnp.zeros_like(n_ref)
        m_ref[...] = jnp.zeros_like(m_ref)

    @pl.when(t > 0)
    def _():
        pad_ref[0:8, :] = pad_ref[T:T + 8, :]

    pad_ref[8:8 + T, :] = zml_ref[0, :, 0:2 * ML_WIDTH]
    conv = cw_ref[0:1, :] * pad_ref[5:5 + T, :]
    for j in range(1, CONV_W):
        conv = conv + cw_ref[j:j + 1, :] * pad_ref[5 + j:5 + j + T, :]
    qk = conv * _sigmoid(conv)
    gates = zg_ref[0] + gb_ref[...]
    logf = _log_sigmoid(gates)

    row = lax.broadcasted_iota(jnp.int32, (L, L), 0)
    col = lax.broadcasted_iota(jnp.int32, (L, L), 1)
    causal = col <= row
    tril = causal.astype(F32)
    triu = (row <= col).astype(F32)

    lane_of = lax.broadcasted_iota(jnp.int32, (LANES, W), 0)
    pick = [(lane_of == g).astype(F32) for g in range(2 * ML_HEADS)]

    for c in range(T // L):
        r0 = c * L
        g_c = gates[r0:r0 + L, :]
        lf_c = logf[r0:r0 + L, :]
        b_cols = jnp.dot(tril, lf_c, preferred_element_type=F32,
                         precision=lax.Precision.HIGHEST)
        b_rows = jnp.dot(lf_c.T, triu, preferred_element_type=F32,
                         precision=lax.Precision.HIGHEST)
        g_rows = g_c.T
        for h in range(ML_HEADS):
            b_col = jnp.dot(b_cols, pick[ML_HEADS + h], preferred_element_type=F32,
                            precision=lax.Precision.HIGHEST)
            ig_col = jnp.dot(g_c, pick[h], preferred_element_type=F32,
                             precision=lax.Precision.HIGHEST)
            b_row = b_rows[ML_HEADS + h:ML_HEADS + h + 1, :]
            ig_row = g_rows[h:h + 1, :]
            qh = qk[r0:r0 + L, h * d:(h + 1) * d]
            kh = qk[r0:r0 + L, ML_WIDTH + h * d:ML_WIDTH + (h + 1) * d] * (d ** -0.5)
            vh = zml_ref[0, r0:r0 + L, 2 * ML_WIDTH + h * d:2 * ML_WIDTH + (h + 1) * d]
            oh = zml_ref[0, r0:r0 + L, 3 * ML_WIDTH + h * d:3 * ML_WIDTH + (h + 1) * d]
            c_prev = c_ref[h]
            n_prev = n_ref[h, 0:1, :]
            m_prev = m_ref[h, 0:1, :]

            g_tot = b_col[L - 1:L, :]
            a_row = g_tot[:, 0:L] - b_row + ig_row
            a_max = jnp.max(a_row, axis=1, keepdims=True)
            w_col = jnp.exp((g_tot - b_col + ig_col - a_max)[:, 0:d])
            c_loc = lax.dot_general(vh * w_col, kh, _TN, preferred_element_type=F32)
            n_loc = jnp.sum(kh * w_col, axis=0, keepdims=True)

            dmat = jnp.where(causal, b_col[:, 0:L] - b_row + ig_row, NEG)
            inter_log = b_col + m_prev
            m_t = jnp.maximum(inter_log, jnp.max(dmat, axis=1, keepdims=True))
            qkt = lax.dot_general(qh, kh, _NT, preferred_element_type=F32)
            wts = jnp.exp(dmat - m_t[:, 0:L]) * qkt
            s_inter = jnp.exp((inter_log - m_t)[:, 0:d])
            num = (jnp.dot(wts, vh, preferred_element_type=F32)
                   + s_inter * lax.dot_general(qh, c_prev, _NT, preferred_element_type=F32))
            den = (jnp.sum(wts, axis=1, keepdims=True)
                   + s_inter * jnp.sum(qh * n_prev, axis=1, keepdims=True))
            hh = num / jnp.maximum(jnp.abs(den), jnp.exp(-m_t[:, 0:d]))

            m_new = jnp.maximum(g_tot + m_prev, a_max)
            s_prev = jnp.exp((g_tot + m_prev - m_new)[:, 0:d])
            s_loc = jnp.exp((a_max - m_new)[:, 0:d])
            c_ref[h] = s_prev * c_prev + s_loc * c_loc
            n_ref[h] = jnp.broadcast_to(s_prev * n_prev + s_loc * n_loc, (8, d))
            m_ref[h] = jnp.broadcast_to(m_new, (8, W))

            yh = _sigmoid(oh) * hh
            y_ref[0, r0:r0 + L, h * d:(h + 1) * d] = _rms(yh, nrm_ref[0:1, h * d:(h + 1) * d])


def _mlstm(zml, zg, conv_w, gate_bias, norm_g, T=256, chunk=ML_TILE_CHUNK):
    B, S, _ = zml.shape
    return pl.pallas_call(
        functools.partial(_mlstm_kernel, T=T, L=chunk),
        grid=(B, S // T),
        in_specs=[pl.BlockSpec((1, T, 1024), lambda b, t: (b, t, 0)),
                  pl.BlockSpec((1, T, 128), lambda b, t: (b, t, 0)),
                  pl.BlockSpec((CONV_W, 2 * ML_WIDTH), lambda b, t: (0, 0)),
                  pl.BlockSpec((1, 128), lambda b, t: (0, 0)),
                  pl.BlockSpec((1, ML_WIDTH), lambda b, t: (0, 0))],
        out_specs=pl.BlockSpec((1, T, ML_WIDTH), lambda b, t: (b, t, 0)),
        out_shape=jax.ShapeDtypeStruct((B, S, ML_WIDTH), F32),
        scratch_shapes=[pltpu.VMEM((T + 8, 2 * ML_WIDTH), F32),
                        pltpu.VMEM((ML_HEADS, ML_DIM, ML_DIM), F32),
                        pltpu.VMEM((ML_HEADS, 8, ML_DIM), F32),
                        pltpu.VMEM((ML_HEADS, 8, max(chunk, ML_DIM)), F32)],
        compiler_params=_cparams(("parallel", "arbitrary")),
        name="mlstm",
    )(zml, zg, conv_w, gate_bias, norm_g)


def _softmax_cols(s, m_ref, p_ref, cols):
    m_old = m_ref[:, cols]
    m_new = jnp.maximum(m_old, jnp.max(s, axis=0, keepdims=True))
    m_ref[:, cols] = m_new
    p_ref[:, cols] = jnp.exp2(s - m_new).astype(BF16)
    return jnp.exp2(m_old - m_new)


def _pipelined_attention(k_tile, vT_tile, qz_ref, s_refs, p_refs, m_ref, acc_ref, softmax,
                         first, n_plain_pairs, n_tail, max_tile, col_groups):
    sa, sb = s_refs
    pa, pb = p_refs

    def load(j):
        return jnp.clip(j, 0, max_tile)

    def half(j, s_cur, s_nxt, p_cur, p_prev, tail):
        s_nxt[...] = jnp.dot(k_tile(load(j + 1)), qz_ref[...], preferred_element_type=F32)
        pv = jnp.dot(vT_tile(load(j - 1)), p_prev[...], preferred_element_type=F32)
        alphas = softmax(s_cur, p_cur, j, tail)
        for cols, alpha in zip(col_groups, alphas):
            acc_ref[:, cols] = alpha * (acc_ref[:, cols] + pv[:, cols])

    def pair(j, tail):
        half(j, sa, sb, pa, pb, tail)
        half(j + 1, sb, sa, pb, pa, tail)

    m_ref[...] = jnp.full_like(m_ref, M_INIT)
    acc_ref[...] = jnp.zeros_like(acc_ref)
    pb[...] = jnp.zeros_like(pb)
    sa[...] = jnp.dot(k_tile(load(first)), qz_ref[...], preferred_element_type=F32)

    if not (isinstance(n_plain_pairs, int) and n_plain_pairs == 0):
        n_trips = n_plain_pairs // LOOP_PAIRS

        def body(i, carry):
            for u in range(LOOP_PAIRS):
                pair(first + 2 * (LOOP_PAIRS * i + u), False)
            return carry

        lax.fori_loop(0, n_trips, body, 0)
        rest = n_plain_pairs - n_trips * LOOP_PAIRS
        for u in range(LOOP_PAIRS - 1):
            @pl.when(u < rest)
            def _():
                pair(first + 2 * (LOOP_PAIRS * n_trips + u), False)
    j = first + 2 * n_plain_pairs
    bufs = ((sa, sb, pa, pb), (sb, sa, pb, pa))
    for t in range(n_tail):
        half(j + t, *bufs[t % 2], True)
    p_last = bufs[(n_tail - 1) % 2][2]
    acc_ref[...] += jnp.dot(vT_tile(load(j + n_tail - 1)), p_last[...], preferred_element_type=F32)


def _da_kernel(lam_ref, qT_ref, k_ref, vT_ref, gain_ref, o_ref,
               qz_ref, sa_ref, sb_ref, pa_ref, pb_ref, m_ref, acc_ref, *, tq, tk, lam_init, S):
    h = pl.program_id(1)
    qi = pl.program_id(2)
    d = DA_QK_DIM
    jd = (qi * tq) // tk
    col_groups = [slice(mp * tq, (mp + 1) * tq) for mp in range(2)]

    qz_ref[...] = jnp.zeros_like(qz_ref)
    for hh in range(2):
        @pl.when(h % 2 == hh)
        def _():
            qz_ref[hh * 2 * d:hh * 2 * d + d, 0:tq] = qT_ref[0, 0:d, :]
            qz_ref[hh * 2 * d + d:(hh + 1) * 2 * d, tq:2 * tq] = qT_ref[0, d:2 * d, :]

    def k_tile(j):
        return k_ref[0, pl.ds(pl.multiple_of(j * tk, tk), tk), :]

    def vT_tile(j):
        return vT_ref[0, :, pl.ds(pl.multiple_of(j * tk, tk), tk)]

    def softmax(s_ref, p_ref, j, tail):
        alphas = []
        for cols in col_groups:
            s = s_ref[:, cols]
            if tail:
                kpos = j * tk + lax.broadcasted_iota(jnp.int32, (tk, tq), 0)
                qpos = qi * tq + lax.broadcasted_iota(jnp.int32, (tk, tq), 1)
                s = jnp.where(kpos <= qpos, s, NEG)
            alphas.append(_softmax_cols(s, m_ref, p_ref, cols))
        return alphas

    _pipelined_attention(k_tile, vT_tile, qz_ref, (sa_ref, sb_ref), (pa_ref, pb_ref), m_ref, acc_ref, softmax,
                         first=0, n_plain_pairs=jd // 2, n_tail=2, max_tile=S // tk - 1, col_groups=col_groups)

    lp = lam_ref[...]
    lam = (jnp.exp(jnp.sum(lp[0:1] * lp[1:2], axis=1, keepdims=True))
           - jnp.exp(jnp.sum(lp[2:3] * lp[3:4], axis=1, keepdims=True)) + lam_init)
    o1 = acc_ref[0:V_DIM, 0:tq] / acc_ref[V_DIM:V_DIM + 1, 0:tq]
    o2 = acc_ref[0:V_DIM, tq:2 * tq] / acc_ref[V_DIM:V_DIM + 1, tq:2 * tq]
    o = o1 - lam * o2
    y = o * lax.rsqrt(jnp.mean(o * o, axis=0, keepdims=True) + EPS) * gain_ref[...]
    o_ref[0] = (y * (1.0 - lam_init)).astype(o_ref.dtype)


def _diff_attn(da_lambda, daqT, dak, davT, gain_col, lam_init, tq=256, tk=256):
    B, _, S = daqT.shape
    return pl.pallas_call(
        functools.partial(_da_kernel, tq=tq, tk=tk, lam_init=lam_init, S=S),
        grid=(B, DA_HEADS, S // tq),
        in_specs=[pl.BlockSpec((4, DA_QK_DIM), lambda b, h, i: (0, 0)),
                  pl.BlockSpec((1, DA_V_DIM, tq), lambda b, h, i: (b, h, i)),
                  pl.BlockSpec((1, S, LANES), lambda b, h, i: (b, 0, h // 2)),
                  pl.BlockSpec((1, V_EXT, S), lambda b, h, i: (b, h, 0)),
                  pl.BlockSpec((DA_V_DIM, 1), lambda b, h, i: (h, 0))],
        out_specs=pl.BlockSpec((1, DA_V_DIM, tq), lambda b, h, i: (b, h, i)),
        out_shape=jax.ShapeDtypeStruct((B, DA_WIDTH, S), BF16),
        scratch_shapes=[pltpu.VMEM((LANES, 2 * tq), BF16),
                        pltpu.VMEM((tk, 2 * tq), F32),
                        pltpu.VMEM((tk, 2 * tq), F32),
                        pltpu.VMEM((tk, 2 * tq), BF16),
                        pltpu.VMEM((tk, 2 * tq), BF16),
                        pltpu.VMEM((1, 2 * tq), F32),
                        pltpu.VMEM((V_EXT, 2 * tq), F32)],
        compiler_params=_cparams(("parallel", "parallel", "arbitrary")),
        name="diff_attn",
    )(da_lambda, daqT, dak, davT, gain_col)


def _gelu_tanh(x):
    return x * (0.5 * (1.0 + jnp.tanh(math.sqrt(2.0 / math.pi) * (x + 0.044715 * (x * x * x)))))


def _compress_kernel(x_ref, pe_ref, w1_ref, w2_ref, o_ref, b_ref, *, feature_major):
    n = x_ref.shape[2]
    half = CMP_STRIDE * NSA_DIM
    x = x_ref[0, 0]
    a = jnp.dot((x + pe_ref[0, 0:1, :]).astype(BF16), w1_ref[0, 0:half, :], preferred_element_type=F32)
    b_ref[0:n, :] = jnp.dot((x + pe_ref[0, 1:2, :]).astype(BF16), w1_ref[0, half:2 * half, :],
                            preferred_element_type=F32)
    b_ref[n:n + 8, :] = jnp.zeros((8, CMP_HIDDEN), F32)
    hid = _gelu_tanh(a + b_ref[1:n + 1, :]).astype(BF16)
    if feature_major:
        o_ref[0, 0] = lax.dot_general(w2_ref[0], hid, _NT, preferred_element_type=F32).astype(o_ref.dtype)
    else:
        o_ref[0, 0] = jnp.dot(hid, w2_ref[0], preferred_element_type=F32).astype(o_ref.dtype)


def _compress(x16, pe2, w1, w2, which, feature_major):
    B, _, n, half = x16.shape
    G = NSA_GROUPS
    if feature_major:
        out_shape, out_block = (B, G, NSA_DIM, n), (1, 1, NSA_DIM, n)
        w2_arr, w2_block = jnp.swapaxes(w2, 1, 2), (1, NSA_DIM, CMP_HIDDEN)
    else:
        out_shape, out_block = (B, G, n, NSA_DIM), (1, 1, n, NSA_DIM)
        w2_arr, w2_block = w2, (1, CMP_HIDDEN, NSA_DIM)
    return pl.pallas_call(
        functools.partial(_compress_kernel, feature_major=feature_major),
        grid=(B, G),
        in_specs=[pl.BlockSpec((1, 1, n, half), lambda b, g: (b, which * G + g, 0, 0)),
                  pl.BlockSpec((1, 2, half), lambda b, g: (which, 0, 0)),
                  pl.BlockSpec((1, 2 * half, CMP_HIDDEN), lambda b, g: (which, 0, 0)),
                  pl.BlockSpec(w2_block, lambda b, g: (which, 0, 0))],
        out_specs=pl.BlockSpec(out_block, lambda b, g: (b, g, 0, 0)),
        out_shape=jax.ShapeDtypeStruct(out_shape, BF16),
        scratch_shapes=[pltpu.VMEM((n + 8, CMP_HIDDEN), F32)],
        compiler_params=_cparams(("parallel", "parallel")),
        name="compress_v" if feature_major else "compress_k",
    )(x16, pe2, w1, w2_arr.astype(BF16))


def _nsa_kernel(qT_ref, gT_ref, kc_ref, vcT_ref, ks_ref, vsT_ref, kw_ref, vwT_ref, o_ref,
                qg_ref, qz_ref, imp_ref, sel_ref, pc_ref, sa_ref, sb_ref, pa_ref, pb_ref, m_ref, acc_ref, out_ref,
                *, tq, tk, S):
    g = pl.program_id(1)
    qi = pl.program_id(2)
    d = NSA_DIM
    R = NSA_REP
    ncb = S // CMP_STRIDE
    nsb = S // SEL_BLOCK
    q0 = qi * tq
    jd = q0 // tk
    ratio = SEL_BLOCK // CMP_STRIDE
    per_tile = tk // SEL_BLOCK

    for r in range(R):
        qg_ref[:, r * tq:(r + 1) * tq] = qT_ref[0, r * d:(r + 1) * d, :]
    qz_ref[...] = jnp.zeros_like(qz_ref)
    for gg in range(NSA_GROUPS):
        @pl.when(g == gg)
        def _():
            qz_ref[gg * d:(gg + 1) * d, :] = qg_ref[...]

    def gate(r, br):
        return _sigmoid(gT_ref[0, r * 3 + br:r * 3 + br + 1, :])

    qpos = q0 + lax.broadcasted_iota(jnp.int32, (1, tq), 1)

    s_all = jnp.dot(kc_ref[0, 0], qg_ref[...], preferred_element_type=F32)
    cend = lax.broadcasted_iota(jnp.int32, (ncb, tq), 0) * CMP_STRIDE + (CMP_BLOCK - 1)
    cvalid = cend <= qpos
    imp = jnp.zeros((ncb, tq), F32)
    for r in range(R):
        cols = slice(r * tq, (r + 1) * tq)
        s = s_all[:, cols]
        mx = jnp.max(jnp.where(cvalid, s, NEG), axis=0, keepdims=True)
        p = jnp.exp2(jnp.where(cvalid, s - mx, NEG))
        den = jnp.sum(p, axis=0, keepdims=True)
        p = p / jnp.where(den > 0, den, 1.0)
        imp = imp + p
        pc_ref[:, cols] = p.astype(BF16)
    o_cmp = jnp.dot(vcT_ref[0, 0], pc_ref[...], preferred_element_type=F32)
    for r in range(R):
        cols = slice(r * tq, (r + 1) * tq)
        out_ref[:, cols] = gate(r, 0) * o_cmp[:, cols]

    slabs = []
    for c in range(tq // LANES):
        imp_ref[c, 0:8, :] = jnp.zeros((8, LANES), F32)
        imp_ref[c, 8:8 + ncb, :] = imp[:, c * LANES:(c + 1) * LANES]
        imp_ref[c, 8 + ncb:16 + ncb, :] = jnp.zeros((8, LANES), F32)
        slab = jnp.zeros((nsb, LANES), F32)
        for o in range(-1, ratio):
            slab = slab + imp_ref[c, pl.ds(8 + o, nsb, stride=ratio), :]
        slabs.append(slab)
    p_slc = slabs[0] if len(slabs) == 1 else jnp.concatenate(slabs, axis=1)
    blk = lax.broadcasted_iota(jnp.int32, (nsb, tq), 0)
    cur = lax.shift_right_logical(qpos, int(math.log2(SEL_BLOCK)))
    forced = (blk == 0) | (blk == cur) | (blk == cur - 1)
    causal_blk = blk * SEL_BLOCK <= qpos
    score = jnp.where(forced, FORCE_SCORE, jnp.where(causal_blk, p_slc, -1.0))
    blk_f = blk.astype(F32)
    for _ in range(SEL_TOPK):
        mx = jnp.max(score, axis=0, keepdims=True)
        first = jnp.min(jnp.where(score == mx, blk_f, float(nsb)), axis=0, keepdims=True)
        score = jnp.where(blk_f == first, TAKEN, score)
    sel_ref[...] = jnp.where(score == TAKEN, 0.0, NEG)

    col_groups = [slice(r * tq, (r + 1) * tq) for r in range(R)]
    max_tile = S // tk - 1

    def tiles_of(k_ref_, vT_ref_):
        def k_tile(j):
            return k_ref_[0, pl.ds(pl.multiple_of(j * tk, tk), tk), :]

        def vT_tile(j):
            return vT_ref_[0, :, pl.ds(pl.multiple_of(j * tk, tk), tk)]
        return k_tile, vT_tile

    def softmax_with(bias_of):
        def softmax(s_ref, p_ref, j, tail):
            bias = bias_of(j, tail)
            return [_softmax_cols(s_ref[:, cols] + bias, m_ref, p_ref, cols) for cols in col_groups]
        return softmax

    def flush(br):
        for r, cols in enumerate(col_groups):
            den = acc_ref[V_DIM:V_DIM + 1, cols]
            out_ref[:, cols] += gate(r, br) * (acc_ref[0:V_DIM, cols] / jnp.where(den > 0, den, 1.0))

    def kpos_of(j):
        return j * tk + lax.broadcasted_iota(jnp.int32, (tk, tq), 0)

    def win_bias(j, tail):
        kpos = kpos_of(j)
        rel = qpos - kpos
        return jnp.where((rel >= 0) & (rel < WINDOW) & (kpos >= 0), 0.0, NEG)

    n_win = WINDOW // tk + 1
    _pipelined_attention(*tiles_of(kw_ref, vwT_ref), qz_ref, (sa_ref, sb_ref), (pa_ref, pb_ref), m_ref, acc_ref,
                         softmax_with(win_bias), first=jd - (n_win - 1), n_plain_pairs=0, n_tail=n_win,
                         max_tile=max_tile, col_groups=col_groups)
    flush(2)

    def sel_bias(j, tail):
        first_blk = jnp.clip(j, 0, max_tile) * per_tile
        rows = [jnp.broadcast_to(sel_ref[pl.ds(first_blk + i, 1), :], (SEL_BLOCK, tq)) for i in range(per_tile)]
        bias = jnp.concatenate(rows, axis=0)
        if tail:
            bias = jnp.where(kpos_of(j) <= qpos, bias, NEG)
        return bias

    _pipelined_attention(*tiles_of(ks_ref, vsT_ref), qz_ref, (sa_ref, sb_ref), (pa_ref, pb_ref), m_ref, acc_ref,
                         softmax_with(sel_bias), first=0, n_plain_pairs=jd // 2, n_tail=2,
                         max_tile=max_tile, col_groups=col_groups)
    flush(1)

    for r in range(R):
        o_ref[0, r * d:(r + 1) * d, :] = out_ref[:, r * tq:(r + 1) * tq].astype(o_ref.dtype)


def _nsa(nsqT, nsgT, kcmp, vcmpT, nsk, nsvT, tq=128, tk=256):
    B, _, S = nsqT.shape
    G, R, d = NSA_GROUPS, NSA_REP, NSA_DIM
    ncb = S // CMP_STRIDE
    nsb = S // SEL_BLOCK
    return pl.pallas_call(
        functools.partial(_nsa_kernel, tq=tq, tk=tk, S=S),
        grid=(B, G, S // tq),
        in_specs=[pl.BlockSpec((1, R * d, tq), lambda b, g, i: (b, g, i)),
                  pl.BlockSpec((1, GATE_PAD, tq), lambda b, g, i: (b, g, i)),
                  pl.BlockSpec((1, 1, ncb, d), lambda b, g, i: (b, g, 0, 0)),
                  pl.BlockSpec((1, 1, d, ncb), lambda b, g, i: (b, g, 0, 0)),
                  pl.BlockSpec((1, S, LANES), lambda b, g, i: (b, 0, 0)),
                  pl.BlockSpec((1, V_EXT, S), lambda b, g, i: (b, g, 0)),
                  pl.BlockSpec((1, S, LANES), lambda b, g, i: (b, 0, 1)),
                  pl.BlockSpec((1, V_EXT, S), lambda b, g, i: (b, G + g, 0))],
        out_specs=pl.BlockSpec((1, R * d, tq), lambda b, g, i: (b, g, i)),
        out_shape=jax.ShapeDtypeStruct((B, NSA_WIDTH, S), BF16),
        scratch_shapes=[pltpu.VMEM((d, R * tq), BF16),
                        pltpu.VMEM((LANES, R * tq), BF16),
                        pltpu.VMEM((tq // LANES, ncb + 16, LANES), F32),
                        pltpu.VMEM((nsb, tq), F32),
                        pltpu.VMEM((ncb, R * tq), BF16),
                        pltpu.VMEM((tk, R * tq), F32),
                        pltpu.VMEM((tk, R * tq), F32),
                        pltpu.VMEM((tk, R * tq), BF16),
                        pltpu.VMEM((tk, R * tq), BF16),
                        pltpu.VMEM((1, R * tq), F32),
                        pltpu.VMEM((V_EXT, R * tq), F32),
                        pltpu.VMEM((d, R * tq), F32)],
        compiler_params=_cparams(("parallel", "parallel", "arbitrary")),
        name="nsa",
    )(nsqT, nsgT, kcmp, vcmpT, nsk, nsvT, nsk, nsvT)


def _outproj_kernel(x_ref, yml_ref, ydaT_ref, ynsT_ref, wo_ref, o_ref):
    acc = x_ref[0] + jnp.dot(yml_ref[0].astype(BF16), wo_ref[0:ML_WIDTH, :], preferred_element_type=F32)
    acc = acc + lax.dot_general(ydaT_ref[0], wo_ref[ML_WIDTH:ML_WIDTH + DA_WIDTH, :], _TN,
                                preferred_element_type=F32)
    acc = acc + lax.dot_general(ynsT_ref[0], wo_ref[ML_WIDTH + DA_WIDTH:, :], _TN,
                                preferred_element_type=F32)
    o_ref[0] = acc


def _outproj(x, yml, ydaT, ynsT, wo, tm=512):
    B, S, D = x.shape
    return pl.pallas_call(
        _outproj_kernel,
        grid=(B, S // tm),
        in_specs=[pl.BlockSpec((1, tm, D), lambda b, i: (b, i, 0)),
                  pl.BlockSpec((1, tm, ML_WIDTH), lambda b, i: (b, i, 0)),
                  pl.BlockSpec((1, DA_WIDTH, tm), lambda b, i: (b, 0, i)),
                  pl.BlockSpec((1, NSA_WIDTH, tm), lambda b, i: (b, 0, i)),
                  pl.BlockSpec((D, D), lambda b, i: (0, 0))],
        out_specs=pl.BlockSpec((1, tm, D), lambda b, i: (b, i, 0)),
        out_shape=jax.ShapeDtypeStruct((B, S, D), F32),
        compiler_params=_cparams(("parallel", "parallel")),
        name="outproj",
    )(x, yml, ydaT, ynsT, wo)


def _ffn_kernel(x_ref, g_ref, w1_ref, w2_ref, gf_ref, o_ref, hb_ref, acc_ref, *, final):
    j = pl.program_id(1)

    @pl.when(j == 0)
    def _():
        x = x_ref[...]
        hb_ref[...] = _rms(x, g_ref[...]).astype(BF16)
        acc_ref[...] = x

    u = jnp.dot(hb_ref[...], w1_ref[...], preferred_element_type=F32)
    a = jnp.square(jnp.maximum(u, 0.0)).astype(BF16)
    acc_ref[...] += jnp.dot(a, w2_ref[...], preferred_element_type=F32)

    @pl.when(j == pl.num_programs(1) - 1)
    def _():
        y = acc_ref[...]
        if final:
            y = _rms(y, gf_ref[...])
        o_ref[...] = y


def _ffn(x2d, g, w1, w2, gf, final, tm=1024, tf=512):
    N, D = x2d.shape
    return pl.pallas_call(
        functools.partial(_ffn_kernel, final=final),
        grid=(N // tm, D_FF // tf),
        in_specs=[pl.BlockSpec((tm, D), lambda i, j: (i, 0)),
                  pl.BlockSpec((1, D), lambda i, j: (0, 0)),
                  pl.BlockSpec((D, tf), lambda i, j: (0, j)),
                  pl.BlockSpec((tf, D), lambda i, j: (j, 0)),
                  pl.BlockSpec((1, D), lambda i, j: (0, 0))],
        out_specs=pl.BlockSpec((tm, D), lambda i, j: (i, 0)),
        out_shape=jax.ShapeDtypeStruct((N, D), F32),
        scratch_shapes=[pltpu.VMEM((tm, D), BF16), pltpu.VMEM((tm, D), F32)],
        compiler_params=_cparams(("parallel", "arbitrary")),
        name="ffn",
    )(x2d, g, w1, w2, gf)


def _split_w_in(w_in_l):
    edges = np.concatenate([[0], np.cumsum(IN_SIZES)])
    return [w_in_l[:, int(edges[i]):int(edges[i + 1])] for i in range(len(IN_SIZES))]


def _pad_cols(w, n):
    return jnp.pad(w, ((0, 0), (0, n - w.shape[1])))


def _inproj_weights(w_in_l):
    (ml_q, ml_k, ml_v, ml_o, ml_i, ml_f, da_q, da_k, da_v,
     ns_q, ns_kc, ns_vc, ns_ks, ns_vs, ns_kw, ns_vw, ns_g) = _split_w_in(w_in_l)
    wt = jnp.concatenate([ml_q, ml_k, ml_v, ml_o, _pad_cols(jnp.concatenate([ml_i, ml_f], 1), LANES),
                          da_k, ns_ks, ns_kw, ns_kc, ns_vc], axis=1)
    per_group = NSA_REP * 3
    ns_g_pad = jnp.concatenate(
        [_pad_cols(ns_g[:, gi * per_group:(gi + 1) * per_group], GATE_PAD) for gi in range(NSA_GROUPS)], axis=1)
    wf = jnp.concatenate([da_q, da_v, ns_q, ns_vs, ns_vw, ns_g_pad], axis=1)
    return wt.astype(BF16), wf.T.astype(BF16)


def _half_blocks(nsc):
    B, S, _ = nsc.shape
    n = S // CMP_STRIDE
    t = nsc.reshape(B, n, CMP_STRIDE, 2 * NSA_GROUPS, NSA_DIM)
    return t.transpose(0, 3, 1, 2, 4).reshape(B, 2 * NSA_GROUPS, n, CMP_STRIDE * NSA_DIM)


def kernel(x, norm1, w_in, ml_conv, ml_gate_bias, ml_norm, da_lambda, da_norm, nsa_pe,
           nsa_w1, nsa_w2, w_out, norm2, w_ff1, w_ff2, final_norm):
    B, S, D = x.shape
    depth = norm1.shape[0]
    for l in range(depth):
        wt, wfT = _inproj_weights(w_in[l])
        (zml, zg, dak, nsk, nsc, daqT, davT, nsqT, nsvT, nsgT) = _inproj(x, norm1[l][None, :], wt, wfT)

        yml = _mlstm(zml, zg, ml_conv[l], _pad_cols(ml_gate_bias[l][None, :], LANES), ml_norm[l][None, :])

        lam_init = 0.8 - 0.6 * math.exp(-0.3 * l)
        ydaT = _diff_attn(da_lambda[l], daqT, dak, davT, da_norm[l][:, None], lam_init)

        x16 = _half_blocks(nsc)
        pe2 = nsa_pe[l].reshape(2, 2, CMP_STRIDE * NSA_DIM)
        w1b = nsa_w1[l].astype(BF16)
        kcmp = _compress(x16, pe2, w1b, nsa_w2[l], 0, False)
        vcmpT = _compress(x16, pe2, w1b, nsa_w2[l], 1, True)
        ynsT = _nsa(nsqT, nsgT, kcmp, vcmpT, nsk, nsvT)

        x = _outproj(x, yml, ydaT, ynsT, w_out[l].astype(BF16))
        x = _ffn(x.reshape(B * S, D), norm2[l][None, :], w_ff1[l].astype(BF16), w_ff2[l].astype(BF16),
                 final_norm[None, :], final=(l == depth - 1)).reshape(B, S, D)
    return x
```

```python
import functools
import math

import numpy as np
import jax
import jax.numpy as jnp
from jax import lax
from jax.experimental import pallas as pl
from jax.experimental.pallas import tpu as pltpu

F32 = jnp.float32
BF16 = jnp.bfloat16

D_MODEL = 1024
ML_HEADS = 4
ML_DIM = 64
ML_WIDTH = ML_HEADS * ML_DIM
ML_TILE_CHUNK = 128
CONV_W = 4
DA_HEADS = 4
DA_QK_DIM = 32
DA_V_DIM = 64
DA_WIDTH = DA_HEADS * DA_V_DIM
NSA_HEADS = 8
NSA_GROUPS = 2
NSA_REP = NSA_HEADS // NSA_GROUPS
NSA_DIM = 64
NSA_WIDTH = NSA_HEADS * NSA_DIM
NSA_KV = NSA_GROUPS * NSA_DIM
CMP_BLOCK = 32
CMP_STRIDE = 16
CMP_HIDDEN = 4 * NSA_DIM
SEL_BLOCK = 64
SEL_TOPK = 16
WINDOW = 512
D_FF = 4 * D_MODEL
EPS = 1e-6
FORCE_SCORE = 1e4
IN_SIZES = (ML_WIDTH, ML_WIDTH, ML_WIDTH, ML_WIDTH, ML_HEADS, ML_HEADS,
            2 * DA_HEADS * DA_QK_DIM, 2 * DA_HEADS * DA_QK_DIM, DA_WIDTH,
            NSA_WIDTH, NSA_KV, NSA_KV, NSA_KV, NSA_KV, NSA_KV, NSA_KV, 3 * NSA_HEADS)

LANES = 128
GATE_PAD = 16
NEG = -1e30
M_INIT = -1e29
LOOP_PAIRS = 4
TAKEN = -3e38
LOG2E = 1.4426950408889634
V_DIM = 64
V_EXT = 80
VMEM_LIMIT = 56 * 1024 * 1024

_T_ML = (0, 1024)
_T_MLG = (1024, 1152)
_T_DAK = (1152, 1408)
_T_NSK = (1408, 1664)
_T_NSC = (1664, 1920)
_T_COLS = 1920
_F_DAQ = (0, 256)
_F_DAV = (256, 512)
_F_NSQ = (512, 1024)
_F_NSV = (1024, 1280)
_F_NSG = (1280, 1280 + NSA_GROUPS * GATE_PAD)
_F_ROWS = _F_NSG[1]

_NT = (((1,), (1,)), ((), ()))
_TN = (((0,), (0,)), ((), ()))


def _cparams(sem):
    return pltpu.CompilerParams(dimension_semantics=sem, vmem_limit_bytes=VMEM_LIMIT)


def _rms(x, g):
    return x * lax.rsqrt(jnp.mean(x * x, axis=-1, keepdims=True) + EPS) * g


def _inproj_kernel(x_ref, g_ref, wt_ref, wf_ref,
                   zml_ref, zg_ref, dak_ref, nsk_ref, nsc_ref,
                   daqT_ref, davT_ref, nsqT_ref, nsvT_ref, nsgT_ref):
    hb = _rms(x_ref[0], g_ref[...]).astype(BF16)

    def tdot(span):
        return jnp.dot(hb, wt_ref[:, span[0]:span[1]], preferred_element_type=F32)

    def fdot(span):
        return lax.dot_general(wf_ref[span[0]:span[1], :], hb, _NT, preferred_element_type=F32)

    zml_ref[0] = tdot(_T_ML)
    zg_ref[0] = tdot(_T_MLG)
    dak_ref[0] = tdot(_T_DAK).astype(BF16)
    nsk_ref[0] = tdot(_T_NSK).astype(BF16)
    nsc_ref[0] = tdot(_T_NSC)
    daqT_ref[0] = (fdot(_F_DAQ) * (DA_QK_DIM ** -0.5 * LOG2E)).astype(BF16)
    nsqT_ref[0] = (fdot(_F_NSQ) * (NSA_DIM ** -0.5 * LOG2E)).astype(BF16)
    nsgT_ref[0] = fdot(_F_NSG)
    tm = hb.shape[0]
    for v_ref, span in ((davT_ref, _F_DAV), (nsvT_ref, _F_NSV)):
        v = fdot(span).astype(BF16)
        for h in range((span[1] - span[0]) // V_DIM):
            v_ref[0, h * V_EXT:h * V_EXT + V_DIM, :] = v[h * V_DIM:(h + 1) * V_DIM, :]
            v_ref[0, h * V_EXT + V_DIM:(h + 1) * V_EXT, :] = jnp.ones((V_EXT - V_DIM, tm), BF16)


def _inproj(x, g, wt, wfT, tm=512):
    B, S, D = x.shape
    tok = lambda w, dt: jax.ShapeDtypeStruct((B, S, w), dt)
    feat = lambda r, dt: jax.ShapeDtypeStruct((B, r, S), dt)
    tspec = lambda w: pl.BlockSpec((1, tm, w), lambda b, i: (b, i, 0))
    fspec = lambda r: pl.BlockSpec((1, r, tm), lambda b, i: (b, 0, i))
    return pl.pallas_call(
        _inproj_kernel,
        grid=(B, S // tm),
        in_specs=[tspec(D),
                  pl.BlockSpec((1, D), lambda b, i: (0, 0)),
                  pl.BlockSpec((D, _T_COLS), lambda b, i: (0, 0)),
                  pl.BlockSpec((_F_ROWS, D), lambda b, i: (0, 0))],
        out_specs=[tspec(1024), tspec(128), tspec(256), tspec(256), tspec(256),
                   fspec(256), fspec(4 * V_EXT), fspec(512), fspec(4 * V_EXT), fspec(NSA_GROUPS * GATE_PAD)],
        out_shape=[tok(1024, F32), tok(128, F32), tok(256, BF16), tok(256, BF16), tok(256, F32),
                   feat(256, BF16), feat(4 * V_EXT, BF16), feat(512, BF16), feat(4 * V_EXT, BF16),
                   feat(NSA_GROUPS * GATE_PAD, F32)],
        compiler_params=_cparams(("parallel", "parallel")),
        name="inproj",
    )(x, g, wt, wfT)


def _log_sigmoid(x):
    return jnp.minimum(x, 0.0) - jnp.log1p(jnp.exp(-jnp.abs(x)))


def _sigmoid(x):
    return 1.0 / (1.0 + jnp.exp(-x))


def _mlstm_kernel(zml_ref, zg_ref, cw_ref, gb_ref, nrm_ref, y_ref,
                  pad_ref, c_ref, n_ref, m_ref, *, T, L):
    d = ML_DIM
    W = max(L, d)
    t = pl.program_id(1)

    @pl.when(t == 0)
    def _():
        pad_ref[0:8, :] = jnp.zeros((8, 2 * ML_WIDTH), F32)
        c_ref[...] = jnp.zeros_like(c_ref)
        n_ref[...] = jnp.zeros_like(n_ref)
        m_ref[...] = jnp.zeros_like(m_ref)

    @pl.when(t > 0)
    def _():
        pad_ref[0:8, :] = pad_ref[T:T + 8, :]

    pad_ref[8:8 + T, :] = zml_ref[0, :, 0:2 * ML_WIDTH]
    conv = cw_ref[0:1, :] * pad_ref[5:5 + T, :]
    for j in range(1, CONV_W):
        conv = conv + cw_ref[j:j + 1, :] * pad_ref[5 + j:5 + j + T, :]
    qk = conv * _sigmoid(conv)
    gates = zg_ref[0] + gb_ref[...]
    logf = _log_sigmoid(gates)

    row = lax.broadcasted_iota(jnp.int32, (L, L), 0)
    col = lax.broadcasted_iota(jnp.int32, (L, L), 1)
    causal = col <= row
    tril = causal.astype(F32)
    triu = (row <= col).astype(F32)

    lane_of = lax.broadcasted_iota(jnp.int32, (LANES, W), 0)
    pick = [(lane_of == g).astype(F32) for g in range(2 * ML_HEADS)]

    for c in range(T // L):
        r0 = c * L
        g_c = gates[r0:r0 + L, :]
        lf_c = logf[r0:r0 + L, :]
        b_cols = jnp.dot(tril, lf_c, preferred_element_type=F32,
                         precision=lax.Precision.HIGHEST)
        b_rows = jnp.dot(lf_c.T, triu, preferred_element_type=F32,
                         precision=lax.Precision.HIGHEST)
        g_rows = g_c.T
        for h in range(ML_HEADS):
            b_col = jnp.dot(b_cols, pick[ML_HEADS + h], preferred_element_type=F32,
                            precision=lax.Precision.HIGHEST)
            ig_col = jnp.dot(g_c, pick[h], preferred_element_type=F32,
                             precision=lax.Precision.HIGHEST)
            b_row = b_rows[ML_HEADS + h:ML_HEADS + h + 1, :]
            ig_row = g_rows[h:h + 1, :]
            qh = qk[r0:r0 + L, h * d:(h + 1) * d]
            kh = qk[r0:r0 + L, ML_WIDTH + h * d:ML_WIDTH + (h + 1) * d] * (d ** -0.5)
            vh = zml_ref[0, r0:r0 + L, 2 * ML_WIDTH + h * d:2 * ML_WIDTH + (h + 1) * d]
            oh = zml_ref[0, r0:r0 + L, 3 * ML_WIDTH + h * d:3 * ML_WIDTH + (h + 1) * d]
            c_prev = c_ref[h]
            n_prev = n_ref[h, 0:1, :]
            m_prev = m_ref[h, 0:1, :]

            g_tot = b_col[L - 1:L, :]
            a_row = g_tot[:, 0:L] - b_row + ig_row
            a_max = jnp.max(a_row, axis=1, keepdims=True)
            w_col = jnp.exp((g_tot - b_col + ig_col - a_max)[:, 0:d])
            c_loc = lax.dot_general(vh * w_col, kh, _TN, preferred_element_type=F32)
            n_loc = jnp.sum(kh * w_col, axis=0, keepdims=True)

            dmat = jnp.where(causal, b_col[:, 0:L] - b_row + ig_row, NEG)
            inter_log = b_col + m_prev
            m_t = jnp.maximum(inter_log, jnp.max(dmat, axis=1, keepdims=True))
            qkt = lax.dot_general(qh, kh, _NT, preferred_element_type=F32)
            wts = jnp.exp(dmat - m_t[:, 0:L]) * qkt
            s_inter = jnp.exp((inter_log - m_t)[:, 0:d])
            num = (jnp.dot(wts, vh, preferred_element_type=F32)
                   + s_inter * lax.dot_general(qh, c_prev, _NT, preferred_element_type=F32))
            den = (jnp.sum(wts, axis=1, keepdims=True)
                   + s_inter * jnp.sum(qh * n_prev, axis=1, keepdims=True))
            hh = num / jnp.maximum(jnp.abs(den), jnp.exp(-m_t[:, 0:d]))

            m_new = jnp.maximum(g_tot + m_prev, a_max)
            s_prev = jnp.exp((g_tot + m_prev - m_new)[:, 0:d])
            s_loc = jnp.exp((a_max - m_new)[:, 0:d])
            c_ref[h] = s_prev * c_prev + s_loc * c_loc
            n_ref[h] = jnp.broadcast_to(s_prev * n_prev + s_loc * n_loc, (8, d))
            m_ref[h] = jnp.broadcast_to(m_new, (8, W))

            yh = _sigmoid(oh) * hh
            y_ref[0, r0:r0 + L, h * d:(h + 1) * d] = _rms(yh, nrm_ref[0:1, h * d:(h + 1) * d])


def _mlstm(zml, zg, conv_w, gate_bias, norm_g, T=256, chunk=ML_TILE_CHUNK):
    B, S, _ = zml.shape
    return pl.pallas_call(
        functools.partial(_mlstm_kernel, T=T, L=chunk),
        grid=(B, S // T),
        in_specs=[pl.BlockSpec((1, T, 1024), lambda b, t: (b, t, 0)),
                  pl.BlockSpec((1, T, 128), lambda b, t: (b, t, 0)),
                  pl.BlockSpec((CONV_W, 2 * ML_WIDTH), lambda b, t: (0, 0)),
                  pl.BlockSpec((1, 128), lambda b, t: (0, 0)),
                  pl.BlockSpec((1, ML_WIDTH), lambda b, t: (0, 0))],
        out_specs=pl.BlockSpec((1, T, ML_WIDTH), lambda b, t: (b, t, 0)),
        out_shape=jax.ShapeDtypeStruct((B, S, ML_WIDTH), F32),
        scratch_shapes=[pltpu.VMEM((T + 8, 2 * ML_WIDTH), F32),
                        pltpu.VMEM((ML_HEADS, ML_DIM, ML_DIM), F32),
                        pltpu.VMEM((ML_HEADS, 8, ML_DIM), F32),
                        pltpu.VMEM((ML_HEADS, 8, max(chunk, ML_DIM)), F32)],
        compiler_params=_cparams(("parallel", "arbitrary")),
        name="mlstm",
    )(zml, zg, conv_w, gate_bias, norm_g)


def _softmax_cols(s, m_ref, p_ref, cols):
    m_old = m_ref[:, cols]
    m_new = jnp.maximum(m_old, jnp.max(s, axis=0, keepdims=True))
    m_ref[:, cols] = m_new
    p_ref[:, cols] = jnp.exp2(s - m_new).astype(BF16)
    return jnp.exp2(m_old - m_new)


def _pipelined_attention(scores, vT_tile, s_refs, p_refs, m_ref, acc_ref, softmax,
                         first, n_plain_pairs, n_tail, max_tile, col_groups):
    sa, sb = s_refs
    pa, pb = p_refs

    def load(j):
        return jnp.clip(j, 0, max_tile)

    def half(j, s_cur, s_nxt, p_cur, p_prev, tail):
        s_nxt[...] = scores(load(j + 1))
        pv = jnp.dot(vT_tile(load(j - 1)), p_prev[...], preferred_element_type=F32)
        alphas = softmax(s_cur, p_cur, j, tail)
        for cols, alpha in zip(col_groups, alphas):
            acc_ref[:, cols] = alpha * (acc_ref[:, cols] + pv[:, cols])

    def pair(j, tail):
        half(j, sa, sb, pa, pb, tail)
        half(j + 1, sb, sa, pb, pa, tail)

    m_ref[...] = jnp.full_like(m_ref, M_INIT)
    acc_ref[...] = jnp.zeros_like(acc_ref)
    pb[...] = jnp.zeros_like(pb)
    sa[...] = scores(load(first))

    if not (isinstance(n_plain_pairs, int) and n_plain_pairs == 0):
        n_trips = n_plain_pairs // LOOP_PAIRS

        def body(i, carry):
            for u in range(LOOP_PAIRS):
                pair(first + 2 * (LOOP_PAIRS * i + u), False)
            return carry

        lax.fori_loop(0, n_trips, body, 0)
        rest = n_plain_pairs - n_trips * LOOP_PAIRS
        for u in range(LOOP_PAIRS - 1):
            @pl.when(u < rest)
            def _():
                pair(first + 2 * (LOOP_PAIRS * n_trips + u), False)
    j = first + 2 * n_plain_pairs
    bufs = ((sa, sb, pa, pb), (sb, sa, pb, pa))
    for t in range(n_tail):
        half(j + t, *bufs[t % 2], True)
    p_last = bufs[(n_tail - 1) % 2][2]
    acc_ref[...] += jnp.dot(vT_tile(load(j + n_tail - 1)), p_last[...], preferred_element_type=F32)


def _da_kernel(lam_ref, qT_ref, k_ref, vT_ref, gain_ref, o_ref,
               qz_ref, sa_ref, sb_ref, pa_ref, pb_ref, m_ref, acc_ref, *, tq, tk, lam_init, S):
    h = pl.program_id(1)
    qi = pl.program_id(2)
    d = DA_QK_DIM
    jd = (qi * tq) // tk
    col_groups = [slice(mp * tq, (mp + 1) * tq) for mp in range(2)]

    qz_ref[...] = jnp.zeros_like(qz_ref)
    for hh in range(2):
        @pl.when(h % 2 == hh)
        def _():
            qz_ref[hh * 2 * d:hh * 2 * d + d, 0:tq] = qT_ref[0, 0:d, :]
            qz_ref[hh * 2 * d + d:(hh + 1) * 2 * d, tq:2 * tq] = qT_ref[0, d:2 * d, :]

    def scores(j):
        return jnp.dot(k_ref[0, pl.ds(pl.multiple_of(j * tk, tk), tk), :], qz_ref[...], preferred_element_type=F32)

    def vT_tile(j):
        return vT_ref[0, :, pl.ds(pl.multiple_of(j * tk, tk), tk)]

    def softmax(s_ref, p_ref, j, tail):
        alphas = []
        for cols in col_groups:
            s = s_ref[:, cols]
            if tail:
                kpos = j * tk + lax.broadcasted_iota(jnp.int32, (tk, tq), 0)
                qpos = qi * tq + lax.broadcasted_iota(jnp.int32, (tk, tq), 1)
                s = jnp.where(kpos <= qpos, s, NEG)
            alphas.append(_softmax_cols(s, m_ref, p_ref, cols))
        return alphas

    _pipelined_attention(scores, vT_tile, (sa_ref, sb_ref), (pa_ref, pb_ref), m_ref, acc_ref, softmax,
                         first=0, n_plain_pairs=jd // 2, n_tail=2, max_tile=S // tk - 1, col_groups=col_groups)

    lp = lam_ref[...]
    lam = (jnp.exp(jnp.sum(lp[0:1] * lp[1:2], axis=1, keepdims=True))
           - jnp.exp(jnp.sum(lp[2:3] * lp[3:4], axis=1, keepdims=True)) + lam_init)
    o1 = acc_ref[0:V_DIM, 0:tq] / acc_ref[V_DIM:V_DIM + 1, 0:tq]
    o2 = acc_ref[0:V_DIM, tq:2 * tq] / acc_ref[V_DIM:V_DIM + 1, tq:2 * tq]
    o = o1 - lam * o2
    y = o * lax.rsqrt(jnp.mean(o * o, axis=0, keepdims=True) + EPS) * gain_ref[...]
    o_ref[0] = (y * (1.0 - lam_init)).astype(o_ref.dtype)


def _diff_attn(da_lambda, daqT, dak, davT, gain_col, lam_init, tq=256, tk=256):
    B, _, S = daqT.shape
    return pl.pallas_call(
        functools.partial(_da_kernel, tq=tq, tk=tk, lam_init=lam_init, S=S),
        grid=(B, DA_HEADS, S // tq),
        in_specs=[pl.BlockSpec((4, DA_QK_DIM), lambda b, h, i: (0, 0)),
                  pl.BlockSpec((1, DA_V_DIM, tq), lambda b, h, i: (b, h, i)),
                  pl.BlockSpec((1, S, LANES), lambda b, h, i: (b, 0, h // 2)),
                  pl.BlockSpec((1, V_EXT, S), lambda b, h, i: (b, h, 0)),
                  pl.BlockSpec((DA_V_DIM, 1), lambda b, h, i: (h, 0))],
        out_specs=pl.BlockSpec((1, DA_V_DIM, tq), lambda b, h, i: (b, h, i)),
        out_shape=jax.ShapeDtypeStruct((B, DA_WIDTH, S), BF16),
        scratch_shapes=[pltpu.VMEM((LANES, 2 * tq), BF16),
                        pltpu.VMEM((tk, 2 * tq), F32),
                        pltpu.VMEM((tk, 2 * tq), F32),
                        pltpu.VMEM((tk, 2 * tq), BF16),
                        pltpu.VMEM((tk, 2 * tq), BF16),
                        pltpu.VMEM((1, 2 * tq), F32),
                        pltpu.VMEM((V_EXT, 2 * tq), F32)],
        compiler_params=_cparams(("parallel", "parallel", "arbitrary")),
        name="diff_attn",
    )(da_lambda, daqT, dak, davT, gain_col)


def _gelu_tanh(x):
    return x * (0.5 * (1.0 + jnp.tanh(math.sqrt(2.0 / math.pi) * (x + 0.044715 * (x * x * x)))))


def _compress_kernel(x_ref, pe_ref, w1_ref, w2_ref, o_ref, b_ref, *, feature_major):
    n = x_ref.shape[2]
    half = CMP_STRIDE * NSA_DIM
    x = x_ref[0, 0]
    a = jnp.dot((x + pe_ref[0, 0:1, :]).astype(BF16), w1_ref[0, 0:half, :], preferred_element_type=F32)
    b_ref[0:n, :] = jnp.dot((x + pe_ref[0, 1:2, :]).astype(BF16), w1_ref[0, half:2 * half, :],
                            preferred_element_type=F32)
    b_ref[n:n + 8, :] = jnp.zeros((8, CMP_HIDDEN), F32)
    hid = _gelu_tanh(a + b_ref[1:n + 1, :]).astype(BF16)
    if feature_major:
        o_ref[0, 0] = lax.dot_general(w2_ref[0], hid, _NT, preferred_element_type=F32).astype(o_ref.dtype)
    else:
        o_ref[0, 0] = jnp.dot(hid, w2_ref[0], preferred_element_type=F32).astype(o_ref.dtype)


def _compress(x16, pe2, w1, w2, which, feature_major):
    B, _, n, half = x16.shape
    G = NSA_GROUPS
    if feature_major:
        out_shape, out_block = (B, G, NSA_DIM, n), (1, 1, NSA_DIM, n)
        w2_arr, w2_block = jnp.swapaxes(w2, 1, 2), (1, NSA_DIM, CMP_HIDDEN)
    else:
        out_shape, out_block = (B, G, n, NSA_DIM), (1, 1, n, NSA_DIM)
        w2_arr, w2_block = w2, (1, CMP_HIDDEN, NSA_DIM)
    return pl.pallas_call(
        functools.partial(_compress_kernel, feature_major=feature_major),
        grid=(B, G),
        in_specs=[pl.BlockSpec((1, 1, n, half), lambda b, g: (b, which * G + g, 0, 0)),
                  pl.BlockSpec((1, 2, half), lambda b, g: (which, 0, 0)),
                  pl.BlockSpec((1, 2 * half, CMP_HIDDEN), lambda b, g: (which, 0, 0)),
                  pl.BlockSpec(w2_block, lambda b, g: (which, 0, 0))],
        out_specs=pl.BlockSpec(out_block, lambda b, g: (b, g, 0, 0)),
        out_shape=jax.ShapeDtypeStruct(out_shape, BF16),
        scratch_shapes=[pltpu.VMEM((n + 8, CMP_HIDDEN), F32)],
        compiler_params=_cparams(("parallel", "parallel")),
        name="compress_v" if feature_major else "compress_k",
    )(x16, pe2, w1, w2_arr.astype(BF16))


def _nsa_kernel(qT_ref, gT_ref, kc_ref, vcT_ref, ks_ref, vsT_ref, kw_ref, vwT_ref, o_ref,
                qg_ref, qz_ref, imp_ref, sel_ref, pc_ref, sa_ref, sb_ref, pa_ref, pb_ref, m_ref, acc_ref, out_ref,
                *, tq, tk, S):
    g = pl.program_id(1)
    qi = pl.program_id(2)
    d = NSA_DIM
    R = NSA_REP
    ncb = S // CMP_STRIDE
    nsb = S // SEL_BLOCK
    q0 = qi * tq
    jd = q0 // tk
    ratio = SEL_BLOCK // CMP_STRIDE
    per_tile = tk // SEL_BLOCK

    for r in range(R):
        qg_ref[:, r * tq:(r + 1) * tq] = qT_ref[0, r * d:(r + 1) * d, :]
    qz_ref[...] = jnp.zeros_like(qz_ref)
    for gg in range(NSA_GROUPS):
        @pl.when(g == gg)
        def _():
            qz_ref[gg * d:(gg + 1) * d, :] = qg_ref[...]

    def gate(r, br):
        return _sigmoid(gT_ref[0, r * 3 + br:r * 3 + br + 1, :])

    qpos = q0 + lax.broadcasted_iota(jnp.int32, (1, tq), 1)

    s_all = jnp.dot(kc_ref[0, 0], qg_ref[...], preferred_element_type=F32)
    cend = lax.broadcasted_iota(jnp.int32, (ncb, tq), 0) * CMP_STRIDE + (CMP_BLOCK - 1)
    cbias = jnp.where(cend <= qpos, 0.0, NEG)
    imp = jnp.zeros((ncb, tq), F32)
    for r in range(R):
        cols = slice(r * tq, (r + 1) * tq)
        s = s_all[:, cols] + cbias
        mx = jnp.maximum(jnp.max(s, axis=0, keepdims=True), M_INIT)
        p = jnp.exp2(s - mx)
        den = jnp.sum(p, axis=0, keepdims=True)
        p = p / jnp.where(den > 0, den, 1.0)
        imp = imp + p
        pc_ref[:, cols] = p.astype(BF16)
    o_cmp = jnp.dot(vcT_ref[0, 0], pc_ref[...], preferred_element_type=F32)
    for r in range(R):
        cols = slice(r * tq, (r + 1) * tq)
        out_ref[:, cols] = gate(r, 0) * o_cmp[:, cols]

    slabs = []
    for c in range(tq // LANES):
        imp_ref[c, 0:8, :] = jnp.zeros((8, LANES), F32)
        imp_ref[c, 8:8 + ncb, :] = imp[:, c * LANES:(c + 1) * LANES]
        imp_ref[c, 8 + ncb:16 + ncb, :] = jnp.zeros((8, LANES), F32)
        slab = jnp.zeros((nsb, LANES), F32)
        for o in range(-1, ratio):
            slab = slab + imp_ref[c, pl.ds(8 + o, nsb, stride=ratio), :]
        slabs.append(slab)
    p_slc = slabs[0] if len(slabs) == 1 else jnp.concatenate(slabs, axis=1)
    blk = lax.broadcasted_iota(jnp.int32, (nsb, tq), 0)
    cur = lax.shift_right_logical(qpos, int(math.log2(SEL_BLOCK)))
    forced = (blk == 0) | (blk == cur) | (blk == cur - 1)
    causal_blk = blk * SEL_BLOCK <= qpos
    score = jnp.where(forced, FORCE_SCORE, jnp.where(causal_blk, p_slc, -1.0))
    blk_f = blk.astype(F32)
    for _ in range(SEL_TOPK):
        mx = jnp.max(score, axis=0, keepdims=True)
        first = jnp.min(jnp.where(score == mx, blk_f, float(nsb)), axis=0, keepdims=True)
        score = jnp.where(blk_f == first, TAKEN, score)
    sel_ref[...] = jnp.where(score == TAKEN, 0.0, NEG)

    col_groups = [slice(r * tq, (r + 1) * tq) for r in range(R)]
    max_tile = S // tk - 1

    def k_tile(k_ref_, j):
        return k_ref_[0, pl.ds(pl.multiple_of(j * tk, tk), tk), :]

    def vT_tile_of(vT_ref_):
        def vT_tile(j):
            return vT_ref_[0, :, pl.ds(pl.multiple_of(j * tk, tk), tk)]
        return vT_tile

    def softmax_with(bias_of):
        def softmax(s_ref, p_ref, j, tail):
            bias = bias_of(j, tail)
            return [_softmax_cols(s_ref[:, cols] if bias is None else s_ref[:, cols] + bias, m_ref, p_ref, cols)
                    for cols in col_groups]
        return softmax

    def flush(br):
        for r, cols in enumerate(col_groups):
            den = acc_ref[V_DIM:V_DIM + 1, cols]
            out_ref[:, cols] += gate(r, br) * (acc_ref[0:V_DIM, cols] / jnp.where(den > 0, den, 1.0))

    def kpos_of(j):
        return j * tk + lax.broadcasted_iota(jnp.int32, (tk, tq), 0)

    def win_bias(j, tail):
        kpos = kpos_of(j)
        rel = qpos - kpos
        return jnp.where((rel >= 0) & (rel < WINDOW) & (kpos >= 0), 0.0, NEG)

    def win_scores(j):
        return jnp.dot(k_tile(kw_ref, j), qz_ref[...], preferred_element_type=F32)

    n_win = WINDOW // tk + 1
    _pipelined_attention(win_scores, vT_tile_of(vwT_ref), (sa_ref, sb_ref), (pa_ref, pb_ref), m_ref, acc_ref,
                         softmax_with(win_bias), first=jd - (n_win - 1), n_plain_pairs=0, n_tail=n_win,
                         max_tile=max_tile, col_groups=col_groups)
    flush(2)

    def sel_scores(j):
        return jnp.dot(k_tile(ks_ref, j), qz_ref[...], preferred_element_type=F32)

    def sel_bias(j, tail):
        first_blk = jnp.clip(j, 0, max_tile) * per_tile
        rows = [jnp.broadcast_to(sel_ref[pl.ds(first_blk + i, 1), :], (SEL_BLOCK, tq)) for i in range(per_tile)]
        bias = jnp.concatenate(rows, axis=0)
        if tail:
            bias = jnp.where(kpos_of(j) <= qpos, bias, NEG)
        return bias

    _pipelined_attention(sel_scores, vT_tile_of(vsT_ref), (sa_ref, sb_ref), (pa_ref, pb_ref), m_ref, acc_ref,
                         softmax_with(sel_bias), first=0, n_plain_pairs=jd // 2, n_tail=2,
                         max_tile=max_tile, col_groups=col_groups)
    flush(1)

    for r in range(R):
        o_ref[0, r * d:(r + 1) * d, :] = out_ref[:, r * tq:(r + 1) * tq].astype(o_ref.dtype)


def _nsa(nsqT, nsgT, kcmp, vcmpT, nsk, nsvT, tq=128, tk=256):
    B, _, S = nsqT.shape
    G, R, d = NSA_GROUPS, NSA_REP, NSA_DIM
    ncb = S // CMP_STRIDE
    nsb = S // SEL_BLOCK
    return pl.pallas_call(
        functools.partial(_nsa_kernel, tq=tq, tk=tk, S=S),
        grid=(B, G, S // tq),
        in_specs=[pl.BlockSpec((1, R * d, tq), lambda b, g, i: (b, g, i)),
                  pl.BlockSpec((1, GATE_PAD, tq), lambda b, g, i: (b, g, i)),
                  pl.BlockSpec((1, 1, ncb, d), lambda b, g, i: (b, g, 0, 0)),
                  pl.BlockSpec((1, 1, d, ncb), lambda b, g, i: (b, g, 0, 0)),
                  pl.BlockSpec((1, S, LANES), lambda b, g, i: (b, 0, 0)),
                  pl.BlockSpec((1, V_EXT, S), lambda b, g, i: (b, g, 0)),
                  pl.BlockSpec((1, S, LANES), lambda b, g, i: (b, 0, 1)),
                  pl.BlockSpec((1, V_EXT, S), lambda b, g, i: (b, G + g, 0))],
        out_specs=pl.BlockSpec((1, R * d, tq), lambda b, g, i: (b, g, i)),
        out_shape=jax.ShapeDtypeStruct((B, NSA_WIDTH, S), BF16),
        scratch_shapes=[pltpu.VMEM((d, R * tq), BF16),
                        pltpu.VMEM((LANES, R * tq), BF16),
                        pltpu.VMEM((tq // LANES, ncb + 16, LANES), F32),
                        pltpu.VMEM((nsb, tq), F32),
                        pltpu.VMEM((ncb, R * tq), BF16),
                        pltpu.VMEM((tk, R * tq), F32),
                        pltpu.VMEM((tk, R * tq), F32),
                        pltpu.VMEM((tk, R * tq), BF16),
                        pltpu.VMEM((tk, R * tq), BF16),
                        pltpu.VMEM((1, R * tq), F32),
                        pltpu.VMEM((V_EXT, R * tq), F32),
                        pltpu.VMEM((d, R * tq), F32)],
        compiler_params=_cparams(("parallel", "parallel", "arbitrary")),
        name="nsa",
    )(nsqT, nsgT, kcmp, vcmpT, nsk, nsvT, nsk, nsvT)


def _outproj_kernel(x_ref, yml_ref, ydaT_ref, ynsT_ref, wo_ref, o_ref):
    acc = x_ref[0] + jnp.dot(yml_ref[0].astype(BF16), wo_ref[0:ML_WIDTH, :], preferred_element_type=F32)
    acc = acc + lax.dot_general(ydaT_ref[0], wo_ref[ML_WIDTH:ML_WIDTH + DA_WIDTH, :], _TN,
                                preferred_element_type=F32)
    acc = acc + lax.dot_general(ynsT_ref[0], wo_ref[ML_WIDTH + DA_WIDTH:, :], _TN,
                                preferred_element_type=F32)
    o_ref[0] = acc


def _outproj(x, yml, ydaT, ynsT, wo, tm=512):
    B, S, D = x.shape
    return pl.pallas_call(
        _outproj_kernel,
        grid=(B, S // tm),
        in_specs=[pl.BlockSpec((1, tm, D), lambda b, i: (b, i, 0)),
                  pl.BlockSpec((1, tm, ML_WIDTH), lambda b, i: (b, i, 0)),
                  pl.BlockSpec((1, DA_WIDTH, tm), lambda b, i: (b, 0, i)),
                  pl.BlockSpec((1, NSA_WIDTH, tm), lambda b, i: (b, 0, i)),
                  pl.BlockSpec((D, D), lambda b, i: (0, 0))],
        out_specs=pl.BlockSpec((1, tm, D), lambda b, i: (b, i, 0)),
        out_shape=jax.ShapeDtypeStruct((B, S, D), F32),
        compiler_params=_cparams(("parallel", "parallel")),
        name="outproj",
    )(x, yml, ydaT, ynsT, wo)


def _ffn_kernel(x_ref, g_ref, w1_ref, w2_ref, gf_ref, o_ref, hb_ref, acc_ref, *, final):
    j = pl.program_id(1)

    @pl.when(j == 0)
    def _():
        x = x_ref[...]
        hb_ref[...] = _rms(x, g_ref[...]).astype(BF16)
        acc_ref[...] = x

    u = jnp.dot(hb_ref[...], w1_ref[...], preferred_element_type=F32)
    a = jnp.square(jnp.maximum(u, 0.0)).astype(BF16)
    acc_ref[...] += jnp.dot(a, w2_ref[...], preferred_element_type=F32)

    @pl.when(j == pl.num_programs(1) - 1)
    def _():
        y = acc_ref[...]
        if final:
            y = _rms(y, gf_ref[...])
        o_ref[...] = y


def _ffn(x2d, g, w1, w2, gf, final, tm=1024, tf=1024):
    N, D = x2d.shape
    return pl.pallas_call(
        functools.partial(_ffn_kernel, final=final),
        grid=(N // tm, D_FF // tf),
        in_specs=[pl.BlockSpec((tm, D), lambda i, j: (i, 0)),
                  pl.BlockSpec((1, D), lambda i, j: (0, 0)),
                  pl.BlockSpec((D, tf), lambda i, j: (0, j)),
                  pl.BlockSpec((tf, D), lambda i, j: (j, 0)),
                  pl.BlockSpec((1, D), lambda i, j: (0, 0))],
        out_specs=pl.BlockSpec((tm, D), lambda i, j: (i, 0)),
        out_shape=jax.ShapeDtypeStruct((N, D), F32),
        scratch_shapes=[pltpu.VMEM((tm, D), BF16), pltpu.VMEM((tm, D), F32)],
        compiler_params=_cparams(("parallel", "arbitrary")),
        name="ffn",
    )(x2d, g, w1, w2, gf)


def _split_w_in(w_in_l):
    edges = np.concatenate([[0], np.cumsum(IN_SIZES)])
    return [w_in_l[:, int(edges[i]):int(edges[i + 1])] for i in range(len(IN_SIZES))]


def _pad_cols(w, n):
    return jnp.pad(w, ((0, 0), (0, n - w.shape[1])))


def _inproj_weights(w_in_l):
    (ml_q, ml_k, ml_v, ml_o, ml_i, ml_f, da_q, da_k, da_v,
     ns_q, ns_kc, ns_vc, ns_ks, ns_vs, ns_kw, ns_vw, ns_g) = _split_w_in(w_in_l)
    wt = jnp.concatenate([ml_q, ml_k, ml_v, ml_o, _pad_cols(jnp.concatenate([ml_i, ml_f], 1), LANES),
                          da_k, ns_ks, ns_kw, ns_kc, ns_vc], axis=1)
    per_group = NSA_REP * 3
    ns_g_pad = jnp.concatenate(
        [_pad_cols(ns_g[:, gi * per_group:(gi + 1) * per_group], GATE_PAD) for gi in range(NSA_GROUPS)], axis=1)
    wf = jnp.concatenate([da_q, da_v, ns_q, ns_vs, ns_vw, ns_g_pad], axis=1)
    return wt.astype(BF16), wf.T.astype(BF16)


def _half_blocks(nsc):
    B, S, _ = nsc.shape
    n = S // CMP_STRIDE
    t = nsc.reshape(B, n, CMP_STRIDE, 2 * NSA_GROUPS, NSA_DIM)
    return t.transpose(0, 3, 1, 2, 4).reshape(B, 2 * NSA_GROUPS, n, CMP_STRIDE * NSA_DIM)


def kernel(x, norm1, w_in, ml_conv, ml_gate_bias, ml_norm, da_lambda, da_norm, nsa_pe,
           nsa_w1, nsa_w2, w_out, norm2, w_ff1, w_ff2, final_norm):
    B, S, D = x.shape
    depth = norm1.shape[0]
    for l in range(depth):
        wt, wfT = _inproj_weights(w_in[l])
        (zml, zg, dak, nsk, nsc, daqT, davT, nsqT, nsvT, nsgT) = _inproj(x, norm1[l][None, :], wt, wfT)

        yml = _mlstm(zml, zg, ml_conv[l], _pad_cols(ml_gate_bias[l][None, :], LANES), ml_norm[l][None, :])

        lam_init = 0.8 - 0.6 * math.exp(-0.3 * l)
        ydaT = _diff_attn(da_lambda[l], daqT, dak, davT, da_norm[l][:, None], lam_init)

        x16 = _half_blocks(nsc)
        pe2 = nsa_pe[l].reshape(2, 2, CMP_STRIDE * NSA_DIM)
        w1b = nsa_w1[l].astype(BF16)
        kcmp = _compress(x16, pe2, w1b, nsa_w2[l], 0, False)
        vcmpT = _compress(x16, pe2, w1b, nsa_w2[l], 1, True)
        ynsT = _nsa(nsqT, nsgT, kcmp, vcmpT, nsk, nsvT)

        x = _outproj(x, yml, ydaT, ynsT, w_out[l].astype(BF16))
        x = _ffn(x.reshape(B * S, D), norm2[l][None, :], w_ff1[l].astype(BF16), w_ff2[l].astype(BF16),
                 final_norm[None, :], final=(l == depth - 1)).reshape(B, S, D)
    return x
```

```python
import functools
import math

import numpy as np
import jax
import jax.numpy as jnp
from jax import lax
from jax.experimental import pallas as pl
from jax.experimental.pallas import tpu as pltpu

F32 = jnp.float32
BF16 = jnp.bfloat16

D_MODEL = 1024
ML_HEADS = 4
ML_DIM = 64
ML_WIDTH = ML_HEADS * ML_DIM
ML_TILE_CHUNK = 128
CONV_W = 4
DA_HEADS = 4
DA_QK_DIM = 32
DA_V_DIM = 64
DA_WIDTH = DA_HEADS * DA_V_DIM
NSA_HEADS = 8
NSA_GROUPS = 2
NSA_REP = NSA_HEADS // NSA_GROUPS
NSA_DIM = 64
NSA_WIDTH = NSA_HEADS * NSA_DIM
NSA_KV = NSA_GROUPS * NSA_DIM
CMP_BLOCK = 32
CMP_STRIDE = 16
CMP_HIDDEN = 4 * NSA_DIM
SEL_BLOCK = 64
SEL_TOPK = 16
WINDOW = 512
D_FF = 4 * D_MODEL
EPS = 1e-6
FORCE_SCORE = 1e4
IN_SIZES = (ML_WIDTH, ML_WIDTH, ML_WIDTH, ML_WIDTH, ML_HEADS, ML_HEADS,
            2 * DA_HEADS * DA_QK_DIM, 2 * DA_HEADS * DA_QK_DIM, DA_WIDTH,
            NSA_WIDTH, NSA_KV, NSA_KV, NSA_KV, NSA_KV, NSA_KV, NSA_KV, 3 * NSA_HEADS)

LANES = 128
GATE_PAD = 16
NEG = -1e30
M_INIT = -1e29
LOOP_PAIRS = 4
TAKEN = -3e38
LOG2E = 1.4426950408889634
V_DIM = 64
V_EXT = 80
VMEM_LIMIT = 56 * 1024 * 1024

_T_ML = (0, 1024)
_T_MLG = (1024, 1152)
_T_DAK = (1152, 1408)
_T_NSK = (1408, 1664)
_T_NSC = (1664, 1920)
_T_COLS = 1920
_F_DAQ = (0, 256)
_F_DAV = (256, 512)
_F_NSQ = (512, 1024)
_F_NSV = (1024, 1280)
_F_NSG = (1280, 1280 + NSA_GROUPS * GATE_PAD)
_F_ROWS = _F_NSG[1]

_NT = (((1,), (1,)), ((), ()))
_TN = (((0,), (0,)), ((), ()))


def _cparams(sem):
    return pltpu.CompilerParams(dimension_semantics=sem, vmem_limit_bytes=VMEM_LIMIT)


def _rms(x, g):
    return x * lax.rsqrt(jnp.mean(x * x, axis=-1, keepdims=True) + EPS) * g


def _inproj_kernel(x_ref, g_ref, wt_ref, wf_ref,
                   zml_ref, zg_ref, dak_ref, nsk_ref, nsc_ref,
                   daqT_ref, davT_ref, nsqT_ref, nsvT_ref, nsgT_ref):
    hb = _rms(x_ref[0], g_ref[...]).astype(BF16)

    def tdot(span):
        return jnp.dot(hb, wt_ref[:, span[0]:span[1]], preferred_element_type=F32)

    def fdot(span):
        return lax.dot_general(wf_ref[span[0]:span[1], :], hb, _NT, preferred_element_type=F32)

    zml_ref[0] = tdot(_T_ML)
    zg_ref[0] = tdot(_T_MLG)
    dak_ref[0] = tdot(_T_DAK).astype(BF16)
    nsk_ref[0] = tdot(_T_NSK).astype(BF16)
    nsc_ref[0] = tdot(_T_NSC)
    daqT_ref[0] = (fdot(_F_DAQ) * (DA_QK_DIM ** -0.5 * LOG2E)).astype(BF16)
    nsqT_ref[0] = (fdot(_F_NSQ) * (NSA_DIM ** -0.5 * LOG2E)).astype(BF16)
    nsgT_ref[0] = fdot(_F_NSG)
    tm = hb.shape[0]
    for v_ref, span in ((davT_ref, _F_DAV), (nsvT_ref, _F_NSV)):
        v = fdot(span).astype(BF16)
        for h in range((span[1] - span[0]) // V_DIM):
            v_ref[0, h * V_EXT:h * V_EXT + V_DIM, :] = v[h * V_DIM:(h + 1) * V_DIM, :]
            v_ref[0, h * V_EXT + V_DIM:(h + 1) * V_EXT, :] = jnp.ones((V_EXT - V_DIM, tm), BF16)


def _inproj(x, g, wt, wfT, tm=512):
    B, S, D = x.shape
    tok = lambda w, dt: jax.ShapeDtypeStruct((B, S, w), dt)
    feat = lambda r, dt: jax.ShapeDtypeStruct((B, r, S), dt)
    tspec = lambda w: pl.BlockSpec((1, tm, w), lambda b, i: (b, i, 0))
    fspec = lambda r: pl.BlockSpec((1, r, tm), lambda b, i: (b, 0, i))
    return pl.pallas_call(
        _inproj_kernel,
        grid=(B, S // tm),
        in_specs=[tspec(D),
                  pl.BlockSpec((1, D), lambda b, i: (0, 0)),
                  pl.BlockSpec((D, _T_COLS), lambda b, i: (0, 0)),
                  pl.BlockSpec((_F_ROWS, D), lambda b, i: (0, 0))],
        out_specs=[tspec(1024), tspec(128), tspec(256), tspec(256), tspec(256),
                   fspec(256), fspec(4 * V_EXT), fspec(512), fspec(4 * V_EXT), fspec(NSA_GROUPS * GATE_PAD)],
        out_shape=[tok(1024, F32), tok(128, F32), tok(256, BF16), tok(256, BF16), tok(256, F32),
                   feat(256, BF16), feat(4 * V_EXT, BF16), feat(512, BF16), feat(4 * V_EXT, BF16),
                   feat(NSA_GROUPS * GATE_PAD, F32)],
        compiler_params=_cparams(("parallel", "parallel")),
        name="inproj",
    )(x, g, wt, wfT)


def _log_sigmoid(x):
    return jnp.minimum(x, 0.0) - jnp.log1p(jnp.exp(-jnp.abs(x)))


def _sigmoid(x):
    return 1.0 / (1.0 + jnp.exp(-x))


def _mlstm_kernel(zml_ref, zg_ref, cw_ref, gb_ref, nrm_ref, y_ref,
                  pad_ref, c_ref, n_ref, m_ref, *, T, L):
    d = ML_DIM
    W = max(L, d)
    t = pl.program_id(1)

    @pl.when(t == 0)
    def _():
        pad_ref[0:8, :] = jnp.zeros((8, 2 * ML_WIDTH), F32)
        c_ref[...] = jnp.zeros_like(c_ref)
        n_ref[...] = jnp.zeros_like(n_ref)
        m_ref[...] = jnp.zeros_like(m_ref)

    @pl.when(t > 0)
    def _():
        pad_ref[0:8, :] = pad_ref[T:T + 8, :]

    pad_ref[8:8 + T, :] = zml_ref[0, :, 0:2 * ML_WIDTH]
    conv = cw_ref[0:1, :] * pad_ref[5:5 + T, :]
    for j in range(1, CONV_W):
        conv = conv + cw_ref[j:j + 1, :] * pad_ref[5 + j:5 + j + T, :]
    qk = conv * _sigmoid(conv)
    gates = zg_ref[0] + gb_ref[...]
    logf = _log_sigmoid(gates)

    row = lax.broadcasted_iota(jnp.int32, (L, L), 0)
    col = lax.broadcasted_iota(jnp.int32, (L, L), 1)
    causal = col <= row
    tril = causal.astype(F32)
    triu = (row <= col).astype(F32)

    lane_of = lax.broadcasted_iota(jnp.int32, (LANES, W), 0)
    pick = [(lane_of == g).astype(F32) for g in range(2 * ML_HEADS)]

    for c in range(T // L):
        r0 = c * L
        g_c = gates[r0:r0 + L, :]
        lf_c = logf[r0:r0 + L, :]
        b_cols = jnp.dot(tril, lf_c, preferred_element_type=F32,
                         precision=lax.Precision.HIGHEST)
        b_rows = jnp.dot(lf_c.T, triu, preferred_element_type=F32,
                         precision=lax.Precision.HIGHEST)
        g_rows = g_c.T
        for h in range(ML_HEADS):
            b_col = jnp.dot(b_cols, pick[ML_HEADS + h], preferred_element_type=F32,
                            precision=lax.Precision.HIGHEST)
            ig_col = jnp.dot(g_c, pick[h], preferred_element_type=F32,
                             precision=lax.Precision.HIGHEST)
            b_row = b_rows[ML_HEADS + h:ML_HEADS + h + 1, :]
            ig_row = g_rows[h:h + 1, :]
            qh = qk[r0:r0 + L, h * d:(h + 1) * d]
            kh = qk[r0:r0 + L, ML_WIDTH + h * d:ML_WIDTH + (h + 1) * d] * (d ** -0.5)
            vh = zml_ref[0, r0:r0 + L, 2 * ML_WIDTH + h * d:2 * ML_WIDTH + (h + 1) * d]
            oh = zml_ref[0, r0:r0 + L, 3 * ML_WIDTH + h * d:3 * ML_WIDTH + (h + 1) * d]
            c_prev = c_ref[h]
            n_prev = n_ref[h, 0:1, :]
            m_prev = m_ref[h, 0:1, :]

            g_tot = b_col[L - 1:L, :]
            a_row = g_tot[:, 0:L] - b_row + ig_row
            a_max = jnp.max(a_row, axis=1, keepdims=True)
            w_col = jnp.exp((g_tot - b_col + ig_col - a_max)[:, 0:d])
            c_loc = lax.dot_general(vh * w_col, kh, _TN, preferred_element_type=F32)
            n_loc = jnp.sum(kh * w_col, axis=0, keepdims=True)

            dmat = jnp.where(causal, b_col[:, 0:L] - b_row + ig_row, NEG)
            inter_log = b_col + m_prev
            m_t = jnp.maximum(inter_log, jnp.max(dmat, axis=1, keepdims=True))
            qkt = lax.dot_general(qh, kh, _NT, preferred_element_type=F32)
            wts = jnp.exp(dmat - m_t[:, 0:L]) * qkt
            s_inter = jnp.exp((inter_log - m_t)[:, 0:d])
            num = (jnp.dot(wts, vh, preferred_element_type=F32)
                   + s_inter * lax.dot_general(qh, c_prev, _NT, preferred_element_type=F32))
            den = (jnp.sum(wts, axis=1, keepdims=True)
                   + s_inter * jnp.sum(qh * n_prev, axis=1, keepdims=True))
            hh = num / jnp.maximum(jnp.abs(den), jnp.exp(-m_t[:, 0:d]))

            m_new = jnp.maximum(g_tot + m_prev, a_max)
            s_prev = jnp.exp((g_tot + m_prev - m_new)[:, 0:d])
            s_loc = jnp.exp((a_max - m_new)[:, 0:d])
            c_ref[h] = s_prev * c_prev + s_loc * c_loc
            n_ref[h] = jnp.broadcast_to(s_prev * n_prev + s_loc * n_loc, (8, d))
            m_ref[h] = jnp.broadcast_to(m_new, (8, W))

            yh = _sigmoid(oh) * hh
            y_ref[0, r0:r0 + L, h * d:(h + 1) * d] = _rms(yh, nrm_ref[0:1, h * d:(h + 1) * d])


def _mlstm(zml, zg, conv_w, gate_bias, norm_g, T=256, chunk=ML_TILE_CHUNK):
    B, S, _ = zml.shape
    return pl.pallas_call(
        functools.partial(_mlstm_kernel, T=T, L=chunk),
        grid=(B, S // T),
        in_specs=[pl.BlockSpec((1, T, 1024), lambda b, t: (b, t, 0)),
                  pl.BlockSpec((1, T, 128), lambda b, t: (b, t, 0)),
                  pl.BlockSpec((CONV_W, 2 * ML_WIDTH), lambda b, t: (0, 0)),
                  pl.BlockSpec((1, 128), lambda b, t: (0, 0)),
                  pl.BlockSpec((1, ML_WIDTH), lambda b, t: (0, 0))],
        out_specs=pl.BlockSpec((1, T, ML_WIDTH), lambda b, t: (b, t, 0)),
        out_shape=jax.ShapeDtypeStruct((B, S, ML_WIDTH), F32),
        scratch_shapes=[pltpu.VMEM((T + 8, 2 * ML_WIDTH), F32),
                        pltpu.VMEM((ML_HEADS, ML_DIM, ML_DIM), F32),
                        pltpu.VMEM((ML_HEADS, 8, ML_DIM), F32),
                        pltpu.VMEM((ML_HEADS, 8, max(chunk, ML_DIM)), F32)],
        compiler_params=_cparams(("parallel", "arbitrary")),
        name="mlstm",
    )(zml, zg, conv_w, gate_bias, norm_g)


def _softmax_cols(s, m_ref, p_ref, cols):
    m_old = m_ref[:, cols]
    m_new = jnp.maximum(m_old, jnp.max(s, axis=0, keepdims=True))
    m_ref[:, cols] = m_new
    p_ref[:, cols] = jnp.exp2(s - m_new).astype(BF16)
    return jnp.exp2(m_old - m_new)


def _softmax_cols_run_bias(s_ref, run_bias, run, m_ref, p_ref, cols):
    m_old = m_ref[:, cols]
    m_new = m_old
    for i, b in enumerate(run_bias):
        m_new = jnp.maximum(m_new, jnp.max(s_ref[i * run:(i + 1) * run, cols], axis=0, keepdims=True) + b)
    m_ref[:, cols] = m_new
    for i, b in enumerate(run_bias):
        p_ref[i * run:(i + 1) * run, cols] = jnp.exp2(s_ref[i * run:(i + 1) * run, cols] + (b - m_new)).astype(BF16)
    return jnp.exp2(m_old - m_new)


def _pipelined_attention(scores, vT_tile, s_refs, p_refs, m_ref, acc_ref, softmax,
                         first, n_plain_pairs, n_tail, max_tile, col_groups):
    sa, sb = s_refs
    pa, pb = p_refs

    def load(j):
        return jnp.clip(j, 0, max_tile)

    def half(j, s_cur, s_nxt, p_cur, p_prev, tail):
        s_nxt[...] = scores(load(j + 1))
        pv = jnp.dot(vT_tile(load(j - 1)), p_prev[...], preferred_element_type=F32)
        alphas = softmax(s_cur, p_cur, j, tail)
        for cols, alpha in zip(col_groups, alphas):
            acc_ref[:, cols] = alpha * (acc_ref[:, cols] + pv[:, cols])

    def pair(j, tail):
        half(j, sa, sb, pa, pb, tail)
        half(j + 1, sb, sa, pb, pa, tail)

    m_ref[...] = jnp.full_like(m_ref, M_INIT)
    acc_ref[...] = jnp.zeros_like(acc_ref)
    pb[...] = jnp.zeros_like(pb)
    sa[...] = scores(load(first))

    if not (isinstance(n_plain_pairs, int) and n_plain_pairs == 0):
        n_trips = n_plain_pairs // LOOP_PAIRS

        def body(i, carry):
            for u in range(LOOP_PAIRS):
                pair(first + 2 * (LOOP_PAIRS * i + u), False)
            return carry

        lax.fori_loop(0, n_trips, body, 0)
        rest = n_plain_pairs - n_trips * LOOP_PAIRS
        for u in range(LOOP_PAIRS - 1):
            @pl.when(u < rest)
            def _():
                pair(first + 2 * (LOOP_PAIRS * n_trips + u), False)
    j = first + 2 * n_plain_pairs
    bufs = ((sa, sb, pa, pb), (sb, sa, pb, pa))
    for t in range(n_tail):
        half(j + t, *bufs[t % 2], True)
    p_last = bufs[(n_tail - 1) % 2][2]
    acc_ref[...] += jnp.dot(vT_tile(load(j + n_tail - 1)), p_last[...], preferred_element_type=F32)


def _da_kernel(lam_ref, qT_ref, k_ref, vT_ref, gain_ref, o_ref,
               qz_ref, sa_ref, sb_ref, pa_ref, pb_ref, m_ref, acc_ref, *, tq, tk, lam_init, S):
    h = pl.program_id(1)
    qi = pl.program_id(2)
    d = DA_QK_DIM
    jd = (qi * tq) // tk
    col_groups = [slice(mp * tq, (mp + 1) * tq) for mp in range(2)]

    qz_ref[...] = jnp.zeros_like(qz_ref)
    for hh in range(2):
        @pl.when(h % 2 == hh)
        def _():
            qz_ref[hh * 2 * d:hh * 2 * d + d, 0:tq] = qT_ref[0, 0:d, :]
            qz_ref[hh * 2 * d + d:(hh + 1) * 2 * d, tq:2 * tq] = qT_ref[0, d:2 * d, :]

    def scores(j):
        return jnp.dot(k_ref[0, pl.ds(pl.multiple_of(j * tk, tk), tk), :], qz_ref[...], preferred_element_type=F32)

    def vT_tile(j):
        return vT_ref[0, :, pl.ds(pl.multiple_of(j * tk, tk), tk)]

    def softmax(s_ref, p_ref, j, tail):
        alphas = []
        for cols in col_groups:
            s = s_ref[:, cols]
            if tail:
                kpos = j * tk + lax.broadcasted_iota(jnp.int32, (tk, tq), 0)
                qpos = qi * tq + lax.broadcasted_iota(jnp.int32, (tk, tq), 1)
                s = jnp.where(kpos <= qpos, s, NEG)
            alphas.append(_softmax_cols(s, m_ref, p_ref, cols))
        return alphas

    _pipelined_attention(scores, vT_tile, (sa_ref, sb_ref), (pa_ref, pb_ref), m_ref, acc_ref, softmax,
                         first=0, n_plain_pairs=jd // 2, n_tail=2, max_tile=S // tk - 1, col_groups=col_groups)

    lp = lam_ref[...]
    lam = (jnp.exp(jnp.sum(lp[0:1] * lp[1:2], axis=1, keepdims=True))
           - jnp.exp(jnp.sum(lp[2:3] * lp[3:4], axis=1, keepdims=True)) + lam_init)
    o1 = acc_ref[0:V_DIM, 0:tq] / acc_ref[V_DIM:V_DIM + 1, 0:tq]
    o2 = acc_ref[0:V_DIM, tq:2 * tq] / acc_ref[V_DIM:V_DIM + 1, tq:2 * tq]
    o = o1 - lam * o2
    y = o * lax.rsqrt(jnp.mean(o * o, axis=0, keepdims=True) + EPS) * gain_ref[...]
    o_ref[0] = (y * (1.0 - lam_init)).astype(o_ref.dtype)


def _diff_attn(da_lambda, daqT, dak, davT, gain_col, lam_init, tq=256, tk=256):
    B, _, S = daqT.shape
    return pl.pallas_call(
        functools.partial(_da_kernel, tq=tq, tk=tk, lam_init=lam_init, S=S),
        grid=(B, DA_HEADS, S // tq),
        in_specs=[pl.BlockSpec((4, DA_QK_DIM), lambda b, h, i: (0, 0)),
                  pl.BlockSpec((1, DA_V_DIM, tq), lambda b, h, i: (b, h, i)),
                  pl.BlockSpec((1, S, LANES), lambda b, h, i: (b, 0, h // 2)),
                  pl.BlockSpec((1, V_EXT, S), lambda b, h, i: (b, h, 0)),
                  pl.BlockSpec((DA_V_DIM, 1), lambda b, h, i: (h, 0))],
        out_specs=pl.BlockSpec((1, DA_V_DIM, tq), lambda b, h, i: (b, h, i)),
        out_shape=jax.ShapeDtypeStruct((B, DA_WIDTH, S), BF16),
        scratch_shapes=[pltpu.VMEM((LANES, 2 * tq), BF16),
                        pltpu.VMEM((tk, 2 * tq), F32),
                        pltpu.VMEM((tk, 2 * tq), F32),
                        pltpu.VMEM((tk, 2 * tq), BF16),
                        pltpu.VMEM((tk, 2 * tq), BF16),
                        pltpu.VMEM((1, 2 * tq), F32),
                        pltpu.VMEM((V_EXT, 2 * tq), F32)],
        compiler_params=_cparams(("parallel", "parallel", "arbitrary")),
        name="diff_attn",
    )(da_lambda, daqT, dak, davT, gain_col)


def _gelu_tanh(x):
    return x * (0.5 * (1.0 + jnp.tanh(math.sqrt(2.0 / math.pi) * (x + 0.044715 * (x * x * x)))))


def _compress_kernel(x_ref, pe_ref, w1_ref, w2_ref, o_ref, b_ref, *, feature_major):
    n = x_ref.shape[2]
    half = CMP_STRIDE * NSA_DIM
    x = x_ref[0, 0]
    a = jnp.dot((x + pe_ref[0, 0:1, :]).astype(BF16), w1_ref[0, 0:half, :], preferred_element_type=F32)
    b_ref[0:n, :] = jnp.dot((x + pe_ref[0, 1:2, :]).astype(BF16), w1_ref[0, half:2 * half, :],
                            preferred_element_type=F32)
    b_ref[n:n + 8, :] = jnp.zeros((8, CMP_HIDDEN), F32)
    hid = _gelu_tanh(a + b_ref[1:n + 1, :]).astype(BF16)
    if feature_major:
        o_ref[0, 0] = lax.dot_general(w2_ref[0], hid, _NT, preferred_element_type=F32).astype(o_ref.dtype)
    else:
        o_ref[0, 0] = jnp.dot(hid, w2_ref[0], preferred_element_type=F32).astype(o_ref.dtype)


def _compress(x16, pe2, w1, w2, which, feature_major):
    B, _, n, half = x16.shape
    G = NSA_GROUPS
    if feature_major:
        out_shape, out_block = (B, G, NSA_DIM, n), (1, 1, NSA_DIM, n)
        w2_arr, w2_block = jnp.swapaxes(w2, 1, 2), (1, NSA_DIM, CMP_HIDDEN)
    else:
        out_shape, out_block = (B, G, n, NSA_DIM), (1, 1, n, NSA_DIM)
        w2_arr, w2_block = w2, (1, CMP_HIDDEN, NSA_DIM)
    return pl.pallas_call(
        functools.partial(_compress_kernel, feature_major=feature_major),
        grid=(B, G),
        in_specs=[pl.BlockSpec((1, 1, n, half), lambda b, g: (b, which * G + g, 0, 0)),
                  pl.BlockSpec((1, 2, half), lambda b, g: (which, 0, 0)),
                  pl.BlockSpec((1, 2 * half, CMP_HIDDEN), lambda b, g: (which, 0, 0)),
                  pl.BlockSpec(w2_block, lambda b, g: (which, 0, 0))],
        out_specs=pl.BlockSpec(out_block, lambda b, g: (b, g, 0, 0)),
        out_shape=jax.ShapeDtypeStruct(out_shape, BF16),
        scratch_shapes=[pltpu.VMEM((n + 8, CMP_HIDDEN), F32)],
        compiler_params=_cparams(("parallel", "parallel")),
        name="compress_v" if feature_major else "compress_k",
    )(x16, pe2, w1, w2_arr.astype(BF16))


def _nsa_kernel(qT_ref, gT_ref, kc_ref, vcT_ref, ks_ref, vsT_ref, kw_ref, vwT_ref, o_ref,
                qg_ref, qz_ref, imp_ref, sel_ref, pc_ref, sa_ref, sb_ref, pa_ref, pb_ref, m_ref, acc_ref, out_ref,
                *, tq, tk, S):
    g = pl.program_id(1)
    qi = pl.program_id(2)
    d = NSA_DIM
    R = NSA_REP
    ncb = S // CMP_STRIDE
    nsb = S // SEL_BLOCK
    q0 = qi * tq
    jd = q0 // tk
    ratio = SEL_BLOCK // CMP_STRIDE
    per_tile = tk // SEL_BLOCK

    for r in range(R):
        qg_ref[:, r * tq:(r + 1) * tq] = qT_ref[0, r * d:(r + 1) * d, :]
    qz_ref[...] = jnp.zeros_like(qz_ref)
    for gg in range(NSA_GROUPS):
        @pl.when(g == gg)
        def _():
            qz_ref[gg * d:(gg + 1) * d, :] = qg_ref[...]

    def gate(r, br):
        return _sigmoid(gT_ref[0, r * 3 + br:r * 3 + br + 1, :])

    qpos = q0 + lax.broadcasted_iota(jnp.int32, (1, tq), 1)

    s_all = jnp.dot(kc_ref[0, 0], qg_ref[...], preferred_element_type=F32)
    cend = lax.broadcasted_iota(jnp.int32, (ncb, tq), 0) * CMP_STRIDE + (CMP_BLOCK - 1)
    cbias = jnp.where(cend <= qpos, 0.0, NEG)
    imp = jnp.zeros((ncb, tq), F32)
    for r in range(R):
        cols = slice(r * tq, (r + 1) * tq)
        s = s_all[:, cols] + cbias
        mx = jnp.maximum(jnp.max(s, axis=0, keepdims=True), M_INIT)
        p = jnp.exp2(s - mx)
        den = jnp.sum(p, axis=0, keepdims=True)
        p = p / jnp.where(den > 0, den, 1.0)
        imp = imp + p
        pc_ref[:, cols] = p.astype(BF16)
    o_cmp = jnp.dot(vcT_ref[0, 0], pc_ref[...], preferred_element_type=F32)
    for r in range(R):
        cols = slice(r * tq, (r + 1) * tq)
        out_ref[:, cols] = gate(r, 0) * o_cmp[:, cols]

    slabs = []
    for c in range(tq // LANES):
        imp_ref[c, 0:8, :] = jnp.zeros((8, LANES), F32)
        imp_ref[c, 8:8 + ncb, :] = imp[:, c * LANES:(c + 1) * LANES]
        imp_ref[c, 8 + ncb:16 + ncb, :] = jnp.zeros((8, LANES), F32)
        slab = jnp.zeros((nsb, LANES), F32)
        for o in range(-1, ratio):
            slab = slab + imp_ref[c, pl.ds(8 + o, nsb, stride=ratio), :]
        slabs.append(slab)
    p_slc = slabs[0] if len(slabs) == 1 else jnp.concatenate(slabs, axis=1)
    blk = lax.broadcasted_iota(jnp.int32, (nsb, tq), 0)
    cur = lax.shift_right_logical(qpos, int(math.log2(SEL_BLOCK)))
    forced = (blk == 0) | (blk == cur) | (blk == cur - 1)
    causal_blk = blk * SEL_BLOCK <= qpos
    score = jnp.where(forced, FORCE_SCORE, jnp.where(causal_blk, p_slc, -1.0))
    blk_f = blk.astype(F32)
    for _ in range(SEL_TOPK):
        mx = jnp.max(score, axis=0, keepdims=True)
        first = jnp.min(jnp.where(score == mx, blk_f, float(nsb)), axis=0, keepdims=True)
        score = jnp.where(blk_f == first, TAKEN, score)
    sel_ref[...] = jnp.where(score == TAKEN, 0.0, NEG)

    col_groups = [slice(r * tq, (r + 1) * tq) for r in range(R)]
    max_tile = S // tk - 1

    def k_tile(k_ref_, j):
        return k_ref_[0, pl.ds(pl.multiple_of(j * tk, tk), tk), :]

    def vT_tile_of(vT_ref_):
        def vT_tile(j):
            return vT_ref_[0, :, pl.ds(pl.multiple_of(j * tk, tk), tk)]
        return vT_tile

    def softmax_with(bias_of):
        def softmax(s_ref, p_ref, j, tail):
            bias = bias_of(j, tail)
            return [_softmax_cols(s_ref[:, cols] if bias is None else s_ref[:, cols] + bias, m_ref, p_ref, cols)
                    for cols in col_groups]
        return softmax

    def flush(br):
        for r, cols in enumerate(col_groups):
            den = acc_ref[V_DIM:V_DIM + 1, cols]
            out_ref[:, cols] += gate(r, br) * (acc_ref[0:V_DIM, cols] / jnp.where(den > 0, den, 1.0))

    def kpos_of(j):
        return j * tk + lax.broadcasted_iota(jnp.int32, (tk, tq), 0)

    def win_bias(j, tail):
        kpos = kpos_of(j)
        rel = qpos - kpos
        return jnp.where((rel >= 0) & (rel < WINDOW) & (kpos >= 0), 0.0, NEG)

    def win_scores(j):
        return jnp.dot(k_tile(kw_ref, j), qz_ref[...], preferred_element_type=F32)

    n_win = WINDOW // tk + 1
    _pipelined_attention(win_scores, vT_tile_of(vwT_ref), (sa_ref, sb_ref), (pa_ref, pb_ref), m_ref, acc_ref,
                         softmax_with(win_bias), first=jd - (n_win - 1), n_plain_pairs=0, n_tail=n_win,
                         max_tile=max_tile, col_groups=col_groups)
    flush(2)

    def sel_scores(j):
        return jnp.dot(k_tile(ks_ref, j), qz_ref[...], preferred_element_type=F32)

    def sel_softmax(s_ref, p_ref, j, tail):
        first_blk = jnp.clip(j, 0, max_tile) * per_tile
        run_bias = [sel_ref[pl.ds(first_blk + i, 1), :] for i in range(per_tile)]
        if not tail:
            return [_softmax_cols_run_bias(s_ref, run_bias, SEL_BLOCK, m_ref, p_ref, cols) for cols in col_groups]
        bias = jnp.concatenate([jnp.broadcast_to(b, (SEL_BLOCK, tq)) for b in run_bias], axis=0)
        bias = jnp.where(kpos_of(j) <= qpos, bias, NEG)
        return [_softmax_cols(s_ref[:, cols] + bias, m_ref, p_ref, cols) for cols in col_groups]

    _pipelined_attention(sel_scores, vT_tile_of(vsT_ref), (sa_ref, sb_ref), (pa_ref, pb_ref), m_ref, acc_ref,
                         sel_softmax, first=0, n_plain_pairs=jd // 2, n_tail=2,
                         max_tile=max_tile, col_groups=col_groups)
    flush(1)

    for r in range(R):
        o_ref[0, r * d:(r + 1) * d, :] = out_ref[:, r * tq:(r + 1) * tq].astype(o_ref.dtype)


def _nsa(nsqT, nsgT, kcmp, vcmpT, nsk, nsvT, tq=128, tk=256):
    B, _, S = nsqT.shape
    G, R, d = NSA_GROUPS, NSA_REP, NSA_DIM
    ncb = S // CMP_STRIDE
    nsb = S // SEL_BLOCK
    return pl.pallas_call(
        functools.partial(_nsa_kernel, tq=tq, tk=tk, S=S),
        grid=(B, G, S // tq),
        in_specs=[pl.BlockSpec((1, R * d, tq), lambda b, g, i: (b, g, i)),
                  pl.BlockSpec((1, GATE_PAD, tq), lambda b, g, i: (b, g, i)),
                  pl.BlockSpec((1, 1, ncb, d), lambda b, g, i: (b, g, 0, 0)),
                  pl.BlockSpec((1, 1, d, ncb), lambda b, g, i: (b, g, 0, 0)),
                  pl.BlockSpec((1, S, LANES), lambda b, g, i: (b, 0, 0)),
                  pl.BlockSpec((1, V_EXT, S), lambda b, g, i: (b, g, 0)),
                  pl.BlockSpec((1, S, LANES), lambda b, g, i: (b, 0, 1)),
                  pl.BlockSpec((1, V_EXT, S), lambda b, g, i: (b, G + g, 0))],
        out_specs=pl.BlockSpec((1, R * d, tq), lambda b, g, i: (b, g, i)),
        out_shape=jax.ShapeDtypeStruct((B, NSA_WIDTH, S), BF16),
        scratch_shapes=[pltpu.VMEM((d, R * tq), BF16),
                        pltpu.VMEM((LANES, R * tq), BF16),
                        pltpu.VMEM((tq // LANES, ncb + 16, LANES), F32),
                        pltpu.VMEM((nsb, tq), F32),
                        pltpu.VMEM((ncb, R * tq), BF16),
                        pltpu.VMEM((tk, R * tq), F32),
                        pltpu.VMEM((tk, R * tq), F32),
                        pltpu.VMEM((tk, R * tq), BF16),
                        pltpu.VMEM((tk, R * tq), BF16),
                        pltpu.VMEM((1, R * tq), F32),
                        pltpu.VMEM((V_EXT, R * tq), F32),
                        pltpu.VMEM((d, R * tq), F32)],
        compiler_params=_cparams(("parallel", "parallel", "arbitrary")),
        name="nsa",
    )(nsqT, nsgT, kcmp, vcmpT, nsk, nsvT, nsk, nsvT)


def _outproj_kernel(x_ref, yml_ref, ydaT_ref, ynsT_ref, wo_ref, o_ref):
    acc = x_ref[0] + jnp.dot(yml_ref[0].astype(BF16), wo_ref[0:ML_WIDTH, :], preferred_element_type=F32)
    acc = acc + lax.dot_general(ydaT_ref[0], wo_ref[ML_WIDTH:ML_WIDTH + DA_WIDTH, :], _TN,
                                preferred_element_type=F32)
    acc = acc + lax.dot_general(ynsT_ref[0], wo_ref[ML_WIDTH + DA_WIDTH:, :], _TN,
                                preferred_element_type=F32)
    o_ref[0] = acc


def _outproj(x, yml, ydaT, ynsT, wo, tm=512):
    B, S, D = x.shape
    return pl.pallas_call(
        _outproj_kernel,
        grid=(B, S // tm),
        in_specs=[pl.BlockSpec((1, tm, D), lambda b, i: (b, i, 0)),
                  pl.BlockSpec((1, tm, ML_WIDTH), lambda b, i: (b, i, 0)),
                  pl.BlockSpec((1, DA_WIDTH, tm), lambda b, i: (b, 0, i)),
                  pl.BlockSpec((1, NSA_WIDTH, tm), lambda b, i: (b, 0, i)),
                  pl.BlockSpec((D, D), lambda b, i: (0, 0))],
        out_specs=pl.BlockSpec((1, tm, D), lambda b, i: (b, i, 0)),
        out_shape=jax.ShapeDtypeStruct((B, S, D), F32),
        compiler_params=_cparams(("parallel", "parallel")),
        name="outproj",
    )(x, yml, ydaT, ynsT, wo)


def _ffn_kernel(x_ref, g_ref, w1_ref, w2_ref, gf_ref, o_ref, hb_ref, acc_ref, *, final):
    j = pl.program_id(1)

    @pl.when(j == 0)
    def _():
        x = x_ref[...]
        hb_ref[...] = _rms(x, g_ref[...]).astype(BF16)
        acc_ref[...] = x

    u = jnp.dot(hb_ref[...], w1_ref[...], preferred_element_type=F32)
    a = jnp.square(jnp.maximum(u, 0.0)).astype(BF16)
    acc_ref[...] += jnp.dot(a, w2_ref[...], preferred_element_type=F32)

    @pl.when(j == pl.num_programs(1) - 1)
    def _():
        y = acc_ref[...]
        if final:
            y = _rms(y, gf_ref[...])
        o_ref[...] = y


def _ffn(x2d, g, w1, w2, gf, final, tm=1024, tf=1024):
    N, D = x2d.shape
    return pl.pallas_call(
        functools.partial(_ffn_kernel, final=final),
        grid=(N // tm, D_FF // tf),
        in_specs=[pl.BlockSpec((tm, D), lambda i, j: (i, 0)),
                  pl.BlockSpec((1, D), lambda i, j: (0, 0)),
                  pl.BlockSpec((D, tf), lambda i, j: (0, j)),
                  pl.BlockSpec((tf, D), lambda i, j: (j, 0)),
                  pl.BlockSpec((1, D), lambda i, j: (0, 0))],
        out_specs=pl.BlockSpec((tm, D), lambda i, j: (i, 0)),
        out_shape=jax.ShapeDtypeStruct((N, D), F32),
        scratch_shapes=[pltpu.VMEM((tm, D), BF16), pltpu.VMEM((tm, D), F32)],
        compiler_params=_cparams(("parallel", "arbitrary")),
        name="ffn",
    )(x2d, g, w1, w2, gf)


def _split_w_in(w_in_l):
    edges = np.concatenate([[0], np.cumsum(IN_SIZES)])
    return [w_in_l[:, int(edges[i]):int(edges[i + 1])] for i in range(len(IN_SIZES))]


def _pad_cols(w, n):
    return jnp.pad(w, ((0, 0), (0, n - w.shape[1])))


def _inproj_weights(w_in_l):
    (ml_q, ml_k, ml_v, ml_o, ml_i, ml_f, da_q, da_k, da_v,
     ns_q, ns_kc, ns_vc, ns_ks, ns_vs, ns_kw, ns_vw, ns_g) = _split_w_in(w_in_l)
    wt = jnp.concatenate([ml_q, ml_k, ml_v, ml_o, _pad_cols(jnp.concatenate([ml_i, ml_f], 1), LANES),
                          da_k, ns_ks, ns_kw, ns_kc, ns_vc], axis=1)
    per_group = NSA_REP * 3
    ns_g_pad = jnp.concatenate(
        [_pad_cols(ns_g[:, gi * per_group:(gi + 1) * per_group], GATE_PAD) for gi in range(NSA_GROUPS)], axis=1)
    wf = jnp.concatenate([da_q, da_v, ns_q, ns_vs, ns_vw, ns_g_pad], axis=1)
    return wt.astype(BF16), wf.T.astype(BF16)


def _half_blocks(nsc):
    B, S, _ = nsc.shape
    n = S // CMP_STRIDE
    t = nsc.reshape(B, n, CMP_STRIDE, 2 * NSA_GROUPS, NSA_DIM)
    return t.transpose(0, 3, 1, 2, 4).reshape(B, 2 * NSA_GROUPS, n, CMP_STRIDE * NSA_DIM)


def kernel(x, norm1, w_in, ml_conv, ml_gate_bias, ml_norm, da_lambda, da_norm, nsa_pe,
           nsa_w1, nsa_w2, w_out, norm2, w_ff1, w_ff2, final_norm):
    B, S, D = x.shape
    depth = norm1.shape[0]
    for l in range(depth):
        wt, wfT = _inproj_weights(w_in[l])
        (zml, zg, dak, nsk, nsc, daqT, davT, nsqT, nsvT, nsgT) = _inproj(x, norm1[l][None, :], wt, wfT)

        yml = _mlstm(zml, zg, ml_conv[l], _pad_cols(ml_gate_bias[l][None, :], LANES), ml_norm[l][None, :])

        lam_init = 0.8 - 0.6 * math.exp(-0.3 * l)
        ydaT = _diff_attn(da_lambda[l], daqT, dak, davT, da_norm[l][:, None], lam_init)

        x16 = _half_blocks(nsc)
        pe2 = nsa_pe[l].reshape(2, 2, CMP_STRIDE * NSA_DIM)
        w1b = nsa_w1[l].astype(BF16)
        kcmp = _compress(x16, pe2, w1b, nsa_w2[l], 0, False)
        vcmpT = _compress(x16, pe2, w1b, nsa_w2[l], 1, True)
        ynsT = _nsa(nsqT, nsgT, kcmp, vcmpT, nsk, nsvT)

        x = _outproj(x, yml, ydaT, ynsT, w_out[l].astype(BF16))
        x = _ffn(x.reshape(B * S, D), norm2[l][None, :], w_ff1[l].astype(BF16), w_ff2[l].astype(BF16),
                 final_norm[None, :], final=(l == depth - 1)).reshape(B, S, D)
    return x
```

```python
import functools
import math

import numpy as np
import jax
import jax.numpy as jnp
from jax import lax
from jax.experimental import pallas as pl
from jax.experimental.pallas import tpu as pltpu

F32 = jnp.float32
BF16 = jnp.bfloat16

D_MODEL = 1024
ML_HEADS = 4
ML_DIM = 64
ML_WIDTH = ML_HEADS * ML_DIM
ML_TILE_CHUNK = 128
CONV_W = 4
DA_HEADS = 4
DA_QK_DIM = 32
DA_V_DIM = 64
DA_WIDTH = DA_HEADS * DA_V_DIM
NSA_HEADS = 8
NSA_GROUPS = 2
NSA_REP = NSA_HEADS // NSA_GROUPS
NSA_DIM = 64
NSA_WIDTH = NSA_HEADS * NSA_DIM
NSA_KV = NSA_GROUPS * NSA_DIM
CMP_BLOCK = 32
CMP_STRIDE = 16
CMP_HIDDEN = 4 * NSA_DIM
SEL_BLOCK = 64
SEL_TOPK = 16
WINDOW = 512
D_FF = 4 * D_MODEL
EPS = 1e-6
FORCE_SCORE = 1e4
IN_SIZES = (ML_WIDTH, ML_WIDTH, ML_WIDTH, ML_WIDTH, ML_HEADS, ML_HEADS,
            2 * DA_HEADS * DA_QK_DIM, 2 * DA_HEADS * DA_QK_DIM, DA_WIDTH,
            NSA_WIDTH, NSA_KV, NSA_KV, NSA_KV, NSA_KV, NSA_KV, NSA_KV, 3 * NSA_HEADS)

LANES = 128
GATE_PAD = 16
NEG = -1e30
M_INIT = -1e29
LOOP_PAIRS = 4
TAKEN = -3e38
LOG2E = 1.4426950408889634
V_DIM = 64
V_EXT = 80
VMEM_LIMIT = 56 * 1024 * 1024

_T_ML = (0, 1024)
_T_MLG = (1024, 1152)
_T_DAK = (1152, 1408)
_T_NSK = (1408, 1664)
_T_NSC = (1664, 1920)
_T_COLS = 1920
_F_DAQ = (0, 256)
_F_DAV = (256, 512)
_F_NSQ = (512, 1024)
_F_NSV = (1024, 1280)
_F_NSG = (1280, 1280 + NSA_GROUPS * GATE_PAD)
_F_ROWS = _F_NSG[1]

_NT = (((1,), (1,)), ((), ()))
_TN = (((0,), (0,)), ((), ()))


def _cparams(sem):
    return pltpu.CompilerParams(dimension_semantics=sem, vmem_limit_bytes=VMEM_LIMIT)


def _rms(x, g):
    return x * lax.rsqrt(jnp.mean(x * x, axis=-1, keepdims=True) + EPS) * g


def _inproj_kernel(x_ref, g_ref, wt_ref, wf_ref,
                   zml_ref, zg_ref, dak_ref, nsk_ref, nsc_ref,
                   daqT_ref, davT_ref, nsqT_ref, nsvT_ref, nsgT_ref):
    hb = _rms(x_ref[0], g_ref[...]).astype(BF16)

    def tdot(span):
        return jnp.dot(hb, wt_ref[:, span[0]:span[1]], preferred_element_type=F32)

    def fdot(span):
        return lax.dot_general(wf_ref[span[0]:span[1], :], hb, _NT, preferred_element_type=F32)

    zml_ref[0] = tdot(_T_ML)
    zg_ref[0] = tdot(_T_MLG)
    dak_ref[0] = tdot(_T_DAK).astype(BF16)
    nsk_ref[0] = tdot(_T_NSK).astype(BF16)
    nsc_ref[0] = tdot(_T_NSC)
    daqT_ref[0] = (fdot(_F_DAQ) * (DA_QK_DIM ** -0.5 * LOG2E)).astype(BF16)
    nsqT_ref[0] = (fdot(_F_NSQ) * (NSA_DIM ** -0.5 * LOG2E)).astype(BF16)
    nsgT_ref[0] = fdot(_F_NSG)
    tm = hb.shape[0]
    for v_ref, span in ((davT_ref, _F_DAV), (nsvT_ref, _F_NSV)):
        v = fdot(span).astype(BF16)
        for h in range((span[1] - span[0]) // V_DIM):
            v_ref[0, h * V_EXT:h * V_EXT + V_DIM, :] = v[h * V_DIM:(h + 1) * V_DIM, :]
            v_ref[0, h * V_EXT + V_DIM:(h + 1) * V_EXT, :] = jnp.ones((V_EXT - V_DIM, tm), BF16)


def _inproj(x, g, wt, wfT, tm=512):
    B, S, D = x.shape
    tok = lambda w, dt: jax.ShapeDtypeStruct((B, S, w), dt)
    feat = lambda r, dt: jax.ShapeDtypeStruct((B, r, S), dt)
    tspec = lambda w: pl.BlockSpec((1, tm, w), lambda b, i: (b, i, 0))
    fspec = lambda r: pl.BlockSpec((1, r, tm), lambda b, i: (b, 0, i))
    return pl.pallas_call(
        _inproj_kernel,
        grid=(B, S // tm),
        in_specs=[tspec(D),
                  pl.BlockSpec((1, D), lambda b, i: (0, 0)),
                  pl.BlockSpec((D, _T_COLS), lambda b, i: (0, 0)),
                  pl.BlockSpec((_F_ROWS, D), lambda b, i: (0, 0))],
        out_specs=[tspec(1024), tspec(128), tspec(256), tspec(256), tspec(256),
                   fspec(256), fspec(4 * V_EXT), fspec(512), fspec(4 * V_EXT), fspec(NSA_GROUPS * GATE_PAD)],
        out_shape=[tok(1024, F32), tok(128, F32), tok(256, BF16), tok(256, BF16), tok(256, F32),
                   feat(256, BF16), feat(4 * V_EXT, BF16), feat(512, BF16), feat(4 * V_EXT, BF16),
                   feat(NSA_GROUPS * GATE_PAD, F32)],
        compiler_params=_cparams(("parallel", "parallel")),
        name="inproj",
    )(x, g, wt, wfT)


def _log_sigmoid(x):
    return jnp.minimum(x, 0.0) - jnp.log1p(jnp.exp(-jnp.abs(x)))


def _sigmoid(x):
    return 1.0 / (1.0 + jnp.exp(-x))


def _mlstm_kernel(zml_ref, zg_ref, cw_ref, gb_ref, nrm_ref, y_ref,
                  pad_ref, c_ref, n_ref, m_ref, *, T, L):
    d = ML_DIM
    W = max(L, d)
    t = pl.program_id(1)

    @pl.when(t == 0)
    def _():
        pad_ref[0:8, :] = jnp.zeros((8, 2 * ML_WIDTH), F32)
        c_ref[...] = jnp.zeros_like(c_ref)
        n_ref[...] = jnp.zeros_like(n_ref)
        m_ref[...] = jnp.zeros_like(m_ref)

    @pl.when(t > 0)
    def _():
        pad_ref[0:8, :] = pad_ref[T:T + 8, :]

    pad_ref[8:8 + T, :] = zml_ref[0, :, 0:2 * ML_WIDTH]
    conv = cw_ref[0:1, :] * pad_ref[5:5 + T, :]
    for j in range(1, CONV_W):
        conv = conv + cw_ref[j:j + 1, :] * pad_ref[5 + j:5 + j + T, :]
    qk = conv * _sigmoid(conv)
    gates = zg_ref[0] + gb_ref[...]
    logf = _log_sigmoid(gates)

    row = lax.broadcasted_iota(jnp.int32, (L, L), 0)
    col = lax.broadcasted_iota(jnp.int32, (L, L), 1)
    causal = col <= row
    tril = causal.astype(F32)
    triu = (row <= col).astype(F32)

    lane_of = lax.broadcasted_iota(jnp.int32, (LANES, W), 0)
    pick = [(lane_of == g).astype(F32) for g in range(2 * ML_HEADS)]

    for c in range(T // L):
        r0 = c * L
        g_c = gates[r0:r0 + L, :]
        lf_c = logf[r0:r0 + L, :]
        b_cols = jnp.dot(tril, lf_c, preferred_element_type=F32,
                         precision=lax.Precision.HIGHEST)
        b_rows = jnp.dot(lf_c.T, triu, preferred_element_type=F32,
                         precision=lax.Precision.HIGHEST)
        g_rows = g_c.T
        for h in range(ML_HEADS):
            b_col = jnp.dot(b_cols, pick[ML_HEADS + h], preferred_element_type=F32,
                            precision=lax.Precision.HIGHEST)
            ig_col = jnp.dot(g_c, pick[h], preferred_element_type=F32,
                             precision=lax.Precision.HIGHEST)
            b_row = b_rows[ML_HEADS + h:ML_HEADS + h + 1, :]
            ig_row = g_rows[h:h + 1, :]
            qh = qk[r0:r0 + L, h * d:(h + 1) * d]
            kh = qk[r0:r0 + L, ML_WIDTH + h * d:ML_WIDTH + (h + 1) * d] * (d ** -0.5)
            vh = zml_ref[0, r0:r0 + L, 2 * ML_WIDTH + h * d:2 * ML_WIDTH + (h + 1) * d]
            oh = zml_ref[0, r0:r0 + L, 3 * ML_WIDTH + h * d:3 * ML_WIDTH + (h + 1) * d]
            c_prev = c_ref[h]
            n_prev = n_ref[h, 0:1, :]
            m_prev = m_ref[h, 0:1, :]

            g_tot = b_col[L - 1:L, :]
            a_row = g_tot[:, 0:L] - b_row + ig_row
            a_max = jnp.max(a_row, axis=1, keepdims=True)
            w_col = jnp.exp((g_tot - b_col + ig_col - a_max)[:, 0:d])
            c_loc = lax.dot_general(vh * w_col, kh, _TN, preferred_element_type=F32)
            n_loc = jnp.sum(kh * w_col, axis=0, keepdims=True)

            dmat = jnp.where(causal, b_col[:, 0:L] - b_row + ig_row, NEG)
            inter_log = b_col + m_prev
            m_t = jnp.maximum(inter_log, jnp.max(dmat, axis=1, keepdims=True))
            qkt = lax.dot_general(qh, kh, _NT, preferred_element_type=F32)
            wts = jnp.exp(dmat - m_t[:, 0:L]) * qkt
            s_inter = jnp.exp((inter_log - m_t)[:, 0:d])
            num = (jnp.dot(wts, vh, preferred_element_type=F32)
                   + s_inter * lax.dot_general(qh, c_prev, _NT, preferred_element_type=F32))
            den = (jnp.sum(wts, axis=1, keepdims=True)
                   + s_inter * jnp.sum(qh * n_prev, axis=1, keepdims=True))
            hh = num / jnp.maximum(jnp.abs(den), jnp.exp(-m_t[:, 0:d]))

            m_new = jnp.maximum(g_tot + m_prev, a_max)
            s_prev = jnp.exp((g_tot + m_prev - m_new)[:, 0:d])
            s_loc = jnp.exp((a_max - m_new)[:, 0:d])
            c_ref[h] = s_prev * c_prev + s_loc * c_loc
            n_ref[h] = jnp.broadcast_to(s_prev * n_prev + s_loc * n_loc, (8, d))
            m_ref[h] = jnp.broadcast_to(m_new, (8, W))

            yh = _sigmoid(oh) * hh
            y_ref[0, r0:r0 + L, h * d:(h + 1) * d] = _rms(yh, nrm_ref[0:1, h * d:(h + 1) * d])


def _mlstm(zml, zg, conv_w, gate_bias, norm_g, T=256, chunk=ML_TILE_CHUNK):
    B, S, _ = zml.shape
    return pl.pallas_call(
        functools.partial(_mlstm_kernel, T=T, L=chunk),
        grid=(B, S // T),
        in_specs=[pl.BlockSpec((1, T, 1024), lambda b, t: (b, t, 0)),
                  pl.BlockSpec((1, T, 128), lambda b, t: (b, t, 0)),
                  pl.BlockSpec((CONV_W, 2 * ML_WIDTH), lambda b, t: (0, 0)),
                  pl.BlockSpec((1, 128), lambda b, t: (0, 0)),
                  pl.BlockSpec((1, ML_WIDTH), lambda b, t: (0, 0))],
        out_specs=pl.BlockSpec((1, T, ML_WIDTH), lambda b, t: (b, t, 0)),
        out_shape=jax.ShapeDtypeStruct((B, S, ML_WIDTH), F32),
        scratch_shapes=[pltpu.VMEM((T + 8, 2 * ML_WIDTH), F32),
                        pltpu.VMEM((ML_HEADS, ML_DIM, ML_DIM), F32),
                        pltpu.VMEM((ML_HEADS, 8, ML_DIM), F32),
                        pltpu.VMEM((ML_HEADS, 8, max(chunk, ML_DIM)), F32)],
        compiler_params=_cparams(("parallel", "arbitrary")),
        name="mlstm",
    )(zml, zg, conv_w, gate_bias, norm_g)


def _softmax_cols(s, m_ref, p_ref, cols):
    m_old = m_ref[:, cols]
    m_new = jnp.maximum(m_old, jnp.max(s, axis=0, keepdims=True))
    m_ref[:, cols] = m_new
    p_ref[:, cols] = jnp.exp2(s - m_new).astype(BF16)
    return jnp.exp2(m_old - m_new)


def _pipelined_attention(scores, vT_tile, s_refs, p_refs, m_ref, acc_ref, softmax,
                         first, n_plain_pairs, n_tail, max_tile, col_groups):
    sa, sb = s_refs
    pa, pb = p_refs

    def load(j):
        return jnp.clip(j, 0, max_tile)

    def half(j, s_cur, s_nxt, p_cur, p_prev, tail):
        s_nxt[...] = scores(load(j + 1))
        pv = jnp.dot(vT_tile(load(j - 1)), p_prev[...], preferred_element_type=F32)
        alphas = softmax(s_cur, p_cur, j, tail)
        for cols, alpha in zip(col_groups, alphas):
            acc_ref[:, cols] = alpha * (acc_ref[:, cols] + pv[:, cols])

    def pair(j, tail):
        half(j, sa, sb, pa, pb, tail)
        half(j + 1, sb, sa, pb, pa, tail)

    m_ref[...] = jnp.full_like(m_ref, M_INIT)
    acc_ref[...] = jnp.zeros_like(acc_ref)
    pb[...] = jnp.zeros_like(pb)
    sa[...] = scores(load(first))

    if not (isinstance(n_plain_pairs, int) and n_plain_pairs == 0):
        n_trips = n_plain_pairs // LOOP_PAIRS

        def body(i, carry):
            for u in range(LOOP_PAIRS):
                pair(first + 2 * (LOOP_PAIRS * i + u), False)
            return carry

        lax.fori_loop(0, n_trips, body, 0)
        rest = n_plain_pairs - n_trips * LOOP_PAIRS
        for u in range(LOOP_PAIRS - 1):
            @pl.when(u < rest)
            def _():
                pair(first + 2 * (LOOP_PAIRS * n_trips + u), False)
    j = first + 2 * n_plain_pairs
    bufs = ((sa, sb, pa, pb), (sb, sa, pb, pa))
    for t in range(n_tail):
        half(j + t, *bufs[t % 2], True)
    p_last = bufs[(n_tail - 1) % 2][2]
    acc_ref[...] += jnp.dot(vT_tile(load(j + n_tail - 1)), p_last[...], preferred_element_type=F32)


def _da_kernel(lam_ref, qT_ref, k_ref, vT_ref, gain_ref, o_ref,
               qz_ref, sa_ref, sb_ref, pa_ref, pb_ref, m_ref, acc_ref, *, tq, tk, lam_init, S):
    h = pl.program_id(1)
    qi = pl.program_id(2)
    d = DA_QK_DIM
    jd = (qi * tq) // tk
    col_groups = [slice(0, 2 * tq)]

    qz_ref[...] = jnp.zeros_like(qz_ref)
    for hh in range(2):
        @pl.when(h % 2 == hh)
        def _():
            qz_ref[hh * 2 * d:hh * 2 * d + d, 0:tq] = qT_ref[0, 0:d, :]
            qz_ref[hh * 2 * d + d:(hh + 1) * 2 * d, tq:2 * tq] = qT_ref[0, d:2 * d, :]

    def scores(j):
        return jnp.dot(k_ref[0, pl.ds(pl.multiple_of(j * tk, tk), tk), :], qz_ref[...], preferred_element_type=F32)

    def vT_tile(j):
        return vT_ref[0, :, pl.ds(pl.multiple_of(j * tk, tk), tk)]

    def softmax(s_ref, p_ref, j, tail):
        s = s_ref[...]
        if tail:
            kpos = j * tk + lax.broadcasted_iota(jnp.int32, (tk, 2 * tq), 0)
            qpos = qi * tq + (lax.broadcasted_iota(jnp.int32, (tk, 2 * tq), 1) & (tq - 1))
            s = jnp.where(kpos <= qpos, s, NEG)
        return [_softmax_cols(s, m_ref, p_ref, col_groups[0])]

    _pipelined_attention(scores, vT_tile, (sa_ref, sb_ref), (pa_ref, pb_ref), m_ref, acc_ref, softmax,
                         first=0, n_plain_pairs=jd // 2, n_tail=2, max_tile=S // tk - 1, col_groups=col_groups)

    lp = lam_ref[...]
    lam = (jnp.exp(jnp.sum(lp[0:1] * lp[1:2], axis=1, keepdims=True))
           - jnp.exp(jnp.sum(lp[2:3] * lp[3:4], axis=1, keepdims=True)) + lam_init)
    o1 = acc_ref[0:V_DIM, 0:tq] / acc_ref[V_DIM:V_DIM + 1, 0:tq]
    o2 = acc_ref[0:V_DIM, tq:2 * tq] / acc_ref[V_DIM:V_DIM + 1, tq:2 * tq]
    o = o1 - lam * o2
    y = o * lax.rsqrt(jnp.mean(o * o, axis=0, keepdims=True) + EPS) * gain_ref[...]
    o_ref[0] = (y * (1.0 - lam_init)).astype(o_ref.dtype)


def _diff_attn(da_lambda, daqT, dak, davT, gain_col, lam_init, tq=256, tk=256):
    B, _, S = daqT.shape
    return pl.pallas_call(
        functools.partial(_da_kernel, tq=tq, tk=tk, lam_init=lam_init, S=S),
        grid=(B, DA_HEADS, S // tq),
        in_specs=[pl.BlockSpec((4, DA_QK_DIM), lambda b, h, i: (0, 0)),
                  pl.BlockSpec((1, DA_V_DIM, tq), lambda b, h, i: (b, h, i)),
                  pl.BlockSpec((1, S, LANES), lambda b, h, i: (b, 0, h // 2)),
                  pl.BlockSpec((1, V_EXT, S), lambda b, h, i: (b, h, 0)),
                  pl.BlockSpec((DA_V_DIM, 1), lambda b, h, i: (h, 0))],
        out_specs=pl.BlockSpec((1, DA_V_DIM, tq), lambda b, h, i: (b, h, i)),
        out_shape=jax.ShapeDtypeStruct((B, DA_WIDTH, S), BF16),
        scratch_shapes=[pltpu.VMEM((LANES, 2 * tq), BF16),
                        pltpu.VMEM((tk, 2 * tq), F32),
                        pltpu.VMEM((tk, 2 * tq), F32),
                        pltpu.VMEM((tk, 2 * tq), BF16),
                        pltpu.VMEM((tk, 2 * tq), BF16),
                        pltpu.VMEM((1, 2 * tq), F32),
                        pltpu.VMEM((V_EXT, 2 * tq), F32)],
        compiler_params=_cparams(("parallel", "parallel", "arbitrary")),
        name="diff_attn",
    )(da_lambda, daqT, dak, davT, gain_col)


def _gelu_tanh(x):
    return x * (0.5 * (1.0 + jnp.tanh(math.sqrt(2.0 / math.pi) * (x + 0.044715 * (x * x * x)))))


def _compress_kernel(x_ref, pe_ref, w1_ref, w2_ref, o_ref, b_ref, *, feature_major):
    n = x_ref.shape[2]
    half = CMP_STRIDE * NSA_DIM
    x = x_ref[0, 0]
    a = jnp.dot((x + pe_ref[0, 0:1, :]).astype(BF16), w1_ref[0, 0:half, :], preferred_element_type=F32)
    b_ref[0:n, :] = jnp.dot((x + pe_ref[0, 1:2, :]).astype(BF16), w1_ref[0, half:2 * half, :],
                            preferred_element_type=F32)
    b_ref[n:n + 8, :] = jnp.zeros((8, CMP_HIDDEN), F32)
    hid = _gelu_tanh(a + b_ref[1:n + 1, :]).astype(BF16)
    if feature_major:
        o_ref[0, 0] = lax.dot_general(w2_ref[0], hid, _NT, preferred_element_type=F32).astype(o_ref.dtype)
    else:
        o_ref[0, 0] = jnp.dot(hid, w2_ref[0], preferred_element_type=F32).astype(o_ref.dtype)


def _compress(x16, pe2, w1, w2, which, feature_major):
    B, _, n, half = x16.shape
    G = NSA_GROUPS
    if feature_major:
        out_shape, out_block = (B, G, NSA_DIM, n), (1, 1, NSA_DIM, n)
        w2_arr, w2_block = jnp.swapaxes(w2, 1, 2), (1, NSA_DIM, CMP_HIDDEN)
    else:
        out_shape, out_block = (B, G, n, NSA_DIM), (1, 1, n, NSA_DIM)
        w2_arr, w2_block = w2, (1, CMP_HIDDEN, NSA_DIM)
    return pl.pallas_call(
        functools.partial(_compress_kernel, feature_major=feature_major),
        grid=(B, G),
        in_specs=[pl.BlockSpec((1, 1, n, half), lambda b, g: (b, which * G + g, 0, 0)),
                  pl.BlockSpec((1, 2, half), lambda b, g: (which, 0, 0)),
                  pl.BlockSpec((1, 2 * half, CMP_HIDDEN), lambda b, g: (which, 0, 0)),
                  pl.BlockSpec(w2_block, lambda b, g: (which, 0, 0))],
        out_specs=pl.BlockSpec(out_block, lambda b, g: (b, g, 0, 0)),
        out_shape=jax.ShapeDtypeStruct(out_shape, BF16),
        scratch_shapes=[pltpu.VMEM((n + 8, CMP_HIDDEN), F32)],
        compiler_params=_cparams(("parallel", "parallel")),
        name="compress_v" if feature_major else "compress_k",
    )(x16, pe2, w1, w2_arr.astype(BF16))


def _nsa_kernel(qT_ref, gT_ref, kc_ref, vcT_ref, ks_ref, vsT_ref, kw_ref, vwT_ref, o_ref,
                qg_ref, qz_ref, imp_ref, sel_ref, pc_ref, sa_ref, sb_ref, pa_ref, pb_ref, m_ref, acc_ref, out_ref,
                *, tq, tk, S):
    g = pl.program_id(1)
    qi = pl.program_id(2)
    d = NSA_DIM
    R = NSA_REP
    ncb = S // CMP_STRIDE
    nsb = S // SEL_BLOCK
    q0 = qi * tq
    jd = q0 // tk
    ratio = SEL_BLOCK // CMP_STRIDE
    per_tile = tk // SEL_BLOCK

    for r in range(R):
        qg_ref[:, r * tq:(r + 1) * tq] = qT_ref[0, r * d:(r + 1) * d, :]
    qz_ref[...] = jnp.zeros_like(qz_ref)
    for gg in range(NSA_GROUPS):
        @pl.when(g == gg)
        def _():
            qz_ref[gg * d:(gg + 1) * d, :] = qg_ref[...]

    def gate(r, br):
        return _sigmoid(gT_ref[0, r * 3 + br:r * 3 + br + 1, :])

    qpos = q0 + lax.broadcasted_iota(jnp.int32, (1, tq), 1)

    s_all = jnp.dot(kc_ref[0, 0], qg_ref[...], preferred_element_type=F32)
    cend = lax.broadcasted_iota(jnp.int32, (ncb, tq), 0) * CMP_STRIDE + (CMP_BLOCK - 1)
    cbias = jnp.where(cend <= qpos, 0.0, NEG)
    imp = jnp.zeros((ncb, tq), F32)
    for r in range(R):
        cols = slice(r * tq, (r + 1) * tq)
        s = s_all[:, cols] + cbias
        mx = jnp.maximum(jnp.max(s, axis=0, keepdims=True), M_INIT)
        p = jnp.exp2(s - mx)
        den = jnp.sum(p, axis=0, keepdims=True)
        p = p / jnp.where(den > 0, den, 1.0)
        imp = imp + p
        pc_ref[:, cols] = p.astype(BF16)
    o_cmp = jnp.dot(vcT_ref[0, 0], pc_ref[...], preferred_element_type=F32)
    for r in range(R):
        cols = slice(r * tq, (r + 1) * tq)
        out_ref[:, cols] = gate(r, 0) * o_cmp[:, cols]

    slabs = []
    for c in range(tq // LANES):
        imp_ref[c, 0:8, :] = jnp.zeros((8, LANES), F32)
        imp_ref[c, 8:8 + ncb, :] = imp[:, c * LANES:(c + 1) * LANES]
        imp_ref[c, 8 + ncb:16 + ncb, :] = jnp.zeros((8, LANES), F32)
        slab = jnp.zeros((nsb, LANES), F32)
        for o in range(-1, ratio):
            slab = slab + imp_ref[c, pl.ds(8 + o, nsb, stride=ratio), :]
        slabs.append(slab)
    p_slc = slabs[0] if len(slabs) == 1 else jnp.concatenate(slabs, axis=1)
    blk = lax.broadcasted_iota(jnp.int32, (nsb, tq), 0)
    cur = lax.shift_right_logical(qpos, int(math.log2(SEL_BLOCK)))
    forced = (blk == 0) | (blk == cur) | (blk == cur - 1)
    causal_blk = blk * SEL_BLOCK <= qpos
    score = jnp.where(forced, FORCE_SCORE, jnp.where(causal_blk, p_slc, -1.0))
    blk_f = blk.astype(F32)
    for _ in range(SEL_TOPK):
        mx = jnp.max(score, axis=0, keepdims=True)
        first = jnp.min(jnp.where(score == mx, blk_f, float(nsb)), axis=0, keepdims=True)
        score = jnp.where(blk_f == first, TAKEN, score)
    sel_ref[...] = jnp.where(score == TAKEN, 0.0, NEG)

    head_cols = [slice(r * tq, (r + 1) * tq) for r in range(R)]
    col_groups = [slice(0, R * tq)]
    max_tile = S // tk - 1

    def k_tile(k_ref_, j):
        return k_ref_[0, pl.ds(pl.multiple_of(j * tk, tk), tk), :]

    def vT_tile_of(vT_ref_):
        def vT_tile(j):
            return vT_ref_[0, :, pl.ds(pl.multiple_of(j * tk, tk), tk)]
        return vT_tile

    def softmax_with(bias_of):
        def softmax(s_ref, p_ref, j, tail):
            bias = bias_of(j, tail)
            s = s_ref[...] + jnp.concatenate([bias] * R, axis=1)
            return [_softmax_cols(s, m_ref, p_ref, col_groups[0])]
        return softmax

    def flush(br):
        for r, cols in enumerate(head_cols):
            den = acc_ref[V_DIM:V_DIM + 1, cols]
            out_ref[:, cols] += gate(r, br) * (acc_ref[0:V_DIM, cols] / jnp.where(den > 0, den, 1.0))

    def kpos_of(j):
        return j * tk + lax.broadcasted_iota(jnp.int32, (tk, tq), 0)

    def win_bias(j, tail):
        kpos = kpos_of(j)
        rel = qpos - kpos
        return jnp.where((rel >= 0) & (rel < WINDOW) & (kpos >= 0), 0.0, NEG)

    def win_scores(j):
        return jnp.dot(k_tile(kw_ref, j), qz_ref[...], preferred_element_type=F32)

    n_win = WINDOW // tk + 1
    _pipelined_attention(win_scores, vT_tile_of(vwT_ref), (sa_ref, sb_ref), (pa_ref, pb_ref), m_ref, acc_ref,
                         softmax_with(win_bias), first=jd - (n_win - 1), n_plain_pairs=0, n_tail=n_win,
                         max_tile=max_tile, col_groups=col_groups)
    flush(2)

    def sel_scores(j):
        return jnp.dot(k_tile(ks_ref, j), qz_ref[...], preferred_element_type=F32)

    def sel_bias(j, tail):
        first_blk = jnp.clip(j, 0, max_tile) * per_tile
        rows = [jnp.broadcast_to(sel_ref[pl.ds(first_blk + i, 1), :], (SEL_BLOCK, tq)) for i in range(per_tile)]
        bias = jnp.concatenate(rows, axis=0)
        if tail:
            bias = jnp.where(kpos_of(j) <= qpos, bias, NEG)
        return bias

    _pipelined_attention(sel_scores, vT_tile_of(vsT_ref), (sa_ref, sb_ref), (pa_ref, pb_ref), m_ref, acc_ref,
                         softmax_with(sel_bias), first=0, n_plain_pairs=jd // 2, n_tail=2,
                         max_tile=max_tile, col_groups=col_groups)
    flush(1)

    for r in range(R):
        o_ref[0, r * d:(r + 1) * d, :] = out_ref[:, r * tq:(r + 1) * tq].astype(o_ref.dtype)


def _nsa(nsqT, nsgT, kcmp, vcmpT, nsk, nsvT, tq=128, tk=256):
    B, _, S = nsqT.shape
    G, R, d = NSA_GROUPS, NSA_REP, NSA_DIM
    ncb = S // CMP_STRIDE
    nsb = S // SEL_BLOCK
    return pl.pallas_call(
        functools.partial(_nsa_kernel, tq=tq, tk=tk, S=S),
        grid=(B, G, S // tq),
        in_specs=[pl.BlockSpec((1, R * d, tq), lambda b, g, i: (b, g, i)),
                  pl.BlockSpec((1, GATE_PAD, tq), lambda b, g, i: (b, g, i)),
                  pl.BlockSpec((1, 1, ncb, d), lambda b, g, i: (b, g, 0, 0)),
                  pl.BlockSpec((1, 1, d, ncb), lambda b, g, i: (b, g, 0, 0)),
                  pl.BlockSpec((1, S, LANES), lambda b, g, i: (b, 0, 0)),
                  pl.BlockSpec((1, V_EXT, S), lambda b, g, i: (b, g, 0)),
                  pl.BlockSpec((1, S, LANES), lambda b, g, i: (b, 0, 1)),
                  pl.BlockSpec((1, V_EXT, S), lambda b, g, i: (b, G + g, 0))],
        out_specs=pl.BlockSpec((1, R * d, tq), lambda b, g, i: (b, g, i)),
        out_shape=jax.ShapeDtypeStruct((B, NSA_WIDTH, S), BF16),
        scratch_shapes=[pltpu.VMEM((d, R * tq), BF16),
                        pltpu.VMEM((LANES, R * tq), BF16),
                        pltpu.VMEM((tq // LANES, ncb + 16, LANES), F32),
                        pltpu.VMEM((nsb, tq), F32),
                        pltpu.VMEM((ncb, R * tq), BF16),
                        pltpu.VMEM((tk, R * tq), F32),
                        pltpu.VMEM((tk, R * tq), F32),
                        pltpu.VMEM((tk, R * tq), BF16),
                        pltpu.VMEM((tk, R * tq), BF16),
                        pltpu.VMEM((1, R * tq), F32),
                        pltpu.VMEM((V_EXT, R * tq), F32),
                        pltpu.VMEM((d, R * tq), F32)],
        compiler_params=_cparams(("parallel", "parallel", "arbitrary")),
        name="nsa",
    )(nsqT, nsgT, kcmp, vcmpT, nsk, nsvT, nsk, nsvT)


def _outproj_kernel(x_ref, yml_ref, ydaT_ref, ynsT_ref, wo_ref, o_ref):
    acc = x_ref[0] + jnp.dot(yml_ref[0].astype(BF16), wo_ref[0:ML_WIDTH, :], preferred_element_type=F32)
    acc = acc + lax.dot_general(ydaT_ref[0], wo_ref[ML_WIDTH:ML_WIDTH + DA_WIDTH, :], _TN,
                                preferred_element_type=F32)
    acc = acc + lax.dot_general(ynsT_ref[0], wo_ref[ML_WIDTH + DA_WIDTH:, :], _TN,
                                preferred_element_type=F32)
    o_ref[0] = acc


def _outproj(x, yml, ydaT, ynsT, wo, tm=512):
    B, S, D = x.shape
    return pl.pallas_call(
        _outproj_kernel,
        grid=(B, S // tm),
        in_specs=[pl.BlockSpec((1, tm, D), lambda b, i: (b, i, 0)),
                  pl.BlockSpec((1, tm, ML_WIDTH), lambda b, i: (b, i, 0)),
                  pl.BlockSpec((1, DA_WIDTH, tm), lambda b, i: (b, 0, i)),
                  pl.BlockSpec((1, NSA_WIDTH, tm), lambda b, i: (b, 0, i)),
                  pl.BlockSpec((D, D), lambda b, i: (0, 0))],
        out_specs=pl.BlockSpec((1, tm, D), lambda b, i: (b, i, 0)),
        out_shape=jax.ShapeDtypeStruct((B, S, D), F32),
        compiler_params=_cparams(("parallel", "parallel")),
        name="outproj",
    )(x, yml, ydaT, ynsT, wo)


def _ffn_kernel(x_ref, g_ref, w1_ref, w2_ref, gf_ref, o_ref, hb_ref, acc_ref, *, final):
    j = pl.program_id(1)

    @pl.when(j == 0)
    def _():
        x = x_ref[...]
        hb_ref[...] = _rms(x, g_ref[...]).astype(BF16)
        acc_ref[...] = x

    u = jnp.dot(hb_ref[...], w1_ref[...], preferred_element_type=F32)
    a = jnp.square(jnp.maximum(u, 0.0)).astype(BF16)
    acc_ref[...] += jnp.dot(a, w2_ref[...], preferred_element_type=F32)

    @pl.when(j == pl.num_programs(1) - 1)
    def _():
        y = acc_ref[...]
        if final:
            y = _rms(y, gf_ref[...])
        o_ref[...] = y


def _ffn(x2d, g, w1, w2, gf, final, tm=1024, tf=1024):
    N, D = x2d.shape
    return pl.pallas_call(
        functools.partial(_ffn_kernel, final=final),
        grid=(N // tm, D_FF // tf),
        in_specs=[pl.BlockSpec((tm, D), lambda i, j: (i, 0)),
                  pl.BlockSpec((1, D), lambda i, j: (0, 0)),
                  pl.BlockSpec((D, tf), lambda i, j: (0, j)),
                  pl.BlockSpec((tf, D), lambda i, j: (j, 0)),
                  pl.BlockSpec((1, D), lambda i, j: (0, 0))],
        out_specs=pl.BlockSpec((tm, D), lambda i, j: (i, 0)),
        out_shape=jax.ShapeDtypeStruct((N, D), F32),
        scratch_shapes=[pltpu.VMEM((tm, D), BF16), pltpu.VMEM((tm, D), F32)],
        compiler_params=_cparams(("parallel", "arbitrary")),
        name="ffn",
    )(x2d, g, w1, w2, gf)


def _split_w_in(w_in_l):
    edges = np.concatenate([[0], np.cumsum(IN_SIZES)])
    return [w_in_l[:, int(edges[i]):int(edges[i + 1])] for i in range(len(IN_SIZES))]


def _pad_cols(w, n):
    return jnp.pad(w, ((0, 0), (0, n - w.shape[1])))


def _inproj_weights(w_in_l):
    (ml_q, ml_k, ml_v, ml_o, ml_i, ml_f, da_q, da_k, da_v,
     ns_q, ns_kc, ns_vc, ns_ks, ns_vs, ns_kw, ns_vw, ns_g) = _split_w_in(w_in_l)
    wt = jnp.concatenate([ml_q, ml_k, ml_v, ml_o, _pad_cols(jnp.concatenate([ml_i, ml_f], 1), LANES),
                          da_k, ns_ks, ns_kw, ns_kc, ns_vc], axis=1)
    per_group = NSA_REP * 3
    ns_g_pad = jnp.concatenate(
        [_pad_cols(ns_g[:, gi * per_group:(gi + 1) * per_group], GATE_PAD) for gi in range(NSA_GROUPS)], axis=1)
    wf = jnp.concatenate([da_q, da_v, ns_q, ns_vs, ns_vw, ns_g_pad], axis=1)
    return wt.astype(BF16), wf.T.astype(BF16)


def _half_blocks(nsc):
    B, S, _ = nsc.shape
    n = S // CMP_STRIDE
    t = nsc.reshape(B, n, CMP_STRIDE, 2 * NSA_GROUPS, NSA_DIM)
    return t.transpose(0, 3, 1, 2, 4).reshape(B, 2 * NSA_GROUPS, n, CMP_STRIDE * NSA_DIM)


def kernel(x, norm1, w_in, ml_conv, ml_gate_bias, ml_norm, da_lambda, da_norm, nsa_pe,
           nsa_w1, nsa_w2, w_out, norm2, w_ff1, w_ff2, final_norm):
    B, S, D = x.shape
    depth = norm1.shape[0]
    for l in range(depth):
        wt, wfT = _inproj_weights(w_in[l])
        (zml, zg, dak, nsk, nsc, daqT, davT, nsqT, nsvT, nsgT) = _inproj(x, norm1[l][None, :], wt, wfT)

        yml = _mlstm(zml, zg, ml_conv[l], _pad_cols(ml_gate_bias[l][None, :], LANES), ml_norm[l][None, :])

        lam_init = 0.8 - 0.6 * math.exp(-0.3 * l)
        ydaT = _diff_attn(da_lambda[l], daqT, dak, davT, da_norm[l][:, None], lam_init)

        x16 = _half_blocks(nsc)
        pe2 = nsa_pe[l].reshape(2, 2, CMP_STRIDE * NSA_DIM)
        w1b = nsa_w1[l].astype(BF16)
        kcmp = _compress(x16, pe2, w1b, nsa_w2[l], 0, False)
        vcmpT = _compress(x16, pe2, w1b, nsa_w2[l], 1, True)
        ynsT = _nsa(nsqT, nsgT, kcmp, vcmpT, nsk, nsvT)

        x = _outproj(x, yml, ydaT, ynsT, w_out[l].astype(BF16))
        x = _ffn(x.reshape(B * S, D), norm2[l][None, :], w_ff1[l].astype(BF16), w_ff2[l].astype(BF16),
                 final_norm[None, :], final=(l == depth - 1)).reshape(B, S, D)
    return x
```

```python
import functools
import math

import numpy as np
import jax
import jax.numpy as jnp
from jax import lax
from jax.experimental import pallas as pl
from jax.experimental.pallas import tpu as pltpu

F32 = jnp.float32
BF16 = jnp.bfloat16

D_MODEL = 1024
ML_HEADS = 4
ML_DIM = 64
ML_WIDTH = ML_HEADS * ML_DIM
ML_TILE_CHUNK = 128
CONV_W = 4
DA_HEADS = 4
DA_QK_DIM = 32
DA_V_DIM = 64
DA_WIDTH = DA_HEADS * DA_V_DIM
NSA_HEADS = 8
NSA_GROUPS = 2
NSA_REP = NSA_HEADS // NSA_GROUPS
NSA_DIM = 64
NSA_WIDTH = NSA_HEADS * NSA_DIM
NSA_KV = NSA_GROUPS * NSA_DIM
CMP_BLOCK = 32
CMP_STRIDE = 16
CMP_HIDDEN = 4 * NSA_DIM
SEL_BLOCK = 64
SEL_TOPK = 16
WINDOW = 512
D_FF = 4 * D_MODEL
EPS = 1e-6
FORCE_SCORE = 1e4
IN_SIZES = (ML_WIDTH, ML_WIDTH, ML_WIDTH, ML_WIDTH, ML_HEADS, ML_HEADS,
            2 * DA_HEADS * DA_QK_DIM, 2 * DA_HEADS * DA_QK_DIM, DA_WIDTH,
            NSA_WIDTH, NSA_KV, NSA_KV, NSA_KV, NSA_KV, NSA_KV, NSA_KV, 3 * NSA_HEADS)

LANES = 128
GATE_PAD = 16
NEG = -1e30
M_INIT = -1e29
LOOP_PAIRS = 4
TAKEN = -3e38
LOG2E = 1.4426950408889634
V_DIM = 64
BF16_ROWS = 16
NSA_TK = 256
V_EXT = 80
VMEM_LIMIT = 56 * 1024 * 1024

_T_ML = (0, 1024)
_T_MLG = (1024, 1152)
_T_DAK = (1152, 1408)
_T_NSK = (1408, 1664)
_T_NSC = (1664, 1920)
_T_COLS = 1920
_F_DAQ = (0, 256)
_F_DAV = (256, 512)
_F_NSQ = (512, 1024)
_F_NSV = (1024, 1280)
_F_NSG = (1280, 1280 + NSA_GROUPS * GATE_PAD)
_F_ROWS = _F_NSG[1]

_NT = (((1,), (1,)), ((), ()))
_TN = (((0,), (0,)), ((), ()))


def _cparams(sem):
    return pltpu.CompilerParams(dimension_semantics=sem, vmem_limit_bytes=VMEM_LIMIT)


def _rms(x, g):
    return x * lax.rsqrt(jnp.mean(x * x, axis=-1, keepdims=True) + EPS) * g


def _inproj_kernel(x_ref, g_ref, wt_ref, wf_ref,
                   zml_ref, zg_ref, dak_ref, nsk_ref, nsks_ref, nsc_ref,
                   daqT_ref, davT_ref, nsqT_ref, nsvT_ref, nsgT_ref):
    hb = _rms(x_ref[0], g_ref[...]).astype(BF16)

    def tdot(span):
        return jnp.dot(hb, wt_ref[:, span[0]:span[1]], preferred_element_type=F32)

    def fdot(span):
        return lax.dot_general(wf_ref[span[0]:span[1], :], hb, _NT, preferred_element_type=F32)

    zml_ref[0] = tdot(_T_ML)
    zg_ref[0] = tdot(_T_MLG)
    dak_ref[0] = tdot(_T_DAK).astype(BF16)
    nsk = tdot(_T_NSK).astype(BF16)
    nsk_ref[0] = nsk
    tm_ = nsk.shape[0]
    row = lax.broadcasted_iota(jnp.int32, (tm_, NSA_DIM), 0)
    lane = lax.broadcasted_iota(jnp.int32, (tm_, NSA_DIM), 1)
    blk_in_tile = lax.shift_right_logical(row & (NSA_TK - 1), int(math.log2(SEL_BLOCK)))
    onehot = jnp.where(lane == blk_in_tile, 1.0, 0.0).astype(BF16)
    for gi in range(NSA_GROUPS):
        nsks_ref[0, gi] = jnp.concatenate([nsk[:, gi * NSA_DIM:(gi + 1) * NSA_DIM], onehot], axis=1)
    nsc_ref[0] = tdot(_T_NSC)
    daqT_ref[0] = (fdot(_F_DAQ) * (DA_QK_DIM ** -0.5 * LOG2E)).astype(BF16)
    nsqT_ref[0] = (fdot(_F_NSQ) * (NSA_DIM ** -0.5 * LOG2E)).astype(BF16)
    nsgT_ref[0] = fdot(_F_NSG)
    tm = hb.shape[0]
    for v_ref, span in ((davT_ref, _F_DAV), (nsvT_ref, _F_NSV)):
        v = fdot(span).astype(BF16)
        for h in range((span[1] - span[0]) // V_DIM):
            v_ref[0, h * V_EXT:h * V_EXT + V_DIM, :] = v[h * V_DIM:(h + 1) * V_DIM, :]
            v_ref[0, h * V_EXT + V_DIM:(h + 1) * V_EXT, :] = jnp.ones((V_EXT - V_DIM, tm), BF16)


def _inproj(x, g, wt, wfT, tm=512):
    B, S, D = x.shape
    tok = lambda w, dt: jax.ShapeDtypeStruct((B, S, w), dt)
    feat = lambda r, dt: jax.ShapeDtypeStruct((B, r, S), dt)
    tspec = lambda w: pl.BlockSpec((1, tm, w), lambda b, i: (b, i, 0))
    fspec = lambda r: pl.BlockSpec((1, r, tm), lambda b, i: (b, 0, i))
    return pl.pallas_call(
        _inproj_kernel,
        grid=(B, S // tm),
        in_specs=[tspec(D),
                  pl.BlockSpec((1, D), lambda b, i: (0, 0)),
                  pl.BlockSpec((D, _T_COLS), lambda b, i: (0, 0)),
                  pl.BlockSpec((_F_ROWS, D), lambda b, i: (0, 0))],
        out_specs=[tspec(1024), tspec(128), tspec(256), tspec(256),
                   pl.BlockSpec((1, NSA_GROUPS, tm, LANES), lambda b, i: (b, 0, i, 0)), tspec(256),
                   fspec(256), fspec(4 * V_EXT), fspec(512), fspec(4 * V_EXT), fspec(NSA_GROUPS * GATE_PAD)],
        out_shape=[tok(1024, F32), tok(128, F32), tok(256, BF16), tok(256, BF16),
                   jax.ShapeDtypeStruct((B, NSA_GROUPS, S, LANES), BF16), tok(256, F32),
                   feat(256, BF16), feat(4 * V_EXT, BF16), feat(512, BF16), feat(4 * V_EXT, BF16),
                   feat(NSA_GROUPS * GATE_PAD, F32)],
        compiler_params=_cparams(("parallel", "parallel")),
        name="inproj",
    )(x, g, wt, wfT)


def _log_sigmoid(x):
    return jnp.minimum(x, 0.0) - jnp.log1p(jnp.exp(-jnp.abs(x)))


def _sigmoid(x):
    return 1.0 / (1.0 + jnp.exp(-x))


def _mlstm_kernel(zml_ref, zg_ref, cw_ref, gb_ref, nrm_ref, y_ref,
                  pad_ref, c_ref, n_ref, m_ref, *, T, L):
    d = ML_DIM
    W = max(L, d)
    t = pl.program_id(1)

    @pl.when(t == 0)
    def _():
        pad_ref[0:8, :] = jnp.zeros((8, 2 * ML_WIDTH), F32)
        c_ref[...] = jnp.zeros_like(c_ref)
        n_ref[...] = jnp.zeros_like(n_ref)
        m_ref[...] = jnp.zeros_like(m_ref)

    @pl.when(t > 0)
    def _():
        pad_ref[0:8, :] = pad_ref[T:T + 8, :]

    pad_ref[8:8 + T, :] = zml_ref[0, :, 0:2 * ML_WIDTH]
    conv = cw_ref[0:1, :] * pad_ref[5:5 + T, :]
    for j in range(1, CONV_W):
        conv = conv + cw_ref[j:j + 1, :] * pad_ref[5 + j:5 + j + T, :]
    qk = conv * _sigmoid(conv)
    gates = zg_ref[0] + gb_ref[...]
    logf = _log_sigmoid(gates)

    row = lax.broadcasted_iota(jnp.int32, (L, L), 0)
    col = lax.broadcasted_iota(jnp.int32, (L, L), 1)
    causal = col <= row
    tril = causal.astype(F32)
    triu = (row <= col).astype(F32)

    lane_of = lax.broadcasted_iota(jnp.int32, (LANES, W), 0)
    pick = [(lane_of == g).astype(F32) for g in range(2 * ML_HEADS)]

    for c in range(T // L):
        r0 = c * L
        g_c = gates[r0:r0 + L, :]
        lf_c = logf[r0:r0 + L, :]
        b_cols = jnp.dot(tril, lf_c, preferred_element_type=F32,
                         precision=lax.Precision.HIGHEST)
        b_rows = jnp.dot(lf_c.T, triu, preferred_element_type=F32,
                         precision=lax.Precision.HIGHEST)
        g_rows = g_c.T
        for h in range(ML_HEADS):
            b_col = jnp.dot(b_cols, pick[ML_HEADS + h], preferred_element_type=F32,
                            precision=lax.Precision.HIGHEST)
            ig_col = jnp.dot(g_c, pick[h], preferred_element_type=F32,
                             precision=lax.Precision.HIGHEST)
            b_row = b_rows[ML_HEADS + h:ML_HEADS + h + 1, :]
            ig_row = g_rows[h:h + 1, :]
            qh = qk[r0:r0 + L, h * d:(h + 1) * d]
            kh = qk[r0:r0 + L, ML_WIDTH + h * d:ML_WIDTH + (h + 1) * d] * (d ** -0.5)
            vh = zml_ref[0, r0:r0 + L, 2 * ML_WIDTH + h * d:2 * ML_WIDTH + (h + 1) * d]
            oh = zml_ref[0, r0:r0 + L, 3 * ML_WIDTH + h * d:3 * ML_WIDTH + (h + 1) * d]
            c_prev = c_ref[h]
            n_prev = n_ref[h, 0:1, :]
            m_prev = m_ref[h, 0:1, :]

            g_tot = b_col[L - 1:L, :]
            a_row = g_tot[:, 0:L] - b_row + ig_row
            a_max = jnp.max(a_row, axis=1, keepdims=True)
            w_col = jnp.exp((g_tot - b_col + ig_col - a_max)[:, 0:d])
            c_loc = lax.dot_general(vh * w_col, kh, _TN, preferred_element_type=F32)
            n_loc = jnp.sum(kh * w_col, axis=0, keepdims=True)

            dmat = jnp.where(causal, b_col[:, 0:L] - b_row + ig_row, NEG)
            inter_log = b_col + m_prev
            m_t = jnp.maximum(inter_log, jnp.max(dmat, axis=1, keepdims=True))
            qkt = lax.dot_general(qh, kh, _NT, preferred_element_type=F32)
            wts = jnp.exp(dmat - m_t[:, 0:L]) * qkt
            s_inter = jnp.exp((inter_log - m_t)[:, 0:d])
            num = (jnp.dot(wts, vh, preferred_element_type=F32)
                   + s_inter * lax.dot_general(qh, c_prev, _NT, preferred_element_type=F32))
            den = (jnp.sum(wts, axis=1, keepdims=True)
                   + s_inter * jnp.sum(qh * n_prev, axis=1, keepdims=True))
            hh = num / jnp.maximum(jnp.abs(den), jnp.exp(-m_t[:, 0:d]))

            m_new = jnp.maximum(g_tot + m_prev, a_max)
            s_prev = jnp.exp((g_tot + m_prev - m_new)[:, 0:d])
            s_loc = jnp.exp((a_max - m_new)[:, 0:d])
            c_ref[h] = s_prev * c_prev + s_loc * c_loc
            n_ref[h] = jnp.broadcast_to(s_prev * n_prev + s_loc * n_loc, (8, d))
            m_ref[h] = jnp.broadcast_to(m_new, (8, W))

            yh = _sigmoid(oh) * hh
            y_ref[0, r0:r0 + L, h * d:(h + 1) * d] = _rms(yh, nrm_ref[0:1, h * d:(h + 1) * d])


def _mlstm(zml, zg, conv_w, gate_bias, norm_g, T=256, chunk=ML_TILE_CHUNK):
    B, S, _ = zml.shape
    return pl.pallas_call(
        functools.partial(_mlstm_kernel, T=T, L=chunk),
        grid=(B, S // T),
        in_specs=[pl.BlockSpec((1, T, 1024), lambda b, t: (b, t, 0)),
                  pl.BlockSpec((1, T, 128), lambda b, t: (b, t, 0)),
                  pl.BlockSpec((CONV_W, 2 * ML_WIDTH), lambda b, t: (0, 0)),
                  pl.BlockSpec((1, 128), lambda b, t: (0, 0)),
                  pl.BlockSpec((1, ML_WIDTH), lambda b, t: (0, 0))],
        out_specs=pl.BlockSpec((1, T, ML_WIDTH), lambda b, t: (b, t, 0)),
        out_shape=jax.ShapeDtypeStruct((B, S, ML_WIDTH), F32),
        scratch_shapes=[pltpu.VMEM((T + 8, 2 * ML_WIDTH), F32),
                        pltpu.VMEM((ML_HEADS, ML_DIM, ML_DIM), F32),
                        pltpu.VMEM((ML_HEADS, 8, ML_DIM), F32),
                        pltpu.VMEM((ML_HEADS, 8, max(chunk, ML_DIM)), F32)],
        compiler_params=_cparams(("parallel", "arbitrary")),
        name="mlstm",
    )(zml, zg, conv_w, gate_bias, norm_g)


def _softmax_cols(s, m_ref, p_ref, cols):
    m_old = m_ref[:, cols]
    m_new = jnp.maximum(m_old, jnp.max(s, axis=0, keepdims=True))
    m_ref[:, cols] = m_new
    p_ref[:, cols] = jnp.exp2(s - m_new).astype(BF16)
    return jnp.exp2(m_old - m_new)


def _pipelined_attention(scores, vT_tile, s_refs, p_refs, m_ref, acc_ref, softmax,
                         first, n_plain_pairs, n_tail, max_tile, col_groups):
    sa, sb = s_refs
    pa, pb = p_refs

    def load(j):
        return jnp.clip(j, 0, max_tile)

    def half(j, s_cur, s_nxt, p_cur, p_prev, tail):
        s_nxt[...] = scores(load(j + 1))
        pv = jnp.dot(vT_tile(load(j - 1)), p_prev[...], preferred_element_type=F32)
        alphas = softmax(s_cur, p_cur, j, tail)
        for cols, alpha in zip(col_groups, alphas):
            acc_ref[:, cols] = alpha * (acc_ref[:, cols] + pv[:, cols])

    def pair(j, tail):
        half(j, sa, sb, pa, pb, tail)
        half(j + 1, sb, sa, pb, pa, tail)

    m_ref[...] = jnp.full_like(m_ref, M_INIT)
    acc_ref[...] = jnp.zeros_like(acc_ref)
    pb[...] = jnp.zeros_like(pb)
    sa[...] = scores(load(first))

    if not (isinstance(n_plain_pairs, int) and n_plain_pairs == 0):
        n_trips = n_plain_pairs // LOOP_PAIRS

        def body(i, carry):
            for u in range(LOOP_PAIRS):
                pair(first + 2 * (LOOP_PAIRS * i + u), False)
            return carry

        lax.fori_loop(0, n_trips, body, 0)
        rest = n_plain_pairs - n_trips * LOOP_PAIRS
        for u in range(LOOP_PAIRS - 1):
            @pl.when(u < rest)
            def _():
                pair(first + 2 * (LOOP_PAIRS * n_trips + u), False)
    j = first + 2 * n_plain_pairs
    bufs = ((sa, sb, pa, pb), (sb, sa, pb, pa))
    for t in range(n_tail):
        half(j + t, *bufs[t % 2], True)
    p_last = bufs[(n_tail - 1) % 2][2]
    acc_ref[...] += jnp.dot(vT_tile(load(j + n_tail - 1)), p_last[...], preferred_element_type=F32)


def _da_kernel(lam_ref, qT_ref, k_ref, vT_ref, gain_ref, o_ref,
               qz_ref, sa_ref, sb_ref, pa_ref, pb_ref, m_ref, acc_ref, *, tq, tk, lam_init, S):
    h = pl.program_id(1)
    qi = pl.program_id(2)
    d = DA_QK_DIM
    jd = (qi * tq) // tk
    col_groups = [slice(mp * tq, (mp + 1) * tq) for mp in range(2)]

    qz_ref[...] = jnp.zeros_like(qz_ref)
    for hh in range(2):
        @pl.when(h % 2 == hh)
        def _():
            qz_ref[hh * 2 * d:hh * 2 * d + d, 0:tq] = qT_ref[0, 0:d, :]
            qz_ref[hh * 2 * d + d:(hh + 1) * 2 * d, tq:2 * tq] = qT_ref[0, d:2 * d, :]

    def scores(j):
        return jnp.dot(k_ref[0, pl.ds(pl.multiple_of(j * tk, tk), tk), :], qz_ref[...], preferred_element_type=F32)

    def vT_tile(j):
        return vT_ref[0, :, pl.ds(pl.multiple_of(j * tk, tk), tk)]

    def softmax(s_ref, p_ref, j, tail):
        alphas = []
        for cols in col_groups:
            s = s_ref[:, cols]
            if tail:
                kpos = j * tk + lax.broadcasted_iota(jnp.int32, (tk, tq), 0)
                qpos = qi * tq + lax.broadcasted_iota(jnp.int32, (tk, tq), 1)
                s = jnp.where(kpos <= qpos, s, NEG)
            alphas.append(_softmax_cols(s, m_ref, p_ref, cols))
        return alphas

    _pipelined_attention(scores, vT_tile, (sa_ref, sb_ref), (pa_ref, pb_ref), m_ref, acc_ref, softmax,
                         first=0, n_plain_pairs=jd // 2, n_tail=2, max_tile=S // tk - 1, col_groups=col_groups)

    lp = lam_ref[...]
    lam = (jnp.exp(jnp.sum(lp[0:1] * lp[1:2], axis=1, keepdims=True))
           - jnp.exp(jnp.sum(lp[2:3] * lp[3:4], axis=1, keepdims=True)) + lam_init)
    o1 = acc_ref[0:V_DIM, 0:tq] / acc_ref[V_DIM:V_DIM + 1, 0:tq]
    o2 = acc_ref[0:V_DIM, tq:2 * tq] / acc_ref[V_DIM:V_DIM + 1, tq:2 * tq]
    o = o1 - lam * o2
    y = o * lax.rsqrt(jnp.mean(o * o, axis=0, keepdims=True) + EPS) * gain_ref[...]
    o_ref[0] = (y * (1.0 - lam_init)).astype(o_ref.dtype)


def _diff_attn(da_lambda, daqT, dak, davT, gain_col, lam_init, tq=256, tk=256):
    B, _, S = daqT.shape
    return pl.pallas_call(
        functools.partial(_da_kernel, tq=tq, tk=tk, lam_init=lam_init, S=S),
        grid=(B, DA_HEADS, S // tq),
        in_specs=[pl.BlockSpec((4, DA_QK_DIM), lambda b, h, i: (0, 0)),
                  pl.BlockSpec((1, DA_V_DIM, tq), lambda b, h, i: (b, h, i)),
                  pl.BlockSpec((1, S, LANES), lambda b, h, i: (b, 0, h // 2)),
                  pl.BlockSpec((1, V_EXT, S), lambda b, h, i: (b, h, 0)),
                  pl.BlockSpec((DA_V_DIM, 1), lambda b, h, i: (h, 0))],
        out_specs=pl.BlockSpec((1, DA_V_DIM, tq), lambda b, h, i: (b, h, i)),
        out_shape=jax.ShapeDtypeStruct((B, DA_WIDTH, S), BF16),
        scratch_shapes=[pltpu.VMEM((LANES, 2 * tq), BF16),
                        pltpu.VMEM((tk, 2 * tq), F32),
                        pltpu.VMEM((tk, 2 * tq), F32),
                        pltpu.VMEM((tk, 2 * tq), BF16),
                        pltpu.VMEM((tk, 2 * tq), BF16),
                        pltpu.VMEM((1, 2 * tq), F32),
                        pltpu.VMEM((V_EXT, 2 * tq), F32)],
        compiler_params=_cparams(("parallel", "parallel", "arbitrary")),
        name="diff_attn",
    )(da_lambda, daqT, dak, davT, gain_col)


def _gelu_tanh(x):
    return x * (0.5 * (1.0 + jnp.tanh(math.sqrt(2.0 / math.pi) * (x + 0.044715 * (x * x * x)))))


def _compress_kernel(x_ref, pe_ref, w1_ref, w2_ref, o_ref, b_ref, *, feature_major):
    n = x_ref.shape[2]
    half = CMP_STRIDE * NSA_DIM
    x = x_ref[0, 0]
    a = jnp.dot((x + pe_ref[0, 0:1, :]).astype(BF16), w1_ref[0, 0:half, :], preferred_element_type=F32)
    b_ref[0:n, :] = jnp.dot((x + pe_ref[0, 1:2, :]).astype(BF16), w1_ref[0, half:2 * half, :],
                            preferred_element_type=F32)
    b_ref[n:n + 8, :] = jnp.zeros((8, CMP_HIDDEN), F32)
    hid = _gelu_tanh(a + b_ref[1:n + 1, :]).astype(BF16)
    if feature_major:
        o_ref[0, 0] = lax.dot_general(w2_ref[0], hid, _NT, preferred_element_type=F32).astype(o_ref.dtype)
    else:
        o_ref[0, 0] = jnp.dot(hid, w2_ref[0], preferred_element_type=F32).astype(o_ref.dtype)


def _compress(x16, pe2, w1, w2, which, feature_major):
    B, _, n, half = x16.shape
    G = NSA_GROUPS
    if feature_major:
        out_shape, out_block = (B, G, NSA_DIM, n), (1, 1, NSA_DIM, n)
        w2_arr, w2_block = jnp.swapaxes(w2, 1, 2), (1, NSA_DIM, CMP_HIDDEN)
    else:
        out_shape, out_block = (B, G, n, NSA_DIM), (1, 1, n, NSA_DIM)
        w2_arr, w2_block = w2, (1, CMP_HIDDEN, NSA_DIM)
    return pl.pallas_call(
        functools.partial(_compress_kernel, feature_major=feature_major),
        grid=(B, G),
        in_specs=[pl.BlockSpec((1, 1, n, half), lambda b, g: (b, which * G + g, 0, 0)),
                  pl.BlockSpec((1, 2, half), lambda b, g: (which, 0, 0)),
                  pl.BlockSpec((1, 2 * half, CMP_HIDDEN), lambda b, g: (which, 0, 0)),
                  pl.BlockSpec(w2_block, lambda b, g: (which, 0, 0))],
        out_specs=pl.BlockSpec(out_block, lambda b, g: (b, g, 0, 0)),
        out_shape=jax.ShapeDtypeStruct(out_shape, BF16),
        scratch_shapes=[pltpu.VMEM((n + 8, CMP_HIDDEN), F32)],
        compiler_params=_cparams(("parallel", "parallel")),
        name="compress_v" if feature_major else "compress_k",
    )(x16, pe2, w1, w2_arr.astype(BF16))


def _nsa_kernel(qT_ref, gT_ref, kc_ref, vcT_ref, ks_ref, vsT_ref, kw_ref, vwT_ref, o_ref,
                qg_ref, qz_ref, qs_ref, imp_ref, sel_ref, pc_ref, sa_ref, sb_ref, pa_ref, pb_ref, m_ref, acc_ref, out_ref,
                *, tq, tk, S):
    g = pl.program_id(1)
    qi = pl.program_id(2)
    d = NSA_DIM
    R = NSA_REP
    ncb = S // CMP_STRIDE
    nsb = S // SEL_BLOCK
    q0 = qi * tq
    jd = q0 // tk
    ratio = SEL_BLOCK // CMP_STRIDE
    per_tile = tk // SEL_BLOCK

    for r in range(R):
        qg_ref[:, r * tq:(r + 1) * tq] = qT_ref[0, r * d:(r + 1) * d, :]
    qz_ref[...] = jnp.zeros_like(qz_ref)
    qs_ref[...] = jnp.zeros_like(qs_ref)
    qs_ref[0:d, :] = qg_ref[...]
    for gg in range(NSA_GROUPS):
        @pl.when(g == gg)
        def _():
            qz_ref[gg * d:(gg + 1) * d, :] = qg_ref[...]

    def gate(r, br):
        return _sigmoid(gT_ref[0, r * 3 + br:r * 3 + br + 1, :])

    qpos = q0 + lax.broadcasted_iota(jnp.int32, (1, tq), 1)

    s_all = jnp.dot(kc_ref[0, 0], qg_ref[...], preferred_element_type=F32)
    cend = lax.broadcasted_iota(jnp.int32, (ncb, tq), 0) * CMP_STRIDE + (CMP_BLOCK - 1)
    cbias = jnp.where(cend <= qpos, 0.0, NEG)
    imp = jnp.zeros((ncb, tq), F32)
    for r in range(R):
        cols = slice(r * tq, (r + 1) * tq)
        s = s_all[:, cols] + cbias
        mx = jnp.maximum(jnp.max(s, axis=0, keepdims=True), M_INIT)
        p = jnp.exp2(s - mx)
        den = jnp.sum(p, axis=0, keepdims=True)
        p = p / jnp.where(den > 0, den, 1.0)
        imp = imp + p
        pc_ref[:, cols] = p.astype(BF16)
    o_cmp = jnp.dot(vcT_ref[0, 0], pc_ref[...], preferred_element_type=F32)
    for r in range(R):
        cols = slice(r * tq, (r + 1) * tq)
        out_ref[:, cols] = gate(r, 0) * o_cmp[:, cols]

    slabs = []
    for c in range(tq // LANES):
        imp_ref[c, 0:8, :] = jnp.zeros((8, LANES), F32)
        imp_ref[c, 8:8 + ncb, :] = imp[:, c * LANES:(c + 1) * LANES]
        imp_ref[c, 8 + ncb:16 + ncb, :] = jnp.zeros((8, LANES), F32)
        slab = jnp.zeros((nsb, LANES), F32)
        for o in range(-1, ratio):
            slab = slab + imp_ref[c, pl.ds(8 + o, nsb, stride=ratio), :]
        slabs.append(slab)
    p_slc = slabs[0] if len(slabs) == 1 else jnp.concatenate(slabs, axis=1)
    blk = lax.broadcasted_iota(jnp.int32, (nsb, tq), 0)
    cur = lax.shift_right_logical(qpos, int(math.log2(SEL_BLOCK)))
    forced = (blk == 0) | (blk == cur) | (blk == cur - 1)
    causal_blk = blk * SEL_BLOCK <= qpos
    score = jnp.where(forced, FORCE_SCORE, jnp.where(causal_blk, p_slc, -1.0))
    blk_f = blk.astype(F32)
    for _ in range(SEL_TOPK):
        mx = jnp.max(score, axis=0, keepdims=True)
        first = jnp.min(jnp.where(score == mx, blk_f, float(nsb)), axis=0, keepdims=True)
        score = jnp.where(blk_f == first, TAKEN, score)
    sel_ref[...] = jnp.where(score == TAKEN, 0.0, NEG)

    head_cols = [slice(r * tq, (r + 1) * tq) for r in range(R)]
    col_groups = [slice(0, R * tq)]
    max_tile = S // tk - 1

    def k_tile(k_ref_, j):
        return k_ref_[0, pl.ds(pl.multiple_of(j * tk, tk), tk), :]

    def vT_tile_of(vT_ref_):
        def vT_tile(j):
            return vT_ref_[0, :, pl.ds(pl.multiple_of(j * tk, tk), tk)]
        return vT_tile

    def softmax_with(bias_of):
        def softmax(s_ref, p_ref, j, tail):
            bias = bias_of(j, tail)
            s = s_ref[...]
            if bias is not None:
                s = s + jnp.concatenate([bias] * R, axis=1)
            return [_softmax_cols(s, m_ref, p_ref, col_groups[0])]
        return softmax

    def flush(br):
        for r, cols in enumerate(head_cols):
            den = acc_ref[V_DIM:V_DIM + 1, cols]
            out_ref[:, cols] += gate(r, br) * (acc_ref[0:V_DIM, cols] / jnp.where(den > 0, den, 1.0))

    def kpos_of(j):
        return j * tk + lax.broadcasted_iota(jnp.int32, (tk, tq), 0)

    def win_bias(j, tail):
        kpos = kpos_of(j)
        rel = qpos - kpos
        return jnp.where((rel >= 0) & (rel < WINDOW) & (kpos >= 0), 0.0, NEG)

    def win_scores(j):
        return jnp.dot(k_tile(kw_ref, j), qz_ref[...], preferred_element_type=F32)

    n_win = WINDOW // tk + 1
    _pipelined_attention(win_scores, vT_tile_of(vwT_ref), (sa_ref, sb_ref), (pa_ref, pb_ref), m_ref, acc_ref,
                         softmax_with(win_bias), first=jd - (n_win - 1), n_plain_pairs=0, n_tail=n_win,
                         max_tile=max_tile, col_groups=col_groups)
    flush(2)

    pad_rows = jnp.zeros((BF16_ROWS - per_tile, tq), F32)

    def sel_scores(j):
        rows = [sel_ref[pl.ds(j * per_tile + i, 1), :] for i in range(per_tile)]
        bias = jnp.concatenate(rows + [pad_rows], axis=0).astype(BF16)
        qs_ref[d:d + BF16_ROWS, :] = jnp.concatenate([bias] * R, axis=1)
        return jnp.dot(ks_ref[0, 0, pl.ds(pl.multiple_of(j * tk, tk), tk), :], qs_ref[...],
                       preferred_element_type=F32)

    def sel_bias(j, tail):
        return jnp.where(kpos_of(j) <= qpos, 0.0, NEG) if tail else None

    _pipelined_attention(sel_scores, vT_tile_of(vsT_ref), (sa_ref, sb_ref), (pa_ref, pb_ref), m_ref, acc_ref,
                         softmax_with(sel_bias), first=0, n_plain_pairs=jd // 2, n_tail=2,
                         max_tile=max_tile, col_groups=col_groups)
    flush(1)

    for r in range(R):
        o_ref[0, r * d:(r + 1) * d, :] = out_ref[:, r * tq:(r + 1) * tq].astype(o_ref.dtype)


def _nsa(nsqT, nsgT, kcmp, vcmpT, nsks, nsk, nsvT, tq=128, tk=NSA_TK):
    B, _, S = nsqT.shape
    G, R, d = NSA_GROUPS, NSA_REP, NSA_DIM
    ncb = S // CMP_STRIDE
    nsb = S // SEL_BLOCK
    return pl.pallas_call(
        functools.partial(_nsa_kernel, tq=tq, tk=tk, S=S),
        grid=(B, G, S // tq),
        in_specs=[pl.BlockSpec((1, R * d, tq), lambda b, g, i: (b, g, i)),
                  pl.BlockSpec((1, GATE_PAD, tq), lambda b, g, i: (b, g, i)),
                  pl.BlockSpec((1, 1, ncb, d), lambda b, g, i: (b, g, 0, 0)),
                  pl.BlockSpec((1, 1, d, ncb), lambda b, g, i: (b, g, 0, 0)),
                  pl.BlockSpec((1, 1, S, LANES), lambda b, g, i: (b, g, 0, 0)),
                  pl.BlockSpec((1, V_EXT, S), lambda b, g, i: (b, g, 0)),
                  pl.BlockSpec((1, S, LANES), lambda b, g, i: (b, 0, 1)),
                  pl.BlockSpec((1, V_EXT, S), lambda b, g, i: (b, G + g, 0))],
        out_specs=pl.BlockSpec((1, R * d, tq), lambda b, g, i: (b, g, i)),
        out_shape=jax.ShapeDtypeStruct((B, NSA_WIDTH, S), BF16),
        scratch_shapes=[pltpu.VMEM((d, R * tq), BF16),
                        pltpu.VMEM((LANES, R * tq), BF16),
                        pltpu.VMEM((LANES, R * tq), BF16),
                        pltpu.VMEM((tq // LANES, ncb + 16, LANES), F32),
                        pltpu.VMEM((nsb, tq), F32),
                        pltpu.VMEM((ncb, R * tq), BF16),
                        pltpu.VMEM((tk, R * tq), F32),
                        pltpu.VMEM((tk, R * tq), F32),
                        pltpu.VMEM((tk, R * tq), BF16),
                        pltpu.VMEM((tk, R * tq), BF16),
                        pltpu.VMEM((1, R * tq), F32),
                        pltpu.VMEM((V_EXT, R * tq), F32),
                        pltpu.VMEM((d, R * tq), F32)],
        compiler_params=_cparams(("parallel", "parallel", "arbitrary")),
        name="nsa",
    )(nsqT, nsgT, kcmp, vcmpT, nsks, nsvT, nsk, nsvT)


def _outproj_kernel(x_ref, yml_ref, ydaT_ref, ynsT_ref, wo_ref, o_ref):
    acc = x_ref[0] + jnp.dot(yml_ref[0].astype(BF16), wo_ref[0:ML_WIDTH, :], preferred_element_type=F32)
    acc = acc + lax.dot_general(ydaT_ref[0], wo_ref[ML_WIDTH:ML_WIDTH + DA_WIDTH, :], _TN,
                                preferred_element_type=F32)
    acc = acc + lax.dot_general(ynsT_ref[0], wo_ref[ML_WIDTH + DA_WIDTH:, :], _TN,
                                preferred_element_type=F32)
    o_ref[0] = acc


def _outproj(x, yml, ydaT, ynsT, wo, tm=512):
    B, S, D = x.shape
    return pl.pallas_call(
        _outproj_kernel,
        grid=(B, S // tm),
        in_specs=[pl.BlockSpec((1, tm, D), lambda b, i: (b, i, 0)),
                  pl.BlockSpec((1, tm, ML_WIDTH), lambda b, i: (b, i, 0)),
                  pl.BlockSpec((1, DA_WIDTH, tm), lambda b, i: (b, 0, i)),
                  pl.BlockSpec((1, NSA_WIDTH, tm), lambda b, i: (b, 0, i)),
                  pl.BlockSpec((D, D), lambda b, i: (0, 0))],
        out_specs=pl.BlockSpec((1, tm, D), lambda b, i: (b, i, 0)),
        out_shape=jax.ShapeDtypeStruct((B, S, D), F32),
        compiler_params=_cparams(("parallel", "parallel")),
        name="outproj",
    )(x, yml, ydaT, ynsT, wo)


def _ffn_kernel(x_ref, g_ref, w1_ref, w2_ref, gf_ref, o_ref, hb_ref, acc_ref, *, final):
    j = pl.program_id(1)

    @pl.when(j == 0)
    def _():
        x = x_ref[...]
        hb_ref[...] = _rms(x, g_ref[...]).astype(BF16)
        acc_ref[...] = x

    u = jnp.dot(hb_ref[...], w1_ref[...], preferred_element_type=F32)
    a = jnp.square(jnp.maximum(u, 0.0)).astype(BF16)
    acc_ref[...] += jnp.dot(a, w2_ref[...], preferred_element_type=F32)

    @pl.when(j == pl.num_programs(1) - 1)
    def _():
        y = acc_ref[...]
        if final:
            y = _rms(y, gf_ref[...])
        o_ref[...] = y


def _ffn(x2d, g, w1, w2, gf, final, tm=1024, tf=1024):
    N, D = x2d.shape
    return pl.pallas_call(
        functools.partial(_ffn_kernel, final=final),
        grid=(N // tm, D_FF // tf),
        in_specs=[pl.BlockSpec((tm, D), lambda i, j: (i, 0)),
                  pl.BlockSpec((1, D), lambda i, j: (0, 0)),
                  pl.BlockSpec((D, tf), lambda i, j: (0, j)),
                  pl.BlockSpec((tf, D), lambda i, j: (j, 0)),
                  pl.BlockSpec((1, D), lambda i, j: (0, 0))],
        out_specs=pl.BlockSpec((tm, D), lambda i, j: (i, 0)),
        out_shape=jax.ShapeDtypeStruct((N, D), F32),
        scratch_shapes=[pltpu.VMEM((tm, D), BF16), pltpu.VMEM((tm, D), F32)],
        compiler_params=_cparams(("parallel", "arbitrary")),
        name="ffn",
    )(x2d, g, w1, w2, gf)


def _split_w_in(w_in_l):
    edges = np.concatenate([[0], np.cumsum(IN_SIZES)])
    return [w_in_l[:, int(edges[i]):int(edges[i + 1])] for i in range(len(IN_SIZES))]


def _pad_cols(w, n):
    return jnp.pad(w, ((0, 0), (0, n - w.shape[1])))


def _inproj_weights(w_in_l):
    (ml_q, ml_k, ml_v, ml_o, ml_i, ml_f, da_q, da_k, da_v,
     ns_q, ns_kc, ns_vc, ns_ks, ns_vs, ns_kw, ns_vw, ns_g) = _split_w_in(w_in_l)
    wt = jnp.concatenate([ml_q, ml_k, ml_v, ml_o, _pad_cols(jnp.concatenate([ml_i, ml_f], 1), LANES),
                          da_k, ns_ks, ns_kw, ns_kc, ns_vc], axis=1)
    per_group = NSA_REP * 3
    ns_g_pad = jnp.concatenate(
        [_pad_cols(ns_g[:, gi * per_group:(gi + 1) * per_group], GATE_PAD) for gi in range(NSA_GROUPS)], axis=1)
    wf = jnp.concatenate([da_q, da_v, ns_q, ns_vs, ns_vw, ns_g_pad], axis=1)
    return wt.astype(BF16), wf.T.astype(BF16)


def _half_blocks(nsc):
    B, S, _ = nsc.shape
    n = S // CMP_STRIDE
    t = nsc.reshape(B, n, CMP_STRIDE, 2 * NSA_GROUPS, NSA_DIM)
    return t.transpose(0, 3, 1, 2, 4).reshape(B, 2 * NSA_GROUPS, n, CMP_STRIDE * NSA_DIM)


def kernel(x, norm1, w_in, ml_conv, ml_gate_bias, ml_norm, da_lambda, da_norm, nsa_pe,
           nsa_w1, nsa_w2, w_out, norm2, w_ff1, w_ff2, final_norm):
    B, S, D = x.shape
    depth = norm1.shape[0]
    for l in range(depth):
        wt, wfT = _inproj_weights(w_in[l])
        (zml, zg, dak, nsk, nsks, nsc, daqT, davT, nsqT, nsvT, nsgT) = _inproj(x, norm1[l][None, :], wt, wfT)

        yml = _mlstm(zml, zg, ml_conv[l], _pad_cols(ml_gate_bias[l][None, :], LANES), ml_norm[l][None, :])

        lam_init = 0.8 - 0.6 * math.exp(-0.3 * l)
        ydaT = _diff_attn(da_lambda[l], daqT, dak, davT, da_norm[l][:, None], lam_init)

        x16 = _half_blocks(nsc)
        pe2 = nsa_pe[l].reshape(2, 2, CMP_STRIDE * NSA_DIM)
        w1b = nsa_w1[l].astype(BF16)
        kcmp = _compress(x16, pe2, w1b, nsa_w2[l], 0, False)
        vcmpT = _compress(x16, pe2, w1b, nsa_w2[l], 1, True)
        ynsT = _nsa(nsqT, nsgT, kcmp, vcmpT, nsks, nsk, nsvT)

        x = _outproj(x, yml, ydaT, ynsT, w_out[l].astype(BF16))
        x = _ffn(x.reshape(B * S, D), norm2[l][None, :], w_ff1[l].astype(BF16), w_ff2[l].astype(BF16),
                 final_norm[None, :], final=(l == depth - 1)).reshape(B, S, D)
    return x
```

```python
import functools
import math

import numpy as np
import jax
import jax.numpy as jnp
from jax import lax
from jax.experimental import pallas as pl
from jax.experimental.pallas import tpu as pltpu

F32 = jnp.float32
BF16 = jnp.bfloat16

D_MODEL = 1024
ML_HEADS = 4
ML_DIM = 64
ML_WIDTH = ML_HEADS * ML_DIM
ML_TILE_CHUNK = 128
CONV_W = 4
DA_HEADS = 4
DA_QK_DIM = 32
DA_V_DIM = 64
DA_WIDTH = DA_HEADS * DA_V_DIM
NSA_HEADS = 8
NSA_GROUPS = 2
NSA_REP = NSA_HEADS // NSA_GROUPS
NSA_DIM = 64
NSA_WIDTH = NSA_HEADS * NSA_DIM
NSA_KV = NSA_GROUPS * NSA_DIM
CMP_BLOCK = 32
CMP_STRIDE = 16
CMP_HIDDEN = 4 * NSA_DIM
SEL_BLOCK = 64
SEL_TOPK = 16
WINDOW = 512
D_FF = 4 * D_MODEL
EPS = 1e-6
FORCE_SCORE = 1e4
IN_SIZES = (ML_WIDTH, ML_WIDTH, ML_WIDTH, ML_WIDTH, ML_HEADS, ML_HEADS,
            2 * DA_HEADS * DA_QK_DIM, 2 * DA_HEADS * DA_QK_DIM, DA_WIDTH,
            NSA_WIDTH, NSA_KV, NSA_KV, NSA_KV, NSA_KV, NSA_KV, NSA_KV, 3 * NSA_HEADS)

LANES = 128
GATE_PAD = 16
NEG = -1e30
M_INIT = -1e29
LOOP_PAIRS = 4
TAKEN = -3e38
LOG2E = 1.4426950408889634
V_DIM = 64
BF16_ROWS = 16
NSA_TK = 256
V_EXT = 80
VMEM_LIMIT = 56 * 1024 * 1024

_T_ML = (0, 1024)
_T_MLG = (1024, 1152)
_T_DAK = (1152, 1408)
_T_NSK = (1408, 1664)
_T_NSC = (1664, 1920)
_T_COLS = 1920
_F_DAQ = (0, 256)
_F_DAV = (256, 512)
_F_NSQ = (512, 1024)
_F_NSV = (1024, 1280)
_F_NSG = (1280, 1280 + NSA_GROUPS * GATE_PAD)
_F_ROWS = _F_NSG[1]

_NT = (((1,), (1,)), ((), ()))
_TN = (((0,), (0,)), ((), ()))


def _cparams(sem):
    return pltpu.CompilerParams(dimension_semantics=sem, vmem_limit_bytes=VMEM_LIMIT)


def _rms(x, g):
    return x * lax.rsqrt(jnp.mean(x * x, axis=-1, keepdims=True) + EPS) * g


def _inproj_kernel(x_ref, g_ref, wt_ref, wf_ref,
                   zml_ref, zg_ref, dak_ref, nsk_ref, nsks_ref, nsc_ref,
                   daqT_ref, davT_ref, nsqT_ref, nsvT_ref, nsgT_ref):
    hb = _rms(x_ref[0], g_ref[...]).astype(BF16)

    def tdot(span):
        return jnp.dot(hb, wt_ref[:, span[0]:span[1]], preferred_element_type=F32)

    def fdot(span):
        return lax.dot_general(wf_ref[span[0]:span[1], :], hb, _NT, preferred_element_type=F32)

    zml_ref[0] = tdot(_T_ML)
    zg_ref[0] = tdot(_T_MLG)
    dak_ref[0] = tdot(_T_DAK).astype(BF16)
    nsk = tdot(_T_NSK).astype(BF16)
    nsk_ref[0] = nsk
    tm_ = nsk.shape[0]
    row = lax.broadcasted_iota(jnp.int32, (tm_, NSA_DIM), 0)
    lane = lax.broadcasted_iota(jnp.int32, (tm_, NSA_DIM), 1)
    blk_in_tile = lax.shift_right_logical(row & (NSA_TK - 1), int(math.log2(SEL_BLOCK)))
    onehot = jnp.where(lane == blk_in_tile, 1.0, 0.0).astype(BF16)
    for gi in range(NSA_GROUPS):
        nsks_ref[0, gi] = jnp.concatenate([nsk[:, gi * NSA_DIM:(gi + 1) * NSA_DIM], onehot], axis=1)
    nsc_ref[0] = tdot(_T_NSC)
    daqT_ref[0] = (fdot(_F_DAQ) * (DA_QK_DIM ** -0.5 * LOG2E)).astype(BF16)
    nsqT_ref[0] = (fdot(_F_NSQ) * (NSA_DIM ** -0.5 * LOG2E)).astype(BF16)
    nsgT_ref[0] = fdot(_F_NSG)
    tm = hb.shape[0]
    for v_ref, span in ((davT_ref, _F_DAV), (nsvT_ref, _F_NSV)):
        v = fdot(span).astype(BF16)
        for h in range((span[1] - span[0]) // V_DIM):
            v_ref[0, h * V_EXT:h * V_EXT + V_DIM, :] = v[h * V_DIM:(h + 1) * V_DIM, :]
            v_ref[0, h * V_EXT + V_DIM:(h + 1) * V_EXT, :] = jnp.ones((V_EXT - V_DIM, tm), BF16)


def _inproj(x, g, wt, wfT, tm=512):
    B, S, D = x.shape
    tok = lambda w, dt: jax.ShapeDtypeStruct((B, S, w), dt)
    feat = lambda r, dt: jax.ShapeDtypeStruct((B, r, S), dt)
    tspec = lambda w: pl.BlockSpec((1, tm, w), lambda b, i: (b, i, 0))
    fspec = lambda r: pl.BlockSpec((1, r, tm), lambda b, i: (b, 0, i))
    return pl.pallas_call(
        _inproj_kernel,
        grid=(B, S // tm),
        in_specs=[tspec(D),
                  pl.BlockSpec((1, D), lambda b, i: (0, 0)),
                  pl.BlockSpec((D, _T_COLS), lambda b, i: (0, 0)),
                  pl.BlockSpec((_F_ROWS, D), lambda b, i: (0, 0))],
        out_specs=[tspec(1024), tspec(128), tspec(256), tspec(256),
                   pl.BlockSpec((1, NSA_GROUPS, tm, LANES), lambda b, i: (b, 0, i, 0)), tspec(256),
                   fspec(256), fspec(4 * V_EXT), fspec(512), fspec(4 * V_EXT), fspec(NSA_GROUPS * GATE_PAD)],
        out_shape=[tok(1024, F32), tok(128, F32), tok(256, BF16), tok(256, BF16),
                   jax.ShapeDtypeStruct((B, NSA_GROUPS, S, LANES), BF16), tok(256, F32),
                   feat(256, BF16), feat(4 * V_EXT, BF16), feat(512, BF16), feat(4 * V_EXT, BF16),
                   feat(NSA_GROUPS * GATE_PAD, F32)],
        compiler_params=_cparams(("parallel", "parallel")),
        name="inproj",
    )(x, g, wt, wfT)


def _log_sigmoid(x):
    return jnp.minimum(x, 0.0) - jnp.log1p(jnp.exp(-jnp.abs(x)))


def _sigmoid(x):
    return 1.0 / (1.0 + jnp.exp(-x))


def _mlstm_kernel(zml_ref, zg_ref, cw_ref, gb_ref, nrm_ref, y_ref,
                  pad_ref, c_ref, n_ref, m_ref, *, T, L):
    d = ML_DIM
    W = max(L, d)
    t = pl.program_id(1)

    @pl.when(t == 0)
    def _():
        pad_ref[0:8, :] = jnp.zeros((8, 2 * ML_WIDTH), F32)
        c_ref[...] = jnp.zeros_like(c_ref)
        n_ref[...] = jnp.zeros_like(n_ref)
        m_ref[...] = jnp.zeros_like(m_ref)

    @pl.when(t > 0)
    def _():
        pad_ref[0:8, :] = pad_ref[T:T + 8, :]

    pad_ref[8:8 + T, :] = zml_ref[0, :, 0:2 * ML_WIDTH]
    conv = cw_ref[0:1, :] * pad_ref[5:5 + T, :]
    for j in range(1, CONV_W):
        conv = conv + cw_ref[j:j + 1, :] * pad_ref[5 + j:5 + j + T, :]
    qk = conv * _sigmoid(conv)
    gates = zg_ref[0] + gb_ref[...]
    logf = _log_sigmoid(gates)

    row = lax.broadcasted_iota(jnp.int32, (L, L), 0)
    col = lax.broadcasted_iota(jnp.int32, (L, L), 1)
    causal = col <= row
    tril = causal.astype(F32)
    triu = (row <= col).astype(F32)

    lane_of = lax.broadcasted_iota(jnp.int32, (LANES, W), 0)
    pick = [(lane_of == g).astype(F32) for g in range(2 * ML_HEADS)]

    for c in range(T // L):
        r0 = c * L
        g_c = gates[r0:r0 + L, :]
        lf_c = logf[r0:r0 + L, :]
        b_cols = jnp.dot(tril, lf_c, preferred_element_type=F32,
                         precision=lax.Precision.HIGHEST)
        b_rows = jnp.dot(lf_c.T, triu, preferred_element_type=F32,
                         precision=lax.Precision.HIGHEST)
        g_rows = g_c.T
        for h in range(ML_HEADS):
            b_col = jnp.dot(b_cols, pick[ML_HEADS + h], preferred_element_type=F32,
                            precision=lax.Precision.HIGHEST)
            ig_col = jnp.dot(g_c, pick[h], preferred_element_type=F32,
                             precision=lax.Precision.HIGHEST)
            b_row = b_rows[ML_HEADS + h:ML_HEADS + h + 1, :]
            ig_row = g_rows[h:h + 1, :]
            qh = qk[r0:r0 + L, h * d:(h + 1) * d]
            kh = qk[r0:r0 + L, ML_WIDTH + h * d:ML_WIDTH + (h + 1) * d] * (d ** -0.5)
            vh = zml_ref[0, r0:r0 + L, 2 * ML_WIDTH + h * d:2 * ML_WIDTH + (h + 1) * d]
            oh = zml_ref[0, r0:r0 + L, 3 * ML_WIDTH + h * d:3 * ML_WIDTH + (h + 1) * d]
            c_prev = c_ref[h]
            n_prev = n_ref[h, 0:1, :]
            m_prev = m_ref[h, 0:1, :]

            g_tot = b_col[L - 1:L, :]
            a_row = g_tot[:, 0:L] - b_row + ig_row
            a_max = jnp.max(a_row, axis=1, keepdims=True)
            w_col = jnp.exp((g_tot - b_col + ig_col - a_max)[:, 0:d])
            c_loc = lax.dot_general(vh * w_col, kh, _TN, preferred_element_type=F32)
            n_loc = jnp.sum(kh * w_col, axis=0, keepdims=True)

            dmat = jnp.where(causal, b_col[:, 0:L] - b_row + ig_row, NEG)
            inter_log = b_col + m_prev
            m_t = jnp.maximum(inter_log, jnp.max(dmat, axis=1, keepdims=True))
            qkt = lax.dot_general(qh, kh, _NT, preferred_element_type=F32)
            wts = jnp.exp(dmat - m_t[:, 0:L]) * qkt
            s_inter = jnp.exp((inter_log - m_t)[:, 0:d])
            num = (jnp.dot(wts, vh, preferred_element_type=F32)
                   + s_inter * lax.dot_general(qh, c_prev, _NT, preferred_element_type=F32))
            den = (jnp.sum(wts, axis=1, keepdims=True)
                   + s_inter * jnp.sum(qh * n_prev, axis=1, keepdims=True))
            hh = num / jnp.maximum(jnp.abs(den), jnp.exp(-m_t[:, 0:d]))

            m_new = jnp.maximum(g_tot + m_prev, a_max)
            s_prev = jnp.exp((g_tot + m_prev - m_new)[:, 0:d])
            s_loc = jnp.exp((a_max - m_new)[:, 0:d])
            c_ref[h] = s_prev * c_prev + s_loc * c_loc
            n_ref[h] = jnp.broadcast_to(s_prev * n_prev + s_loc * n_loc, (8, d))
            m_ref[h] = jnp.broadcast_to(m_new, (8, W))

            yh = _sigmoid(oh) * hh
            y_ref[0, r0:r0 + L, h * d:(h + 1) * d] = _rms(yh, nrm_ref[0:1, h * d:(h + 1) * d])


def _mlstm(zml, zg, conv_w, gate_bias, norm_g, T=256, chunk=ML_TILE_CHUNK):
    B, S, _ = zml.shape
    return pl.pallas_call(
        functools.partial(_mlstm_kernel, T=T, L=chunk),
        grid=(B, S // T),
        in_specs=[pl.BlockSpec((1, T, 1024), lambda b, t: (b, t, 0)),
                  pl.BlockSpec((1, T, 128), lambda b, t: (b, t, 0)),
                  pl.BlockSpec((CONV_W, 2 * ML_WIDTH), lambda b, t: (0, 0)),
                  pl.BlockSpec((1, 128), lambda b, t: (0, 0)),
                  pl.BlockSpec((1, ML_WIDTH), lambda b, t: (0, 0))],
        out_specs=pl.BlockSpec((1, T, ML_WIDTH), lambda b, t: (b, t, 0)),
        out_shape=jax.ShapeDtypeStruct((B, S, ML_WIDTH), F32),
        scratch_shapes=[pltpu.VMEM((T + 8, 2 * ML_WIDTH), F32),
                        pltpu.VMEM((ML_HEADS, ML_DIM, ML_DIM), F32),
                        pltpu.VMEM((ML_HEADS, 8, ML_DIM), F32),
                        pltpu.VMEM((ML_HEADS, 8, max(chunk, ML_DIM)), F32)],
        compiler_params=_cparams(("parallel", "arbitrary")),
        name="mlstm",
    )(zml, zg, conv_w, gate_bias, norm_g)


def _softmax_cols(s, m_ref, p_ref, cols):
    m_old = m_ref[:, cols]
    m_new = jnp.maximum(m_old, jnp.max(s, axis=0, keepdims=True))
    m_ref[:, cols] = m_new
    p_ref[:, cols] = jnp.exp2(s - m_new).astype(BF16)
    return jnp.exp2(m_old - m_new)


def _pipelined_attention(scores, vT_tile, s_refs, p_refs, m_ref, acc_ref, softmax,
                         first, n_plain_pairs, n_tail, max_tile, col_groups):
    sa, sb = s_refs
    pa, pb = p_refs

    def load(j):
        return jnp.clip(j, 0, max_tile)

    def half(j, s_cur, s_nxt, p_cur, p_prev, tail):
        s_nxt[...] = scores(load(j + 1))
        pv = jnp.dot(vT_tile(load(j - 1)), p_prev[...], preferred_element_type=F32)
        alphas = softmax(s_cur, p_cur, j, tail)
        for cols, alpha in zip(col_groups, alphas):
            acc_ref[:, cols] = alpha * (acc_ref[:, cols] + pv[:, cols])

    def pair(j, tail):
        half(j, sa, sb, pa, pb, tail)
        half(j + 1, sb, sa, pb, pa, tail)

    m_ref[...] = jnp.full_like(m_ref, M_INIT)
    acc_ref[...] = jnp.zeros_like(acc_ref)
    pb[...] = jnp.zeros_like(pb)
    sa[...] = scores(load(first))

    if not (isinstance(n_plain_pairs, int) and n_plain_pairs == 0):
        n_trips = n_plain_pairs // LOOP_PAIRS

        def body(i, carry):
            for u in range(LOOP_PAIRS):
                pair(first + 2 * (LOOP_PAIRS * i + u), False)
            return carry

        lax.fori_loop(0, n_trips, body, 0)
        rest = n_plain_pairs - n_trips * LOOP_PAIRS
        for u in range(LOOP_PAIRS - 1):
            @pl.when(u < rest)
            def _():
                pair(first + 2 * (LOOP_PAIRS * n_trips + u), False)
    j = first + 2 * n_plain_pairs
    bufs = ((sa, sb, pa, pb), (sb, sa, pb, pa))
    for t in range(n_tail):
        half(j + t, *bufs[t % 2], True)
    p_last = bufs[(n_tail - 1) % 2][2]
    acc_ref[...] += jnp.dot(vT_tile(load(j + n_tail - 1)), p_last[...], preferred_element_type=F32)


def _da_kernel(lam_ref, qT_ref, k_ref, vT_ref, gain_ref, o_ref,
               qz_ref, sa_ref, sb_ref, pa_ref, pb_ref, m_ref, acc_ref, *, tq, tk, lam_init, S):
    h = pl.program_id(1)
    qi = pl.program_id(2)
    d = DA_QK_DIM
    jd = (qi * tq) // tk
    col_groups = [slice(mp * tq, (mp + 1) * tq) for mp in range(2)]

    qz_ref[...] = jnp.zeros_like(qz_ref)
    for hh in range(2):
        @pl.when(h % 2 == hh)
        def _():
            qz_ref[hh * 2 * d:hh * 2 * d + d, 0:tq] = qT_ref[0, 0:d, :]
            qz_ref[hh * 2 * d + d:(hh + 1) * 2 * d, tq:2 * tq] = qT_ref[0, d:2 * d, :]

    def scores(j):
        return jnp.dot(k_ref[0, pl.ds(pl.multiple_of(j * tk, tk), tk), :], qz_ref[...], preferred_element_type=F32)

    def vT_tile(j):
        return vT_ref[0, :, pl.ds(pl.multiple_of(j * tk, tk), tk)]

    def softmax(s_ref, p_ref, j, tail):
        alphas = []
        for cols in col_groups:
            s = s_ref[:, cols]
            if tail:
                kpos = j * tk + lax.broadcasted_iota(jnp.int32, (tk, tq), 0)
                qpos = qi * tq + lax.broadcasted_iota(jnp.int32, (tk, tq), 1)
                s = jnp.where(kpos <= qpos, s, NEG)
            alphas.append(_softmax_cols(s, m_ref, p_ref, cols))
        return alphas

    _pipelined_attention(scores, vT_tile, (sa_ref, sb_ref), (pa_ref, pb_ref), m_ref, acc_ref, softmax,
                         first=0, n_plain_pairs=jd // 2, n_tail=2, max_tile=S // tk - 1, col_groups=col_groups)

    lp = lam_ref[...]
    lam = (jnp.exp(jnp.sum(lp[0:1] * lp[1:2], axis=1, keepdims=True))
           - jnp.exp(jnp.sum(lp[2:3] * lp[3:4], axis=1, keepdims=True)) + lam_init)
    o1 = acc_ref[0:V_DIM, 0:tq] / acc_ref[V_DIM:V_DIM + 1, 0:tq]
    o2 = acc_ref[0:V_DIM, tq:2 * tq] / acc_ref[V_DIM:V_DIM + 1, tq:2 * tq]
    o = o1 - lam * o2
    y = o * lax.rsqrt(jnp.mean(o * o, axis=0, keepdims=True) + EPS) * gain_ref[...]
    o_ref[0] = (y * (1.0 - lam_init)).astype(o_ref.dtype)


def _diff_attn(da_lambda, daqT, dak, davT, gain_col, lam_init, tq=256, tk=256):
    B, _, S = daqT.shape
    return pl.pallas_call(
        functools.partial(_da_kernel, tq=tq, tk=tk, lam_init=lam_init, S=S),
        grid=(B, DA_HEADS, S // tq),
        in_specs=[pl.BlockSpec((4, DA_QK_DIM), lambda b, h, i: (0, 0)),
                  pl.BlockSpec((1, DA_V_DIM, tq), lambda b, h, i: (b, h, i)),
                  pl.BlockSpec((1, S, LANES), lambda b, h, i: (b, 0, h // 2)),
                  pl.BlockSpec((1, V_EXT, S), lambda b, h, i: (b, h, 0)),
                  pl.BlockSpec((DA_V_DIM, 1), lambda b, h, i: (h, 0))],
        out_specs=pl.BlockSpec((1, DA_V_DIM, tq), lambda b, h, i: (b, h, i)),
        out_shape=jax.ShapeDtypeStruct((B, DA_WIDTH, S), BF16),
        scratch_shapes=[pltpu.VMEM((LANES, 2 * tq), BF16),
                        pltpu.VMEM((tk, 2 * tq), F32),
                        pltpu.VMEM((tk, 2 * tq), F32),
                        pltpu.VMEM((tk, 2 * tq), BF16),
                        pltpu.VMEM((tk, 2 * tq), BF16),
                        pltpu.VMEM((1, 2 * tq), F32),
                        pltpu.VMEM((V_EXT, 2 * tq), F32)],
        compiler_params=_cparams(("parallel", "parallel", "arbitrary")),
        name="diff_attn",
    )(da_lambda, daqT, dak, davT, gain_col)


def _gelu_tanh(x):
    return x * (0.5 * (1.0 + jnp.tanh(math.sqrt(2.0 / math.pi) * (x + 0.044715 * (x * x * x)))))


def _compress_kernel(x_ref, pe_ref, w1_ref, w2_ref, o_ref, b_ref, *, feature_major):
    n = x_ref.shape[2]
    half = CMP_STRIDE * NSA_DIM
    x = x_ref[0, 0]
    a = jnp.dot((x + pe_ref[0, 0:1, :]).astype(BF16), w1_ref[0, 0:half, :], preferred_element_type=F32)
    b_ref[0:n, :] = jnp.dot((x + pe_ref[0, 1:2, :]).astype(BF16), w1_ref[0, half:2 * half, :],
                            preferred_element_type=F32)
    b_ref[n:n + 8, :] = jnp.zeros((8, CMP_HIDDEN), F32)
    hid = _gelu_tanh(a + b_ref[1:n + 1, :]).astype(BF16)
    if feature_major:
        o_ref[0, 0] = lax.dot_general(w2_ref[0], hid, _NT, preferred_element_type=F32).astype(o_ref.dtype)
    else:
        o_ref[0, 0] = jnp.dot(hid, w2_ref[0], preferred_element_type=F32).astype(o_ref.dtype)


def _compress(x16, pe2, w1, w2, which, feature_major):
    B, _, n, half = x16.shape
    G = NSA_GROUPS
    if feature_major:
        out_shape, out_block = (B, G, NSA_DIM, n), (1, 1, NSA_DIM, n)
        w2_arr, w2_block = jnp.swapaxes(w2, 1, 2), (1, NSA_DIM, CMP_HIDDEN)
    else:
        out_shape, out_block = (B, G, n, NSA_DIM), (1, 1, n, NSA_DIM)
        w2_arr, w2_block = w2, (1, CMP_HIDDEN, NSA_DIM)
    return pl.pallas_call(
        functools.partial(_compress_kernel, feature_major=feature_major),
        grid=(B, G),
        in_specs=[pl.BlockSpec((1, 1, n, half), lambda b, g: (b, which * G + g, 0, 0)),
                  pl.BlockSpec((1, 2, half), lambda b, g: (which, 0, 0)),
                  pl.BlockSpec((1, 2 * half, CMP_HIDDEN), lambda b, g: (which, 0, 0)),
                  pl.BlockSpec(w2_block, lambda b, g: (which, 0, 0))],
        out_specs=pl.BlockSpec(out_block, lambda b, g: (b, g, 0, 0)),
        out_shape=jax.ShapeDtypeStruct(out_shape, BF16),
        scratch_shapes=[pltpu.VMEM((n + 8, CMP_HIDDEN), F32)],
        compiler_params=_cparams(("parallel", "parallel")),
        name="compress_v" if feature_major else "compress_k",
    )(x16, pe2, w1, w2_arr.astype(BF16))


def _nsa_kernel(qT_ref, gT_ref, kc_ref, vcT_ref, ks_ref, vsT_ref, kw_ref, vwT_ref, o_ref,
                qg_ref, qz_ref, qs_ref, imp_ref, sel_ref, pc_ref, sa_ref, sb_ref, pa_ref, pb_ref, m_ref, acc_ref,
                wsa_ref, wsb_ref, wpa_ref, wpb_ref, wm_ref, wacc_ref, out_ref,
                *, tq, tk, S):
    g = pl.program_id(1)
    qi = pl.program_id(2)
    d = NSA_DIM
    R = NSA_REP
    ncb = S // CMP_STRIDE
    nsb = S // SEL_BLOCK
    q0 = qi * tq
    jd = q0 // tk
    ratio = SEL_BLOCK // CMP_STRIDE
    per_tile = tk // SEL_BLOCK

    for r in range(R):
        qg_ref[:, r * tq:(r + 1) * tq] = qT_ref[0, r * d:(r + 1) * d, :]
    qz_ref[...] = jnp.zeros_like(qz_ref)
    qs_ref[...] = jnp.zeros_like(qs_ref)
    qs_ref[0:d, :] = qg_ref[...]
    for gg in range(NSA_GROUPS):
        @pl.when(g == gg)
        def _():
            qz_ref[gg * d:(gg + 1) * d, :] = qg_ref[...]

    def gate(r, br):
        return _sigmoid(gT_ref[0, r * 3 + br:r * 3 + br + 1, :])

    qpos = q0 + lax.broadcasted_iota(jnp.int32, (1, tq), 1)

    s_all = jnp.dot(kc_ref[0, 0], qg_ref[...], preferred_element_type=F32)
    cend = lax.broadcasted_iota(jnp.int32, (ncb, tq), 0) * CMP_STRIDE + (CMP_BLOCK - 1)
    cbias = jnp.where(cend <= qpos, 0.0, NEG)
    imp = jnp.zeros((ncb, tq), F32)
    for r in range(R):
        cols = slice(r * tq, (r + 1) * tq)
        s = s_all[:, cols] + cbias
        mx = jnp.maximum(jnp.max(s, axis=0, keepdims=True), M_INIT)
        p = jnp.exp2(s - mx)
        den = jnp.sum(p, axis=0, keepdims=True)
        p = p / jnp.where(den > 0, den, 1.0)
        imp = imp + p
        pc_ref[:, cols] = p.astype(BF16)
    o_cmp = jnp.dot(vcT_ref[0, 0], pc_ref[...], preferred_element_type=F32)
    for r in range(R):
        cols = slice(r * tq, (r + 1) * tq)
        out_ref[:, cols] = gate(r, 0) * o_cmp[:, cols]

    slabs = []
    for c in range(tq // LANES):
        imp_ref[c, 0:8, :] = jnp.zeros((8, LANES), F32)
        imp_ref[c, 8:8 + ncb, :] = imp[:, c * LANES:(c + 1) * LANES]
        imp_ref[c, 8 + ncb:16 + ncb, :] = jnp.zeros((8, LANES), F32)
        slab = jnp.zeros((nsb, LANES), F32)
        for o in range(-1, ratio):
            slab = slab + imp_ref[c, pl.ds(8 + o, nsb, stride=ratio), :]
        slabs.append(slab)
    p_slc = slabs[0] if len(slabs) == 1 else jnp.concatenate(slabs, axis=1)
    blk = lax.broadcasted_iota(jnp.int32, (nsb, tq), 0)
    cur = lax.shift_right_logical(qpos, int(math.log2(SEL_BLOCK)))
    forced = (blk == 0) | (blk == cur) | (blk == cur - 1)
    causal_blk = blk * SEL_BLOCK <= qpos
    score = jnp.where(forced, FORCE_SCORE, jnp.where(causal_blk, p_slc, -1.0))
    blk_f = blk.astype(F32)
    for _ in range(SEL_TOPK):
        mx = jnp.max(score, axis=0, keepdims=True)
        first = jnp.min(jnp.where(score == mx, blk_f, float(nsb)), axis=0, keepdims=True)
        score = jnp.where(blk_f == first, TAKEN, score)
    sel_ref[...] = jnp.where(score == TAKEN, 0.0, NEG)

    head_cols = [slice(r * tq, (r + 1) * tq) for r in range(R)]
    col_groups = [slice(0, R * tq)]
    max_tile = S // tk - 1

    def k_tile(k_ref_, j):
        return k_ref_[0, pl.ds(pl.multiple_of(j * tk, tk), tk), :]

    def vT_tile_of(vT_ref_):
        def vT_tile(j):
            return vT_ref_[0, :, pl.ds(pl.multiple_of(j * tk, tk), tk)]
        return vT_tile

    def softmax_with(bias_of, m_ref_):
        def softmax(s_ref, p_ref, j, tail):
            bias = bias_of(j, tail)
            s = s_ref[...]
            if bias is not None:
                s = s + jnp.concatenate([bias] * R, axis=1)
            return [_softmax_cols(s, m_ref_, p_ref, col_groups[0])]
        return softmax

    def flush(br, acc_ref_):
        for r, cols in enumerate(head_cols):
            den = acc_ref_[V_DIM:V_DIM + 1, cols]
            out_ref[:, cols] += gate(r, br) * (acc_ref_[0:V_DIM, cols] / jnp.where(den > 0, den, 1.0))

    def kpos_of(j):
        return j * tk + lax.broadcasted_iota(jnp.int32, (tk, tq), 0)

    def win_bias(j, tail):
        kpos = kpos_of(j)
        rel = qpos - kpos
        return jnp.where((rel >= 0) & (rel < WINDOW) & (kpos >= 0), 0.0, NEG)

    def win_scores(j):
        return jnp.dot(k_tile(kw_ref, j), qz_ref[...], preferred_element_type=F32)

    n_win = WINDOW // tk + 1
    _pipelined_attention(win_scores, vT_tile_of(vwT_ref), (wsa_ref, wsb_ref), (wpa_ref, wpb_ref), wm_ref, wacc_ref,
                         softmax_with(win_bias, wm_ref), first=jd - (n_win - 1), n_plain_pairs=0, n_tail=n_win,
                         max_tile=max_tile, col_groups=col_groups)
    flush(2, wacc_ref)

    pad_rows = jnp.zeros((BF16_ROWS - per_tile, tq), F32)

    def sel_scores(j):
        rows = [sel_ref[pl.ds(j * per_tile + i, 1), :] for i in range(per_tile)]
        bias = jnp.concatenate(rows + [pad_rows], axis=0).astype(BF16)
        qs_ref[d:d + BF16_ROWS, :] = jnp.concatenate([bias] * R, axis=1)
        return jnp.dot(ks_ref[0, 0, pl.ds(pl.multiple_of(j * tk, tk), tk), :], qs_ref[...],
                       preferred_element_type=F32)

    def sel_bias(j, tail):
        return jnp.where(kpos_of(j) <= qpos, 0.0, NEG) if tail else None

    _pipelined_attention(sel_scores, vT_tile_of(vsT_ref), (sa_ref, sb_ref), (pa_ref, pb_ref), m_ref, acc_ref,
                         softmax_with(sel_bias, m_ref), first=0, n_plain_pairs=jd // 2, n_tail=2,
                         max_tile=max_tile, col_groups=col_groups)
    flush(1, acc_ref)

    for r in range(R):
        o_ref[0, r * d:(r + 1) * d, :] = out_ref[:, r * tq:(r + 1) * tq].astype(o_ref.dtype)


def _nsa(nsqT, nsgT, kcmp, vcmpT, nsks, nsk, nsvT, tq=128, tk=NSA_TK):
    B, _, S = nsqT.shape
    G, R, d = NSA_GROUPS, NSA_REP, NSA_DIM
    ncb = S // CMP_STRIDE
    nsb = S // SEL_BLOCK
    return pl.pallas_call(
        functools.partial(_nsa_kernel, tq=tq, tk=tk, S=S),
        grid=(B, G, S // tq),
        in_specs=[pl.BlockSpec((1, R * d, tq), lambda b, g, i: (b, g, i)),
                  pl.BlockSpec((1, GATE_PAD, tq), lambda b, g, i: (b, g, i)),
                  pl.BlockSpec((1, 1, ncb, d), lambda b, g, i: (b, g, 0, 0)),
                  pl.BlockSpec((1, 1, d, ncb), lambda b, g, i: (b, g, 0, 0)),
                  pl.BlockSpec((1, 1, S, LANES), lambda b, g, i: (b, g, 0, 0)),
                  pl.BlockSpec((1, V_EXT, S), lambda b, g, i: (b, g, 0)),
                  pl.BlockSpec((1, S, LANES), lambda b, g, i: (b, 0, 1)),
                  pl.BlockSpec((1, V_EXT, S), lambda b, g, i: (b, G + g, 0))],
        out_specs=pl.BlockSpec((1, R * d, tq), lambda b, g, i: (b, g, i)),
        out_shape=jax.ShapeDtypeStruct((B, NSA_WIDTH, S), BF16),
        scratch_shapes=[pltpu.VMEM((d, R * tq), BF16),
                        pltpu.VMEM((LANES, R * tq), BF16),
                        pltpu.VMEM((LANES, R * tq), BF16),
                        pltpu.VMEM((tq // LANES, ncb + 16, LANES), F32),
                        pltpu.VMEM((nsb, tq), F32),
                        pltpu.VMEM((ncb, R * tq), BF16),
                        pltpu.VMEM((tk, R * tq), F32),
                        pltpu.VMEM((tk, R * tq), F32),
                        pltpu.VMEM((tk, R * tq), BF16),
                        pltpu.VMEM((tk, R * tq), BF16),
                        pltpu.VMEM((1, R * tq), F32),
                        pltpu.VMEM((V_EXT, R * tq), F32),
                        pltpu.VMEM((tk, R * tq), F32),
                        pltpu.VMEM((tk, R * tq), F32),
                        pltpu.VMEM((tk, R * tq), BF16),
                        pltpu.VMEM((tk, R * tq), BF16),
                        pltpu.VMEM((1, R * tq), F32),
                        pltpu.VMEM((V_EXT, R * tq), F32),
                        pltpu.VMEM((d, R * tq), F32)],
        compiler_params=_cparams(("parallel", "parallel", "arbitrary")),
        name="nsa",
    )(nsqT, nsgT, kcmp, vcmpT, nsks, nsvT, nsk, nsvT)


def _outproj_kernel(x_ref, yml_ref, ydaT_ref, ynsT_ref, wo_ref, o_ref):
    acc = x_ref[0] + jnp.dot(yml_ref[0].astype(BF16), wo_ref[0:ML_WIDTH, :], preferred_element_type=F32)
    acc = acc + lax.dot_general(ydaT_ref[0], wo_ref[ML_WIDTH:ML_WIDTH + DA_WIDTH, :], _TN,
                                preferred_element_type=F32)
    acc = acc + lax.dot_general(ynsT_ref[0], wo_ref[ML_WIDTH + DA_WIDTH:, :], _TN,
                                preferred_element_type=F32)
    o_ref[0] = acc


def _outproj(x, yml, ydaT, ynsT, wo, tm=512):
    B, S, D = x.shape
    return pl.pallas_call(
        _outproj_kernel,
        grid=(B, S // tm),
        in_specs=[pl.BlockSpec((1, tm, D), lambda b, i: (b, i, 0)),
                  pl.BlockSpec((1, tm, ML_WIDTH), lambda b, i: (b, i, 0)),
                  pl.BlockSpec((1, DA_WIDTH, tm), lambda b, i: (b, 0, i)),
                  pl.BlockSpec((1, NSA_WIDTH, tm), lambda b, i: (b, 0, i)),
                  pl.BlockSpec((D, D), lambda b, i: (0, 0))],
        out_specs=pl.BlockSpec((1, tm, D), lambda b, i: (b, i, 0)),
        out_shape=jax.ShapeDtypeStruct((B, S, D), F32),
        compiler_params=_cparams(("parallel", "parallel")),
        name="outproj",
    )(x, yml, ydaT, ynsT, wo)


def _ffn_kernel(x_ref, g_ref, w1_ref, w2_ref, gf_ref, o_ref, hb_ref, acc_ref, *, final):
    j = pl.program_id(1)

    @pl.when(j == 0)
    def _():
        x = x_ref[...]
        hb_ref[...] = _rms(x, g_ref[...]).astype(BF16)
        acc_ref[...] = x

    u = jnp.dot(hb_ref[...], w1_ref[...], preferred_element_type=F32)
    a = jnp.square(jnp.maximum(u, 0.0)).astype(BF16)
    acc_ref[...] += jnp.dot(a, w2_ref[...], preferred_element_type=F32)

    @pl.when(j == pl.num_programs(1) - 1)
    def _():
        y = acc_ref[...]
        if final:
            y = _rms(y, gf_ref[...])
        o_ref[...] = y


def _ffn(x2d, g, w1, w2, gf, final, tm=1024, tf=1024):
    N, D = x2d.shape
    return pl.pallas_call(
        functools.partial(_ffn_kernel, final=final),
        grid=(N // tm, D_FF // tf),
        in_specs=[pl.BlockSpec((tm, D), lambda i, j: (i, 0)),
                  pl.BlockSpec((1, D), lambda i, j: (0, 0)),
                  pl.BlockSpec((D, tf), lambda i, j: (0, j)),
                  pl.BlockSpec((tf, D), lambda i, j: (j, 0)),
                  pl.BlockSpec((1, D), lambda i, j: (0, 0))],
        out_specs=pl.BlockSpec((tm, D), lambda i, j: (i, 0)),
        out_shape=jax.ShapeDtypeStruct((N, D), F32),
        scratch_shapes=[pltpu.VMEM((tm, D), BF16), pltpu.VMEM((tm, D), F32)],
        compiler_params=_cparams(("parallel", "arbitrary")),
        name="ffn",
    )(x2d, g, w1, w2, gf)


def _split_w_in(w_in_l):
    edges = np.concatenate([[0], np.cumsum(IN_SIZES)])
    return [w_in_l[:, int(edges[i]):int(edges[i + 1])] for i in range(len(IN_SIZES))]


def _pad_cols(w, n):
    return jnp.pad(w, ((0, 0), (0, n - w.shape[1])))


def _inproj_weights(w_in_l):
    (ml_q, ml_k, ml_v, ml_o, ml_i, ml_f, da_q, da_k, da_v,
     ns_q, ns_kc, ns_vc, ns_ks, ns_vs, ns_kw, ns_vw, ns_g) = _split_w_in(w_in_l)
    wt = jnp.concatenate([ml_q, ml_k, ml_v, ml_o, _pad_cols(jnp.concatenate([ml_i, ml_f], 1), LANES),
                          da_k, ns_ks, ns_kw, ns_kc, ns_vc], axis=1)
    per_group = NSA_REP * 3
    ns_g_pad = jnp.concatenate(
        [_pad_cols(ns_g[:, gi * per_group:(gi + 1) * per_group], GATE_PAD) for gi in range(NSA_GROUPS)], axis=1)
    wf = jnp.concatenate([da_q, da_v, ns_q, ns_vs, ns_vw, ns_g_pad], axis=1)
    return wt.astype(BF16), wf.T.astype(BF16)


def _half_blocks(nsc):
    B, S, _ = nsc.shape
    n = S // CMP_STRIDE
    t = nsc.reshape(B, n, CMP_STRIDE, 2 * NSA_GROUPS, NSA_DIM)
    return t.transpose(0, 3, 1, 2, 4).reshape(B, 2 * NSA_GROUPS, n, CMP_STRIDE * NSA_DIM)


def kernel(x, norm1, w_in, ml_conv, ml_gate_bias, ml_norm, da_lambda, da_norm, nsa_pe,
           nsa_w1, nsa_w2, w_out, norm2, w_ff1, w_ff2, final_norm):
    B, S, D = x.shape
    depth = norm1.shape[0]
    for l in range(depth):
        wt, wfT = _inproj_weights(w_in[l])
        (zml, zg, dak, nsk, nsks, nsc, daqT, davT, nsqT, nsvT, nsgT) = _inproj(x, norm1[l][None, :], wt, wfT)

        yml = _mlstm(zml, zg, ml_conv[l], _pad_cols(ml_gate_bias[l][None, :], LANES), ml_norm[l][None, :])

        lam_init = 0.8 - 0.6 * math.exp(-0.3 * l)
        ydaT = _diff_attn(da_lambda[l], daqT, dak, davT, da_norm[l][:, None], lam_init)

        x16 = _half_blocks(nsc)
        pe2 = nsa_pe[l].reshape(2, 2, CMP_STRIDE * NSA_DIM)
        w1b = nsa_w1[l].astype(BF16)
        kcmp = _compress(x16, pe2, w1b, nsa_w2[l], 0, False)
        vcmpT = _compress(x16, pe2, w1b, nsa_w2[l], 1, True)
        ynsT = _nsa(nsqT, nsgT, kcmp, vcmpT, nsks, nsk, nsvT)

        x = _outproj(x, yml, ydaT, ynsT, w_out[l].astype(BF16))
        x = _ffn(x.reshape(B * S, D), norm2[l][None, :], w_ff1[l].astype(BF16), w_ff2[l].astype(BF16),
                 final_norm[None, :], final=(l == depth - 1)).reshape(B, S, D)
    return x
```

```python
import functools
import math

import numpy as np
import jax
import jax.numpy as jnp
from jax import lax
from jax.experimental import pallas as pl
from jax.experimental.pallas import tpu as pltpu

F32 = jnp.float32
BF16 = jnp.bfloat16

D_MODEL = 1024
ML_HEADS = 4
ML_DIM = 64
ML_WIDTH = ML_HEADS * ML_DIM
ML_TILE_CHUNK = 128
CONV_W = 4
DA_HEADS = 4
DA_QK_DIM = 32
DA_V_DIM = 64
DA_WIDTH = DA_HEADS * DA_V_DIM
NSA_HEADS = 8
NSA_GROUPS = 2
NSA_REP = NSA_HEADS // NSA_GROUPS
NSA_DIM = 64
NSA_WIDTH = NSA_HEADS * NSA_DIM
NSA_KV = NSA_GROUPS * NSA_DIM
CMP_BLOCK = 32
CMP_STRIDE = 16
CMP_HIDDEN = 4 * NSA_DIM
SEL_BLOCK = 64
SEL_TOPK = 16
WINDOW = 512
D_FF = 4 * D_MODEL
EPS = 1e-6
FORCE_SCORE = 1e4
IN_SIZES = (ML_WIDTH, ML_WIDTH, ML_WIDTH, ML_WIDTH, ML_HEADS, ML_HEADS,
            2 * DA_HEADS * DA_QK_DIM, 2 * DA_HEADS * DA_QK_DIM, DA_WIDTH,
            NSA_WIDTH, NSA_KV, NSA_KV, NSA_KV, NSA_KV, NSA_KV, NSA_KV, 3 * NSA_HEADS)

LANES = 128
GATE_PAD = 16
NEG = -1e30
M_INIT = -1e29
LOOP_PAIRS = 4
TAKEN = -3e38
LOG2E = 1.4426950408889634
V_DIM = 64
BF16_ROWS = 16
NSA_TK = 256
V_EXT = 80
VMEM_LIMIT = 56 * 1024 * 1024

_T_ML = (0, 1024)
_T_MLG = (1024, 1152)
_T_DAK = (1152, 1408)
_T_NSK = (1408, 1664)
_T_NSC = (1664, 1920)
_T_COLS = 1920
_F_DAQ = (0, 256)
_F_DAV = (256, 512)
_F_NSQ = (512, 1024)
_F_NSV = (1024, 1280)
_F_NSG = (1280, 1280 + NSA_GROUPS * GATE_PAD)
_F_ROWS = _F_NSG[1]

_NT = (((1,), (1,)), ((), ()))
_TN = (((0,), (0,)), ((), ()))


def _cparams(sem):
    return pltpu.CompilerParams(dimension_semantics=sem, vmem_limit_bytes=VMEM_LIMIT)


def _rms(x, g):
    return x * lax.rsqrt(jnp.mean(x * x, axis=-1, keepdims=True) + EPS) * g


def _inproj_kernel(x_ref, g_ref, wt_ref, wf_ref,
                   zml_ref, zg_ref, dak_ref, nsk_ref, nsks_ref, nsc_ref,
                   daqT_ref, davT_ref, nsqT_ref, nsvT_ref, nsgT_ref):
    hb = _rms(x_ref[0], g_ref[...]).astype(BF16)

    def tdot(span):
        return jnp.dot(hb, wt_ref[:, span[0]:span[1]], preferred_element_type=F32)

    def fdot(span):
        return lax.dot_general(wf_ref[span[0]:span[1], :], hb, _NT, preferred_element_type=F32)

    zml_ref[0] = tdot(_T_ML)
    zg_ref[0] = tdot(_T_MLG)
    dak_ref[0] = tdot(_T_DAK).astype(BF16)
    nsk = tdot(_T_NSK).astype(BF16)
    nsk_ref[0] = nsk
    tm_ = nsk.shape[0]
    row = lax.broadcasted_iota(jnp.int32, (tm_, NSA_DIM), 0)
    lane = lax.broadcasted_iota(jnp.int32, (tm_, NSA_DIM), 1)
    blk_in_tile = lax.shift_right_logical(row & (NSA_TK - 1), int(math.log2(SEL_BLOCK)))
    onehot = jnp.where(lane == blk_in_tile, 1.0, 0.0).astype(BF16)
    for gi in range(NSA_GROUPS):
        nsks_ref[0, gi] = jnp.concatenate([nsk[:, gi * NSA_DIM:(gi + 1) * NSA_DIM], onehot], axis=1)
    nsc_ref[0] = tdot(_T_NSC)
    daqT_ref[0] = (fdot(_F_DAQ) * (DA_QK_DIM ** -0.5 * LOG2E)).astype(BF16)
    nsqT_ref[0] = (fdot(_F_NSQ) * (NSA_DIM ** -0.5 * LOG2E)).astype(BF16)
    nsgT_ref[0] = fdot(_F_NSG)
    tm = hb.shape[0]
    for v_ref, span in ((davT_ref, _F_DAV), (nsvT_ref, _F_NSV)):
        v = fdot(span).astype(BF16)
        for h in range((span[1] - span[0]) // V_DIM):
            v_ref[0, h * V_EXT:h * V_EXT + V_DIM, :] = v[h * V_DIM:(h + 1) * V_DIM, :]
            v_ref[0, h * V_EXT + V_DIM:(h + 1) * V_EXT, :] = jnp.ones((V_EXT - V_DIM, tm), BF16)


def _inproj(x, g, wt, wfT, tm=512):
    B, S, D = x.shape
    tok = lambda w, dt: jax.ShapeDtypeStruct((B, S, w), dt)
    feat = lambda r, dt: jax.ShapeDtypeStruct((B, r, S), dt)
    tspec = lambda w: pl.BlockSpec((1, tm, w), lambda b, i: (b, i, 0))
    fspec = lambda r: pl.BlockSpec((1, r, tm), lambda b, i: (b, 0, i))
    return pl.pallas_call(
        _inproj_kernel,
        grid=(B, S // tm),
        in_specs=[tspec(D),
                  pl.BlockSpec((1, D), lambda b, i: (0, 0)),
                  pl.BlockSpec((D, _T_COLS), lambda b, i: (0, 0)),
                  pl.BlockSpec((_F_ROWS, D), lambda b, i: (0, 0))],
        out_specs=[tspec(1024), tspec(128), tspec(256), tspec(256),
                   pl.BlockSpec((1, NSA_GROUPS, tm, LANES), lambda b, i: (b, 0, i, 0)), tspec(256),
                   fspec(256), fspec(4 * V_EXT), fspec(512), fspec(4 * V_EXT), fspec(NSA_GROUPS * GATE_PAD)],
        out_shape=[tok(1024, F32), tok(128, F32), tok(256, BF16), tok(256, BF16),
                   jax.ShapeDtypeStruct((B, NSA_GROUPS, S, LANES), BF16), tok(256, F32),
                   feat(256, BF16), feat(4 * V_EXT, BF16), feat(512, BF16), feat(4 * V_EXT, BF16),
                   feat(NSA_GROUPS * GATE_PAD, F32)],
        compiler_params=_cparams(("parallel", "parallel")),
        name="inproj",
    )(x, g, wt, wfT)


def _log_sigmoid(x):
    return jnp.minimum(x, 0.0) - jnp.log1p(jnp.exp(-jnp.abs(x)))


def _sigmoid(x):
    return 1.0 / (1.0 + jnp.exp(-x))


def _mlstm_kernel(zml_ref, zg_ref, cw_ref, gb_ref, nrm_ref, y_ref,
                  pad_ref, c_ref, n_ref, m_ref, *, T, L):
    d = ML_DIM
    W = max(L, d)
    t = pl.program_id(1)

    @pl.when(t == 0)
    def _():
        pad_ref[0:8, :] = jnp.zeros((8, 2 * ML_WIDTH), F32)
        c_ref[...] = jnp.zeros_like(c_ref)
        n_ref[...] = jnp.zeros_like(n_ref)
        m_ref[...] = jnp.zeros_like(m_ref)

    @pl.when(t > 0)
    def _():
        pad_ref[0:8, :] = pad_ref[T:T + 8, :]

    pad_ref[8:8 + T, :] = zml_ref[0, :, 0:2 * ML_WIDTH]
    conv = cw_ref[0:1, :] * pad_ref[5:5 + T, :]
    for j in range(1, CONV_W):
        conv = conv + cw_ref[j:j + 1, :] * pad_ref[5 + j:5 + j + T, :]
    qk = conv * _sigmoid(conv)
    gates = zg_ref[0] + gb_ref[...]
    logf = _log_sigmoid(gates)

    row = lax.broadcasted_iota(jnp.int32, (L, L), 0)
    col = lax.broadcasted_iota(jnp.int32, (L, L), 1)
    causal = col <= row
    tril = causal.astype(F32)
    triu = (row <= col).astype(F32)

    lane_of = lax.broadcasted_iota(jnp.int32, (LANES, W), 0)
    pick = [(lane_of == g).astype(F32) for g in range(2 * ML_HEADS)]

    H = range(ML_HEADS)

    def local_part(c):
        r0 = c * L
        g_c = gates[r0:r0 + L, :]
        lf_c = logf[r0:r0 + L, :]
        b_cols = jnp.dot(tril, lf_c, preferred_element_type=F32,
                         precision=lax.Precision.HIGHEST)
        b_rows = jnp.dot(lf_c.T, triu, preferred_element_type=F32,
                         precision=lax.Precision.HIGHEST)
        g_rows = g_c.T
        b_col = [jnp.dot(b_cols, pick[ML_HEADS + h], preferred_element_type=F32,
                         precision=lax.Precision.HIGHEST) for h in H]
        ig_col = [jnp.dot(g_c, pick[h], preferred_element_type=F32,
                          precision=lax.Precision.HIGHEST) for h in H]
        b_row = [b_rows[ML_HEADS + h:ML_HEADS + h + 1, :] for h in H]
        ig_row = [g_rows[h:h + 1, :] for h in H]
        qh = [qk[r0:r0 + L, h * d:(h + 1) * d] for h in H]
        kh = [qk[r0:r0 + L, ML_WIDTH + h * d:ML_WIDTH + (h + 1) * d] * (d ** -0.5) for h in H]
        vh = [zml_ref[0, r0:r0 + L, 2 * ML_WIDTH + h * d:2 * ML_WIDTH + (h + 1) * d] for h in H]
        qkt = [lax.dot_general(qh[h], kh[h], _NT, preferred_element_type=F32) for h in H]
        g_tot = [b_col[h][L - 1:L, :] for h in H]
        a_max = [jnp.max(g_tot[h][:, 0:L] - b_row[h] + ig_row[h], axis=1, keepdims=True) for h in H]
        w_col = [jnp.exp((g_tot[h] - b_col[h] + ig_col[h] - a_max[h])[:, 0:d]) for h in H]
        c_loc = [lax.dot_general(vh[h] * w_col[h], kh[h], _TN, preferred_element_type=F32) for h in H]
        n_loc = [jnp.sum(kh[h] * w_col[h], axis=0, keepdims=True) for h in H]
        dmat = [jnp.where(causal, b_col[h][:, 0:L] - b_row[h] + ig_row[h], NEG) for h in H]
        d_max = [jnp.max(dmat[h], axis=1, keepdims=True) for h in H]
        return dict(b_col=b_col, qh=qh, vh=vh, qkt=qkt, g_tot=g_tot, a_max=a_max, c_loc=c_loc, n_loc=n_loc,
                    dmat=dmat, d_max=d_max)

    def carried_part(c, lp):
        r0 = c * L
        b_col, qh, vh, qkt, g_tot, a_max = lp["b_col"], lp["qh"], lp["vh"], lp["qkt"], lp["g_tot"], lp["a_max"]
        oh = [zml_ref[0, r0:r0 + L, 3 * ML_WIDTH + h * d:3 * ML_WIDTH + (h + 1) * d] for h in H]
        c_prev = [c_ref[h] for h in H]
        n_prev = [n_ref[h, 0:1, :] for h in H]
        m_prev = [m_ref[h, 0:1, :] for h in H]
        q_c = [lax.dot_general(qh[h], c_prev[h], _NT, preferred_element_type=F32) for h in H]
        inter_log = [b_col[h] + m_prev[h] for h in H]
        m_t = [jnp.maximum(inter_log[h], lp["d_max"][h]) for h in H]
        wts = [jnp.exp(lp["dmat"][h] - m_t[h][:, 0:L]) * qkt[h] for h in H]
        s_inter = [jnp.exp((inter_log[h] - m_t[h])[:, 0:d]) for h in H]
        num = [jnp.dot(wts[h], vh[h], preferred_element_type=F32) + s_inter[h] * q_c[h] for h in H]
        den = [jnp.sum(wts[h], axis=1, keepdims=True)
               + s_inter[h] * jnp.sum(qh[h] * n_prev[h], axis=1, keepdims=True) for h in H]
        hh = [num[h] / jnp.maximum(jnp.abs(den[h]), jnp.exp(-m_t[h][:, 0:d])) for h in H]
        m_new = [jnp.maximum(g_tot[h] + m_prev[h], a_max[h]) for h in H]
        for h in H:
            s_prev = jnp.exp((g_tot[h] + m_prev[h] - m_new[h])[:, 0:d])
            s_loc = jnp.exp((a_max[h] - m_new[h])[:, 0:d])
            c_ref[h] = s_prev * c_prev[h] + s_loc * lp["c_loc"][h]
            n_ref[h] = jnp.broadcast_to(s_prev * n_prev[h] + s_loc * lp["n_loc"][h], (8, d))
            m_ref[h] = jnp.broadcast_to(m_new[h], (8, W))
        for h in H:
            yh = _sigmoid(oh[h]) * hh[h]
            y_ref[0, r0:r0 + L, h * d:(h + 1) * d] = _rms(yh, nrm_ref[0:1, h * d:(h + 1) * d])

    for c in range(T // L):
        carried_part(c, local_part(c))


def _mlstm(zml, zg, conv_w, gate_bias, norm_g, T=256, chunk=ML_TILE_CHUNK):
    B, S, _ = zml.shape
    return pl.pallas_call(
        functools.partial(_mlstm_kernel, T=T, L=chunk),
        grid=(B, S // T),
        in_specs=[pl.BlockSpec((1, T, 1024), lambda b, t: (b, t, 0)),
                  pl.BlockSpec((1, T, 128), lambda b, t: (b, t, 0)),
                  pl.BlockSpec((CONV_W, 2 * ML_WIDTH), lambda b, t: (0, 0)),
                  pl.BlockSpec((1, 128), lambda b, t: (0, 0)),
                  pl.BlockSpec((1, ML_WIDTH), lambda b, t: (0, 0))],
        out_specs=pl.BlockSpec((1, T, ML_WIDTH), lambda b, t: (b, t, 0)),
        out_shape=jax.ShapeDtypeStruct((B, S, ML_WIDTH), F32),
        scratch_shapes=[pltpu.VMEM((T + 8, 2 * ML_WIDTH), F32),
                        pltpu.VMEM((ML_HEADS, ML_DIM, ML_DIM), F32),
                        pltpu.VMEM((ML_HEADS, 8, ML_DIM), F32),
                        pltpu.VMEM((ML_HEADS, 8, max(chunk, ML_DIM)), F32)],
        compiler_params=_cparams(("parallel", "arbitrary")),
        name="mlstm",
    )(zml, zg, conv_w, gate_bias, norm_g)


def _softmax_cols(s, m_ref, p_ref, cols):
    m_old = m_ref[:, cols]
    m_new = jnp.maximum(m_old, jnp.max(s, axis=0, keepdims=True))
    m_ref[:, cols] = m_new
    p_ref[:, cols] = jnp.exp2(s - m_new).astype(BF16)
    return jnp.exp2(m_old - m_new)


def _pipelined_attention(scores, vT_tile, s_refs, p_refs, m_ref, acc_ref, softmax,
                         first, n_plain_pairs, n_tail, max_tile, col_groups):
    sa, sb = s_refs
    pa, pb = p_refs

    def load(j):
        return jnp.clip(j, 0, max_tile)

    def half(j, s_cur, s_nxt, p_cur, p_prev, tail):
        s_nxt[...] = scores(load(j + 1))
        pv = jnp.dot(vT_tile(load(j - 1)), p_prev[...], preferred_element_type=F32)
        alphas = softmax(s_cur, p_cur, j, tail)
        for cols, alpha in zip(col_groups, alphas):
            acc_ref[:, cols] = alpha * (acc_ref[:, cols] + pv[:, cols])

    def pair(j, tail):
        half(j, sa, sb, pa, pb, tail)
        half(j + 1, sb, sa, pb, pa, tail)

    m_ref[...] = jnp.full_like(m_ref, M_INIT)
    acc_ref[...] = jnp.zeros_like(acc_ref)
    pb[...] = jnp.zeros_like(pb)
    sa[...] = scores(load(first))

    if not (isinstance(n_plain_pairs, int) and n_plain_pairs == 0):
        n_trips = n_plain_pairs // LOOP_PAIRS

        def body(i, carry):
            for u in range(LOOP_PAIRS):
                pair(first + 2 * (LOOP_PAIRS * i + u), False)
            return carry

        lax.fori_loop(0, n_trips, body, 0)
        rest = n_plain_pairs - n_trips * LOOP_PAIRS
        for u in range(LOOP_PAIRS - 1):
            @pl.when(u < rest)
            def _():
                pair(first + 2 * (LOOP_PAIRS * n_trips + u), False)
    j = first + 2 * n_plain_pairs
    bufs = ((sa, sb, pa, pb), (sb, sa, pb, pa))
    for t in range(n_tail):
        half(j + t, *bufs[t % 2], True)
    p_last = bufs[(n_tail - 1) % 2][2]
    acc_ref[...] += jnp.dot(vT_tile(load(j + n_tail - 1)), p_last[...], preferred_element_type=F32)


def _da_kernel(lam_ref, qT_ref, k_ref, vT_ref, gain_ref, o_ref,
               qz_ref, sa_ref, sb_ref, pa_ref, pb_ref, m_ref, acc_ref, *, tq, tk, lam_init, S):
    h = pl.program_id(1)
    qi = pl.program_id(2)
    d = DA_QK_DIM
    jd = (qi * tq) // tk
    col_groups = [slice(mp * tq, (mp + 1) * tq) for mp in range(2)]

    qz_ref[...] = jnp.zeros_like(qz_ref)
    for hh in range(2):
        @pl.when(h % 2 == hh)
        def _():
            qz_ref[hh * 2 * d:hh * 2 * d + d, 0:tq] = qT_ref[0, 0:d, :]
            qz_ref[hh * 2 * d + d:(hh + 1) * 2 * d, tq:2 * tq] = qT_ref[0, d:2 * d, :]

    def scores(j):
        return jnp.dot(k_ref[0, pl.ds(pl.multiple_of(j * tk, tk), tk), :], qz_ref[...], preferred_element_type=F32)

    def vT_tile(j):
        return vT_ref[0, :, pl.ds(pl.multiple_of(j * tk, tk), tk)]

    def softmax(s_ref, p_ref, j, tail):
        alphas = []
        for cols in col_groups:
            s = s_ref[:, cols]
            if tail:
                kpos = j * tk + lax.broadcasted_iota(jnp.int32, (tk, tq), 0)
                qpos = qi * tq + lax.broadcasted_iota(jnp.int32, (tk, tq), 1)
                s = jnp.where(kpos <= qpos, s, NEG)
            alphas.append(_softmax_cols(s, m_ref, p_ref, cols))
        return alphas

    _pipelined_attention(scores, vT_tile, (sa_ref, sb_ref), (pa_ref, pb_ref), m_ref, acc_ref, softmax,
                         first=0, n_plain_pairs=jd // 2, n_tail=2, max_tile=S // tk - 1, col_groups=col_groups)

    lp = lam_ref[...]
    lam = (jnp.exp(jnp.sum(lp[0:1] * lp[1:2], axis=1, keepdims=True))
           - jnp.exp(jnp.sum(lp[2:3] * lp[3:4], axis=1, keepdims=True)) + lam_init)
    o1 = acc_ref[0:V_DIM, 0:tq] / acc_ref[V_DIM:V_DIM + 1, 0:tq]
    o2 = acc_ref[0:V_DIM, tq:2 * tq] / acc_ref[V_DIM:V_DIM + 1, tq:2 * tq]
    o = o1 - lam * o2
    y = o * lax.rsqrt(jnp.mean(o * o, axis=0, keepdims=True) + EPS) * gain_ref[...]
    o_ref[0] = (y * (1.0 - lam_init)).astype(o_ref.dtype)


def _diff_attn(da_lambda, daqT, dak, davT, gain_col, lam_init, tq=256, tk=256):
    B, _, S = daqT.shape
    return pl.pallas_call(
        functools.partial(_da_kernel, tq=tq, tk=tk, lam_init=lam_init, S=S),
        grid=(B, DA_HEADS, S // tq),
        in_specs=[pl.BlockSpec((4, DA_QK_DIM), lambda b, h, i: (0, 0)),
                  pl.BlockSpec((1, DA_V_DIM, tq), lambda b, h, i: (b, h, i)),
                  pl.BlockSpec((1, S, LANES), lambda b, h, i: (b, 0, h // 2)),
                  pl.BlockSpec((1, V_EXT, S), lambda b, h, i: (b, h, 0)),
                  pl.BlockSpec((DA_V_DIM, 1), lambda b, h, i: (h, 0))],
        out_specs=pl.BlockSpec((1, DA_V_DIM, tq), lambda b, h, i: (b, h, i)),
        out_shape=jax.ShapeDtypeStruct((B, DA_WIDTH, S), BF16),
        scratch_shapes=[pltpu.VMEM((LANES, 2 * tq), BF16),
                        pltpu.VMEM((tk, 2 * tq), F32),
                        pltpu.VMEM((tk, 2 * tq), F32),
                        pltpu.VMEM((tk, 2 * tq), BF16),
                        pltpu.VMEM((tk, 2 * tq), BF16),
                        pltpu.VMEM((1, 2 * tq), F32),
                        pltpu.VMEM((V_EXT, 2 * tq), F32)],
        compiler_params=_cparams(("parallel", "parallel", "arbitrary")),
        name="diff_attn",
    )(da_lambda, daqT, dak, davT, gain_col)


def _gelu_tanh(x):
    return x * (0.5 * (1.0 + jnp.tanh(math.sqrt(2.0 / math.pi) * (x + 0.044715 * (x * x * x)))))


def _compress_kernel(x_ref, pe_ref, w1_ref, w2_ref, o_ref, b_ref, *, feature_major):
    n = x_ref.shape[2]
    half = CMP_STRIDE * NSA_DIM
    x = x_ref[0, 0]
    a = jnp.dot((x + pe_ref[0, 0:1, :]).astype(BF16), w1_ref[0, 0:half, :], preferred_element_type=F32)
    b_ref[0:n, :] = jnp.dot((x + pe_ref[0, 1:2, :]).astype(BF16), w1_ref[0, half:2 * half, :],
                            preferred_element_type=F32)
    b_ref[n:n + 8, :] = jnp.zeros((8, CMP_HIDDEN), F32)
    hid = _gelu_tanh(a + b_ref[1:n + 1, :]).astype(BF16)
    if feature_major:
        o_ref[0, 0] = lax.dot_general(w2_ref[0], hid, _NT, preferred_element_type=F32).astype(o_ref.dtype)
    else:
        o_ref[0, 0] = jnp.dot(hid, w2_ref[0], preferred_element_type=F32).astype(o_ref.dtype)


def _compress(x16, pe2, w1, w2, which, feature_major):
    B, _, n, half = x16.shape
    G = NSA_GROUPS
    if feature_major:
        out_shape, out_block = (B, G, NSA_DIM, n), (1, 1, NSA_DIM, n)
        w2_arr, w2_block = jnp.swapaxes(w2, 1, 2), (1, NSA_DIM, CMP_HIDDEN)
    else:
        out_shape, out_block = (B, G, n, NSA_DIM), (1, 1, n, NSA_DIM)
        w2_arr, w2_block = w2, (1, CMP_HIDDEN, NSA_DIM)
    return pl.pallas_call(
        functools.partial(_compress_kernel, feature_major=feature_major),
        grid=(B, G),
        in_specs=[pl.BlockSpec((1, 1, n, half), lambda b, g: (b, which * G + g, 0, 0)),
                  pl.BlockSpec((1, 2, half), lambda b, g: (which, 0, 0)),
                  pl.BlockSpec((1, 2 * half, CMP_HIDDEN), lambda b, g: (which, 0, 0)),
                  pl.BlockSpec(w2_block, lambda b, g: (which, 0, 0))],
        out_specs=pl.BlockSpec(out_block, lambda b, g: (b, g, 0, 0)),
        out_shape=jax.ShapeDtypeStruct(out_shape, BF16),
        scratch_shapes=[pltpu.VMEM((n + 8, CMP_HIDDEN), F32)],
        compiler_params=_cparams(("parallel", "parallel")),
        name="compress_v" if feature_major else "compress_k",
    )(x16, pe2, w1, w2_arr.astype(BF16))


def _nsa_kernel(qT_ref, gT_ref, kc_ref, vcT_ref, ks_ref, vsT_ref, kw_ref, vwT_ref, o_ref,
                qg_ref, qz_ref, qs_ref, imp_ref, sel_ref, pc_ref, sa_ref, sb_ref, pa_ref, pb_ref, m_ref, acc_ref,
                wsa_ref, wsb_ref, wpa_ref, wpb_ref, wm_ref, wacc_ref, out_ref,
                *, tq, tk, S):
    g = pl.program_id(1)
    qi = pl.program_id(2)
    d = NSA_DIM
    R = NSA_REP
    ncb = S // CMP_STRIDE
    nsb = S // SEL_BLOCK
    q0 = qi * tq
    jd = q0 // tk
    ratio = SEL_BLOCK // CMP_STRIDE
    per_tile = tk // SEL_BLOCK

    for r in range(R):
        qg_ref[:, r * tq:(r + 1) * tq] = qT_ref[0, r * d:(r + 1) * d, :]
    qz_ref[...] = jnp.zeros_like(qz_ref)
    qs_ref[...] = jnp.zeros_like(qs_ref)
    qs_ref[0:d, :] = qg_ref[...]
    for gg in range(NSA_GROUPS):
        @pl.when(g == gg)
        def _():
            qz_ref[gg * d:(gg + 1) * d, :] = qg_ref[...]

    def gate(r, br):
        return _sigmoid(gT_ref[0, r * 3 + br:r * 3 + br + 1, :])

    qpos = q0 + lax.broadcasted_iota(jnp.int32, (1, tq), 1)

    s_all = jnp.dot(kc_ref[0, 0], qg_ref[...], preferred_element_type=F32)
    cend = lax.broadcasted_iota(jnp.int32, (ncb, tq), 0) * CMP_STRIDE + (CMP_BLOCK - 1)
    cbias = jnp.where(cend <= qpos, 0.0, NEG)
    hs = range(R)
    sc = [s_all[:, r * tq:(r + 1) * tq] + cbias for r in hs]
    mx = [jnp.maximum(jnp.max(sc[r], axis=0, keepdims=True), M_INIT) for r in hs]
    pu = [jnp.exp2(sc[r] - mx[r]) for r in hs]
    den = [jnp.sum(pu[r], axis=0, keepdims=True) for r in hs]
    pn = [pu[r] / jnp.where(den[r] > 0, den[r], 1.0) for r in hs]
    imp = pn[0]
    for r in range(1, R):
        imp = imp + pn[r]
    for r in hs:
        pc_ref[:, r * tq:(r + 1) * tq] = pn[r].astype(BF16)
    o_cmp = jnp.dot(vcT_ref[0, 0], pc_ref[...], preferred_element_type=F32)
    for r in range(R):
        cols = slice(r * tq, (r + 1) * tq)
        out_ref[:, cols] = gate(r, 0) * o_cmp[:, cols]

    slabs = []
    for c in range(tq // LANES):
        imp_ref[c, 0:8, :] = jnp.zeros((8, LANES), F32)
        imp_ref[c, 8:8 + ncb, :] = imp[:, c * LANES:(c + 1) * LANES]
        imp_ref[c, 8 + ncb:16 + ncb, :] = jnp.zeros((8, LANES), F32)
        slab = jnp.zeros((nsb, LANES), F32)
        for o in range(-1, ratio):
            slab = slab + imp_ref[c, pl.ds(8 + o, nsb, stride=ratio), :]
        slabs.append(slab)
    p_slc = slabs[0] if len(slabs) == 1 else jnp.concatenate(slabs, axis=1)
    blk = lax.broadcasted_iota(jnp.int32, (nsb, tq), 0)
    cur = lax.shift_right_logical(qpos, int(math.log2(SEL_BLOCK)))
    forced = (blk == 0) | (blk == cur) | (blk == cur - 1)
    causal_blk = blk * SEL_BLOCK <= qpos
    score = jnp.where(forced, FORCE_SCORE, jnp.where(causal_blk, p_slc, -1.0))
    blk_f = blk.astype(F32)
    for _ in range(SEL_TOPK):
        mx = jnp.max(score, axis=0, keepdims=True)
        first = jnp.min(jnp.where(score == mx, blk_f, float(nsb)), axis=0, keepdims=True)
        score = jnp.where(blk_f == first, TAKEN, score)
    sel_ref[...] = jnp.where(score == TAKEN, 0.0, NEG)

    head_cols = [slice(r * tq, (r + 1) * tq) for r in range(R)]
    col_groups = [slice(0, R * tq)]
    max_tile = S // tk - 1

    def k_tile(k_ref_, j):
        return k_ref_[0, pl.ds(pl.multiple_of(j * tk, tk), tk), :]

    def vT_tile_of(vT_ref_):
        def vT_tile(j):
            return vT_ref_[0, :, pl.ds(pl.multiple_of(j * tk, tk), tk)]
        return vT_tile

    def softmax_with(bias_of, m_ref_):
        def softmax(s_ref, p_ref, j, tail):
            bias = bias_of(j, tail)
            s = s_ref[...]
            if bias is not None:
                s = s + jnp.concatenate([bias] * R, axis=1)
            return [_softmax_cols(s, m_ref_, p_ref, col_groups[0])]
        return softmax

    def flush(br, acc_ref_):
        for r, cols in enumerate(head_cols):
            den = acc_ref_[V_DIM:V_DIM + 1, cols]
            out_ref[:, cols] += gate(r, br) * (acc_ref_[0:V_DIM, cols] / jnp.where(den > 0, den, 1.0))

    def kpos_of(j):
        return j * tk + lax.broadcasted_iota(jnp.int32, (tk, tq), 0)

    def win_bias(j, tail):
        kpos = kpos_of(j)
        rel = qpos - kpos
        return jnp.where((rel >= 0) & (rel < WINDOW) & (kpos >= 0), 0.0, NEG)

    def win_scores(j):
        return jnp.dot(k_tile(kw_ref, j), qz_ref[...], preferred_element_type=F32)

    n_win = WINDOW // tk + 1
    _pipelined_attention(win_scores, vT_tile_of(vwT_ref), (wsa_ref, wsb_ref), (wpa_ref, wpb_ref), wm_ref, wacc_ref,
                         softmax_with(win_bias, wm_ref), first=jd - (n_win - 1), n_plain_pairs=0, n_tail=n_win,
                         max_tile=max_tile, col_groups=col_groups)
    flush(2, wacc_ref)

    pad_rows = jnp.zeros((BF16_ROWS - per_tile, tq), F32)

    def sel_scores(j):
        rows = [sel_ref[pl.ds(j * per_tile + i, 1), :] for i in range(per_tile)]
        bias = jnp.concatenate(rows + [pad_rows], axis=0).astype(BF16)
        qs_ref[d:d + BF16_ROWS, :] = jnp.concatenate([bias] * R, axis=1)
        return jnp.dot(ks_ref[0, 0, pl.ds(pl.multiple_of(j * tk, tk), tk), :], qs_ref[...],
                       preferred_element_type=F32)

    def sel_bias(j, tail):
        return jnp.where(kpos_of(j) <= qpos, 0.0, NEG) if tail else None

    _pipelined_attention(sel_scores, vT_tile_of(vsT_ref), (sa_ref, sb_ref), (pa_ref, pb_ref), m_ref, acc_ref,
                         softmax_with(sel_bias, m_ref), first=0, n_plain_pairs=jd // 2, n_tail=2,
                         max_tile=max_tile, col_groups=col_groups)
    flush(1, acc_ref)

    for r in range(R):
        o_ref[0, r * d:(r + 1) * d, :] = out_ref[:, r * tq:(r + 1) * tq].astype(o_ref.dtype)


def _nsa(nsqT, nsgT, kcmp, vcmpT, nsks, nsk, nsvT, tq=128, tk=NSA_TK):
    B, _, S = nsqT.shape
    G, R, d = NSA_GROUPS, NSA_REP, NSA_DIM
    ncb = S // CMP_STRIDE
    nsb = S // SEL_BLOCK
    return pl.pallas_call(
        functools.partial(_nsa_kernel, tq=tq, tk=tk, S=S),
        grid=(B, G, S // tq),
        in_specs=[pl.BlockSpec((1, R * d, tq), lambda b, g, i: (b, g, i)),
                  pl.BlockSpec((1, GATE_PAD, tq), lambda b, g, i: (b, g, i)),
                  pl.BlockSpec((1, 1, ncb, d), lambda b, g, i: (b, g, 0, 0)),
                  pl.BlockSpec((1, 1, d, ncb), lambda b, g, i: (b, g, 0, 0)),
                  pl.BlockSpec((1, 1, S, LANES), lambda b, g, i: (b, g, 0, 0)),
                  pl.BlockSpec((1, V_EXT, S), lambda b, g, i: (b, g, 0)),
                  pl.BlockSpec((1, S, LANES), lambda b, g, i: (b, 0, 1)),
                  pl.BlockSpec((1, V_EXT, S), lambda b, g, i: (b, G + g, 0))],
        out_specs=pl.BlockSpec((1, R * d, tq), lambda b, g, i: (b, g, i)),
        out_shape=jax.ShapeDtypeStruct((B, NSA_WIDTH, S), BF16),
        scratch_shapes=[pltpu.VMEM((d, R * tq), BF16),
                        pltpu.VMEM((LANES, R * tq), BF16),
                        pltpu.VMEM((LANES, R * tq), BF16),
                        pltpu.VMEM((tq // LANES, ncb + 16, LANES), F32),
                        pltpu.VMEM((nsb, tq), F32),
                        pltpu.VMEM((ncb, R * tq), BF16),
                        pltpu.VMEM((tk, R * tq), F32),
                        pltpu.VMEM((tk, R * tq), F32),
                        pltpu.VMEM((tk, R * tq), BF16),
                        pltpu.VMEM((tk, R * tq), BF16),
                        pltpu.VMEM((1, R * tq), F32),
                        pltpu.VMEM((V_EXT, R * tq), F32),
                        pltpu.VMEM((tk, R * tq), F32),
                        pltpu.VMEM((tk, R * tq), F32),
                        pltpu.VMEM((tk, R * tq), BF16),
                        pltpu.VMEM((tk, R * tq), BF16),
                        pltpu.VMEM((1, R * tq), F32),
                        pltpu.VMEM((V_EXT, R * tq), F32),
                        pltpu.VMEM((d, R * tq), F32)],
        compiler_params=_cparams(("parallel", "parallel", "arbitrary")),
        name="nsa",
    )(nsqT, nsgT, kcmp, vcmpT, nsks, nsvT, nsk, nsvT)


def _outproj_kernel(x_ref, yml_ref, ydaT_ref, ynsT_ref, wo_ref, o_ref):
    acc = x_ref[0] + jnp.dot(yml_ref[0].astype(BF16), wo_ref[0:ML_WIDTH, :], preferred_element_type=F32)
    acc = acc + lax.dot_general(ydaT_ref[0], wo_ref[ML_WIDTH:ML_WIDTH + DA_WIDTH, :], _TN,
                                preferred_element_type=F32)
    acc = acc + lax.dot_general(ynsT_ref[0], wo_ref[ML_WIDTH + DA_WIDTH:, :], _TN,
                                preferred_element_type=F32)
    o_ref[0] = acc


def _outproj(x, yml, ydaT, ynsT, wo, tm=512):
    B, S, D = x.shape
    return pl.pallas_call(
        _outproj_kernel,
        grid=(B, S // tm),
        in_specs=[pl.BlockSpec((1, tm, D), lambda b, i: (b, i, 0)),
                  pl.BlockSpec((1, tm, ML_WIDTH), lambda b, i: (b, i, 0)),
                  pl.BlockSpec((1, DA_WIDTH, tm), lambda b, i: (b, 0, i)),
                  pl.BlockSpec((1, NSA_WIDTH, tm), lambda b, i: (b, 0, i)),
                  pl.BlockSpec((D, D), lambda b, i: (0, 0))],
        out_specs=pl.BlockSpec((1, tm, D), lambda b, i: (b, i, 0)),
        out_shape=jax.ShapeDtypeStruct((B, S, D), F32),
        compiler_params=_cparams(("parallel", "parallel")),
        name="outproj",
    )(x, yml, ydaT, ynsT, wo)


def _ffn_kernel(x_ref, g_ref, w1_ref, w2_ref, gf_ref, o_ref, hb_ref, acc_ref, *, final):
    j = pl.program_id(1)

    @pl.when(j == 0)
    def _():
        x = x_ref[...]
        hb_ref[...] = _rms(x, g_ref[...]).astype(BF16)
        acc_ref[...] = x

    u = jnp.dot(hb_ref[...], w1_ref[...], preferred_element_type=F32)
    a = jnp.square(jnp.maximum(u, 0.0)).astype(BF16)
    acc_ref[...] += jnp.dot(a, w2_ref[...], preferred_element_type=F32)

    @pl.when(j == pl.num_programs(1) - 1)
    def _():
        y = acc_ref[...]
        if final:
            y = _rms(y, gf_ref[...])
        o_ref[...] = y


def _ffn(x2d, g, w1, w2, gf, final, tm=1024, tf=1024):
    N, D = x2d.shape
    return pl.pallas_call(
        functools.partial(_ffn_kernel, final=final),
        grid=(N // tm, D_FF // tf),
        in_specs=[pl.BlockSpec((tm, D), lambda i, j: (i, 0)),
                  pl.BlockSpec((1, D), lambda i, j: (0, 0)),
                  pl.BlockSpec((D, tf), lambda i, j: (0, j)),
                  pl.BlockSpec((tf, D), lambda i, j: (j, 0)),
                  pl.BlockSpec((1, D), lambda i, j: (0, 0))],
        out_specs=pl.BlockSpec((tm, D), lambda i, j: (i, 0)),
        out_shape=jax.ShapeDtypeStruct((N, D), F32),
        scratch_shapes=[pltpu.VMEM((tm, D), BF16), pltpu.VMEM((tm, D), F32)],
        compiler_params=_cparams(("parallel", "arbitrary")),
        name="ffn",
    )(x2d, g, w1, w2, gf)


def _split_w_in(w_in_l):
    edges = np.concatenate([[0], np.cumsum(IN_SIZES)])
    return [w_in_l[:, int(edges[i]):int(edges[i + 1])] for i in range(len(IN_SIZES))]


def _pad_cols(w, n):
    return jnp.pad(w, ((0, 0), (0, n - w.shape[1])))


def _inproj_weights(w_in_l):
    (ml_q, ml_k, ml_v, ml_o, ml_i, ml_f, da_q, da_k, da_v,
     ns_q, ns_kc, ns_vc, ns_ks, ns_vs, ns_kw, ns_vw, ns_g) = _split_w_in(w_in_l)
    wt = jnp.concatenate([ml_q, ml_k, ml_v, ml_o, _pad_cols(jnp.concatenate([ml_i, ml_f], 1), LANES),
                          da_k, ns_ks, ns_kw, ns_kc, ns_vc], axis=1)
    per_group = NSA_REP * 3
    ns_g_pad = jnp.concatenate(
        [_pad_cols(ns_g[:, gi * per_group:(gi + 1) * per_group], GATE_PAD) for gi in range(NSA_GROUPS)], axis=1)
    wf = jnp.concatenate([da_q, da_v, ns_q, ns_vs, ns_vw, ns_g_pad], axis=1)
    return wt.astype(BF16), wf.T.astype(BF16)


def _half_blocks(nsc):
    B, S, _ = nsc.shape
    n = S // CMP_STRIDE
    t = nsc.reshape(B, n, CMP_STRIDE, 2 * NSA_GROUPS, NSA_DIM)
    return t.transpose(0, 3, 1, 2, 4).reshape(B, 2 * NSA_GROUPS, n, CMP_STRIDE * NSA_DIM)


def kernel(x, norm1, w_in, ml_conv, ml_gate_bias, ml_norm, da_lambda, da_norm, nsa_pe,
           nsa_w1, nsa_w2, w_out, norm2, w_ff1, w_ff2, final_norm):
    B, S, D = x.shape
    depth = norm1.shape[0]
    for l in range(depth):
        wt, wfT = _inproj_weights(w_in[l])
        (zml, zg, dak, nsk, nsks, nsc, daqT, davT, nsqT, nsvT, nsgT) = _inproj(x, norm1[l][None, :], wt, wfT)

        yml = _mlstm(zml, zg, ml_conv[l], _pad_cols(ml_gate_bias[l][None, :], LANES), ml_norm[l][None, :])

        lam_init = 0.8 - 0.6 * math.exp(-0.3 * l)
        ydaT = _diff_attn(da_lambda[l], daqT, dak, davT, da_norm[l][:, None], lam_init)

        x16 = _half_blocks(nsc)
        pe2 = nsa_pe[l].reshape(2, 2, CMP_STRIDE * NSA_DIM)
        w1b = nsa_w1[l].astype(BF16)
        kcmp = _compress(x16, pe2, w1b, nsa_w2[l], 0, False)
        vcmpT = _compress(x16, pe2, w1b, nsa_w2[l], 1, True)
        ynsT = _nsa(nsqT, nsgT, kcmp, vcmpT, nsks, nsk, nsvT)

        x = _outproj(x, yml, ydaT, ynsT, w_out[l].astype(BF16))
        x = _ffn(x.reshape(B * S, D), norm2[l][None, :], w_ff1[l].astype(BF16), w_ff2[l].astype(BF16),
                 final_norm[None, :], final=(l == depth - 1)).reshape(B, S, D)
    return x
```

```python
import functools
import math

import numpy as np
import jax
import jax.numpy as jnp
from jax import lax
from jax.experimental import pallas as pl
from jax.experimental.pallas import tpu as pltpu

F32 = jnp.float32
BF16 = jnp.bfloat16

D_MODEL = 1024
ML_HEADS = 4
ML_DIM = 64
ML_WIDTH = ML_HEADS * ML_DIM
ML_TILE_CHUNK = 128
CONV_W = 4
DA_HEADS = 4
DA_QK_DIM = 32
DA_V_DIM = 64
DA_WIDTH = DA_HEADS * DA_V_DIM
NSA_HEADS = 8
NSA_GROUPS = 2
NSA_REP = NSA_HEADS // NSA_GROUPS
NSA_DIM = 64
NSA_WIDTH = NSA_HEADS * NSA_DIM
NSA_KV = NSA_GROUPS * NSA_DIM
CMP_BLOCK = 32
CMP_STRIDE = 16
CMP_HIDDEN = 4 * NSA_DIM
SEL_BLOCK = 64
SEL_TOPK = 16
WINDOW = 512
D_FF = 4 * D_MODEL
EPS = 1e-6
FORCE_SCORE = 1e4
IN_SIZES = (ML_WIDTH, ML_WIDTH, ML_WIDTH, ML_WIDTH, ML_HEADS, ML_HEADS,
            2 * DA_HEADS * DA_QK_DIM, 2 * DA_HEADS * DA_QK_DIM, DA_WIDTH,
            NSA_WIDTH, NSA_KV, NSA_KV, NSA_KV, NSA_KV, NSA_KV, NSA_KV, 3 * NSA_HEADS)

LANES = 128
GATE_PAD = 16
NEG = -1e30
M_INIT = -1e29
LOOP_PAIRS = 4
TAKEN = -3e38
LOG2E = 1.4426950408889634
V_DIM = 64
BF16_ROWS = 16
NSA_TK = 256
V_EXT = 80
VMEM_LIMIT = 56 * 1024 * 1024

_T_ML = (0, 1024)
_T_MLG = (1024, 1152)
_T_DAK = (1152, 1408)
_T_NSK = (1408, 1664)
_T_NSC = (1664, 1920)
_T_COLS = 1920
_F_DAQ = (0, 256)
_F_DAV = (256, 512)
_F_NSQ = (512, 1024)
_F_NSV = (1024, 1280)
_F_NSG = (1280, 1280 + NSA_GROUPS * GATE_PAD)
_F_ROWS = _F_NSG[1]

_NT = (((1,), (1,)), ((), ()))
_TN = (((0,), (0,)), ((), ()))


def _cparams(sem):
    return pltpu.CompilerParams(dimension_semantics=sem, vmem_limit_bytes=VMEM_LIMIT)


def _rms(x, g):
    return x * lax.rsqrt(jnp.mean(x * x, axis=-1, keepdims=True) + EPS) * g


def _inproj_kernel(x_ref, g_ref, wt_ref, wf_ref,
                   zml_ref, zg_ref, dak_ref, nsk_ref, nsks_ref, nsc_ref,
                   daqT_ref, davT_ref, nsqT_ref, nsvT_ref, nsgT_ref):
    hb = _rms(x_ref[0], g_ref[...]).astype(BF16)

    def tdot(span):
        return jnp.dot(hb, wt_ref[:, span[0]:span[1]], preferred_element_type=F32)

    def fdot(span):
        return lax.dot_general(wf_ref[span[0]:span[1], :], hb, _NT, preferred_element_type=F32)

    zml_ref[0] = tdot(_T_ML)
    zg_ref[0] = tdot(_T_MLG)
    dak_ref[0] = tdot(_T_DAK).astype(BF16)
    nsk = tdot(_T_NSK).astype(BF16)
    nsk_ref[0] = nsk
    tm_ = nsk.shape[0]
    row = lax.broadcasted_iota(jnp.int32, (tm_, NSA_DIM), 0)
    lane = lax.broadcasted_iota(jnp.int32, (tm_, NSA_DIM), 1)
    blk_in_tile = lax.shift_right_logical(row & (NSA_TK - 1), int(math.log2(SEL_BLOCK)))
    onehot = jnp.where(lane == blk_in_tile, 1.0, 0.0).astype(BF16)
    for gi in range(NSA_GROUPS):
        nsks_ref[0, gi] = jnp.concatenate([nsk[:, gi * NSA_DIM:(gi + 1) * NSA_DIM], onehot], axis=1)
    nsc_ref[0] = tdot(_T_NSC)
    daqT_ref[0] = (fdot(_F_DAQ) * (DA_QK_DIM ** -0.5 * LOG2E)).astype(BF16)
    nsqT_ref[0] = (fdot(_F_NSQ) * (NSA_DIM ** -0.5 * LOG2E)).astype(BF16)
    nsgT_ref[0] = fdot(_F_NSG)
    tm = hb.shape[0]
    for v_ref, span in ((davT_ref, _F_DAV), (nsvT_ref, _F_NSV)):
        v = fdot(span).astype(BF16)
        for h in range((span[1] - span[0]) // V_DIM):
            v_ref[0, h * V_EXT:h * V_EXT + V_DIM, :] = v[h * V_DIM:(h + 1) * V_DIM, :]
            v_ref[0, h * V_EXT + V_DIM:(h + 1) * V_EXT, :] = jnp.ones((V_EXT - V_DIM, tm), BF16)


def _inproj(x, g, wt, wfT, tm=512):
    B, S, D = x.shape
    tok = lambda w, dt: jax.ShapeDtypeStruct((B, S, w), dt)
    feat = lambda r, dt: jax.ShapeDtypeStruct((B, r, S), dt)
    tspec = lambda w: pl.BlockSpec((1, tm, w), lambda b, i: (b, i, 0))
    fspec = lambda r: pl.BlockSpec((1, r, tm), lambda b, i: (b, 0, i))
    return pl.pallas_call(
        _inproj_kernel,
        grid=(B, S // tm),
        in_specs=[tspec(D),
                  pl.BlockSpec((1, D), lambda b, i: (0, 0)),
                  pl.BlockSpec((D, _T_COLS), lambda b, i: (0, 0)),
                  pl.BlockSpec((_F_ROWS, D), lambda b, i: (0, 0))],
        out_specs=[tspec(1024), tspec(128), tspec(256), tspec(256),
                   pl.BlockSpec((1, NSA_GROUPS, tm, LANES), lambda b, i: (b, 0, i, 0)), tspec(256),
                   fspec(256), fspec(4 * V_EXT), fspec(512), fspec(4 * V_EXT), fspec(NSA_GROUPS * GATE_PAD)],
        out_shape=[tok(1024, F32), tok(128, F32), tok(256, BF16), tok(256, BF16),
                   jax.ShapeDtypeStruct((B, NSA_GROUPS, S, LANES), BF16), tok(256, F32),
                   feat(256, BF16), feat(4 * V_EXT, BF16), feat(512, BF16), feat(4 * V_EXT, BF16),
                   feat(NSA_GROUPS * GATE_PAD, F32)],
        compiler_params=_cparams(("parallel", "parallel")),
        name="inproj",
    )(x, g, wt, wfT)


def _log_sigmoid(x):
    return jnp.minimum(x, 0.0) - jnp.log1p(jnp.exp(-jnp.abs(x)))


def _sigmoid(x):
    return 1.0 / (1.0 + jnp.exp(-x))


def _mlstm_kernel(zml_ref, zg_ref, cw_ref, gb_ref, nrm_ref, y_ref,
                  pad_ref, c_ref, n_ref, m_ref, *, T, L):
    d = ML_DIM
    W = max(L, d)
    t = pl.program_id(1)

    @pl.when(t == 0)
    def _():
        pad_ref[0:8, :] = jnp.zeros((8, 2 * ML_WIDTH), F32)
        c_ref[...] = jnp.zeros_like(c_ref)
        n_ref[...] = jnp.zeros_like(n_ref)
        m_ref[...] = jnp.zeros_like(m_ref)

    @pl.when(t > 0)
    def _():
        pad_ref[0:8, :] = pad_ref[T:T + 8, :]

    pad_ref[8:8 + T, :] = zml_ref[0, :, 0:2 * ML_WIDTH]
    conv = cw_ref[0:1, :] * pad_ref[5:5 + T, :]
    for j in range(1, CONV_W):
        conv = conv + cw_ref[j:j + 1, :] * pad_ref[5 + j:5 + j + T, :]
    qk = conv * _sigmoid(conv)
    gates = zg_ref[0] + gb_ref[...]
    logf = _log_sigmoid(gates)

    row = lax.broadcasted_iota(jnp.int32, (L, L), 0)
    col = lax.broadcasted_iota(jnp.int32, (L, L), 1)
    causal = col <= row
    tril = causal.astype(F32)
    triu = (row <= col).astype(F32)

    lane_of = lax.broadcasted_iota(jnp.int32, (LANES, W), 0)
    pick = [(lane_of == g).astype(F32) for g in range(2 * ML_HEADS)]

    H = range(ML_HEADS)

    def local_part(c):
        r0 = c * L
        g_c = gates[r0:r0 + L, :]
        lf_c = logf[r0:r0 + L, :]
        b_cols = jnp.dot(tril, lf_c, preferred_element_type=F32,
                         precision=lax.Precision.HIGHEST)
        b_rows = jnp.dot(lf_c.T, triu, preferred_element_type=F32,
                         precision=lax.Precision.HIGHEST)
        g_rows = g_c.T
        b_col = [jnp.dot(b_cols, pick[ML_HEADS + h], preferred_element_type=F32,
                         precision=lax.Precision.HIGHEST) for h in H]
        ig_col = [jnp.dot(g_c, pick[h], preferred_element_type=F32,
                          precision=lax.Precision.HIGHEST) for h in H]
        b_row = [b_rows[ML_HEADS + h:ML_HEADS + h + 1, :] for h in H]
        ig_row = [g_rows[h:h + 1, :] for h in H]
        qh = [qk[r0:r0 + L, h * d:(h + 1) * d] for h in H]
        kh = [qk[r0:r0 + L, ML_WIDTH + h * d:ML_WIDTH + (h + 1) * d] * (d ** -0.5) for h in H]
        vh = [zml_ref[0, r0:r0 + L, 2 * ML_WIDTH + h * d:2 * ML_WIDTH + (h + 1) * d] for h in H]
        qkt = [lax.dot_general(qh[h], kh[h], _NT, preferred_element_type=F32) for h in H]
        g_tot = [b_col[h][L - 1:L, :] for h in H]
        a_max = [jnp.max(g_tot[h][:, 0:L] - b_row[h] + ig_row[h], axis=1, keepdims=True) for h in H]
        w_col = [jnp.exp((g_tot[h] - b_col[h] + ig_col[h] - a_max[h])[:, 0:d]) for h in H]
        c_loc = [lax.dot_general(vh[h] * w_col[h], kh[h], _TN, preferred_element_type=F32) for h in H]
        n_loc = [jnp.sum(kh[h] * w_col[h], axis=0, keepdims=True) for h in H]
        dmat = [jnp.where(causal, b_col[h][:, 0:L] - b_row[h] + ig_row[h], NEG) for h in H]
        d_max = [jnp.max(dmat[h], axis=1, keepdims=True) for h in H]
        return dict(b_col=b_col, qh=qh, vh=vh, qkt=qkt, g_tot=g_tot, a_max=a_max, c_loc=c_loc, n_loc=n_loc,
                    dmat=dmat, d_max=d_max)

    def carried_part(c, lp):
        r0 = c * L
        b_col, qh, vh, qkt, g_tot, a_max = lp["b_col"], lp["qh"], lp["vh"], lp["qkt"], lp["g_tot"], lp["a_max"]
        oh = [zml_ref[0, r0:r0 + L, 3 * ML_WIDTH + h * d:3 * ML_WIDTH + (h + 1) * d] for h in H]
        c_prev = [c_ref[h] for h in H]
        n_prev = [n_ref[h, 0:1, :] for h in H]
        m_prev = [m_ref[h, 0:1, :] for h in H]
        q_c = [lax.dot_general(qh[h], c_prev[h], _NT, preferred_element_type=F32) for h in H]
        inter_log = [b_col[h] + m_prev[h] for h in H]
        m_t = [jnp.maximum(inter_log[h], lp["d_max"][h]) for h in H]
        wts = [jnp.exp(lp["dmat"][h] - m_t[h][:, 0:L]) * qkt[h] for h in H]
        s_inter = [jnp.exp((inter_log[h] - m_t[h])[:, 0:d]) for h in H]
        num = [jnp.dot(wts[h], vh[h], preferred_element_type=F32) + s_inter[h] * q_c[h] for h in H]
        den = [jnp.sum(wts[h], axis=1, keepdims=True)
               + s_inter[h] * jnp.sum(qh[h] * n_prev[h], axis=1, keepdims=True) for h in H]
        hh = [num[h] / jnp.maximum(jnp.abs(den[h]), jnp.exp(-m_t[h][:, 0:d])) for h in H]
        m_new = [jnp.maximum(g_tot[h] + m_prev[h], a_max[h]) for h in H]
        for h in H:
            s_prev = jnp.exp((g_tot[h] + m_prev[h] - m_new[h])[:, 0:d])
            s_loc = jnp.exp((a_max[h] - m_new[h])[:, 0:d])
            c_ref[h] = s_prev * c_prev[h] + s_loc * lp["c_loc"][h]
            n_ref[h] = jnp.broadcast_to(s_prev * n_prev[h] + s_loc * lp["n_loc"][h], (8, d))
            m_ref[h] = jnp.broadcast_to(m_new[h], (8, W))
        for h in H:
            yh = _sigmoid(oh[h]) * hh[h]
            y_ref[0, r0:r0 + L, h * d:(h + 1) * d] = _rms(yh, nrm_ref[0:1, h * d:(h + 1) * d])

    for c in range(T // L):
        carried_part(c, local_part(c))


def _mlstm(zml, zg, conv_w, gate_bias, norm_g, T=256, chunk=ML_TILE_CHUNK):
    B, S, _ = zml.shape
    return pl.pallas_call(
        functools.partial(_mlstm_kernel, T=T, L=chunk),
        grid=(B, S // T),
        in_specs=[pl.BlockSpec((1, T, 1024), lambda b, t: (b, t, 0)),
                  pl.BlockSpec((1, T, 128), lambda b, t: (b, t, 0)),
                  pl.BlockSpec((CONV_W, 2 * ML_WIDTH), lambda b, t: (0, 0)),
                  pl.BlockSpec((1, 128), lambda b, t: (0, 0)),
                  pl.BlockSpec((1, ML_WIDTH), lambda b, t: (0, 0))],
        out_specs=pl.BlockSpec((1, T, ML_WIDTH), lambda b, t: (b, t, 0)),
        out_shape=jax.ShapeDtypeStruct((B, S, ML_WIDTH), F32),
        scratch_shapes=[pltpu.VMEM((T + 8, 2 * ML_WIDTH), F32),
                        pltpu.VMEM((ML_HEADS, ML_DIM, ML_DIM), F32),
                        pltpu.VMEM((ML_HEADS, 8, ML_DIM), F32),
                        pltpu.VMEM((ML_HEADS, 8, max(chunk, ML_DIM)), F32)],
        compiler_params=_cparams(("parallel", "arbitrary")),
        name="mlstm",
    )(zml, zg, conv_w, gate_bias, norm_g)


def _softmax_cols(s, m_ref, p_ref, cols):
    m_old = m_ref[:, cols]
    m_new = jnp.maximum(m_old, jnp.max(s, axis=0, keepdims=True))
    m_ref[:, cols] = m_new
    p_ref[:, cols] = jnp.exp2(s - m_new).astype(BF16)
    return jnp.exp2(m_old - m_new)


def _pipelined_attention(scores, vT_tile, s_refs, p_refs, m_ref, acc_ref, softmax,
                         first, n_plain_pairs, n_tail, max_tile, col_groups):
    sa, sb = s_refs
    pa, pb = p_refs

    def load(j):
        return jnp.clip(j, 0, max_tile)

    def half(j, s_cur, s_nxt, p_cur, p_prev, tail):
        s_nxt[...] = scores(load(j + 1))
        pv = jnp.dot(vT_tile(load(j - 1)), p_prev[...], preferred_element_type=F32)
        alphas = softmax(s_cur, p_cur, j, tail)
        for cols, alpha in zip(col_groups, alphas):
            acc_ref[:, cols] = alpha * (acc_ref[:, cols] + pv[:, cols])

    def pair(j, tail):
        half(j, sa, sb, pa, pb, tail)
        half(j + 1, sb, sa, pb, pa, tail)

    m_ref[...] = jnp.full_like(m_ref, M_INIT)
    acc_ref[...] = jnp.zeros_like(acc_ref)
    pb[...] = jnp.zeros_like(pb)
    sa[...] = scores(load(first))

    if not (isinstance(n_plain_pairs, int) and n_plain_pairs == 0):
        n_trips = n_plain_pairs // LOOP_PAIRS

        def body(i, carry):
            for u in range(LOOP_PAIRS):
                pair(first + 2 * (LOOP_PAIRS * i + u), False)
            return carry

        lax.fori_loop(0, n_trips, body, 0)
        rest = n_plain_pairs - n_trips * LOOP_PAIRS
        for u in range(LOOP_PAIRS - 1):
            @pl.when(u < rest)
            def _():
                pair(first + 2 * (LOOP_PAIRS * n_trips + u), False)
    j = first + 2 * n_plain_pairs
    bufs = ((sa, sb, pa, pb), (sb, sa, pb, pa))
    for t in range(n_tail):
        half(j + t, *bufs[t % 2], True)
    p_last = bufs[(n_tail - 1) % 2][2]
    acc_ref[...] += jnp.dot(vT_tile(load(j + n_tail - 1)), p_last[...], preferred_element_type=F32)


def _da_kernel(lam_ref, qT_ref, k_ref, vT_ref, gain_ref, o_ref,
               qz_ref, sa_ref, sb_ref, pa_ref, pb_ref, m_ref, acc_ref, *, tq, tk, lam_init, S):
    h = pl.program_id(1)
    qi = pl.program_id(2)
    d = DA_QK_DIM
    jd = (qi * tq) // tk
    col_groups = [slice(mp * tq, (mp + 1) * tq) for mp in range(2)]

    qz_ref[...] = jnp.zeros_like(qz_ref)
    qz_ref[0:d, 0:tq] = qT_ref[0, 0:d, :]
    qz_ref[d:2 * d, tq:2 * tq] = qT_ref[0, d:2 * d, :]

    def scores(j):
        return jnp.dot(k_ref[0, 0, pl.ds(pl.multiple_of(j * tk, tk), tk), :], qz_ref[...],
                       preferred_element_type=F32)

    def vT_tile(j):
        return vT_ref[0, :, pl.ds(pl.multiple_of(j * tk, tk), tk)]

    def softmax(s_ref, p_ref, j, tail):
        alphas = []
        for cols in col_groups:
            s = s_ref[:, cols]
            if tail:
                kpos = j * tk + lax.broadcasted_iota(jnp.int32, (tk, tq), 0)
                qpos = qi * tq + lax.broadcasted_iota(jnp.int32, (tk, tq), 1)
                s = jnp.where(kpos <= qpos, s, NEG)
            alphas.append(_softmax_cols(s, m_ref, p_ref, cols))
        return alphas

    _pipelined_attention(scores, vT_tile, (sa_ref, sb_ref), (pa_ref, pb_ref), m_ref, acc_ref, softmax,
                         first=0, n_plain_pairs=jd // 2, n_tail=2, max_tile=S // tk - 1, col_groups=col_groups)

    lp = lam_ref[...]
    lam = (jnp.exp(jnp.sum(lp[0:1] * lp[1:2], axis=1, keepdims=True))
           - jnp.exp(jnp.sum(lp[2:3] * lp[3:4], axis=1, keepdims=True)) + lam_init)
    o1 = acc_ref[0:V_DIM, 0:tq] / acc_ref[V_DIM:V_DIM + 1, 0:tq]
    o2 = acc_ref[0:V_DIM, tq:2 * tq] / acc_ref[V_DIM:V_DIM + 1, tq:2 * tq]
    o = o1 - lam * o2
    y = o * lax.rsqrt(jnp.mean(o * o, axis=0, keepdims=True) + EPS) * gain_ref[...]
    o_ref[0] = (y * (1.0 - lam_init)).astype(o_ref.dtype)


def _diff_attn(da_lambda, daqT, dak, davT, gain_col, lam_init, tq=256, tk=256):
    B, _, S = daqT.shape
    return pl.pallas_call(
        functools.partial(_da_kernel, tq=tq, tk=tk, lam_init=lam_init, S=S),
        grid=(B, DA_HEADS, S // tq),
        in_specs=[pl.BlockSpec((4, DA_QK_DIM), lambda b, h, i: (0, 0)),
                  pl.BlockSpec((1, DA_V_DIM, tq), lambda b, h, i: (b, h, i)),
                  pl.BlockSpec((1, 1, S, 2 * DA_QK_DIM), lambda b, h, i: (b, h, 0, 0)),
                  pl.BlockSpec((1, V_EXT, S), lambda b, h, i: (b, h, 0)),
                  pl.BlockSpec((DA_V_DIM, 1), lambda b, h, i: (h, 0))],
        out_specs=pl.BlockSpec((1, DA_V_DIM, tq), lambda b, h, i: (b, h, i)),
        out_shape=jax.ShapeDtypeStruct((B, DA_WIDTH, S), BF16),
        scratch_shapes=[pltpu.VMEM((2 * DA_QK_DIM, 2 * tq), BF16),
                        pltpu.VMEM((tk, 2 * tq), F32),
                        pltpu.VMEM((tk, 2 * tq), F32),
                        pltpu.VMEM((tk, 2 * tq), BF16),
                        pltpu.VMEM((tk, 2 * tq), BF16),
                        pltpu.VMEM((1, 2 * tq), F32),
                        pltpu.VMEM((V_EXT, 2 * tq), F32)],
        compiler_params=_cparams(("parallel", "parallel", "arbitrary")),
        name="diff_attn",
    )(da_lambda, daqT, dak, davT, gain_col)


def _gelu_tanh(x):
    return x * (0.5 * (1.0 + jnp.tanh(math.sqrt(2.0 / math.pi) * (x + 0.044715 * (x * x * x)))))


def _compress_kernel(x_ref, pe_ref, w1_ref, w2_ref, o_ref, b_ref, *, feature_major):
    n = x_ref.shape[2]
    half = CMP_STRIDE * NSA_DIM
    x = x_ref[0, 0]
    a = jnp.dot((x + pe_ref[0, 0:1, :]).astype(BF16), w1_ref[0, 0:half, :], preferred_element_type=F32)
    b_ref[0:n, :] = jnp.dot((x + pe_ref[0, 1:2, :]).astype(BF16), w1_ref[0, half:2 * half, :],
                            preferred_element_type=F32)
    b_ref[n:n + 8, :] = jnp.zeros((8, CMP_HIDDEN), F32)
    hid = _gelu_tanh(a + b_ref[1:n + 1, :]).astype(BF16)
    if feature_major:
        o_ref[0, 0] = lax.dot_general(w2_ref[0], hid, _NT, preferred_element_type=F32).astype(o_ref.dtype)
    else:
        o_ref[0, 0] = jnp.dot(hid, w2_ref[0], preferred_element_type=F32).astype(o_ref.dtype)


def _compress(x16, pe2, w1, w2, which, feature_major):
    B, _, n, half = x16.shape
    G = NSA_GROUPS
    if feature_major:
        out_shape, out_block = (B, G, NSA_DIM, n), (1, 1, NSA_DIM, n)
        w2_arr, w2_block = jnp.swapaxes(w2, 1, 2), (1, NSA_DIM, CMP_HIDDEN)
    else:
        out_shape, out_block = (B, G, n, NSA_DIM), (1, 1, n, NSA_DIM)
        w2_arr, w2_block = w2, (1, CMP_HIDDEN, NSA_DIM)
    return pl.pallas_call(
        functools.partial(_compress_kernel, feature_major=feature_major),
        grid=(B, G),
        in_specs=[pl.BlockSpec((1, 1, n, half), lambda b, g: (b, which * G + g, 0, 0)),
                  pl.BlockSpec((1, 2, half), lambda b, g: (which, 0, 0)),
                  pl.BlockSpec((1, 2 * half, CMP_HIDDEN), lambda b, g: (which, 0, 0)),
                  pl.BlockSpec(w2_block, lambda b, g: (which, 0, 0))],
        out_specs=pl.BlockSpec(out_block, lambda b, g: (b, g, 0, 0)),
        out_shape=jax.ShapeDtypeStruct(out_shape, BF16),
        scratch_shapes=[pltpu.VMEM((n + 8, CMP_HIDDEN), F32)],
        compiler_params=_cparams(("parallel", "parallel")),
        name="compress_v" if feature_major else "compress_k",
    )(x16, pe2, w1, w2_arr.astype(BF16))


def _nsa_kernel(qT_ref, gT_ref, kc_ref, vcT_ref, ks_ref, vsT_ref, kw_ref, vwT_ref, o_ref,
                qg_ref, qz_ref, qs_ref, imp_ref, sel_ref, pc_ref, sa_ref, sb_ref, pa_ref, pb_ref, m_ref, acc_ref,
                wsa_ref, wsb_ref, wpa_ref, wpb_ref, wm_ref, wacc_ref, out_ref,
                *, tq, tk, S):
    g = pl.program_id(1)
    qi = pl.program_id(2)
    d = NSA_DIM
    R = NSA_REP
    ncb = S // CMP_STRIDE
    nsb = S // SEL_BLOCK
    q0 = qi * tq
    jd = q0 // tk
    ratio = SEL_BLOCK // CMP_STRIDE
    per_tile = tk // SEL_BLOCK

    for r in range(R):
        qg_ref[:, r * tq:(r + 1) * tq] = qT_ref[0, r * d:(r + 1) * d, :]
    qz_ref[...] = jnp.zeros_like(qz_ref)
    qs_ref[...] = jnp.zeros_like(qs_ref)
    qs_ref[0:d, :] = qg_ref[...]
    for gg in range(NSA_GROUPS):
        @pl.when(g == gg)
        def _():
            qz_ref[gg * d:(gg + 1) * d, :] = qg_ref[...]

    def gate(r, br):
        return _sigmoid(gT_ref[0, r * 3 + br:r * 3 + br + 1, :])

    qpos = q0 + lax.broadcasted_iota(jnp.int32, (1, tq), 1)

    def compress_and_select(n_c):
        n_s = n_c // ratio
        s_all = jnp.dot(kc_ref[0, 0, 0:n_c, :], qg_ref[...], preferred_element_type=F32)
        cend = lax.broadcasted_iota(jnp.int32, (n_c, tq), 0) * CMP_STRIDE + (CMP_BLOCK - 1)
        cbias = jnp.where(cend <= qpos, 0.0, NEG)
        hs = range(R)
        sc = [s_all[:, r * tq:(r + 1) * tq] + cbias for r in hs]
        mx = [jnp.maximum(jnp.max(sc[r], axis=0, keepdims=True), M_INIT) for r in hs]
        pu = [jnp.exp2(sc[r] - mx[r]) for r in hs]
        den = [jnp.sum(pu[r], axis=0, keepdims=True) for r in hs]
        pn = [pu[r] / jnp.where(den[r] > 0, den[r], 1.0) for r in hs]
        imp = pn[0]
        for r in range(1, R):
            imp = imp + pn[r]
        for r in hs:
            pc_ref[0:n_c, r * tq:(r + 1) * tq] = pn[r].astype(BF16)
        o_cmp = jnp.dot(vcT_ref[0, 0, :, 0:n_c], pc_ref[0:n_c, :], preferred_element_type=F32)
        for r in range(R):
            cols = slice(r * tq, (r + 1) * tq)
            out_ref[:, cols] = gate(r, 0) * o_cmp[:, cols]

        slabs = []
        for c in range(tq // LANES):
            imp_ref[c, 0:8, :] = jnp.zeros((8, LANES), F32)
            imp_ref[c, 8:8 + n_c, :] = imp[:, c * LANES:(c + 1) * LANES]
            imp_ref[c, 8 + n_c:16 + n_c, :] = jnp.zeros((8, LANES), F32)
            slab = jnp.zeros((n_s, LANES), F32)
            for o in range(-1, ratio):
                slab = slab + imp_ref[c, pl.ds(8 + o, n_s, stride=ratio), :]
            slabs.append(slab)
        p_slc = slabs[0] if len(slabs) == 1 else jnp.concatenate(slabs, axis=1)
        blk = lax.broadcasted_iota(jnp.int32, (n_s, tq), 0)
        cur = lax.shift_right_logical(qpos, int(math.log2(SEL_BLOCK)))
        forced = (blk == 0) | (blk == cur) | (blk == cur - 1)
        causal_blk = blk * SEL_BLOCK <= qpos
        score = jnp.where(forced, FORCE_SCORE, jnp.where(causal_blk, p_slc, -1.0))
        blk_f = blk.astype(F32)
        for _ in range(SEL_TOPK):
            mx_s = jnp.max(score, axis=0, keepdims=True)
            first = jnp.min(jnp.where(score == mx_s, blk_f, float(nsb)), axis=0, keepdims=True)
            score = jnp.where(blk_f == first, TAKEN, score)
        sel_ref[0:n_s, :] = jnp.where(score == TAKEN, 0.0, NEG)
        if n_s < nsb:
            sel_ref[n_s:nsb, :] = jnp.full((nsb - n_s, tq), NEG, F32)

    n_classes = 4
    visible_class = (q0 + tq - 1) // (S // n_classes)
    for k in range(n_classes):
        @pl.when(visible_class == k)
        def _():
            compress_and_select((k + 1) * ncb // n_classes)

    head_cols = [slice(r * tq, (r + 1) * tq) for r in range(R)]
    col_groups = [slice(0, R * tq)]
    max_tile = S // tk - 1

    def k_tile(k_ref_, j):
        return k_ref_[0, pl.ds(pl.multiple_of(j * tk, tk), tk), :]

    def vT_tile_of(vT_ref_):
        def vT_tile(j):
            return vT_ref_[0, :, pl.ds(pl.multiple_of(j * tk, tk), tk)]
        return vT_tile

    def softmax_with(bias_of, m_ref_):
        def softmax(s_ref, p_ref, j, tail):
            bias = bias_of(j, tail)
            s = s_ref[...]
            if bias is not None:
                s = s + jnp.concatenate([bias] * R, axis=1)
            return [_softmax_cols(s, m_ref_, p_ref, col_groups[0])]
        return softmax

    def flush(br, acc_ref_):
        for r, cols in enumerate(head_cols):
            den = acc_ref_[V_DIM:V_DIM + 1, cols]
            out_ref[:, cols] += gate(r, br) * (acc_ref_[0:V_DIM, cols] / jnp.where(den > 0, den, 1.0))

    def kpos_of(j):
        return j * tk + lax.broadcasted_iota(jnp.int32, (tk, tq), 0)

    def win_bias(j, tail):
        kpos = kpos_of(j)
        rel = qpos - kpos
        return jnp.where((rel >= 0) & (rel < WINDOW) & (kpos >= 0), 0.0, NEG)

    def win_scores(j):
        return jnp.dot(k_tile(kw_ref, j), qz_ref[...], preferred_element_type=F32)

    n_win = WINDOW // tk + 1
    _pipelined_attention(win_scores, vT_tile_of(vwT_ref), (wsa_ref, wsb_ref), (wpa_ref, wpb_ref), wm_ref, wacc_ref,
                         softmax_with(win_bias, wm_ref), first=jd - (n_win - 1), n_plain_pairs=0, n_tail=n_win,
                         max_tile=max_tile, col_groups=col_groups)
    flush(2, wacc_ref)

    pad_rows = jnp.zeros((BF16_ROWS - per_tile, tq), F32)

    def sel_scores(j):
        rows = [sel_ref[pl.ds(j * per_tile + i, 1), :] for i in range(per_tile)]
        bias = jnp.concatenate(rows + [pad_rows], axis=0).astype(BF16)
        qs_ref[d:d + BF16_ROWS, :] = jnp.concatenate([bias] * R, axis=1)
        return jnp.dot(ks_ref[0, 0, pl.ds(pl.multiple_of(j * tk, tk), tk), :], qs_ref[...],
                       preferred_element_type=F32)

    def sel_bias(j, tail):
        return jnp.where(kpos_of(j) <= qpos, 0.0, NEG) if tail else None

    _pipelined_attention(sel_scores, vT_tile_of(vsT_ref), (sa_ref, sb_ref), (pa_ref, pb_ref), m_ref, acc_ref,
                         softmax_with(sel_bias, m_ref), first=0, n_plain_pairs=jd // 2, n_tail=2,
                         max_tile=max_tile, col_groups=col_groups)
    flush(1, acc_ref)

    for r in range(R):
        o_ref[0, r * d:(r + 1) * d, :] = out_ref[:, r * tq:(r + 1) * tq].astype(o_ref.dtype)


def _nsa(nsqT, nsgT, kcmp, vcmpT, nsks, nsk, nsvT, tq=128, tk=NSA_TK):
    B, _, S = nsqT.shape
    G, R, d = NSA_GROUPS, NSA_REP, NSA_DIM
    ncb = S // CMP_STRIDE
    nsb = S // SEL_BLOCK
    return pl.pallas_call(
        functools.partial(_nsa_kernel, tq=tq, tk=tk, S=S),
        grid=(B, G, S // tq),
        in_specs=[pl.BlockSpec((1, R * d, tq), lambda b, g, i: (b, g, i)),
                  pl.BlockSpec((1, GATE_PAD, tq), lambda b, g, i: (b, g, i)),
                  pl.BlockSpec((1, 1, ncb, d), lambda b, g, i: (b, g, 0, 0)),
                  pl.BlockSpec((1, 1, d, ncb), lambda b, g, i: (b, g, 0, 0)),
                  pl.BlockSpec((1, 1, S, LANES), lambda b, g, i: (b, g, 0, 0)),
                  pl.BlockSpec((1, V_EXT, S), lambda b, g, i: (b, g, 0)),
                  pl.BlockSpec((1, S, LANES), lambda b, g, i: (b, 0, 1)),
                  pl.BlockSpec((1, V_EXT, S), lambda b, g, i: (b, G + g, 0))],
        out_specs=pl.BlockSpec((1, R * d, tq), lambda b, g, i: (b, g, i)),
        out_shape=jax.ShapeDtypeStruct((B, NSA_WIDTH, S), BF16),
        scratch_shapes=[pltpu.VMEM((d, R * tq), BF16),
                        pltpu.VMEM((LANES, R * tq), BF16),
                        pltpu.VMEM((LANES, R * tq), BF16),
                        pltpu.VMEM((tq // LANES, ncb + 16, LANES), F32),
                        pltpu.VMEM((nsb, tq), F32),
                        pltpu.VMEM((ncb, R * tq), BF16),
                        pltpu.VMEM((tk, R * tq), F32),
                        pltpu.VMEM((tk, R * tq), F32),
                        pltpu.VMEM((tk, R * tq), BF16),
                        pltpu.VMEM((tk, R * tq), BF16),
                        pltpu.VMEM((1, R * tq), F32),
                        pltpu.VMEM((V_EXT, R * tq), F32),
                        pltpu.VMEM((tk, R * tq), F32),
                        pltpu.VMEM((tk, R * tq), F32),
                        pltpu.VMEM((tk, R * tq), BF16),
                        pltpu.VMEM((tk, R * tq), BF16),
                        pltpu.VMEM((1, R * tq), F32),
                        pltpu.VMEM((V_EXT, R * tq), F32),
                        pltpu.VMEM((d, R * tq), F32)],
        compiler_params=_cparams(("parallel", "parallel", "arbitrary")),
        name="nsa",
    )(nsqT, nsgT, kcmp, vcmpT, nsks, nsvT, nsk, nsvT)


def _mix_ffn_kernel(x_ref, yml_ref, ydaT_ref, ynsT_ref, wo_ref, g_ref, w1_ref, w2_ref, gf_ref, o_ref,
                    hb_ref, acc_ref, *, final):
    j = pl.program_id(2)

    @pl.when(j == 0)
    def _():
        x1 = x_ref[0] + jnp.dot(yml_ref[0].astype(BF16), wo_ref[0:ML_WIDTH, :], preferred_element_type=F32)
        x1 = x1 + lax.dot_general(ydaT_ref[0], wo_ref[ML_WIDTH:ML_WIDTH + DA_WIDTH, :], _TN,
                                  preferred_element_type=F32)
        x1 = x1 + lax.dot_general(ynsT_ref[0], wo_ref[ML_WIDTH + DA_WIDTH:, :], _TN,
                                  preferred_element_type=F32)
        hb_ref[...] = _rms(x1, g_ref[...]).astype(BF16)
        acc_ref[...] = x1

    u = jnp.dot(hb_ref[...], w1_ref[...], preferred_element_type=F32)
    a = jnp.square(jnp.maximum(u, 0.0)).astype(BF16)
    acc_ref[...] += jnp.dot(a, w2_ref[...], preferred_element_type=F32)

    @pl.when(j == pl.num_programs(2) - 1)
    def _():
        y = acc_ref[...]
        if final:
            y = _rms(y, gf_ref[...])
        o_ref[0] = y


def _mix_ffn(x, yml, ydaT, ynsT, wo, g, w1, w2, gf, final, tm=1024, tf=1024):
    B, S, D = x.shape
    return pl.pallas_call(
        functools.partial(_mix_ffn_kernel, final=final),
        grid=(B, S // tm, D_FF // tf),
        in_specs=[pl.BlockSpec((1, tm, D), lambda b, i, j: (b, i, 0)),
                  pl.BlockSpec((1, tm, ML_WIDTH), lambda b, i, j: (b, i, 0)),
                  pl.BlockSpec((1, DA_WIDTH, tm), lambda b, i, j: (b, 0, i)),
                  pl.BlockSpec((1, NSA_WIDTH, tm), lambda b, i, j: (b, 0, i)),
                  pl.BlockSpec((D, D), lambda b, i, j: (0, 0)),
                  pl.BlockSpec((1, D), lambda b, i, j: (0, 0)),
                  pl.BlockSpec((D, tf), lambda b, i, j: (0, j)),
                  pl.BlockSpec((tf, D), lambda b, i, j: (j, 0)),
                  pl.BlockSpec((1, D), lambda b, i, j: (0, 0))],
        out_specs=pl.BlockSpec((1, tm, D), lambda b, i, j: (b, i, 0)),
        out_shape=jax.ShapeDtypeStruct((B, S, D), F32),
        scratch_shapes=[pltpu.VMEM((tm, D), BF16), pltpu.VMEM((tm, D), F32)],
        compiler_params=_cparams(("parallel", "parallel", "arbitrary")),
        name="mix_ffn",
    )(x, yml, ydaT, ynsT, wo, g, w1, w2, gf)


def _split_w_in(w_in_l):
    edges = np.concatenate([[0], np.cumsum(IN_SIZES)])
    return [w_in_l[:, int(edges[i]):int(edges[i + 1])] for i in range(len(IN_SIZES))]


def _pad_cols(w, n):
    return jnp.pad(w, ((0, 0), (0, n - w.shape[1])))


def _inproj_weights(w_in_l):
    (ml_q, ml_k, ml_v, ml_o, ml_i, ml_f, da_q, da_k, da_v,
     ns_q, ns_kc, ns_vc, ns_ks, ns_vs, ns_kw, ns_vw, ns_g) = _split_w_in(w_in_l)
    wt = jnp.concatenate([ml_q, ml_k, ml_v, ml_o, _pad_cols(jnp.concatenate([ml_i, ml_f], 1), LANES),
                          da_k, ns_ks, ns_kw, ns_kc, ns_vc], axis=1)
    per_group = NSA_REP * 3
    ns_g_pad = jnp.concatenate(
        [_pad_cols(ns_g[:, gi * per_group:(gi + 1) * per_group], GATE_PAD) for gi in range(NSA_GROUPS)], axis=1)
    wf = jnp.concatenate([da_q, da_v, ns_q, ns_vs, ns_vw, ns_g_pad], axis=1)
    return wt.astype(BF16), wf.T.astype(BF16)


def _half_blocks(nsc):
    B, S, _ = nsc.shape
    n = S // CMP_STRIDE
    t = nsc.reshape(B, n, CMP_STRIDE, 2 * NSA_GROUPS, NSA_DIM)
    return t.transpose(0, 3, 1, 2, 4).reshape(B, 2 * NSA_GROUPS, n, CMP_STRIDE * NSA_DIM)


def kernel(x, norm1, w_in, ml_conv, ml_gate_bias, ml_norm, da_lambda, da_norm, nsa_pe,
           nsa_w1, nsa_w2, w_out, norm2, w_ff1, w_ff2, final_norm):
    B, S, D = x.shape
    depth = norm1.shape[0]
    for l in range(depth):
        wt, wfT = _inproj_weights(w_in[l])
        (zml, zg, dak, nsk, nsks, nsc, daqT, davT, nsqT, nsvT, nsgT) = _inproj(x, norm1[l][None, :], wt, wfT)

        yml = _mlstm(zml, zg, ml_conv[l], _pad_cols(ml_gate_bias[l][None, :], LANES), ml_norm[l][None, :])

        lam_init = 0.8 - 0.6 * math.exp(-0.3 * l)
        dak_heads = dak.reshape(B, S, DA_HEADS, 2 * DA_QK_DIM).transpose(0, 2, 1, 3)
        ydaT = _diff_attn(da_lambda[l], daqT, dak_heads, davT, da_norm[l][:, None], lam_init)

        x16 = _half_blocks(nsc)
        pe2 = nsa_pe[l].reshape(2, 2, CMP_STRIDE * NSA_DIM)
        w1b = nsa_w1[l].astype(BF16)
        kcmp = _compress(x16, pe2, w1b, nsa_w2[l], 0, False)
        vcmpT = _compress(x16, pe2, w1b, nsa_w2[l], 1, True)
        ynsT = _nsa(nsqT, nsgT, kcmp, vcmpT, nsks, nsk, nsvT)

        x = _mix_ffn(x, yml, ydaT, ynsT, w_out[l].astype(BF16), norm2[l][None, :], w_ff1[l].astype(BF16),
                     w_ff2[l].astype(BF16), final_norm[None, :], final=(l == depth - 1))
    return x
```

```python
import functools
import math

import numpy as np
import jax
import jax.numpy as jnp
from jax import lax
from jax.experimental import pallas as pl
from jax.experimental.pallas import tpu as pltpu

F32 = jnp.float32
BF16 = jnp.bfloat16

D_MODEL = 1024
ML_HEADS = 4
ML_DIM = 64
ML_WIDTH = ML_HEADS * ML_DIM
ML_TILE_CHUNK = 128
CONV_W = 4
DA_HEADS = 4
DA_QK_DIM = 32
DA_V_DIM = 64
DA_WIDTH = DA_HEADS * DA_V_DIM
NSA_HEADS = 8
NSA_GROUPS = 2
NSA_REP = NSA_HEADS // NSA_GROUPS
NSA_DIM = 64
NSA_WIDTH = NSA_HEADS * NSA_DIM
NSA_KV = NSA_GROUPS * NSA_DIM
CMP_BLOCK = 32
CMP_STRIDE = 16
CMP_HIDDEN = 4 * NSA_DIM
SEL_BLOCK = 64
SEL_TOPK = 16
WINDOW = 512
D_FF = 4 * D_MODEL
EPS = 1e-6
FORCE_SCORE = 1e4
IN_SIZES = (ML_WIDTH, ML_WIDTH, ML_WIDTH, ML_WIDTH, ML_HEADS, ML_HEADS,
            2 * DA_HEADS * DA_QK_DIM, 2 * DA_HEADS * DA_QK_DIM, DA_WIDTH,
            NSA_WIDTH, NSA_KV, NSA_KV, NSA_KV, NSA_KV, NSA_KV, NSA_KV, 3 * NSA_HEADS)

LANES = 128
GATE_PAD = 16
NEG = -1e30
M_INIT = -1e29
LOOP_PAIRS = 4
TAKEN = -3e38
LOG2E = 1.4426950408889634
V_DIM = 64
BF16_ROWS = 16
NSA_TK = 256
V_EXT = 80
VMEM_LIMIT = 56 * 1024 * 1024

_T_ML = (0, 1024)
_T_MLG = (1024, 1152)
_T_DAK = (1152, 1408)
_T_NSK = (1408, 1664)
_T_NSC = (1664, 1920)
_T_COLS = 1920
_F_DAQ = (0, 256)
_F_DAV = (256, 512)
_F_NSQ = (512, 1024)
_F_NSV = (1024, 1280)
_F_NSG = (1280, 1280 + NSA_GROUPS * GATE_PAD)
_F_ROWS = _F_NSG[1]

_NT = (((1,), (1,)), ((), ()))
_TN = (((0,), (0,)), ((), ()))


def _cparams(sem):
    return pltpu.CompilerParams(dimension_semantics=sem, vmem_limit_bytes=VMEM_LIMIT)


def _rms(x, g):
    return x * lax.rsqrt(jnp.mean(x * x, axis=-1, keepdims=True) + EPS) * g


def _inproj_kernel(x_ref, g_ref, wt_ref, wf_ref,
                   zml_ref, zg_ref, dak_ref, nsk_ref, nsks_ref, x16_ref,
                   daqT_ref, davT_ref, nsqT_ref, nsvT_ref, nsgT_ref, nsc_ref):
    hb = _rms(x_ref[0], g_ref[...]).astype(BF16)

    def tdot(span):
        return jnp.dot(hb, wt_ref[:, span[0]:span[1]], preferred_element_type=F32)

    def fdot(span):
        return lax.dot_general(wf_ref[span[0]:span[1], :], hb, _NT, preferred_element_type=F32)

    zml_ref[0] = tdot(_T_ML)
    zg_ref[0] = tdot(_T_MLG)
    dak = tdot(_T_DAK).astype(BF16)
    for h in range(DA_HEADS):
        dak_ref[0, h] = dak[:, h * 2 * DA_QK_DIM:(h + 1) * 2 * DA_QK_DIM]
    nsk = tdot(_T_NSK).astype(BF16)
    nsk_ref[0] = nsk
    tm_ = nsk.shape[0]
    row = lax.broadcasted_iota(jnp.int32, (tm_, NSA_DIM), 0)
    lane = lax.broadcasted_iota(jnp.int32, (tm_, NSA_DIM), 1)
    blk_in_tile = lax.shift_right_logical(row & (NSA_TK - 1), int(math.log2(SEL_BLOCK)))
    onehot = jnp.where(lane == blk_in_tile, 1.0, 0.0).astype(BF16)
    for gi in range(NSA_GROUPS):
        nsks_ref[0, gi] = jnp.concatenate([nsk[:, gi * NSA_DIM:(gi + 1) * NSA_DIM], onehot], axis=1)
    nsc = tdot(_T_NSC)
    for half in range(2):
        nsc_ref[half] = nsc[:, half * LANES:(half + 1) * LANES]
    for i in range(CMP_STRIDE):
        for half in range(2):
            rows = nsc_ref[half, pl.ds(i, tm_ // CMP_STRIDE, stride=CMP_STRIDE), :]
            for w in range(2):
                x16_ref[0, 2 * half + w, :, i * NSA_DIM:(i + 1) * NSA_DIM] = rows[:, w * NSA_DIM:(w + 1) * NSA_DIM]
    daqT_ref[0] = (fdot(_F_DAQ) * (DA_QK_DIM ** -0.5 * LOG2E)).astype(BF16)
    nsqT_ref[0] = (fdot(_F_NSQ) * (NSA_DIM ** -0.5 * LOG2E)).astype(BF16)
    nsgT_ref[0] = fdot(_F_NSG)
    tm = hb.shape[0]
    for v_ref, span in ((davT_ref, _F_DAV), (nsvT_ref, _F_NSV)):
        v = fdot(span).astype(BF16)
        for h in range((span[1] - span[0]) // V_DIM):
            v_ref[0, h * V_EXT:h * V_EXT + V_DIM, :] = v[h * V_DIM:(h + 1) * V_DIM, :]
            v_ref[0, h * V_EXT + V_DIM:(h + 1) * V_EXT, :] = jnp.ones((V_EXT - V_DIM, tm), BF16)


def _inproj(x, g, wt, wfT, tm=512):
    B, S, D = x.shape
    tok = lambda w, dt: jax.ShapeDtypeStruct((B, S, w), dt)
    feat = lambda r, dt: jax.ShapeDtypeStruct((B, r, S), dt)
    tspec = lambda w: pl.BlockSpec((1, tm, w), lambda b, i: (b, i, 0))
    fspec = lambda r: pl.BlockSpec((1, r, tm), lambda b, i: (b, 0, i))
    return pl.pallas_call(
        _inproj_kernel,
        grid=(B, S // tm),
        in_specs=[tspec(D),
                  pl.BlockSpec((1, D), lambda b, i: (0, 0)),
                  pl.BlockSpec((D, _T_COLS), lambda b, i: (0, 0)),
                  pl.BlockSpec((_F_ROWS, D), lambda b, i: (0, 0))],
        out_specs=[tspec(1024), tspec(128),
                   pl.BlockSpec((1, DA_HEADS, tm, 2 * DA_QK_DIM), lambda b, i: (b, 0, i, 0)), tspec(256),
                   pl.BlockSpec((1, NSA_GROUPS, tm, LANES), lambda b, i: (b, 0, i, 0)),
                   pl.BlockSpec((1, 2 * NSA_GROUPS, tm // CMP_STRIDE, CMP_STRIDE * NSA_DIM), lambda b, i: (b, 0, i, 0)),
                   fspec(256), fspec(4 * V_EXT), fspec(512), fspec(4 * V_EXT), fspec(NSA_GROUPS * GATE_PAD)],
        out_shape=[tok(1024, F32), tok(128, F32),
                   jax.ShapeDtypeStruct((B, DA_HEADS, S, 2 * DA_QK_DIM), BF16), tok(256, BF16),
                   jax.ShapeDtypeStruct((B, NSA_GROUPS, S, LANES), BF16),
                   jax.ShapeDtypeStruct((B, 2 * NSA_GROUPS, S // CMP_STRIDE, CMP_STRIDE * NSA_DIM), F32),
                   feat(256, BF16), feat(4 * V_EXT, BF16), feat(512, BF16), feat(4 * V_EXT, BF16),
                   feat(NSA_GROUPS * GATE_PAD, F32)],
        scratch_shapes=[pltpu.VMEM((2, tm, LANES), F32)],
        compiler_params=_cparams(("parallel", "parallel")),
        name="inproj",
    )(x, g, wt, wfT)


def _log_sigmoid(x):
    return jnp.minimum(x, 0.0) - jnp.log1p(jnp.exp(-jnp.abs(x)))


def _sigmoid(x):
    return 1.0 / (1.0 + jnp.exp(-x))


def _mlstm_kernel(zml_ref, zg_ref, cw_ref, gb_ref, nrm_ref, y_ref,
                  pad_ref, c_ref, n_ref, m_ref, *, T, L):
    d = ML_DIM
    W = max(L, d)
    t = pl.program_id(1)

    @pl.when(t == 0)
    def _():
        pad_ref[0:8, :] = jnp.zeros((8, 2 * ML_WIDTH), F32)
        c_ref[...] = jnp.zeros_like(c_ref)
        n_ref[...] = jnp.zeros_like(n_ref)
        m_ref[...] = jnp.zeros_like(m_ref)

    @pl.when(t > 0)
    def _():
        pad_ref[0:8, :] = pad_ref[T:T + 8, :]

    pad_ref[8:8 + T, :] = zml_ref[0, :, 0:2 * ML_WIDTH]
    conv = cw_ref[0:1, :] * pad_ref[5:5 + T, :]
    for j in range(1, CONV_W):
        conv = conv + cw_ref[j:j + 1, :] * pad_ref[5 + j:5 + j + T, :]
    qk = conv * _sigmoid(conv)
    gates = zg_ref[0] + gb_ref[...]
    logf = _log_sigmoid(gates)

    row = lax.broadcasted_iota(jnp.int32, (L, L), 0)
    col = lax.broadcasted_iota(jnp.int32, (L, L), 1)
    causal = col <= row
    tril = causal.astype(F32)
    triu = (row <= col).astype(F32)

    lane_of = lax.broadcasted_iota(jnp.int32, (LANES, W), 0)
    pick = [(lane_of == g).astype(F32) for g in range(2 * ML_HEADS)]

    H = range(ML_HEADS)

    def local_part(c):
        r0 = c * L
        g_c = gates[r0:r0 + L, :]
        lf_c = logf[r0:r0 + L, :]
        b_cols = jnp.dot(tril, lf_c, preferred_element_type=F32,
                         precision=lax.Precision.HIGHEST)
        b_rows = jnp.dot(lf_c.T, triu, preferred_element_type=F32,
                         precision=lax.Precision.HIGHEST)
        g_rows = g_c.T
        b_col = [jnp.dot(b_cols, pick[ML_HEADS + h], preferred_element_type=F32,
                         precision=lax.Precision.HIGHEST) for h in H]
        ig_col = [jnp.dot(g_c, pick[h], preferred_element_type=F32,
                          precision=lax.Precision.HIGHEST) for h in H]
        b_row = [b_rows[ML_HEADS + h:ML_HEADS + h + 1, :] for h in H]
        ig_row = [g_rows[h:h + 1, :] for h in H]
        qh = [qk[r0:r0 + L, h * d:(h + 1) * d] for h in H]
        kh = [qk[r0:r0 + L, ML_WIDTH + h * d:ML_WIDTH + (h + 1) * d] * (d ** -0.5) for h in H]
        vh = [zml_ref[0, r0:r0 + L, 2 * ML_WIDTH + h * d:2 * ML_WIDTH + (h + 1) * d] for h in H]
        qkt = [lax.dot_general(qh[h], kh[h], _NT, preferred_element_type=F32) for h in H]
        g_tot = [b_col[h][L - 1:L, :] for h in H]
        a_max = [jnp.max(g_tot[h][:, 0:L] - b_row[h] + ig_row[h], axis=1, keepdims=True) for h in H]
        w_col = [jnp.exp((g_tot[h] - b_col[h] + ig_col[h] - a_max[h])[:, 0:d]) for h in H]
        c_loc = [lax.dot_general(vh[h] * w_col[h], kh[h], _TN, preferred_element_type=F32) for h in H]
        n_loc = [jnp.sum(kh[h] * w_col[h], axis=0, keepdims=True) for h in H]
        dmat = [jnp.where(causal, b_col[h][:, 0:L] - b_row[h] + ig_row[h], NEG) for h in H]
        d_max = [jnp.max(dmat[h], axis=1, keepdims=True) for h in H]
        return dict(b_col=b_col, qh=qh, vh=vh, qkt=qkt, g_tot=g_tot, a_max=a_max, c_loc=c_loc, n_loc=n_loc,
                    dmat=dmat, d_max=d_max)

    def carried_part(c, lp):
        r0 = c * L
        b_col, qh, vh, qkt, g_tot, a_max = lp["b_col"], lp["qh"], lp["vh"], lp["qkt"], lp["g_tot"], lp["a_max"]
        oh = [zml_ref[0, r0:r0 + L, 3 * ML_WIDTH + h * d:3 * ML_WIDTH + (h + 1) * d] for h in H]
        c_prev = [c_ref[h] for h in H]
        n_prev = [n_ref[h, 0:1, :] for h in H]
        m_prev = [m_ref[h, 0:1, :] for h in H]
        q_c = [lax.dot_general(qh[h], c_prev[h], _NT, preferred_element_type=F32) for h in H]
        inter_log = [b_col[h] + m_prev[h] for h in H]
        m_t = [jnp.maximum(inter_log[h], lp["d_max"][h]) for h in H]
        wts = [jnp.exp(lp["dmat"][h] - m_t[h][:, 0:L]) * qkt[h] for h in H]
        s_inter = [jnp.exp((inter_log[h] - m_t[h])[:, 0:d]) for h in H]
        num = [jnp.dot(wts[h], vh[h], preferred_element_type=F32) + s_inter[h] * q_c[h] for h in H]
        den = [jnp.sum(wts[h], axis=1, keepdims=True)
               + s_inter[h] * jnp.sum(qh[h] * n_prev[h], axis=1, keepdims=True) for h in H]
        hh = [num[h] / jnp.maximum(jnp.abs(den[h]), jnp.exp(-m_t[h][:, 0:d])) for h in H]
        m_new = [jnp.maximum(g_tot[h] + m_prev[h], a_max[h]) for h in H]
        for h in H:
            s_prev = jnp.exp((g_tot[h] + m_prev[h] - m_new[h])[:, 0:d])
            s_loc = jnp.exp((a_max[h] - m_new[h])[:, 0:d])
            c_ref[h] = s_prev * c_prev[h] + s_loc * lp["c_loc"][h]
            n_ref[h] = jnp.broadcast_to(s_prev * n_prev[h] + s_loc * lp["n_loc"][h], (8, d))
            m_ref[h] = jnp.broadcast_to(m_new[h], (8, W))
        for h in H:
            yh = _sigmoid(oh[h]) * hh[h]
            y_ref[0, r0:r0 + L, h * d:(h + 1) * d] = _rms(yh, nrm_ref[0:1, h * d:(h + 1) * d])

    for c in range(T // L):
        carried_part(c, local_part(c))


def _mlstm(zml, zg, conv_w, gate_bias, norm_g, T=256, chunk=ML_TILE_CHUNK):
    B, S, _ = zml.shape
    return pl.pallas_call(
        functools.partial(_mlstm_kernel, T=T, L=chunk),
        grid=(B, S // T),
        in_specs=[pl.BlockSpec((1, T, 1024), lambda b, t: (b, t, 0)),
                  pl.BlockSpec((1, T, 128), lambda b, t: (b, t, 0)),
                  pl.BlockSpec((CONV_W, 2 * ML_WIDTH), lambda b, t: (0, 0)),
                  pl.BlockSpec((1, 128), lambda b, t: (0, 0)),
                  pl.BlockSpec((1, ML_WIDTH), lambda b, t: (0, 0))],
        out_specs=pl.BlockSpec((1, T, ML_WIDTH), lambda b, t: (b, t, 0)),
        out_shape=jax.ShapeDtypeStruct((B, S, ML_WIDTH), F32),
        scratch_shapes=[pltpu.VMEM((T + 8, 2 * ML_WIDTH), F32),
                        pltpu.VMEM((ML_HEADS, ML_DIM, ML_DIM), F32),
                        pltpu.VMEM((ML_HEADS, 8, ML_DIM), F32),
                        pltpu.VMEM((ML_HEADS, 8, max(chunk, ML_DIM)), F32)],
        compiler_params=_cparams(("parallel", "arbitrary")),
        name="mlstm",
    )(zml, zg, conv_w, gate_bias, norm_g)


def _softmax_cols(s, m_ref, p_ref, cols):
    m_old = m_ref[:, cols]
    m_new = jnp.maximum(m_old, jnp.max(s, axis=0, keepdims=True))
    m_ref[:, cols] = m_new
    p_ref[:, cols] = jnp.exp2(s - m_new).astype(BF16)
    return jnp.exp2(m_old - m_new)


def _pipelined_attention(scores, vT_tile, s_refs, p_refs, m_ref, acc_ref, softmax,
                         first, n_plain_pairs, n_tail, max_tile, col_groups):
    sa, sb = s_refs
    pa, pb = p_refs

    def load(j):
        return jnp.clip(j, 0, max_tile)

    def half(j, s_cur, s_nxt, p_cur, p_prev, tail):
        s_nxt[...] = scores(load(j + 1))
        pv = jnp.dot(vT_tile(load(j - 1)), p_prev[...], preferred_element_type=F32)
        alphas = softmax(s_cur, p_cur, j, tail)
        for cols, alpha in zip(col_groups, alphas):
            acc_ref[:, cols] = alpha * (acc_ref[:, cols] + pv[:, cols])

    def pair(j, tail):
        half(j, sa, sb, pa, pb, tail)
        half(j + 1, sb, sa, pb, pa, tail)

    m_ref[...] = jnp.full_like(m_ref, M_INIT)
    acc_ref[...] = jnp.zeros_like(acc_ref)
    pb[...] = jnp.zeros_like(pb)
    sa[...] = scores(load(first))

    if not (isinstance(n_plain_pairs, int) and n_plain_pairs == 0):
        n_trips = n_plain_pairs // LOOP_PAIRS

        def body(i, carry):
            for u in range(LOOP_PAIRS):
                pair(first + 2 * (LOOP_PAIRS * i + u), False)
            return carry

        lax.fori_loop(0, n_trips, body, 0)
        rest = n_plain_pairs - n_trips * LOOP_PAIRS
        for u in range(LOOP_PAIRS - 1):
            @pl.when(u < rest)
            def _():
                pair(first + 2 * (LOOP_PAIRS * n_trips + u), False)
    j = first + 2 * n_plain_pairs
    bufs = ((sa, sb, pa, pb), (sb, sa, pb, pa))
    for t in range(n_tail):
        half(j + t, *bufs[t % 2], True)
    p_last = bufs[(n_tail - 1) % 2][2]
    acc_ref[...] += jnp.dot(vT_tile(load(j + n_tail - 1)), p_last[...], preferred_element_type=F32)


def _da_kernel(lam_ref, qT_ref, k_ref, vT_ref, gain_ref, o_ref,
               qz_ref, sa_ref, sb_ref, pa_ref, pb_ref, m_ref, acc_ref, *, tq, tk, lam_init, S):
    h = pl.program_id(1)
    qi = pl.program_id(2)
    d = DA_QK_DIM
    jd = (qi * tq) // tk
    col_groups = [slice(mp * tq, (mp + 1) * tq) for mp in range(2)]

    qz_ref[...] = jnp.zeros_like(qz_ref)
    qz_ref[0:d, 0:tq] = qT_ref[0, 0:d, :]
    qz_ref[d:2 * d, tq:2 * tq] = qT_ref[0, d:2 * d, :]

    def scores(j):
        return jnp.dot(k_ref[0, 0, pl.ds(pl.multiple_of(j * tk, tk), tk), :], qz_ref[...],
                       preferred_element_type=F32)

    def vT_tile(j):
        return vT_ref[0, :, pl.ds(pl.multiple_of(j * tk, tk), tk)]

    def softmax(s_ref, p_ref, j, tail):
        alphas = []
        for cols in col_groups:
            s = s_ref[:, cols]
            if tail:
                kpos = j * tk + lax.broadcasted_iota(jnp.int32, (tk, tq), 0)
                qpos = qi * tq + lax.broadcasted_iota(jnp.int32, (tk, tq), 1)
                s = jnp.where(kpos <= qpos, s, NEG)
            alphas.append(_softmax_cols(s, m_ref, p_ref, cols))
        return alphas

    _pipelined_attention(scores, vT_tile, (sa_ref, sb_ref), (pa_ref, pb_ref), m_ref, acc_ref, softmax,
                         first=0, n_plain_pairs=jd // 2, n_tail=2, max_tile=S // tk - 1, col_groups=col_groups)

    lp = lam_ref[...]
    lam = (jnp.exp(jnp.sum(lp[0:1] * lp[1:2], axis=1, keepdims=True))
           - jnp.exp(jnp.sum(lp[2:3] * lp[3:4], axis=1, keepdims=True)) + lam_init)
    o1 = acc_ref[0:V_DIM, 0:tq] / acc_ref[V_DIM:V_DIM + 1, 0:tq]
    o2 = acc_ref[0:V_DIM, tq:2 * tq] / acc_ref[V_DIM:V_DIM + 1, tq:2 * tq]
    o = o1 - lam * o2
    y = o * lax.rsqrt(jnp.mean(o * o, axis=0, keepdims=True) + EPS) * gain_ref[...]
    o_ref[0] = (y * (1.0 - lam_init)).astype(o_ref.dtype)


def _diff_attn(da_lambda, daqT, dak, davT, gain_col, lam_init, tq=256, tk=256):
    B, _, S = daqT.shape
    return pl.pallas_call(
        functools.partial(_da_kernel, tq=tq, tk=tk, lam_init=lam_init, S=S),
        grid=(B, DA_HEADS, S // tq),
        in_specs=[pl.BlockSpec((4, DA_QK_DIM), lambda b, h, i: (0, 0)),
                  pl.BlockSpec((1, DA_V_DIM, tq), lambda b, h, i: (b, h, i)),
                  pl.BlockSpec((1, 1, S, 2 * DA_QK_DIM), lambda b, h, i: (b, h, 0, 0)),
                  pl.BlockSpec((1, V_EXT, S), lambda b, h, i: (b, h, 0)),
                  pl.BlockSpec((DA_V_DIM, 1), lambda b, h, i: (h, 0))],
        out_specs=pl.BlockSpec((1, DA_V_DIM, tq), lambda b, h, i: (b, h, i)),
        out_shape=jax.ShapeDtypeStruct((B, DA_WIDTH, S), BF16),
        scratch_shapes=[pltpu.VMEM((2 * DA_QK_DIM, 2 * tq), BF16),
                        pltpu.VMEM((tk, 2 * tq), F32),
                        pltpu.VMEM((tk, 2 * tq), F32),
                        pltpu.VMEM((tk, 2 * tq), BF16),
                        pltpu.VMEM((tk, 2 * tq), BF16),
                        pltpu.VMEM((1, 2 * tq), F32),
                        pltpu.VMEM((V_EXT, 2 * tq), F32)],
        compiler_params=_cparams(("parallel", "parallel", "arbitrary")),
        name="diff_attn",
    )(da_lambda, daqT, dak, davT, gain_col)


def _gelu_tanh(x):
    return x * (0.5 * (1.0 + jnp.tanh(math.sqrt(2.0 / math.pi) * (x + 0.044715 * (x * x * x)))))


def _compress_kernel(x_ref, pe_ref, w1_ref, w2_ref, o_ref, b_ref, *, feature_major):
    n = x_ref.shape[2]
    half = CMP_STRIDE * NSA_DIM
    x = x_ref[0, 0]
    a = jnp.dot((x + pe_ref[0, 0:1, :]).astype(BF16), w1_ref[0, 0:half, :], preferred_element_type=F32)
    b_ref[0:n, :] = jnp.dot((x + pe_ref[0, 1:2, :]).astype(BF16), w1_ref[0, half:2 * half, :],
                            preferred_element_type=F32)
    b_ref[n:n + 8, :] = jnp.zeros((8, CMP_HIDDEN), F32)
    hid = _gelu_tanh(a + b_ref[1:n + 1, :]).astype(BF16)
    if feature_major:
        o_ref[0, 0] = lax.dot_general(w2_ref[0], hid, _NT, preferred_element_type=F32).astype(o_ref.dtype)
    else:
        o_ref[0, 0] = jnp.dot(hid, w2_ref[0], preferred_element_type=F32).astype(o_ref.dtype)


def _compress(x16, pe2, w1, w2, which, feature_major):
    B, _, n, half = x16.shape
    G = NSA_GROUPS
    if feature_major:
        out_shape, out_block = (B, G, NSA_DIM, n), (1, 1, NSA_DIM, n)
        w2_arr, w2_block = jnp.swapaxes(w2, 1, 2), (1, NSA_DIM, CMP_HIDDEN)
    else:
        out_shape, out_block = (B, G, n, NSA_DIM), (1, 1, n, NSA_DIM)
        w2_arr, w2_block = w2, (1, CMP_HIDDEN, NSA_DIM)
    return pl.pallas_call(
        functools.partial(_compress_kernel, feature_major=feature_major),
        grid=(B, G),
        in_specs=[pl.BlockSpec((1, 1, n, half), lambda b, g: (b, which * G + g, 0, 0)),
                  pl.BlockSpec((1, 2, half), lambda b, g: (which, 0, 0)),
                  pl.BlockSpec((1, 2 * half, CMP_HIDDEN), lambda b, g: (which, 0, 0)),
                  pl.BlockSpec(w2_block, lambda b, g: (which, 0, 0))],
        out_specs=pl.BlockSpec(out_block, lambda b, g: (b, g, 0, 0)),
        out_shape=jax.ShapeDtypeStruct(out_shape, BF16),
        scratch_shapes=[pltpu.VMEM((n + 8, CMP_HIDDEN), F32)],
        compiler_params=_cparams(("parallel", "parallel")),
        name="compress_v" if feature_major else "compress_k",
    )(x16, pe2, w1, w2_arr.astype(BF16))


def _nsa_kernel(qT_ref, gT_ref, kc_ref, vcT_ref, ks_ref, vsT_ref, kw_ref, vwT_ref, o_ref,
                qg_ref, qz_ref, qs_ref, imp_ref, sel_ref, pc_ref, sa_ref, sb_ref, pa_ref, pb_ref, m_ref, acc_ref,
                wsa_ref, wsb_ref, wpa_ref, wpb_ref, wm_ref, wacc_ref, out_ref,
                *, tq, tk, S):
    g = pl.program_id(1)
    qi = pl.program_id(2)
    d = NSA_DIM
    R = NSA_REP
    ncb = S // CMP_STRIDE
    nsb = S // SEL_BLOCK
    q0 = qi * tq
    jd = q0 // tk
    ratio = SEL_BLOCK // CMP_STRIDE
    per_tile = tk // SEL_BLOCK

    for r in range(R):
        qg_ref[:, r * tq:(r + 1) * tq] = qT_ref[0, r * d:(r + 1) * d, :]
    qz_ref[...] = jnp.zeros_like(qz_ref)
    qs_ref[...] = jnp.zeros_like(qs_ref)
    qs_ref[0:d, :] = qg_ref[...]
    for gg in range(NSA_GROUPS):
        @pl.when(g == gg)
        def _():
            qz_ref[gg * d:(gg + 1) * d, :] = qg_ref[...]

    def gate(r, br):
        return _sigmoid(gT_ref[0, r * 3 + br:r * 3 + br + 1, :])

    qpos = q0 + lax.broadcasted_iota(jnp.int32, (1, tq), 1)

    def compress_and_select(n_c):
        n_s = n_c // ratio
        s_all = jnp.dot(kc_ref[0, 0, 0:n_c, :], qg_ref[...], preferred_element_type=F32)
        cend = lax.broadcasted_iota(jnp.int32, (n_c, tq), 0) * CMP_STRIDE + (CMP_BLOCK - 1)
        cbias = jnp.where(cend <= qpos, 0.0, NEG)
        hs = range(R)
        sc = [s_all[:, r * tq:(r + 1) * tq] + cbias for r in hs]
        mx = [jnp.maximum(jnp.max(sc[r], axis=0, keepdims=True), M_INIT) for r in hs]
        pu = [jnp.exp2(sc[r] - mx[r]) for r in hs]
        den = [jnp.sum(pu[r], axis=0, keepdims=True) for r in hs]
        pn = [pu[r] / jnp.where(den[r] > 0, den[r], 1.0) for r in hs]
        imp = pn[0]
        for r in range(1, R):
            imp = imp + pn[r]
        for r in hs:
            pc_ref[0:n_c, r * tq:(r + 1) * tq] = pn[r].astype(BF16)
        o_cmp = jnp.dot(vcT_ref[0, 0, :, 0:n_c], pc_ref[0:n_c, :], preferred_element_type=F32)
        for r in range(R):
            cols = slice(r * tq, (r + 1) * tq)
            out_ref[:, cols] = gate(r, 0) * o_cmp[:, cols]

        slabs = []
        for c in range(tq // LANES):
            imp_ref[c, 0:8, :] = jnp.zeros((8, LANES), F32)
            imp_ref[c, 8:8 + n_c, :] = imp[:, c * LANES:(c + 1) * LANES]
            imp_ref[c, 8 + n_c:16 + n_c, :] = jnp.zeros((8, LANES), F32)
            slab = jnp.zeros((n_s, LANES), F32)
            for o in range(-1, ratio):
                slab = slab + imp_ref[c, pl.ds(8 + o, n_s, stride=ratio), :]
            slabs.append(slab)
        p_slc = slabs[0] if len(slabs) == 1 else jnp.concatenate(slabs, axis=1)
        blk = lax.broadcasted_iota(jnp.int32, (n_s, tq), 0)
        cur = lax.shift_right_logical(qpos, int(math.log2(SEL_BLOCK)))
        forced = (blk == 0) | (blk == cur) | (blk == cur - 1)
        causal_blk = blk * SEL_BLOCK <= qpos
        score = jnp.where(forced, FORCE_SCORE, jnp.where(causal_blk, p_slc, -1.0))
        blk_f = blk.astype(F32)
        for _ in range(SEL_TOPK):
            mx_s = jnp.max(score, axis=0, keepdims=True)
            first = jnp.min(jnp.where(score == mx_s, blk_f, float(nsb)), axis=0, keepdims=True)
            score = jnp.where(blk_f == first, TAKEN, score)
        sel_ref[0:n_s, :] = jnp.where(score == TAKEN, 0.0, NEG)
        if n_s < nsb:
            sel_ref[n_s:nsb, :] = jnp.full((nsb - n_s, tq), NEG, F32)

    n_classes = 4
    visible_class = (q0 + tq - 1) // (S // n_classes)
    for k in range(n_classes):
        @pl.when(visible_class == k)
        def _():
            compress_and_select((k + 1) * ncb // n_classes)

    head_cols = [slice(r * tq, (r + 1) * tq) for r in range(R)]
    col_groups = [slice(0, R * tq)]
    max_tile = S // tk - 1

    def k_tile(k_ref_, j):
        return k_ref_[0, pl.ds(pl.multiple_of(j * tk, tk), tk), :]

    def vT_tile_of(vT_ref_):
        def vT_tile(j):
            return vT_ref_[0, :, pl.ds(pl.multiple_of(j * tk, tk), tk)]
        return vT_tile

    def softmax_with(bias_of, m_ref_):
        def softmax(s_ref, p_ref, j, tail):
            bias = bias_of(j, tail)
            s = s_ref[...]
            if bias is not None:
                s = s + jnp.concatenate([bias] * R, axis=1)
            return [_softmax_cols(s, m_ref_, p_ref, col_groups[0])]
        return softmax

    def flush(br, acc_ref_):
        for r, cols in enumerate(head_cols):
            den = acc_ref_[V_DIM:V_DIM + 1, cols]
            out_ref[:, cols] += gate(r, br) * (acc_ref_[0:V_DIM, cols] / jnp.where(den > 0, den, 1.0))

    def kpos_of(j):
        return j * tk + lax.broadcasted_iota(jnp.int32, (tk, tq), 0)

    def win_bias(j, tail):
        kpos = kpos_of(j)
        rel = qpos - kpos
        return jnp.where((rel >= 0) & (rel < WINDOW) & (kpos >= 0), 0.0, NEG)

    def win_scores(j):
        return jnp.dot(k_tile(kw_ref, j), qz_ref[...], preferred_element_type=F32)

    n_win = WINDOW // tk + 1
    _pipelined_attention(win_scores, vT_tile_of(vwT_ref), (wsa_ref, wsb_ref), (wpa_ref, wpb_ref), wm_ref, wacc_ref,
                         softmax_with(win_bias, wm_ref), first=jd - (n_win - 1), n_plain_pairs=0, n_tail=n_win,
                         max_tile=max_tile, col_groups=col_groups)
    flush(2, wacc_ref)

    pad_rows = jnp.zeros((BF16_ROWS - per_tile, tq), F32)

    def sel_scores(j):
        rows = [sel_ref[pl.ds(j * per_tile + i, 1), :] for i in range(per_tile)]
        bias = jnp.concatenate(rows + [pad_rows], axis=0).astype(BF16)
        qs_ref[d:d + BF16_ROWS, :] = jnp.concatenate([bias] * R, axis=1)
        return jnp.dot(ks_ref[0, 0, pl.ds(pl.multiple_of(j * tk, tk), tk), :], qs_ref[...],
                       preferred_element_type=F32)

    def sel_bias(j, tail):
        return jnp.where(kpos_of(j) <= qpos, 0.0, NEG) if tail else None

    _pipelined_attention(sel_scores, vT_tile_of(vsT_ref), (sa_ref, sb_ref), (pa_ref, pb_ref), m_ref, acc_ref,
                         softmax_with(sel_bias, m_ref), first=0, n_plain_pairs=jd // 2, n_tail=2,
                         max_tile=max_tile, col_groups=col_groups)
    flush(1, acc_ref)

    for r in range(R):
        o_ref[0, r * d:(r + 1) * d, :] = out_ref[:, r * tq:(r + 1) * tq].astype(o_ref.dtype)


def _nsa(nsqT, nsgT, kcmp, vcmpT, nsks, nsk, nsvT, tq=128, tk=NSA_TK):
    B, _, S = nsqT.shape
    G, R, d = NSA_GROUPS, NSA_REP, NSA_DIM
    ncb = S // CMP_STRIDE
    nsb = S // SEL_BLOCK
    return pl.pallas_call(
        functools.partial(_nsa_kernel, tq=tq, tk=tk, S=S),
        grid=(B, G, S // tq),
        in_specs=[pl.BlockSpec((1, R * d, tq), lambda b, g, i: (b, g, i)),
                  pl.BlockSpec((1, GATE_PAD, tq), lambda b, g, i: (b, g, i)),
                  pl.BlockSpec((1, 1, ncb, d), lambda b, g, i: (b, g, 0, 0)),
                  pl.BlockSpec((1, 1, d, ncb), lambda b, g, i: (b, g, 0, 0)),
                  pl.BlockSpec((1, 1, S, LANES), lambda b, g, i: (b, g, 0, 0)),
                  pl.BlockSpec((1, V_EXT, S), lambda b, g, i: (b, g, 0)),
                  pl.BlockSpec((1, S, LANES), lambda b, g, i: (b, 0, 1)),
                  pl.BlockSpec((1, V_EXT, S), lambda b, g, i: (b, G + g, 0))],
        out_specs=pl.BlockSpec((1, R * d, tq), lambda b, g, i: (b, g, i)),
        out_shape=jax.ShapeDtypeStruct((B, NSA_WIDTH, S), BF16),
        scratch_shapes=[pltpu.VMEM((d, R * tq), BF16),
                        pltpu.VMEM((LANES, R * tq), BF16),
                        pltpu.VMEM((LANES, R * tq), BF16),
                        pltpu.VMEM((tq // LANES, ncb + 16, LANES), F32),
                        pltpu.VMEM((nsb, tq), F32),
                        pltpu.VMEM((ncb, R * tq), BF16),
                        pltpu.VMEM((tk, R * tq), F32),
                        pltpu.VMEM((tk, R * tq), F32),
                        pltpu.VMEM((tk, R * tq), BF16),
                        pltpu.VMEM((tk, R * tq), BF16),
                        pltpu.VMEM((1, R * tq), F32),
                        pltpu.VMEM((V_EXT, R * tq), F32),
                        pltpu.VMEM((tk, R * tq), F32),
                        pltpu.VMEM((tk, R * tq), F32),
                        pltpu.VMEM((tk, R * tq), BF16),
                        pltpu.VMEM((tk, R * tq), BF16),
                        pltpu.VMEM((1, R * tq), F32),
                        pltpu.VMEM((V_EXT, R * tq), F32),
                        pltpu.VMEM((d, R * tq), F32)],
        compiler_params=_cparams(("parallel", "parallel", "arbitrary")),
        name="nsa",
    )(nsqT, nsgT, kcmp, vcmpT, nsks, nsvT, nsk, nsvT)


def _mix_ffn_kernel(x_ref, yml_ref, ydaT_ref, ynsT_ref, wo_ref, g_ref, w1_ref, w2_ref, gf_ref, o_ref,
                    hb_ref, acc_ref, *, final):
    j = pl.program_id(2)

    @pl.when(j == 0)
    def _():
        x1 = x_ref[0] + jnp.dot(yml_ref[0].astype(BF16), wo_ref[0:ML_WIDTH, :], preferred_element_type=F32)
        x1 = x1 + lax.dot_general(ydaT_ref[0], wo_ref[ML_WIDTH:ML_WIDTH + DA_WIDTH, :], _TN,
                                  preferred_element_type=F32)
        x1 = x1 + lax.dot_general(ynsT_ref[0], wo_ref[ML_WIDTH + DA_WIDTH:, :], _TN,
                                  preferred_element_type=F32)
        hb_ref[...] = _rms(x1, g_ref[...]).astype(BF16)
        acc_ref[...] = x1

    u = jnp.dot(hb_ref[...], w1_ref[...], preferred_element_type=F32)
    a = jnp.square(jnp.maximum(u, 0.0)).astype(BF16)
    acc_ref[...] += jnp.dot(a, w2_ref[...], preferred_element_type=F32)

    @pl.when(j == pl.num_programs(2) - 1)
    def _():
        y = acc_ref[...]
        if final:
            y = _rms(y, gf_ref[...])
        o_ref[0] = y


def _mix_ffn(x, yml, ydaT, ynsT, wo, g, w1, w2, gf, final, tm=1024, tf=1024):
    B, S, D = x.shape
    return pl.pallas_call(
        functools.partial(_mix_ffn_kernel, final=final),
        grid=(B, S // tm, D_FF // tf),
        in_specs=[pl.BlockSpec((1, tm, D), lambda b, i, j: (b, i, 0)),
                  pl.BlockSpec((1, tm, ML_WIDTH), lambda b, i, j: (b, i, 0)),
                  pl.BlockSpec((1, DA_WIDTH, tm), lambda b, i, j: (b, 0, i)),
                  pl.BlockSpec((1, NSA_WIDTH, tm), lambda b, i, j: (b, 0, i)),
                  pl.BlockSpec((D, D), lambda b, i, j: (0, 0)),
                  pl.BlockSpec((1, D), lambda b, i, j: (0, 0)),
                  pl.BlockSpec((D, tf), lambda b, i, j: (0, j)),
                  pl.BlockSpec((tf, D), lambda b, i, j: (j, 0)),
                  pl.BlockSpec((1, D), lambda b, i, j: (0, 0))],
        out_specs=pl.BlockSpec((1, tm, D), lambda b, i, j: (b, i, 0)),
        out_shape=jax.ShapeDtypeStruct((B, S, D), F32),
        scratch_shapes=[pltpu.VMEM((tm, D), BF16), pltpu.VMEM((tm, D), F32)],
        compiler_params=_cparams(("parallel", "parallel", "arbitrary")),
        name="mix_ffn",
    )(x, yml, ydaT, ynsT, wo, g, w1, w2, gf)


def _split_w_in(w_in_l):
    edges = np.concatenate([[0], np.cumsum(IN_SIZES)])
    return [w_in_l[:, int(edges[i]):int(edges[i + 1])] for i in range(len(IN_SIZES))]


def _pad_cols(w, n):
    return jnp.pad(w, ((0, 0), (0, n - w.shape[1])))


def _inproj_weights(w_in_l):
    (ml_q, ml_k, ml_v, ml_o, ml_i, ml_f, da_q, da_k, da_v,
     ns_q, ns_kc, ns_vc, ns_ks, ns_vs, ns_kw, ns_vw, ns_g) = _split_w_in(w_in_l)
    wt = jnp.concatenate([ml_q, ml_k, ml_v, ml_o, _pad_cols(jnp.concatenate([ml_i, ml_f], 1), LANES),
                          da_k, ns_ks, ns_kw, ns_kc, ns_vc], axis=1)
    per_group = NSA_REP * 3
    ns_g_pad = jnp.concatenate(
        [_pad_cols(ns_g[:, gi * per_group:(gi + 1) * per_group], GATE_PAD) for gi in range(NSA_GROUPS)], axis=1)
    wf = jnp.concatenate([da_q, da_v, ns_q, ns_vs, ns_vw, ns_g_pad], axis=1)
    return wt.astype(BF16), wf.T.astype(BF16)


def kernel(x, norm1, w_in, ml_conv, ml_gate_bias, ml_norm, da_lambda, da_norm, nsa_pe,
           nsa_w1, nsa_w2, w_out, norm2, w_ff1, w_ff2, final_norm):
    B, S, D = x.shape
    depth = norm1.shape[0]
    for l in range(depth):
        wt, wfT = _inproj_weights(w_in[l])
        (zml, zg, dak, nsk, nsks, x16, daqT, davT, nsqT, nsvT, nsgT) = _inproj(x, norm1[l][None, :], wt, wfT)

        yml = _mlstm(zml, zg, ml_conv[l], _pad_cols(ml_gate_bias[l][None, :], LANES), ml_norm[l][None, :])

        lam_init = 0.8 - 0.6 * math.exp(-0.3 * l)
        ydaT = _diff_attn(da_lambda[l], daqT, dak, davT, da_norm[l][:, None], lam_init)

        pe2 = nsa_pe[l].reshape(2, 2, CMP_STRIDE * NSA_DIM)
        w1b = nsa_w1[l].astype(BF16)
        kcmp = _compress(x16, pe2, w1b, nsa_w2[l], 0, False)
        vcmpT = _compress(x16, pe2, w1b, nsa_w2[l], 1, True)
        ynsT = _nsa(nsqT, nsgT, kcmp, vcmpT, nsks, nsk, nsvT)

        x = _mix_ffn(x, yml, ydaT, ynsT, w_out[l].astype(BF16), norm2[l][None, :], w_ff1[l].astype(BF16),
                     w_ff2[l].astype(BF16), final_norm[None, :], final=(l == depth - 1))
    return x
```

```python
import functools
import math

import numpy as np
import jax
import jax.numpy as jnp
from jax import lax
from jax.experimental import pallas as pl
from jax.experimental.pallas import tpu as pltpu

F32 = jnp.float32
BF16 = jnp.bfloat16

D_MODEL = 1024
ML_HEADS = 4
ML_DIM = 64
ML_WIDTH = ML_HEADS * ML_DIM
ML_TILE_CHUNK = 128
CONV_W = 4
DA_HEADS = 4
DA_QK_DIM = 32
DA_V_DIM = 64
DA_WIDTH = DA_HEADS * DA_V_DIM
NSA_HEADS = 8
NSA_GROUPS = 2
NSA_REP = NSA_HEADS // NSA_GROUPS
NSA_DIM = 64
NSA_WIDTH = NSA_HEADS * NSA_DIM
NSA_KV = NSA_GROUPS * NSA_DIM
CMP_BLOCK = 32
CMP_STRIDE = 16
CMP_HIDDEN = 4 * NSA_DIM
SEL_BLOCK = 64
SEL_TOPK = 16
WINDOW = 512
D_FF = 4 * D_MODEL
EPS = 1e-6
FORCE_SCORE = 1e4
IN_SIZES = (ML_WIDTH, ML_WIDTH, ML_WIDTH, ML_WIDTH, ML_HEADS, ML_HEADS,
            2 * DA_HEADS * DA_QK_DIM, 2 * DA_HEADS * DA_QK_DIM, DA_WIDTH,
            NSA_WIDTH, NSA_KV, NSA_KV, NSA_KV, NSA_KV, NSA_KV, NSA_KV, 3 * NSA_HEADS)

LANES = 128
GATE_PAD = 16
NEG = -1e30
M_INIT = -1e29
LOOP_PAIRS = 4
TAKEN = -3e38
LOG2E = 1.4426950408889634
V_DIM = 64
BF16_ROWS = 16
NSA_TK = 256
V_EXT = 80
VMEM_LIMIT = 56 * 1024 * 1024

_T_ML = (0, 1024)
_T_MLG = (1024, 1152)
_T_DAK = (1152, 1408)
_T_NSK = (1408, 1664)
_T_NSC = (1664, 1920)
_T_COLS = 1920
_F_DAQ = (0, 256)
_F_DAV = (256, 512)
_F_NSQ = (512, 1024)
_F_NSV = (1024, 1280)
_F_NSG = (1280, 1280 + NSA_GROUPS * GATE_PAD)
_F_ROWS = _F_NSG[1]

_NT = (((1,), (1,)), ((), ()))
_TN = (((0,), (0,)), ((), ()))


def _cparams(sem):
    return pltpu.CompilerParams(dimension_semantics=sem, vmem_limit_bytes=VMEM_LIMIT)


def _rms(x, g):
    return x * lax.rsqrt(jnp.mean(x * x, axis=-1, keepdims=True) + EPS) * g


def _inproj_kernel(x_ref, g_ref, wt_ref, wf_ref,
                   zml_ref, zg_ref, dak_ref, nsk_ref, nsks_ref, x16_ref,
                   daqT_ref, davT_ref, nsqT_ref, nsvT_ref, nsgT_ref, nsc_ref):
    hb = _rms(x_ref[0], g_ref[...]).astype(BF16)

    def tdot(span):
        return jnp.dot(hb, wt_ref[:, span[0]:span[1]], preferred_element_type=F32)

    def fdot(span):
        return lax.dot_general(wf_ref[span[0]:span[1], :], hb, _NT, preferred_element_type=F32)

    zml_ref[0] = tdot(_T_ML)
    zg_ref[0] = tdot(_T_MLG)
    dak = tdot(_T_DAK).astype(BF16)
    for h in range(DA_HEADS):
        dak_ref[0, h] = dak[:, h * 2 * DA_QK_DIM:(h + 1) * 2 * DA_QK_DIM]
    nsk = tdot(_T_NSK).astype(BF16)
    nsk_ref[0] = nsk
    tm_ = nsk.shape[0]
    row = lax.broadcasted_iota(jnp.int32, (tm_, NSA_DIM), 0)
    lane = lax.broadcasted_iota(jnp.int32, (tm_, NSA_DIM), 1)
    blk_in_tile = lax.shift_right_logical(row & (NSA_TK - 1), int(math.log2(SEL_BLOCK)))
    onehot = jnp.where(lane == blk_in_tile, 1.0, 0.0).astype(BF16)
    for gi in range(NSA_GROUPS):
        nsks_ref[0, gi] = jnp.concatenate([nsk[:, gi * NSA_DIM:(gi + 1) * NSA_DIM], onehot], axis=1)
    nsc = tdot(_T_NSC)
    for half in range(2):
        nsc_ref[half] = nsc[:, half * LANES:(half + 1) * LANES]
    for i in range(CMP_STRIDE):
        for half in range(2):
            rows = nsc_ref[half, pl.ds(i, tm_ // CMP_STRIDE, stride=CMP_STRIDE), :]
            for w in range(2):
                x16_ref[0, 2 * half + w, :, i * NSA_DIM:(i + 1) * NSA_DIM] = rows[:, w * NSA_DIM:(w + 1) * NSA_DIM]
    daqT_ref[0] = (fdot(_F_DAQ) * (DA_QK_DIM ** -0.5 * LOG2E)).astype(BF16)
    nsqT_ref[0] = (fdot(_F_NSQ) * (NSA_DIM ** -0.5 * LOG2E)).astype(BF16)
    nsgT_ref[0] = fdot(_F_NSG)
    tm = hb.shape[0]
    for v_ref, span in ((davT_ref, _F_DAV), (nsvT_ref, _F_NSV)):
        v = fdot(span).astype(BF16)
        for h in range((span[1] - span[0]) // V_DIM):
            v_ref[0, h * V_EXT:h * V_EXT + V_DIM, :] = v[h * V_DIM:(h + 1) * V_DIM, :]
            v_ref[0, h * V_EXT + V_DIM:(h + 1) * V_EXT, :] = jnp.ones((V_EXT - V_DIM, tm), BF16)


def _inproj(x, g, wt, wfT, tm=512):
    B, S, D = x.shape
    tok = lambda w, dt: jax.ShapeDtypeStruct((B, S, w), dt)
    feat = lambda r, dt: jax.ShapeDtypeStruct((B, r, S), dt)
    tspec = lambda w: pl.BlockSpec((1, tm, w), lambda b, i: (b, i, 0))
    fspec = lambda r: pl.BlockSpec((1, r, tm), lambda b, i: (b, 0, i))
    return pl.pallas_call(
        _inproj_kernel,
        grid=(B, S // tm),
        in_specs=[tspec(D),
                  pl.BlockSpec((1, D), lambda b, i: (0, 0)),
                  pl.BlockSpec((D, _T_COLS), lambda b, i: (0, 0)),
                  pl.BlockSpec((_F_ROWS, D), lambda b, i: (0, 0))],
        out_specs=[tspec(1024), tspec(128),
                   pl.BlockSpec((1, DA_HEADS, tm, 2 * DA_QK_DIM), lambda b, i: (b, 0, i, 0)), tspec(256),
                   pl.BlockSpec((1, NSA_GROUPS, tm, LANES), lambda b, i: (b, 0, i, 0)),
                   pl.BlockSpec((1, 2 * NSA_GROUPS, tm // CMP_STRIDE, CMP_STRIDE * NSA_DIM), lambda b, i: (b, 0, i, 0)),
                   fspec(256), fspec(4 * V_EXT), fspec(512), fspec(4 * V_EXT), fspec(NSA_GROUPS * GATE_PAD)],
        out_shape=[tok(1024, F32), tok(128, F32),
                   jax.ShapeDtypeStruct((B, DA_HEADS, S, 2 * DA_QK_DIM), BF16), tok(256, BF16),
                   jax.ShapeDtypeStruct((B, NSA_GROUPS, S, LANES), BF16),
                   jax.ShapeDtypeStruct((B, 2 * NSA_GROUPS, S // CMP_STRIDE, CMP_STRIDE * NSA_DIM), F32),
                   feat(256, BF16), feat(4 * V_EXT, BF16), feat(512, BF16), feat(4 * V_EXT, BF16),
                   feat(NSA_GROUPS * GATE_PAD, F32)],
        scratch_shapes=[pltpu.VMEM((2, tm, LANES), F32)],
        compiler_params=_cparams(("parallel", "parallel")),
        name="inproj",
    )(x, g, wt, wfT)


def _log_sigmoid(x):
    return jnp.minimum(x, 0.0) - jnp.log1p(jnp.exp(-jnp.abs(x)))


def _sigmoid(x):
    return 1.0 / (1.0 + jnp.exp(-x))


def _mlstm_kernel(zml_ref, zg_ref, cw_ref, gb_ref, nrm_ref, y_ref,
                  pad_ref, c_ref, n_ref, m_ref, *, T, L):
    d = ML_DIM
    W = max(L, d)
    t = pl.program_id(1)

    @pl.when(t == 0)
    def _():
        pad_ref[0:8, :] = jnp.zeros((8, 2 * ML_WIDTH), F32)
        c_ref[...] = jnp.zeros_like(c_ref)
        n_ref[...] = jnp.zeros_like(n_ref)
        m_ref[...] = jnp.zeros_like(m_ref)

    @pl.when(t > 0)
    def _():
        pad_ref[0:8, :] = pad_ref[T:T + 8, :]

    pad_ref[8:8 + T, :] = zml_ref[0, :, 0:2 * ML_WIDTH]
    conv = cw_ref[0:1, :] * pad_ref[5:5 + T, :]
    for j in range(1, CONV_W):
        conv = conv + cw_ref[j:j + 1, :] * pad_ref[5 + j:5 + j + T, :]
    qk = conv * _sigmoid(conv)
    gates = zg_ref[0] + gb_ref[...]
    logf = _log_sigmoid(gates)

    row = lax.broadcasted_iota(jnp.int32, (L, L), 0)
    col = lax.broadcasted_iota(jnp.int32, (L, L), 1)
    causal = col <= row
    tril = causal.astype(F32)
    triu = (row <= col).astype(F32)

    lane_of = lax.broadcasted_iota(jnp.int32, (LANES, W), 0)
    pick = [(lane_of == g).astype(F32) for g in range(2 * ML_HEADS)]

    H = range(ML_HEADS)

    def local_part(c):
        r0 = c * L
        g_c = gates[r0:r0 + L, :]
        lf_c = logf[r0:r0 + L, :]
        b_cols = jnp.dot(tril, lf_c, preferred_element_type=F32,
                         precision=lax.Precision.HIGHEST)
        b_rows = jnp.dot(lf_c.T, triu, preferred_element_type=F32,
                         precision=lax.Precision.HIGHEST)
        g_rows = g_c.T
        b_col = [jnp.dot(b_cols, pick[ML_HEADS + h], preferred_element_type=F32,
                         precision=lax.Precision.HIGHEST) for h in H]
        ig_col = [jnp.dot(g_c, pick[h], preferred_element_type=F32,
                          precision=lax.Precision.HIGHEST) for h in H]
        b_row = [b_rows[ML_HEADS + h:ML_HEADS + h + 1, :] for h in H]
        ig_row = [g_rows[h:h + 1, :] for h in H]
        qh = [qk[r0:r0 + L, h * d:(h + 1) * d] for h in H]
        kh = [qk[r0:r0 + L, ML_WIDTH + h * d:ML_WIDTH + (h + 1) * d] * (d ** -0.5) for h in H]
        vh = [zml_ref[0, r0:r0 + L, 2 * ML_WIDTH + h * d:2 * ML_WIDTH + (h + 1) * d] for h in H]
        qkt = [lax.dot_general(qh[h], kh[h], _NT, preferred_element_type=F32) for h in H]
        g_tot = [b_col[h][L - 1:L, :] for h in H]
        a_max = [jnp.max(g_tot[h][:, 0:L] - b_row[h] + ig_row[h], axis=1, keepdims=True) for h in H]
        w_col = [jnp.exp((g_tot[h] - b_col[h] + ig_col[h] - a_max[h])[:, 0:d]) for h in H]
        c_loc = [lax.dot_general(vh[h] * w_col[h], kh[h], _TN, preferred_element_type=F32) for h in H]
        n_loc = [jnp.sum(kh[h] * w_col[h], axis=0, keepdims=True) for h in H]
        dmat = [jnp.where(causal, b_col[h][:, 0:L] - b_row[h] + ig_row[h], NEG) for h in H]
        d_max = [jnp.max(dmat[h], axis=1, keepdims=True) for h in H]
        return dict(b_col=b_col, qh=qh, vh=vh, qkt=qkt, g_tot=g_tot, a_max=a_max, c_loc=c_loc, n_loc=n_loc,
                    dmat=dmat, d_max=d_max)

    def carried_part(c, lp):
        r0 = c * L
        b_col, qh, vh, qkt, g_tot, a_max = lp["b_col"], lp["qh"], lp["vh"], lp["qkt"], lp["g_tot"], lp["a_max"]
        oh = [zml_ref[0, r0:r0 + L, 3 * ML_WIDTH + h * d:3 * ML_WIDTH + (h + 1) * d] for h in H]
        c_prev = [c_ref[h] for h in H]
        n_prev = [n_ref[h, 0:1, :] for h in H]
        m_prev = [m_ref[h, 0:1, :] for h in H]
        q_c = [lax.dot_general(qh[h], c_prev[h], _NT, preferred_element_type=F32) for h in H]
        inter_log = [b_col[h] + m_prev[h] for h in H]
        m_t = [jnp.maximum(inter_log[h], lp["d_max"][h]) for h in H]
        wts = [jnp.exp(lp["dmat"][h] - m_t[h][:, 0:L]) * qkt[h] for h in H]
        s_inter = [jnp.exp((inter_log[h] - m_t[h])[:, 0:d]) for h in H]
        num = [jnp.dot(wts[h], vh[h], preferred_element_type=F32) + s_inter[h] * q_c[h] for h in H]
        den = [jnp.sum(wts[h], axis=1, keepdims=True)
               + s_inter[h] * jnp.sum(qh[h] * n_prev[h], axis=1, keepdims=True) for h in H]
        hh = [num[h] / jnp.maximum(jnp.abs(den[h]), jnp.exp(-m_t[h][:, 0:d])) for h in H]
        m_new = [jnp.maximum(g_tot[h] + m_prev[h], a_max[h]) for h in H]
        for h in H:
            s_prev = jnp.exp((g_tot[h] + m_prev[h] - m_new[h])[:, 0:d])
            s_loc = jnp.exp((a_max[h] - m_new[h])[:, 0:d])
            c_ref[h] = s_prev * c_prev[h] + s_loc * lp["c_loc"][h]
            n_ref[h] = jnp.broadcast_to(s_prev * n_prev[h] + s_loc * lp["n_loc"][h], (8, d))
            m_ref[h] = jnp.broadcast_to(m_new[h], (8, W))
        for h in H:
            yh = _sigmoid(oh[h]) * hh[h]
            y_ref[0, r0:r0 + L, h * d:(h + 1) * d] = _rms(yh, nrm_ref[0:1, h * d:(h + 1) * d])

    for c in range(T // L):
        carried_part(c, local_part(c))


def _mlstm(zml, zg, conv_w, gate_bias, norm_g, T=256, chunk=ML_TILE_CHUNK):
    B, S, _ = zml.shape
    return pl.pallas_call(
        functools.partial(_mlstm_kernel, T=T, L=chunk),
        grid=(B, S // T),
        in_specs=[pl.BlockSpec((1, T, 1024), lambda b, t: (b, t, 0)),
                  pl.BlockSpec((1, T, 128), lambda b, t: (b, t, 0)),
                  pl.BlockSpec((CONV_W, 2 * ML_WIDTH), lambda b, t: (0, 0)),
                  pl.BlockSpec((1, 128), lambda b, t: (0, 0)),
                  pl.BlockSpec((1, ML_WIDTH), lambda b, t: (0, 0))],
        out_specs=pl.BlockSpec((1, T, ML_WIDTH), lambda b, t: (b, t, 0)),
        out_shape=jax.ShapeDtypeStruct((B, S, ML_WIDTH), F32),
        scratch_shapes=[pltpu.VMEM((T + 8, 2 * ML_WIDTH), F32),
                        pltpu.VMEM((ML_HEADS, ML_DIM, ML_DIM), F32),
                        pltpu.VMEM((ML_HEADS, 8, ML_DIM), F32),
                        pltpu.VMEM((ML_HEADS, 8, max(chunk, ML_DIM)), F32)],
        compiler_params=_cparams(("parallel", "arbitrary")),
        name="mlstm",
    )(zml, zg, conv_w, gate_bias, norm_g)


def _softmax_cols(s, m_ref, p_ref, cols):
    m_old = m_ref[:, cols]
    m_new = jnp.maximum(m_old, jnp.max(s, axis=0, keepdims=True))
    m_ref[:, cols] = m_new
    p_ref[:, cols] = jnp.exp2(s - m_new).astype(BF16)
    return jnp.exp2(m_old - m_new)


def _pipelined_attention(scores, vT_tile, s_refs, p_refs, m_ref, acc_ref, softmax,
                         first, n_plain_pairs, n_tail, max_tile, col_groups):
    sa, sb = s_refs
    pa, pb = p_refs

    def load(j):
        return jnp.clip(j, 0, max_tile)

    def half(j, s_cur, s_nxt, p_cur, p_prev, tail):
        s_nxt[...] = scores(load(j + 1))
        pv = jnp.dot(vT_tile(load(j - 1)), p_prev[...], preferred_element_type=F32)
        alphas = softmax(s_cur, p_cur, j, tail)
        for cols, alpha in zip(col_groups, alphas):
            acc_ref[:, cols] = alpha * (acc_ref[:, cols] + pv[:, cols])

    def pair(j, tail):
        half(j, sa, sb, pa, pb, tail)
        half(j + 1, sb, sa, pb, pa, tail)

    m_ref[...] = jnp.full_like(m_ref, M_INIT)
    acc_ref[...] = jnp.zeros_like(acc_ref)
    pb[...] = jnp.zeros_like(pb)
    sa[...] = scores(load(first))

    if not (isinstance(n_plain_pairs, int) and n_plain_pairs == 0):
        n_trips = n_plain_pairs // LOOP_PAIRS

        def body(i, carry):
            for u in range(LOOP_PAIRS):
                pair(first + 2 * (LOOP_PAIRS * i + u), False)
            return carry

        lax.fori_loop(0, n_trips, body, 0)
        rest = n_plain_pairs - n_trips * LOOP_PAIRS
        for u in range(LOOP_PAIRS - 1):
            @pl.when(u < rest)
            def _():
                pair(first + 2 * (LOOP_PAIRS * n_trips + u), False)
    j = first + 2 * n_plain_pairs
    bufs = ((sa, sb, pa, pb), (sb, sa, pb, pa))
    for t in range(n_tail):
        half(j + t, *bufs[t % 2], True)
    p_last = bufs[(n_tail - 1) % 2][2]
    acc_ref[...] += jnp.dot(vT_tile(load(j + n_tail - 1)), p_last[...], preferred_element_type=F32)


def _da_kernel(lam_ref, qT_ref, k_ref, vT_ref, gain_ref, o_ref,
               qz_ref, sa_ref, sb_ref, pa_ref, pb_ref, m_ref, acc_ref, *, tq, tk, qb, lam_init, S):
    d = DA_QK_DIM
    col_groups = [slice(mp * tq, (mp + 1) * tq) for mp in range(2)]

    def scores(j):
        return jnp.dot(k_ref[0, 0, pl.ds(pl.multiple_of(j * tk, tk), tk), :], qz_ref[...],
                       preferred_element_type=F32)

    def vT_tile(j):
        return vT_ref[0, :, pl.ds(pl.multiple_of(j * tk, tk), tk)]

    lp = lam_ref[...]
    lam = (jnp.exp(jnp.sum(lp[0:1] * lp[1:2], axis=1, keepdims=True))
           - jnp.exp(jnp.sum(lp[2:3] * lp[3:4], axis=1, keepdims=True)) + lam_init)

    def query_block(qq, carry):
        qi = pl.program_id(2) * qb + qq
        jd = (qi * tq) // tk
        cols_q = pl.ds(pl.multiple_of(qq * tq, tq), tq)

        qz_ref[...] = jnp.zeros_like(qz_ref)
        qz_ref[0:d, 0:tq] = qT_ref[0, 0:d, cols_q]
        qz_ref[d:2 * d, tq:2 * tq] = qT_ref[0, d:2 * d, cols_q]

        def softmax(s_ref, p_ref, j, tail):
            alphas = []
            for cols in col_groups:
                s = s_ref[:, cols]
                if tail:
                    kpos = j * tk + lax.broadcasted_iota(jnp.int32, (tk, tq), 0)
                    qpos = qi * tq + lax.broadcasted_iota(jnp.int32, (tk, tq), 1)
                    s = jnp.where(kpos <= qpos, s, NEG)
                alphas.append(_softmax_cols(s, m_ref, p_ref, cols))
            return alphas

        _pipelined_attention(scores, vT_tile, (sa_ref, sb_ref), (pa_ref, pb_ref), m_ref, acc_ref, softmax,
                             first=0, n_plain_pairs=jd // 2, n_tail=2, max_tile=S // tk - 1,
                             col_groups=col_groups)

        o1 = acc_ref[0:V_DIM, 0:tq] / acc_ref[V_DIM:V_DIM + 1, 0:tq]
        o2 = acc_ref[0:V_DIM, tq:2 * tq] / acc_ref[V_DIM:V_DIM + 1, tq:2 * tq]
        o = o1 - lam * o2
        y = o * lax.rsqrt(jnp.mean(o * o, axis=0, keepdims=True) + EPS) * gain_ref[...]
        o_ref[0, :, cols_q] = (y * (1.0 - lam_init)).astype(o_ref.dtype)
        return carry

    lax.fori_loop(0, qb, query_block, 0)


def _diff_attn(da_lambda, daqT, dak, davT, gain_col, lam_init, tq=256, tk=256, qb=4):
    B, _, S = daqT.shape
    return pl.pallas_call(
        functools.partial(_da_kernel, tq=tq, tk=tk, qb=qb, lam_init=lam_init, S=S),
        grid=(B, DA_HEADS, S // (qb * tq)),
        in_specs=[pl.BlockSpec((4, DA_QK_DIM), lambda b, h, i: (0, 0)),
                  pl.BlockSpec((1, DA_V_DIM, qb * tq), lambda b, h, i: (b, h, i)),
                  pl.BlockSpec((1, 1, S, 2 * DA_QK_DIM), lambda b, h, i: (b, h, 0, 0)),
                  pl.BlockSpec((1, V_EXT, S), lambda b, h, i: (b, h, 0)),
                  pl.BlockSpec((DA_V_DIM, 1), lambda b, h, i: (h, 0))],
        out_specs=pl.BlockSpec((1, DA_V_DIM, qb * tq), lambda b, h, i: (b, h, i)),
        out_shape=jax.ShapeDtypeStruct((B, DA_WIDTH, S), BF16),
        scratch_shapes=[pltpu.VMEM((2 * DA_QK_DIM, 2 * tq), BF16),
                        pltpu.VMEM((tk, 2 * tq), F32),
                        pltpu.VMEM((tk, 2 * tq), F32),
                        pltpu.VMEM((tk, 2 * tq), BF16),
                        pltpu.VMEM((tk, 2 * tq), BF16),
                        pltpu.VMEM((1, 2 * tq), F32),
                        pltpu.VMEM((V_EXT, 2 * tq), F32)],
        compiler_params=_cparams(("parallel", "parallel", "arbitrary")),
        name="diff_attn",
    )(da_lambda, daqT, dak, davT, gain_col)


def _gelu_tanh(x):
    return x * (0.5 * (1.0 + jnp.tanh(math.sqrt(2.0 / math.pi) * (x + 0.044715 * (x * x * x)))))


def _compress_kernel(x_ref, pe_ref, w1_ref, w2_ref, o_ref, b_ref, *, feature_major):
    n = x_ref.shape[2]
    half = CMP_STRIDE * NSA_DIM
    x = x_ref[0, 0]
    a = jnp.dot((x + pe_ref[0, 0:1, :]).astype(BF16), w1_ref[0, 0:half, :], preferred_element_type=F32)
    b_ref[0:n, :] = jnp.dot((x + pe_ref[0, 1:2, :]).astype(BF16), w1_ref[0, half:2 * half, :],
                            preferred_element_type=F32)
    b_ref[n:n + 8, :] = jnp.zeros((8, CMP_HIDDEN), F32)
    hid = _gelu_tanh(a + b_ref[1:n + 1, :]).astype(BF16)
    if feature_major:
        o_ref[0, 0] = lax.dot_general(w2_ref[0], hid, _NT, preferred_element_type=F32).astype(o_ref.dtype)
    else:
        o_ref[0, 0] = jnp.dot(hid, w2_ref[0], preferred_element_type=F32).astype(o_ref.dtype)


def _compress(x16, pe2, w1, w2, which, feature_major):
    B, _, n, half = x16.shape
    G = NSA_GROUPS
    if feature_major:
        out_shape, out_block = (B, G, NSA_DIM, n), (1, 1, NSA_DIM, n)
        w2_arr, w2_block = jnp.swapaxes(w2, 1, 2), (1, NSA_DIM, CMP_HIDDEN)
    else:
        out_shape, out_block = (B, G, n, NSA_DIM), (1, 1, n, NSA_DIM)
        w2_arr, w2_block = w2, (1, CMP_HIDDEN, NSA_DIM)
    return pl.pallas_call(
        functools.partial(_compress_kernel, feature_major=feature_major),
        grid=(B, G),
        in_specs=[pl.BlockSpec((1, 1, n, half), lambda b, g: (b, which * G + g, 0, 0)),
                  pl.BlockSpec((1, 2, half), lambda b, g: (which, 0, 0)),
                  pl.BlockSpec((1, 2 * half, CMP_HIDDEN), lambda b, g: (which, 0, 0)),
                  pl.BlockSpec(w2_block, lambda b, g: (which, 0, 0))],
        out_specs=pl.BlockSpec(out_block, lambda b, g: (b, g, 0, 0)),
        out_shape=jax.ShapeDtypeStruct(out_shape, BF16),
        scratch_shapes=[pltpu.VMEM((n + 8, CMP_HIDDEN), F32)],
        compiler_params=_cparams(("parallel", "parallel")),
        name="compress_v" if feature_major else "compress_k",
    )(x16, pe2, w1, w2_arr.astype(BF16))


def _nsa_kernel(qT_ref, gT_ref, kc_ref, vcT_ref, ks_ref, vsT_ref, kw_ref, vwT_ref, o_ref,
                qg_ref, qz_ref, qs_ref, imp_ref, sel_ref, pc_ref, sa_ref, sb_ref, pa_ref, pb_ref, m_ref, acc_ref,
                wsa_ref, wsb_ref, wpa_ref, wpb_ref, wm_ref, wacc_ref, out_ref,
                *, qi, tq, tk, S):
    g = pl.program_id(1)
    d = NSA_DIM
    R = NSA_REP
    ncb = S // CMP_STRIDE
    nsb = S // SEL_BLOCK
    q0 = qi * tq
    jd = q0 // tk
    ratio = SEL_BLOCK // CMP_STRIDE
    per_tile = tk // SEL_BLOCK

    for r in range(R):
        qg_ref[:, r * tq:(r + 1) * tq] = qT_ref[0, r * d:(r + 1) * d, :]
    qz_ref[...] = jnp.zeros_like(qz_ref)
    qs_ref[...] = jnp.zeros_like(qs_ref)
    qs_ref[0:d, :] = qg_ref[...]
    for gg in range(NSA_GROUPS):
        @pl.when(g == gg)
        def _():
            qz_ref[gg * d:(gg + 1) * d, :] = qg_ref[...]

    def gate(r, br):
        return _sigmoid(gT_ref[0, r * 3 + br:r * 3 + br + 1, :])

    qpos = q0 + lax.broadcasted_iota(jnp.int32, (1, tq), 1)

    def compress_and_select(n_c):
        n_s = n_c // ratio
        s_all = jnp.dot(kc_ref[0, 0, 0:n_c, :], qg_ref[...], preferred_element_type=F32)
        cend = lax.broadcasted_iota(jnp.int32, (n_c, tq), 0) * CMP_STRIDE + (CMP_BLOCK - 1)
        cbias = jnp.where(cend <= qpos, 0.0, NEG)
        hs = range(R)
        sc = [s_all[:, r * tq:(r + 1) * tq] + cbias for r in hs]
        mx = [jnp.maximum(jnp.max(sc[r], axis=0, keepdims=True), M_INIT) for r in hs]
        pu = [jnp.exp2(sc[r] - mx[r]) for r in hs]
        den = [jnp.sum(pu[r], axis=0, keepdims=True) for r in hs]
        pn = [pu[r] / jnp.where(den[r] > 0, den[r], 1.0) for r in hs]
        imp = pn[0]
        for r in range(1, R):
            imp = imp + pn[r]
        for r in hs:
            pc_ref[0:n_c, r * tq:(r + 1) * tq] = pn[r].astype(BF16)
        o_cmp = jnp.dot(vcT_ref[0, 0, :, 0:n_c], pc_ref[0:n_c, :], preferred_element_type=F32)
        for r in range(R):
            cols = slice(r * tq, (r + 1) * tq)
            out_ref[:, cols] = gate(r, 0) * o_cmp[:, cols]

        slabs = []
        for c in range(tq // LANES):
            imp_ref[c, 0:8, :] = jnp.zeros((8, LANES), F32)
            imp_ref[c, 8:8 + n_c, :] = imp[:, c * LANES:(c + 1) * LANES]
            imp_ref[c, 8 + n_c:16 + n_c, :] = jnp.zeros((8, LANES), F32)
            slab = jnp.zeros((n_s, LANES), F32)
            for o in range(-1, ratio):
                slab = slab + imp_ref[c, pl.ds(8 + o, n_s, stride=ratio), :]
            slabs.append(slab)
        p_slc = slabs[0] if len(slabs) == 1 else jnp.concatenate(slabs, axis=1)
        blk = lax.broadcasted_iota(jnp.int32, (n_s, tq), 0)
        cur = lax.shift_right_logical(qpos, int(math.log2(SEL_BLOCK)))
        forced = (blk == 0) | (blk == cur) | (blk == cur - 1)
        causal_blk = blk * SEL_BLOCK <= qpos
        score = jnp.where(forced, FORCE_SCORE, jnp.where(causal_blk, p_slc, -1.0))
        blk_f = blk.astype(F32)
        for _ in range(SEL_TOPK):
            mx_s = jnp.max(score, axis=0, keepdims=True)
            first = jnp.min(jnp.where(score == mx_s, blk_f, float(nsb)), axis=0, keepdims=True)
            score = jnp.where(blk_f == first, TAKEN, score)
        sel_ref[0:n_s, :] = jnp.where(score == TAKEN, 0.0, NEG)
        if n_s < nsb:
            sel_ref[n_s:nsb, :] = jnp.full((nsb - n_s, tq), NEG, F32)

    n_classes = 4
    visible_class = (q0 + tq - 1) // (S // n_classes)
    for k in range(n_classes):
        @pl.when(visible_class == k)
        def _():
            compress_and_select((k + 1) * ncb // n_classes)

    head_cols = [slice(r * tq, (r + 1) * tq) for r in range(R)]
    col_groups = [slice(0, R * tq)]
    max_tile = S // tk - 1

    def k_tile(k_ref_, j):
        return k_ref_[0, pl.ds(pl.multiple_of(j * tk, tk), tk), :]

    def vT_tile_of(vT_ref_):
        def vT_tile(j):
            return vT_ref_[0, :, pl.ds(pl.multiple_of(j * tk, tk), tk)]
        return vT_tile

    def softmax_with(bias_of, m_ref_):
        def softmax(s_ref, p_ref, j, tail):
            bias = bias_of(j, tail)
            s = s_ref[...]
            if bias is not None:
                s = s + jnp.concatenate([bias] * R, axis=1)
            return [_softmax_cols(s, m_ref_, p_ref, col_groups[0])]
        return softmax

    def flush(br, acc_ref_):
        for r, cols in enumerate(head_cols):
            den = acc_ref_[V_DIM:V_DIM + 1, cols]
            out_ref[:, cols] += gate(r, br) * (acc_ref_[0:V_DIM, cols] / jnp.where(den > 0, den, 1.0))

    def kpos_of(j):
        return j * tk + lax.broadcasted_iota(jnp.int32, (tk, tq), 0)

    def win_bias(j, tail):
        kpos = kpos_of(j)
        rel = qpos - kpos
        return jnp.where((rel >= 0) & (rel < WINDOW) & (kpos >= 0), 0.0, NEG)

    def win_scores(j):
        return jnp.dot(k_tile(kw_ref, j), qz_ref[...], preferred_element_type=F32)

    n_win = WINDOW // tk + 1
    _pipelined_attention(win_scores, vT_tile_of(vwT_ref), (wsa_ref, wsb_ref), (wpa_ref, wpb_ref), wm_ref, wacc_ref,
                         softmax_with(win_bias, wm_ref), first=jd - (n_win - 1), n_plain_pairs=0, n_tail=n_win,
                         max_tile=max_tile, col_groups=col_groups)
    flush(2, wacc_ref)

    pad_rows = jnp.zeros((BF16_ROWS - per_tile, tq), F32)

    def sel_scores(j):
        rows = [sel_ref[pl.ds(j * per_tile + i, 1), :] for i in range(per_tile)]
        bias = jnp.concatenate(rows + [pad_rows], axis=0).astype(BF16)
        qs_ref[d:d + BF16_ROWS, :] = jnp.concatenate([bias] * R, axis=1)
        return jnp.dot(ks_ref[0, 0, pl.ds(pl.multiple_of(j * tk, tk), tk), :], qs_ref[...],
                       preferred_element_type=F32)

    def sel_bias(j, tail):
        return jnp.where(kpos_of(j) <= qpos, 0.0, NEG) if tail else None

    _pipelined_attention(sel_scores, vT_tile_of(vsT_ref), (sa_ref, sb_ref), (pa_ref, pb_ref), m_ref, acc_ref,
                         softmax_with(sel_bias, m_ref), first=0, n_plain_pairs=jd // 2, n_tail=2,
                         max_tile=max_tile, col_groups=col_groups)
    flush(1, acc_ref)

    for r in range(R):
        o_ref[0, r * d:(r + 1) * d, :] = out_ref[:, r * tq:(r + 1) * tq].astype(o_ref.dtype)


def _nsa_step_kernel(qT_ref, gT_ref, kc_ref, vcT_ref, ks_ref, vsT_ref, kw_ref, vwT_ref, o_ref, *scratch,
                     tq, tk, S, qb):
    def query_block(qq, carry):
        cols = pl.ds(pl.multiple_of(qq * tq, tq), tq)
        _nsa_kernel(qT_ref.at[:, :, cols], gT_ref.at[:, :, cols], kc_ref, vcT_ref, ks_ref, vsT_ref, kw_ref, vwT_ref,
                    o_ref.at[:, :, cols], *scratch, qi=pl.program_id(2) * qb + qq, tq=tq, tk=tk, S=S)
        return carry

    lax.fori_loop(0, qb, query_block, 0)


def _nsa(nsqT, nsgT, kcmp, vcmpT, nsks, nsk, nsvT, tq=128, tk=NSA_TK, qb=4):
    B, _, S = nsqT.shape
    G, R, d = NSA_GROUPS, NSA_REP, NSA_DIM
    ncb = S // CMP_STRIDE
    nsb = S // SEL_BLOCK
    return pl.pallas_call(
        functools.partial(_nsa_step_kernel, tq=tq, tk=tk, S=S, qb=qb),
        grid=(B, G, S // (qb * tq)),
        in_specs=[pl.BlockSpec((1, R * d, qb * tq), lambda b, g, i: (b, g, i)),
                  pl.BlockSpec((1, GATE_PAD, qb * tq), lambda b, g, i: (b, g, i)),
                  pl.BlockSpec((1, 1, ncb, d), lambda b, g, i: (b, g, 0, 0)),
                  pl.BlockSpec((1, 1, d, ncb), lambda b, g, i: (b, g, 0, 0)),
                  pl.BlockSpec((1, 1, S, LANES), lambda b, g, i: (b, g, 0, 0)),
                  pl.BlockSpec((1, V_EXT, S), lambda b, g, i: (b, g, 0)),
                  pl.BlockSpec((1, S, LANES), lambda b, g, i: (b, 0, 1)),
                  pl.BlockSpec((1, V_EXT, S), lambda b, g, i: (b, G + g, 0))],
        out_specs=pl.BlockSpec((1, R * d, qb * tq), lambda b, g, i: (b, g, i)),
        out_shape=jax.ShapeDtypeStruct((B, NSA_WIDTH, S), BF16),
        scratch_shapes=[pltpu.VMEM((d, R * tq), BF16),
                        pltpu.VMEM((LANES, R * tq), BF16),
                        pltpu.VMEM((LANES, R * tq), BF16),
                        pltpu.VMEM((tq // LANES, ncb + 16, LANES), F32),
                        pltpu.VMEM((nsb, tq), F32),
                        pltpu.VMEM((ncb, R * tq), BF16),
                        pltpu.VMEM((tk, R * tq), F32),
                        pltpu.VMEM((tk, R * tq), F32),
                        pltpu.VMEM((tk, R * tq), BF16),
                        pltpu.VMEM((tk, R * tq), BF16),
                        pltpu.VMEM((1, R * tq), F32),
                        pltpu.VMEM((V_EXT, R * tq), F32),
                        pltpu.VMEM((tk, R * tq), F32),
                        pltpu.VMEM((tk, R * tq), F32),
                        pltpu.VMEM((tk, R * tq), BF16),
                        pltpu.VMEM((tk, R * tq), BF16),
                        pltpu.VMEM((1, R * tq), F32),
                        pltpu.VMEM((V_EXT, R * tq), F32),
                        pltpu.VMEM((d, R * tq), F32)],
        compiler_params=_cparams(("parallel", "parallel", "arbitrary")),
        name="nsa",
    )(nsqT, nsgT, kcmp, vcmpT, nsks, nsvT, nsk, nsvT)


def _mix_ffn_kernel(x_ref, yml_ref, ydaT_ref, ynsT_ref, wo_ref, g_ref, w1_ref, w2_ref, gf_ref, o_ref,
                    hb_ref, acc_ref, *, final):
    j = pl.program_id(2)

    @pl.when(j == 0)
    def _():
        x1 = x_ref[0] + jnp.dot(yml_ref[0].astype(BF16), wo_ref[0:ML_WIDTH, :], preferred_element_type=F32)
        x1 = x1 + lax.dot_general(ydaT_ref[0], wo_ref[ML_WIDTH:ML_WIDTH + DA_WIDTH, :], _TN,
                                  preferred_element_type=F32)
        x1 = x1 + lax.dot_general(ynsT_ref[0], wo_ref[ML_WIDTH + DA_WIDTH:, :], _TN,
                                  preferred_element_type=F32)
        hb_ref[...] = _rms(x1, g_ref[...]).astype(BF16)
        acc_ref[...] = x1

    u = jnp.dot(hb_ref[...], w1_ref[...], preferred_element_type=F32)
    a = jnp.square(jnp.maximum(u, 0.0)).astype(BF16)
    acc_ref[...] += jnp.dot(a, w2_ref[...], preferred_element_type=F32)

    @pl.when(j == pl.num_programs(2) - 1)
    def _():
        y = acc_ref[...]
        if final:
            y = _rms(y, gf_ref[...])
        o_ref[0] = y


def _mix_ffn(x, yml, ydaT, ynsT, wo, g, w1, w2, gf, final, tm=1024, tf=1024):
    B, S, D = x.shape
    return pl.pallas_call(
        functools.partial(_mix_ffn_kernel, final=final),
        grid=(B, S // tm, D_FF // tf),
        in_specs=[pl.BlockSpec((1, tm, D), lambda b, i, j: (b, i, 0)),
                  pl.BlockSpec((1, tm, ML_WIDTH), lambda b, i, j: (b, i, 0)),
                  pl.BlockSpec((1, DA_WIDTH, tm), lambda b, i, j: (b, 0, i)),
                  pl.BlockSpec((1, NSA_WIDTH, tm), lambda b, i, j: (b, 0, i)),
                  pl.BlockSpec((D, D), lambda b, i, j: (0, 0)),
                  pl.BlockSpec((1, D), lambda b, i, j: (0, 0)),
                  pl.BlockSpec((D, tf), lambda b, i, j: (0, j)),
                  pl.BlockSpec((tf, D), lambda b, i, j: (j, 0)),
                  pl.BlockSpec((1, D), lambda b, i, j: (0, 0))],
        out_specs=pl.BlockSpec((1, tm, D), lambda b, i, j: (b, i, 0)),
        out_shape=jax.ShapeDtypeStruct((B, S, D), F32),
        scratch_shapes=[pltpu.VMEM((tm, D), BF16), pltpu.VMEM((tm, D), F32)],
        compiler_params=_cparams(("parallel", "parallel", "arbitrary")),
        name="mix_ffn",
    )(x, yml, ydaT, ynsT, wo, g, w1, w2, gf)


def _split_w_in(w_in_l):
    edges = np.concatenate([[0], np.cumsum(IN_SIZES)])
    return [w_in_l[:, int(edges[i]):int(edges[i + 1])] for i in range(len(IN_SIZES))]


def _pad_cols(w, n):
    return jnp.pad(w, ((0, 0), (0, n - w.shape[1])))


def _inproj_weights(w_in_l):
    (ml_q, ml_k, ml_v, ml_o, ml_i, ml_f, da_q, da_k, da_v,
     ns_q, ns_kc, ns_vc, ns_ks, ns_vs, ns_kw, ns_vw, ns_g) = _split_w_in(w_in_l)
    wt = jnp.concatenate([ml_q, ml_k, ml_v, ml_o, _pad_cols(jnp.concatenate([ml_i, ml_f], 1), LANES),
                          da_k, ns_ks, ns_kw, ns_kc, ns_vc], axis=1)
    per_group = NSA_REP * 3
    ns_g_pad = jnp.concatenate(
        [_pad_cols(ns_g[:, gi * per_group:(gi + 1) * per_group], GATE_PAD) for gi in range(NSA_GROUPS)], axis=1)
    wf = jnp.concatenate([da_q, da_v, ns_q, ns_vs, ns_vw, ns_g_pad], axis=1)
    return wt.astype(BF16), wf.T.astype(BF16)


def kernel(x, norm1, w_in, ml_conv, ml_gate_bias, ml_norm, da_lambda, da_norm, nsa_pe,
           nsa_w1, nsa_w2, w_out, norm2, w_ff1, w_ff2, final_norm):
    B, S, D = x.shape
    depth = norm1.shape[0]
    for l in range(depth):
        wt, wfT = _inproj_weights(w_in[l])
        (zml, zg, dak, nsk, nsks, x16, daqT, davT, nsqT, nsvT, nsgT) = _inproj(x, norm1[l][None, :], wt, wfT)

        yml = _mlstm(zml, zg, ml_conv[l], _pad_cols(ml_gate_bias[l][None, :], LANES), ml_norm[l][None, :])

        lam_init = 0.8 - 0.6 * math.exp(-0.3 * l)
        ydaT = _diff_attn(da_lambda[l], daqT, dak, davT, da_norm[l][:, None], lam_init)

        pe2 = nsa_pe[l].reshape(2, 2, CMP_STRIDE * NSA_DIM)
        w1b = nsa_w1[l].astype(BF16)
        kcmp = _compress(x16, pe2, w1b, nsa_w2[l], 0, False)
        vcmpT = _compress(x16, pe2, w1b, nsa_w2[l], 1, True)
        ynsT = _nsa(nsqT, nsgT, kcmp, vcmpT, nsks, nsk, nsvT)

        x = _mix_ffn(x, yml, ydaT, ynsT, w_out[l].astype(BF16), norm2[l][None, :], w_ff1[l].astype(BF16),
                     w_ff2[l].astype(BF16), final_norm[None, :], final=(l == depth - 1))
    return x
```

```python
import functools
import math

import numpy as np
import jax
import jax.numpy as jnp
from jax import lax
from jax.experimental import pallas as pl
from jax.experimental.pallas import tpu as pltpu

F32 = jnp.float32
BF16 = jnp.bfloat16

D_MODEL = 1024
ML_HEADS = 4
ML_DIM = 64
ML_WIDTH = ML_HEADS * ML_DIM
ML_TILE_CHUNK = 128
CONV_W = 4
DA_HEADS = 4
DA_QK_DIM = 32
DA_V_DIM = 64
DA_WIDTH = DA_HEADS * DA_V_DIM
NSA_HEADS = 8
NSA_GROUPS = 2
NSA_REP = NSA_HEADS // NSA_GROUPS
NSA_DIM = 64
NSA_WIDTH = NSA_HEADS * NSA_DIM
NSA_KV = NSA_GROUPS * NSA_DIM
CMP_BLOCK = 32
CMP_STRIDE = 16
CMP_HIDDEN = 4 * NSA_DIM
SEL_BLOCK = 64
SEL_TOPK = 16
WINDOW = 512
D_FF = 4 * D_MODEL
EPS = 1e-6
FORCE_SCORE = 1e4
IN_SIZES = (ML_WIDTH, ML_WIDTH, ML_WIDTH, ML_WIDTH, ML_HEADS, ML_HEADS,
            2 * DA_HEADS * DA_QK_DIM, 2 * DA_HEADS * DA_QK_DIM, DA_WIDTH,
            NSA_WIDTH, NSA_KV, NSA_KV, NSA_KV, NSA_KV, NSA_KV, NSA_KV, 3 * NSA_HEADS)

LANES = 128
GATE_PAD = 16
NEG = -1e30
M_INIT = -1e29
LOOP_PAIRS = 4
TAKEN = -3e38
LOG2E = 1.4426950408889634
V_DIM = 64
BF16_ROWS = 16
NSA_TK = 256
VMEM_LIMIT = 56 * 1024 * 1024

_T_ML = (0, 1024)
_T_MLG = (1024, 1152)
_T_DAK = (1152, 1408)
_T_NSK = (1408, 1664)
_T_NSC = (1664, 1920)
_T_COLS = 1920
_F_DAQ = (0, 256)
_F_DAV = (256, 512)
_F_NSQ = (512, 1024)
_F_NSV = (1024, 1280)
_F_NSG = (1280, 1280 + NSA_GROUPS * GATE_PAD)
_F_ROWS = _F_NSG[1]

_NT = (((1,), (1,)), ((), ()))
_TN = (((0,), (0,)), ((), ()))


def _cparams(sem):
    return pltpu.CompilerParams(dimension_semantics=sem, vmem_limit_bytes=VMEM_LIMIT)


def _rms(x, g):
    return x * lax.rsqrt(jnp.mean(x * x, axis=-1, keepdims=True) + EPS) * g


def _inproj_kernel(x_ref, g_ref, wt_ref, wf_ref,
                   zml_ref, zg_ref, dak_ref, nsk_ref, nsks_ref, x16_ref,
                   daqT_ref, davT_ref, nsqT_ref, nsvT_ref, nsgT_ref, nsc_ref):
    hb = _rms(x_ref[0], g_ref[...]).astype(BF16)

    def tdot(span):
        return jnp.dot(hb, wt_ref[:, span[0]:span[1]], preferred_element_type=F32)

    def fdot(span):
        return lax.dot_general(wf_ref[span[0]:span[1], :], hb, _NT, preferred_element_type=F32)

    zml_ref[0] = tdot(_T_ML)
    zg_ref[0] = tdot(_T_MLG)
    dak = tdot(_T_DAK).astype(BF16)
    for h in range(DA_HEADS):
        dak_ref[0, h] = dak[:, h * 2 * DA_QK_DIM:(h + 1) * 2 * DA_QK_DIM]
    nsk = tdot(_T_NSK).astype(BF16)
    nsk_ref[0] = nsk
    tm_ = nsk.shape[0]
    row = lax.broadcasted_iota(jnp.int32, (tm_, NSA_DIM), 0)
    lane = lax.broadcasted_iota(jnp.int32, (tm_, NSA_DIM), 1)
    blk_in_tile = lax.shift_right_logical(row & (NSA_TK - 1), int(math.log2(SEL_BLOCK)))
    onehot = jnp.where(lane == blk_in_tile, 1.0, 0.0).astype(BF16)
    for gi in range(NSA_GROUPS):
        nsks_ref[0, gi] = jnp.concatenate([nsk[:, gi * NSA_DIM:(gi + 1) * NSA_DIM], onehot], axis=1)
    nsc = tdot(_T_NSC)
    for half in range(2):
        nsc_ref[half] = nsc[:, half * LANES:(half + 1) * LANES]
    for i in range(CMP_STRIDE):
        for half in range(2):
            rows = nsc_ref[half, pl.ds(i, tm_ // CMP_STRIDE, stride=CMP_STRIDE), :]
            for w in range(2):
                x16_ref[0, 2 * half + w, :, i * NSA_DIM:(i + 1) * NSA_DIM] = rows[:, w * NSA_DIM:(w + 1) * NSA_DIM]
    daqT_ref[0] = (fdot(_F_DAQ) * (DA_QK_DIM ** -0.5 * LOG2E)).astype(BF16)
    nsqT_ref[0] = (fdot(_F_NSQ) * (NSA_DIM ** -0.5 * LOG2E)).astype(BF16)
    nsgT_ref[0] = fdot(_F_NSG)
    davT_ref[0] = fdot(_F_DAV).astype(BF16)
    nsvT_ref[0] = fdot(_F_NSV).astype(BF16)


def _inproj(x, g, wt, wfT, tm=512):
    B, S, D = x.shape
    tok = lambda w, dt: jax.ShapeDtypeStruct((B, S, w), dt)
    feat = lambda r, dt: jax.ShapeDtypeStruct((B, r, S), dt)
    tspec = lambda w: pl.BlockSpec((1, tm, w), lambda b, i: (b, i, 0))
    fspec = lambda r: pl.BlockSpec((1, r, tm), lambda b, i: (b, 0, i))
    return pl.pallas_call(
        _inproj_kernel,
        grid=(B, S // tm),
        in_specs=[tspec(D),
                  pl.BlockSpec((1, D), lambda b, i: (0, 0)),
                  pl.BlockSpec((D, _T_COLS), lambda b, i: (0, 0)),
                  pl.BlockSpec((_F_ROWS, D), lambda b, i: (0, 0))],
        out_specs=[tspec(1024), tspec(128),
                   pl.BlockSpec((1, DA_HEADS, tm, 2 * DA_QK_DIM), lambda b, i: (b, 0, i, 0)), tspec(256),
                   pl.BlockSpec((1, NSA_GROUPS, tm, LANES), lambda b, i: (b, 0, i, 0)),
                   pl.BlockSpec((1, 2 * NSA_GROUPS, tm // CMP_STRIDE, CMP_STRIDE * NSA_DIM), lambda b, i: (b, 0, i, 0)),
                   fspec(256), fspec(256), fspec(512), fspec(256), fspec(NSA_GROUPS * GATE_PAD)],
        out_shape=[tok(1024, F32), tok(128, F32),
                   jax.ShapeDtypeStruct((B, DA_HEADS, S, 2 * DA_QK_DIM), BF16), tok(256, BF16),
                   jax.ShapeDtypeStruct((B, NSA_GROUPS, S, LANES), BF16),
                   jax.ShapeDtypeStruct((B, 2 * NSA_GROUPS, S // CMP_STRIDE, CMP_STRIDE * NSA_DIM), F32),
                   feat(256, BF16), feat(256, BF16), feat(512, BF16), feat(256, BF16),
                   feat(NSA_GROUPS * GATE_PAD, F32)],
        scratch_shapes=[pltpu.VMEM((2, tm, LANES), F32)],
        compiler_params=_cparams(("parallel", "parallel")),
        name="inproj",
    )(x, g, wt, wfT)


def _log_sigmoid(x):
    return jnp.minimum(x, 0.0) - jnp.log1p(jnp.exp(-jnp.abs(x)))


def _sigmoid(x):
    return 1.0 / (1.0 + jnp.exp(-x))


def _mlstm_kernel(zml_ref, zg_ref, cw_ref, gb_ref, nrm_ref, y_ref,
                  pad_ref, c_ref, n_ref, m_ref, *, T, L):
    d = ML_DIM
    W = max(L, d)
    t = pl.program_id(1)

    @pl.when(t == 0)
    def _():
        pad_ref[0:8, :] = jnp.zeros((8, 2 * ML_WIDTH), F32)
        c_ref[...] = jnp.zeros_like(c_ref)
        n_ref[...] = jnp.zeros_like(n_ref)
        m_ref[...] = jnp.zeros_like(m_ref)

    @pl.when(t > 0)
    def _():
        pad_ref[0:8, :] = pad_ref[T:T + 8, :]

    pad_ref[8:8 + T, :] = zml_ref[0, :, 0:2 * ML_WIDTH]
    conv = cw_ref[0:1, :] * pad_ref[5:5 + T, :]
    for j in range(1, CONV_W):
        conv = conv + cw_ref[j:j + 1, :] * pad_ref[5 + j:5 + j + T, :]
    qk = conv * _sigmoid(conv)
    gates = zg_ref[0] + gb_ref[...]
    logf = _log_sigmoid(gates)

    row = lax.broadcasted_iota(jnp.int32, (L, L), 0)
    col = lax.broadcasted_iota(jnp.int32, (L, L), 1)
    causal = col <= row
    tril = causal.astype(F32)
    triu = (row <= col).astype(F32)

    lane_of = lax.broadcasted_iota(jnp.int32, (LANES, W), 0)
    pick = [(lane_of == g).astype(F32) for g in range(2 * ML_HEADS)]

    H = range(ML_HEADS)

    def local_part(c):
        r0 = c * L
        g_c = gates[r0:r0 + L, :]
        lf_c = logf[r0:r0 + L, :]
        b_cols = jnp.dot(tril, lf_c, preferred_element_type=F32,
                         precision=lax.Precision.HIGHEST)
        b_rows = jnp.dot(lf_c.T, triu, preferred_element_type=F32,
                         precision=lax.Precision.HIGHEST)
        g_rows = g_c.T
        b_col = [jnp.dot(b_cols, pick[ML_HEADS + h], preferred_element_type=F32,
                         precision=lax.Precision.HIGHEST) for h in H]
        ig_col = [jnp.dot(g_c, pick[h], preferred_element_type=F32,
                          precision=lax.Precision.HIGHEST) for h in H]
        b_row = [b_rows[ML_HEADS + h:ML_HEADS + h + 1, :] for h in H]
        ig_row = [g_rows[h:h + 1, :] for h in H]
        qh = [qk[r0:r0 + L, h * d:(h + 1) * d] for h in H]
        kh = [qk[r0:r0 + L, ML_WIDTH + h * d:ML_WIDTH + (h + 1) * d] * (d ** -0.5) for h in H]
        vh = [zml_ref[0, r0:r0 + L, 2 * ML_WIDTH + h * d:2 * ML_WIDTH + (h + 1) * d] for h in H]
        qkt = [lax.dot_general(qh[h], kh[h], _NT, preferred_element_type=F32) for h in H]
        g_tot = [b_col[h][L - 1:L, :] for h in H]
        a_max = [jnp.max(g_tot[h][:, 0:L] - b_row[h] + ig_row[h], axis=1, keepdims=True) for h in H]
        w_col = [jnp.exp((g_tot[h] - b_col[h] + ig_col[h] - a_max[h])[:, 0:d]) for h in H]
        c_loc = [lax.dot_general(vh[h] * w_col[h], kh[h], _TN, preferred_element_type=F32) for h in H]
        n_loc = [jnp.sum(kh[h] * w_col[h], axis=0, keepdims=True) for h in H]
        dmat = [jnp.where(causal, b_col[h][:, 0:L] - b_row[h] + ig_row[h], NEG) for h in H]
        d_max = [jnp.max(dmat[h], axis=1, keepdims=True) for h in H]
        return dict(b_col=b_col, qh=qh, vh=vh, qkt=qkt, g_tot=g_tot, a_max=a_max, c_loc=c_loc, n_loc=n_loc,
                    dmat=dmat, d_max=d_max)

    def carried_part(c, lp):
        r0 = c * L
        b_col, qh, vh, qkt, g_tot, a_max = lp["b_col"], lp["qh"], lp["vh"], lp["qkt"], lp["g_tot"], lp["a_max"]
        oh = [zml_ref[0, r0:r0 + L, 3 * ML_WIDTH + h * d:3 * ML_WIDTH + (h + 1) * d] for h in H]
        c_prev = [c_ref[h] for h in H]
        n_prev = [n_ref[h, 0:1, :] for h in H]
        m_prev = [m_ref[h, 0:1, :] for h in H]
        q_c = [lax.dot_general(qh[h], c_prev[h], _NT, preferred_element_type=F32) for h in H]
        inter_log = [b_col[h] + m_prev[h] for h in H]
        m_t = [jnp.maximum(inter_log[h], lp["d_max"][h]) for h in H]
        wts = [jnp.exp(lp["dmat"][h] - m_t[h][:, 0:L]) * qkt[h] for h in H]
        s_inter = [jnp.exp((inter_log[h] - m_t[h])[:, 0:d]) for h in H]
        num = [jnp.dot(wts[h], vh[h], preferred_element_type=F32) + s_inter[h] * q_c[h] for h in H]
        den = [jnp.sum(wts[h], axis=1, keepdims=True)
               + s_inter[h] * jnp.sum(qh[h] * n_prev[h], axis=1, keepdims=True) for h in H]
        hh = [num[h] / jnp.maximum(jnp.abs(den[h]), jnp.exp(-m_t[h][:, 0:d])) for h in H]
        m_new = [jnp.maximum(g_tot[h] + m_prev[h], a_max[h]) for h in H]
        for h in H:
            s_prev = jnp.exp((g_tot[h] + m_prev[h] - m_new[h])[:, 0:d])
            s_loc = jnp.exp((a_max[h] - m_new[h])[:, 0:d])
            c_ref[h] = s_prev * c_prev[h] + s_loc * lp["c_loc"][h]
            n_ref[h] = jnp.broadcast_to(s_prev * n_prev[h] + s_loc * lp["n_loc"][h], (8, d))
            m_ref[h] = jnp.broadcast_to(m_new[h], (8, W))
        for h in H:
            yh = _sigmoid(oh[h]) * hh[h]
            y_ref[0, r0:r0 + L, h * d:(h + 1) * d] = _rms(yh, nrm_ref[0:1, h * d:(h + 1) * d])

    for c in range(T // L):
        carried_part(c, local_part(c))


def _mlstm(zml, zg, conv_w, gate_bias, norm_g, T=256, chunk=ML_TILE_CHUNK):
    B, S, _ = zml.shape
    return pl.pallas_call(
        functools.partial(_mlstm_kernel, T=T, L=chunk),
        grid=(B, S // T),
        in_specs=[pl.BlockSpec((1, T, 1024), lambda b, t: (b, t, 0)),
                  pl.BlockSpec((1, T, 128), lambda b, t: (b, t, 0)),
                  pl.BlockSpec((CONV_W, 2 * ML_WIDTH), lambda b, t: (0, 0)),
                  pl.BlockSpec((1, 128), lambda b, t: (0, 0)),
                  pl.BlockSpec((1, ML_WIDTH), lambda b, t: (0, 0))],
        out_specs=pl.BlockSpec((1, T, ML_WIDTH), lambda b, t: (b, t, 0)),
        out_shape=jax.ShapeDtypeStruct((B, S, ML_WIDTH), F32),
        scratch_shapes=[pltpu.VMEM((T + 8, 2 * ML_WIDTH), F32),
                        pltpu.VMEM((ML_HEADS, ML_DIM, ML_DIM), F32),
                        pltpu.VMEM((ML_HEADS, 8, ML_DIM), F32),
                        pltpu.VMEM((ML_HEADS, 8, max(chunk, ML_DIM)), F32)],
        compiler_params=_cparams(("parallel", "arbitrary")),
        name="mlstm",
    )(zml, zg, conv_w, gate_bias, norm_g)


def _softmax_cols(s, m_ref, l_ref, p_ref, cols):
    m_old = m_ref[:, cols]
    m_new = jnp.maximum(m_old, jnp.max(s, axis=0, keepdims=True))
    alpha = jnp.exp2(m_old - m_new)
    p = jnp.exp2(s - m_new)
    m_ref[:, cols] = m_new
    l_ref[:, cols] = alpha * l_ref[:, cols] + jnp.sum(p, axis=0, keepdims=True)
    p_ref[:, cols] = p.astype(BF16)
    return alpha


def _pipelined_attention(scores, vT_tile, s_refs, p_refs, m_ref, l_ref, acc_ref, softmax,
                         first, n_plain_pairs, n_tail, max_tile, col_groups):
    sa, sb = s_refs
    pa, pb = p_refs

    def load(j):
        return jnp.clip(j, 0, max_tile)

    def half(j, s_cur, s_nxt, p_cur, p_prev, tail):
        s_nxt[...] = scores(load(j + 1))
        pv = jnp.dot(vT_tile(load(j - 1)), p_prev[...], preferred_element_type=F32)
        alphas = softmax(s_cur, p_cur, j, tail)
        for cols, alpha in zip(col_groups, alphas):
            acc_ref[:, cols] = alpha * (acc_ref[:, cols] + pv[:, cols])

    def pair(j, tail):
        half(j, sa, sb, pa, pb, tail)
        half(j + 1, sb, sa, pb, pa, tail)

    m_ref[...] = jnp.full_like(m_ref, M_INIT)
    l_ref[...] = jnp.zeros_like(l_ref)
    acc_ref[...] = jnp.zeros_like(acc_ref)
    pb[...] = jnp.zeros_like(pb)
    sa[...] = scores(load(first))

    if not (isinstance(n_plain_pairs, int) and n_plain_pairs == 0):
        n_trips = n_plain_pairs // LOOP_PAIRS

        def body(i, carry):
            for u in range(LOOP_PAIRS):
                pair(first + 2 * (LOOP_PAIRS * i + u), False)
            return carry

        lax.fori_loop(0, n_trips, body, 0)
        rest = n_plain_pairs - n_trips * LOOP_PAIRS
        for u in range(LOOP_PAIRS - 1):
            @pl.when(u < rest)
            def _():
                pair(first + 2 * (LOOP_PAIRS * n_trips + u), False)
    j = first + 2 * n_plain_pairs
    bufs = ((sa, sb, pa, pb), (sb, sa, pb, pa))
    for t in range(n_tail):
        half(j + t, *bufs[t % 2], True)
    p_last = bufs[(n_tail - 1) % 2][2]
    acc_ref[...] += jnp.dot(vT_tile(load(j + n_tail - 1)), p_last[...], preferred_element_type=F32)


def _da_kernel(lam_ref, qT_ref, k_ref, vT_ref, gain_ref, o_ref,
               qz_ref, sa_ref, sb_ref, pa_ref, pb_ref, m_ref, l_ref, acc_ref, *, tq, tk, qb, lam_init, S):
    d = DA_QK_DIM
    col_groups = [slice(mp * tq, (mp + 1) * tq) for mp in range(2)]

    def scores(j):
        return jnp.dot(k_ref[0, 0, pl.ds(pl.multiple_of(j * tk, tk), tk), :], qz_ref[...],
                       preferred_element_type=F32)

    def vT_tile(j):
        return vT_ref[0, :, pl.ds(pl.multiple_of(j * tk, tk), tk)]

    lp = lam_ref[...]
    lam = (jnp.exp(jnp.sum(lp[0:1] * lp[1:2], axis=1, keepdims=True))
           - jnp.exp(jnp.sum(lp[2:3] * lp[3:4], axis=1, keepdims=True)) + lam_init)

    def query_block(qq, carry):
        qi = pl.program_id(2) * qb + qq
        jd = (qi * tq) // tk
        cols_q = pl.ds(pl.multiple_of(qq * tq, tq), tq)

        qz_ref[...] = jnp.zeros_like(qz_ref)
        qz_ref[0:d, 0:tq] = qT_ref[0, 0:d, cols_q]
        qz_ref[d:2 * d, tq:2 * tq] = qT_ref[0, d:2 * d, cols_q]

        def softmax(s_ref, p_ref, j, tail):
            alphas = []
            for cols in col_groups:
                s = s_ref[:, cols]
                if tail:
                    kpos = j * tk + lax.broadcasted_iota(jnp.int32, (tk, tq), 0)
                    qpos = qi * tq + lax.broadcasted_iota(jnp.int32, (tk, tq), 1)
                    s = jnp.where(kpos <= qpos, s, NEG)
                alphas.append(_softmax_cols(s, m_ref, l_ref, p_ref, cols))
            return alphas

        _pipelined_attention(scores, vT_tile, (sa_ref, sb_ref), (pa_ref, pb_ref), m_ref, l_ref, acc_ref, softmax,
                             first=0, n_plain_pairs=jd // 2, n_tail=2, max_tile=S // tk - 1,
                             col_groups=col_groups)

        o1 = acc_ref[:, 0:tq] / l_ref[:, 0:tq]
        o2 = acc_ref[:, tq:2 * tq] / l_ref[:, tq:2 * tq]
        o = o1 - lam * o2
        y = o * lax.rsqrt(jnp.mean(o * o, axis=0, keepdims=True) + EPS) * gain_ref[...]
        o_ref[0, :, cols_q] = (y * (1.0 - lam_init)).astype(o_ref.dtype)
        return carry

    lax.fori_loop(0, qb, query_block, 0)


def _diff_attn(da_lambda, daqT, dak, davT, gain_col, lam_init, tq=256, tk=256, qb=4):
    B, _, S = daqT.shape
    return pl.pallas_call(
        functools.partial(_da_kernel, tq=tq, tk=tk, qb=qb, lam_init=lam_init, S=S),
        grid=(B, DA_HEADS, S // (qb * tq)),
        in_specs=[pl.BlockSpec((4, DA_QK_DIM), lambda b, h, i: (0, 0)),
                  pl.BlockSpec((1, DA_V_DIM, qb * tq), lambda b, h, i: (b, h, i)),
                  pl.BlockSpec((1, 1, S, 2 * DA_QK_DIM), lambda b, h, i: (b, h, 0, 0)),
                  pl.BlockSpec((1, DA_V_DIM, S), lambda b, h, i: (b, h, 0)),
                  pl.BlockSpec((DA_V_DIM, 1), lambda b, h, i: (h, 0))],
        out_specs=pl.BlockSpec((1, DA_V_DIM, qb * tq), lambda b, h, i: (b, h, i)),
        out_shape=jax.ShapeDtypeStruct((B, DA_WIDTH, S), BF16),
        scratch_shapes=[pltpu.VMEM((2 * DA_QK_DIM, 2 * tq), BF16),
                        pltpu.VMEM((tk, 2 * tq), F32),
                        pltpu.VMEM((tk, 2 * tq), F32),
                        pltpu.VMEM((tk, 2 * tq), BF16),
                        pltpu.VMEM((tk, 2 * tq), BF16),
                        pltpu.VMEM((1, 2 * tq), F32),
                        pltpu.VMEM((1, 2 * tq), F32),
                        pltpu.VMEM((DA_V_DIM, 2 * tq), F32)],
        compiler_params=_cparams(("parallel", "parallel", "arbitrary")),
        name="diff_attn",
    )(da_lambda, daqT, dak, davT, gain_col)


def _gelu_tanh(x):
    return x * (0.5 * (1.0 + jnp.tanh(math.sqrt(2.0 / math.pi) * (x + 0.044715 * (x * x * x)))))


def _compress_kernel(x_ref, pe_ref, w1_ref, w2_ref, o_ref, b_ref, *, feature_major):
    n = x_ref.shape[2]
    half = CMP_STRIDE * NSA_DIM
    x = x_ref[0, 0]
    a = jnp.dot((x + pe_ref[0, 0:1, :]).astype(BF16), w1_ref[0, 0:half, :], preferred_element_type=F32)
    b_ref[0:n, :] = jnp.dot((x + pe_ref[0, 1:2, :]).astype(BF16), w1_ref[0, half:2 * half, :],
                            preferred_element_type=F32)
    b_ref[n:n + 8, :] = jnp.zeros((8, CMP_HIDDEN), F32)
    hid = _gelu_tanh(a + b_ref[1:n + 1, :]).astype(BF16)
    if feature_major:
        o_ref[0, 0] = lax.dot_general(w2_ref[0], hid, _NT, preferred_element_type=F32).astype(o_ref.dtype)
    else:
        o_ref[0, 0] = jnp.dot(hid, w2_ref[0], preferred_element_type=F32).astype(o_ref.dtype)


def _compress(x16, pe2, w1, w2, which, feature_major):
    B, _, n, half = x16.shape
    G = NSA_GROUPS
    if feature_major:
        out_shape, out_block = (B, G, NSA_DIM, n), (1, 1, NSA_DIM, n)
        w2_arr, w2_block = jnp.swapaxes(w2, 1, 2), (1, NSA_DIM, CMP_HIDDEN)
    else:
        out_shape, out_block = (B, G, n, NSA_DIM), (1, 1, n, NSA_DIM)
        w2_arr, w2_block = w2, (1, CMP_HIDDEN, NSA_DIM)
    return pl.pallas_call(
        functools.partial(_compress_kernel, feature_major=feature_major),
        grid=(B, G),
        in_specs=[pl.BlockSpec((1, 1, n, half), lambda b, g: (b, which * G + g, 0, 0)),
                  pl.BlockSpec((1, 2, half), lambda b, g: (which, 0, 0)),
                  pl.BlockSpec((1, 2 * half, CMP_HIDDEN), lambda b, g: (which, 0, 0)),
                  pl.BlockSpec(w2_block, lambda b, g: (which, 0, 0))],
        out_specs=pl.BlockSpec(out_block, lambda b, g: (b, g, 0, 0)),
        out_shape=jax.ShapeDtypeStruct(out_shape, BF16),
        scratch_shapes=[pltpu.VMEM((n + 8, CMP_HIDDEN), F32)],
        compiler_params=_cparams(("parallel", "parallel")),
        name="compress_v" if feature_major else "compress_k",
    )(x16, pe2, w1, w2_arr.astype(BF16))


def _nsa_kernel(qT_ref, gT_ref, kc_ref, vcT_ref, ks_ref, vsT_ref, kw_ref, vwT_ref, o_ref,
                qg_ref, qz_ref, qs_ref, imp_ref, sel_ref, pc_ref, sa_ref, sb_ref, pa_ref, pb_ref, m_ref, l_ref, acc_ref,
                wsa_ref, wsb_ref, wpa_ref, wpb_ref, wm_ref, wl_ref, wacc_ref, out_ref,
                *, qi, tq, tk, S):
    g = pl.program_id(1)
    d = NSA_DIM
    R = NSA_REP
    ncb = S // CMP_STRIDE
    nsb = S // SEL_BLOCK
    q0 = qi * tq
    jd = q0 // tk
    ratio = SEL_BLOCK // CMP_STRIDE
    per_tile = tk // SEL_BLOCK

    for r in range(R):
        qg_ref[:, r * tq:(r + 1) * tq] = qT_ref[0, r * d:(r + 1) * d, :]
    qz_ref[...] = jnp.zeros_like(qz_ref)
    qs_ref[...] = jnp.zeros_like(qs_ref)
    qs_ref[0:d, :] = qg_ref[...]
    for gg in range(NSA_GROUPS):
        @pl.when(g == gg)
        def _():
            qz_ref[gg * d:(gg + 1) * d, :] = qg_ref[...]

    def gate(r, br):
        return _sigmoid(gT_ref[0, r * 3 + br:r * 3 + br + 1, :])

    qpos = q0 + lax.broadcasted_iota(jnp.int32, (1, tq), 1)

    def compress_and_select(n_c):
        n_s = n_c // ratio
        s_all = jnp.dot(kc_ref[0, 0, 0:n_c, :], qg_ref[...], preferred_element_type=F32)
        cend = lax.broadcasted_iota(jnp.int32, (n_c, tq), 0) * CMP_STRIDE + (CMP_BLOCK - 1)
        cbias = jnp.where(cend <= qpos, 0.0, NEG)
        hs = range(R)
        sc = [s_all[:, r * tq:(r + 1) * tq] + cbias for r in hs]
        mx = [jnp.maximum(jnp.max(sc[r], axis=0, keepdims=True), M_INIT) for r in hs]
        pu = [jnp.exp2(sc[r] - mx[r]) for r in hs]
        den = [jnp.sum(pu[r], axis=0, keepdims=True) for r in hs]
        pn = [pu[r] / jnp.where(den[r] > 0, den[r], 1.0) for r in hs]
        imp = pn[0]
        for r in range(1, R):
            imp = imp + pn[r]
        for r in hs:
            pc_ref[0:n_c, r * tq:(r + 1) * tq] = pn[r].astype(BF16)
        o_cmp = jnp.dot(vcT_ref[0, 0, :, 0:n_c], pc_ref[0:n_c, :], preferred_element_type=F32)
        for r in range(R):
            cols = slice(r * tq, (r + 1) * tq)
            out_ref[:, cols] = gate(r, 0) * o_cmp[:, cols]

        slabs = []
        for c in range(tq // LANES):
            imp_ref[c, 0:8, :] = jnp.zeros((8, LANES), F32)
            imp_ref[c, 8:8 + n_c, :] = imp[:, c * LANES:(c + 1) * LANES]
            imp_ref[c, 8 + n_c:16 + n_c, :] = jnp.zeros((8, LANES), F32)
            slab = jnp.zeros((n_s, LANES), F32)
            for o in range(-1, ratio):
                slab = slab + imp_ref[c, pl.ds(8 + o, n_s, stride=ratio), :]
            slabs.append(slab)
        p_slc = slabs[0] if len(slabs) == 1 else jnp.concatenate(slabs, axis=1)
        blk = lax.broadcasted_iota(jnp.int32, (n_s, tq), 0)
        cur = lax.shift_right_logical(qpos, int(math.log2(SEL_BLOCK)))
        forced = (blk == 0) | (blk == cur) | (blk == cur - 1)
        causal_blk = blk * SEL_BLOCK <= qpos
        score = jnp.where(forced, FORCE_SCORE, jnp.where(causal_blk, p_slc, -1.0))
        blk_f = blk.astype(F32)
        for _ in range(SEL_TOPK):
            mx_s = jnp.max(score, axis=0, keepdims=True)
            first = jnp.min(jnp.where(score == mx_s, blk_f, float(nsb)), axis=0, keepdims=True)
            score = jnp.where(blk_f == first, TAKEN, score)
        sel_ref[0:n_s, :] = jnp.where(score == TAKEN, 0.0, NEG)
        if n_s < nsb:
            sel_ref[n_s:nsb, :] = jnp.full((nsb - n_s, tq), NEG, F32)

    n_classes = 4
    visible_class = (q0 + tq - 1) // (S // n_classes)
    for k in range(n_classes):
        @pl.when(visible_class == k)
        def _():
            compress_and_select((k + 1) * ncb // n_classes)

    head_cols = [slice(r * tq, (r + 1) * tq) for r in range(R)]
    col_groups = [slice(0, R * tq)]
    max_tile = S // tk - 1

    def k_tile(k_ref_, j):
        return k_ref_[0, pl.ds(pl.multiple_of(j * tk, tk), tk), :]

    def vT_tile_of(vT_ref_):
        def vT_tile(j):
            return vT_ref_[0, :, pl.ds(pl.multiple_of(j * tk, tk), tk)]
        return vT_tile

    def softmax_with(bias_of, m_ref_, l_ref_):
        def softmax(s_ref, p_ref, j, tail):
            bias = bias_of(j, tail)
            s = s_ref[...]
            if bias is not None:
                s = s + jnp.concatenate([bias] * R, axis=1)
            return [_softmax_cols(s, m_ref_, l_ref_, p_ref, col_groups[0])]
        return softmax

    def flush(br, acc_ref_, l_ref_):
        for r, cols in enumerate(head_cols):
            den = l_ref_[:, cols]
            out_ref[:, cols] += gate(r, br) * (acc_ref_[:, cols] / jnp.where(den > 0, den, 1.0))

    def kpos_of(j):
        return j * tk + lax.broadcasted_iota(jnp.int32, (tk, tq), 0)

    def win_bias(j, tail):
        kpos = kpos_of(j)
        rel = qpos - kpos
        return jnp.where((rel >= 0) & (rel < WINDOW) & (kpos >= 0), 0.0, NEG)

    def win_scores(j):
        return jnp.dot(k_tile(kw_ref, j), qz_ref[...], preferred_element_type=F32)

    n_win = WINDOW // tk + 1
    _pipelined_attention(win_scores, vT_tile_of(vwT_ref), (wsa_ref, wsb_ref), (wpa_ref, wpb_ref), wm_ref, wl_ref,
                         wacc_ref, softmax_with(win_bias, wm_ref, wl_ref), first=jd - (n_win - 1), n_plain_pairs=0, n_tail=n_win,
                         max_tile=max_tile, col_groups=col_groups)
    flush(2, wacc_ref, wl_ref)

    pad_rows = jnp.zeros((BF16_ROWS - per_tile, tq), F32)

    def sel_scores(j):
        rows = [sel_ref[pl.ds(j * per_tile + i, 1), :] for i in range(per_tile)]
        bias = jnp.concatenate(rows + [pad_rows], axis=0).astype(BF16)
        qs_ref[d:d + BF16_ROWS, :] = jnp.concatenate([bias] * R, axis=1)
        return jnp.dot(ks_ref[0, 0, pl.ds(pl.multiple_of(j * tk, tk), tk), :], qs_ref[...],
                       preferred_element_type=F32)

    def sel_bias(j, tail):
        return jnp.where(kpos_of(j) <= qpos, 0.0, NEG) if tail else None

    _pipelined_attention(sel_scores, vT_tile_of(vsT_ref), (sa_ref, sb_ref), (pa_ref, pb_ref), m_ref, l_ref,
                         acc_ref, softmax_with(sel_bias, m_ref, l_ref), first=0, n_plain_pairs=jd // 2, n_tail=2,
                         max_tile=max_tile, col_groups=col_groups)
    flush(1, acc_ref, l_ref)

    for r in range(R):
        o_ref[0, r * d:(r + 1) * d, :] = out_ref[:, r * tq:(r + 1) * tq].astype(o_ref.dtype)


def _nsa_step_kernel(qT_ref, gT_ref, kc_ref, vcT_ref, ks_ref, vsT_ref, kw_ref, vwT_ref, o_ref, *scratch,
                     tq, tk, S, qb):
    def query_block(qq, carry):
        cols = pl.ds(pl.multiple_of(qq * tq, tq), tq)
        _nsa_kernel(qT_ref.at[:, :, cols], gT_ref.at[:, :, cols], kc_ref, vcT_ref, ks_ref, vsT_ref, kw_ref, vwT_ref,
                    o_ref.at[:, :, cols], *scratch, qi=pl.program_id(2) * qb + qq, tq=tq, tk=tk, S=S)
        return carry

    lax.fori_loop(0, qb, query_block, 0)


def _nsa(nsqT, nsgT, kcmp, vcmpT, nsks, nsk, nsvT, tq=128, tk=NSA_TK, qb=4):
    B, _, S = nsqT.shape
    G, R, d = NSA_GROUPS, NSA_REP, NSA_DIM
    ncb = S // CMP_STRIDE
    nsb = S // SEL_BLOCK
    return pl.pallas_call(
        functools.partial(_nsa_step_kernel, tq=tq, tk=tk, S=S, qb=qb),
        grid=(B, G, S // (qb * tq)),
        in_specs=[pl.BlockSpec((1, R * d, qb * tq), lambda b, g, i: (b, g, i)),
                  pl.BlockSpec((1, GATE_PAD, qb * tq), lambda b, g, i: (b, g, i)),
                  pl.BlockSpec((1, 1, ncb, d), lambda b, g, i: (b, g, 0, 0)),
                  pl.BlockSpec((1, 1, d, ncb), lambda b, g, i: (b, g, 0, 0)),
                  pl.BlockSpec((1, 1, S, LANES), lambda b, g, i: (b, g, 0, 0)),
                  pl.BlockSpec((1, d, S), lambda b, g, i: (b, g, 0)),
                  pl.BlockSpec((1, S, LANES), lambda b, g, i: (b, 0, 1)),
                  pl.BlockSpec((1, d, S), lambda b, g, i: (b, G + g, 0))],
        out_specs=pl.BlockSpec((1, R * d, qb * tq), lambda b, g, i: (b, g, i)),
        out_shape=jax.ShapeDtypeStruct((B, NSA_WIDTH, S), BF16),
        scratch_shapes=[pltpu.VMEM((d, R * tq), BF16),
                        pltpu.VMEM((LANES, R * tq), BF16),
                        pltpu.VMEM((LANES, R * tq), BF16),
                        pltpu.VMEM((tq // LANES, ncb + 16, LANES), F32),
                        pltpu.VMEM((nsb, tq), F32),
                        pltpu.VMEM((ncb, R * tq), BF16),
                        pltpu.VMEM((tk, R * tq), F32),
                        pltpu.VMEM((tk, R * tq), F32),
                        pltpu.VMEM((tk, R * tq), BF16),
                        pltpu.VMEM((tk, R * tq), BF16),
                        pltpu.VMEM((1, R * tq), F32),
                        pltpu.VMEM((1, R * tq), F32),
                        pltpu.VMEM((d, R * tq), F32),
                        pltpu.VMEM((tk, R * tq), F32),
                        pltpu.VMEM((tk, R * tq), F32),
                        pltpu.VMEM((tk, R * tq), BF16),
                        pltpu.VMEM((tk, R * tq), BF16),
                        pltpu.VMEM((1, R * tq), F32),
                        pltpu.VMEM((1, R * tq), F32),
                        pltpu.VMEM((d, R * tq), F32),
                        pltpu.VMEM((d, R * tq), F32)],
        compiler_params=_cparams(("parallel", "parallel", "arbitrary")),
        name="nsa",
    )(nsqT, nsgT, kcmp, vcmpT, nsks, nsvT, nsk, nsvT)


def _mix_ffn_kernel(x_ref, yml_ref, ydaT_ref, ynsT_ref, wo_ref, g_ref, w1_ref, w2_ref, gf_ref, o_ref,
                    hb_ref, acc_ref, *, final):
    j = pl.program_id(2)

    @pl.when(j == 0)
    def _():
        x1 = x_ref[0] + jnp.dot(yml_ref[0].astype(BF16), wo_ref[0:ML_WIDTH, :], preferred_element_type=F32)
        x1 = x1 + lax.dot_general(ydaT_ref[0], wo_ref[ML_WIDTH:ML_WIDTH + DA_WIDTH, :], _TN,
                                  preferred_element_type=F32)
        x1 = x1 + lax.dot_general(ynsT_ref[0], wo_ref[ML_WIDTH + DA_WIDTH:, :], _TN,
                                  preferred_element_type=F32)
        hb_ref[...] = _rms(x1, g_ref[...]).astype(BF16)
        acc_ref[...] = x1

    u = jnp.dot(hb_ref[...], w1_ref[...], preferred_element_type=F32)
    a = jnp.square(jnp.maximum(u, 0.0)).astype(BF16)
    acc_ref[...] += jnp.dot(a, w2_ref[...], preferred_element_type=F32)

    @pl.when(j == pl.num_programs(2) - 1)
    def _():
        y = acc_ref[...]
        if final:
            y = _rms(y, gf_ref[...])
        o_ref[0] = y


def _mix_ffn(x, yml, ydaT, ynsT, wo, g, w1, w2, gf, final, tm=1024, tf=1024):
    B, S, D = x.shape
    return pl.pallas_call(
        functools.partial(_mix_ffn_kernel, final=final),
        grid=(B, S // tm, D_FF // tf),
        in_specs=[pl.BlockSpec((1, tm, D), lambda b, i, j: (b, i, 0)),
                  pl.BlockSpec((1, tm, ML_WIDTH), lambda b, i, j: (b, i, 0)),
                  pl.BlockSpec((1, DA_WIDTH, tm), lambda b, i, j: (b, 0, i)),
                  pl.BlockSpec((1, NSA_WIDTH, tm), lambda b, i, j: (b, 0, i)),
                  pl.BlockSpec((D, D), lambda b, i, j: (0, 0)),
                  pl.BlockSpec((1, D), lambda b, i, j: (0, 0)),
                  pl.BlockSpec((D, tf), lambda b, i, j: (0, j)),
                  pl.BlockSpec((tf, D), lambda b, i, j: (j, 0)),
                  pl.BlockSpec((1, D), lambda b, i, j: (0, 0))],
        out_specs=pl.BlockSpec((1, tm, D), lambda b, i, j: (b, i, 0)),
        out_shape=jax.ShapeDtypeStruct((B, S, D), F32),
        scratch_shapes=[pltpu.VMEM((tm, D), BF16), pltpu.VMEM((tm, D), F32)],
        compiler_params=_cparams(("parallel", "parallel", "arbitrary")),
        name="mix_ffn",
    )(x, yml, ydaT, ynsT, wo, g, w1, w2, gf)


def _split_w_in(w_in_l):
    edges = np.concatenate([[0], np.cumsum(IN_SIZES)])
    return [w_in_l[:, int(edges[i]):int(edges[i + 1])] for i in range(len(IN_SIZES))]


def _pad_cols(w, n):
    return jnp.pad(w, ((0, 0), (0, n - w.shape[1])))


def _inproj_weights(w_in_l):
    (ml_q, ml_k, ml_v, ml_o, ml_i, ml_f, da_q, da_k, da_v,
     ns_q, ns_kc, ns_vc, ns_ks, ns_vs, ns_kw, ns_vw, ns_g) = _split_w_in(w_in_l)
    wt = jnp.concatenate([ml_q, ml_k, ml_v, ml_o, _pad_cols(jnp.concatenate([ml_i, ml_f], 1), LANES),
                          da_k, ns_ks, ns_kw, ns_kc, ns_vc], axis=1)
    per_group = NSA_REP * 3
    ns_g_pad = jnp.concatenate(
        [_pad_cols(ns_g[:, gi * per_group:(gi + 1) * per_group], GATE_PAD) for gi in range(NSA_GROUPS)], axis=1)
    wf = jnp.concatenate([da_q, da_v, ns_q, ns_vs, ns_vw, ns_g_pad], axis=1)
    return wt.astype(BF16), wf.T.astype(BF16)


def kernel(x, norm1, w_in, ml_conv, ml_gate_bias, ml_norm, da_lambda, da_norm, nsa_pe,
           nsa_w1, nsa_w2, w_out, norm2, w_ff1, w_ff2, final_norm):
    B, S, D = x.shape
    depth = norm1.shape[0]
    for l in range(depth):
        wt, wfT = _inproj_weights(w_in[l])
        (zml, zg, dak, nsk, nsks, x16, daqT, davT, nsqT, nsvT, nsgT) = _inproj(x, norm1[l][None, :], wt, wfT)

        yml = _mlstm(zml, zg, ml_conv[l], _pad_cols(ml_gate_bias[l][None, :], LANES), ml_norm[l][None, :])

        lam_init = 0.8 - 0.6 * math.exp(-0.3 * l)
        ydaT = _diff_attn(da_lambda[l], daqT, dak, davT, da_norm[l][:, None], lam_init)

        pe2 = nsa_pe[l].reshape(2, 2, CMP_STRIDE * NSA_DIM)
        w1b = nsa_w1[l].astype(BF16)
        kcmp = _compress(x16, pe2, w1b, nsa_w2[l], 0, False)
        vcmpT = _compress(x16, pe2, w1b, nsa_w2[l], 1, True)
        ynsT = _nsa(nsqT, nsgT, kcmp, vcmpT, nsks, nsk, nsvT)

        x = _mix_ffn(x, yml, ydaT, ynsT, w_out[l].astype(BF16), norm2[l][None, :], w_ff1[l].astype(BF16),
                     w_ff2[l].astype(BF16), final_norm[None, :], final=(l == depth - 1))
    return x
```

```python
import functools
import math

import numpy as np
import jax
import jax.numpy as jnp
from jax import lax
from jax.experimental import pallas as pl
from jax.experimental.pallas import tpu as pltpu

F32 = jnp.float32
BF16 = jnp.bfloat16

D_MODEL = 1024
ML_HEADS = 4
ML_DIM = 64
ML_WIDTH = ML_HEADS * ML_DIM
ML_TILE_CHUNK = 128
CONV_W = 4
DA_HEADS = 4
DA_QK_DIM = 32
DA_V_DIM = 64
DA_WIDTH = DA_HEADS * DA_V_DIM
NSA_HEADS = 8
NSA_GROUPS = 2
NSA_REP = NSA_HEADS // NSA_GROUPS
NSA_DIM = 64
NSA_WIDTH = NSA_HEADS * NSA_DIM
NSA_KV = NSA_GROUPS * NSA_DIM
CMP_BLOCK = 32
CMP_STRIDE = 16
CMP_HIDDEN = 4 * NSA_DIM
SEL_BLOCK = 64
SEL_TOPK = 16
WINDOW = 512
D_FF = 4 * D_MODEL
EPS = 1e-6
FORCE_SCORE = 1e4
IN_SIZES = (ML_WIDTH, ML_WIDTH, ML_WIDTH, ML_WIDTH, ML_HEADS, ML_HEADS,
            2 * DA_HEADS * DA_QK_DIM, 2 * DA_HEADS * DA_QK_DIM, DA_WIDTH,
            NSA_WIDTH, NSA_KV, NSA_KV, NSA_KV, NSA_KV, NSA_KV, NSA_KV, 3 * NSA_HEADS)

LANES = 128
GATE_PAD = 16
NEG = -1e30
M_INIT = -1e29
LOOP_PAIRS = 4
TAKEN = -3e38
LOG2E = 1.4426950408889634
V_DIM = 64
BF16_ROWS = 16
NSA_TK = 256
V_EXT = 80
VMEM_LIMIT = 56 * 1024 * 1024

_T_ML = (0, 1024)
_T_MLG = (1024, 1152)
_T_DAK = (1152, 1408)
_T_NSK = (1408, 1664)
_T_NSC = (1664, 1920)
_T_COLS = 1920
_F_DAQ = (0, 256)
_F_DAV = (256, 512)
_F_NSQ = (512, 1024)
_F_NSV = (1024, 1280)
_F_NSG = (1280, 1280 + NSA_GROUPS * GATE_PAD)
_F_ROWS = _F_NSG[1]

_NT = (((1,), (1,)), ((), ()))
_TN = (((0,), (0,)), ((), ()))


def _cparams(sem):
    return pltpu.CompilerParams(dimension_semantics=sem, vmem_limit_bytes=VMEM_LIMIT)


def _rms(x, g):
    return x * lax.rsqrt(jnp.mean(x * x, axis=-1, keepdims=True) + EPS) * g


def _inproj_kernel(x_ref, g_ref, wt_ref, wf_ref,
                   zml_ref, zg_ref, dak_ref, nsk_ref, nsks_ref, x16_ref,
                   daqT_ref, davT_ref, nsqT_ref, nsvT_ref, nsgT_ref, nsc_ref):
    hb = _rms(x_ref[0], g_ref[...]).astype(BF16)

    def tdot(span):
        return jnp.dot(hb, wt_ref[:, span[0]:span[1]], preferred_element_type=F32)

    def fdot(span):
        return lax.dot_general(wf_ref[span[0]:span[1], :], hb, _NT, preferred_element_type=F32)

    zml_ref[0] = tdot(_T_ML)
    zg_ref[0] = tdot(_T_MLG)
    dak = tdot(_T_DAK).astype(BF16)
    for h in range(DA_HEADS):
        dak_ref[0, h] = dak[:, h * 2 * DA_QK_DIM:(h + 1) * 2 * DA_QK_DIM]
    nsk = tdot(_T_NSK).astype(BF16)
    nsk_ref[0] = nsk
    tm_ = nsk.shape[0]
    row = lax.broadcasted_iota(jnp.int32, (tm_, NSA_DIM), 0)
    lane = lax.broadcasted_iota(jnp.int32, (tm_, NSA_DIM), 1)
    blk_in_tile = lax.shift_right_logical(row & (NSA_TK - 1), int(math.log2(SEL_BLOCK)))
    onehot = jnp.where(lane == blk_in_tile, 1.0, 0.0).astype(BF16)
    for gi in range(NSA_GROUPS):
        nsks_ref[0, gi] = jnp.concatenate([nsk[:, gi * NSA_DIM:(gi + 1) * NSA_DIM], onehot], axis=1)
    nsc = tdot(_T_NSC)
    for half in range(2):
        nsc_ref[half] = nsc[:, half * LANES:(half + 1) * LANES]
    for i in range(CMP_STRIDE):
        for half in range(2):
            rows = nsc_ref[half, pl.ds(i, tm_ // CMP_STRIDE, stride=CMP_STRIDE), :]
            for w in range(2):
                x16_ref[0, 2 * half + w, :, i * NSA_DIM:(i + 1) * NSA_DIM] = rows[:, w * NSA_DIM:(w + 1) * NSA_DIM]
    daqT_ref[0] = (fdot(_F_DAQ) * (DA_QK_DIM ** -0.5 * LOG2E)).astype(BF16)
    nsqT_ref[0] = (fdot(_F_NSQ) * (NSA_DIM ** -0.5 * LOG2E)).astype(BF16)
    nsgT_ref[0] = fdot(_F_NSG)
    tm = hb.shape[0]
    for v_ref, span in ((davT_ref, _F_DAV), (nsvT_ref, _F_NSV)):
        v = fdot(span).astype(BF16)
        for h in range((span[1] - span[0]) // V_DIM):
            v_ref[0, h * V_EXT:h * V_EXT + V_DIM, :] = v[h * V_DIM:(h + 1) * V_DIM, :]
            v_ref[0, h * V_EXT + V_DIM:(h + 1) * V_EXT, :] = jnp.ones((V_EXT - V_DIM, tm), BF16)


def _inproj(x, g, wt, wfT, tm=512):
    B, S, D = x.shape
    tok = lambda w, dt: jax.ShapeDtypeStruct((B, S, w), dt)
    feat = lambda r, dt: jax.ShapeDtypeStruct((B, r, S), dt)
    tspec = lambda w: pl.BlockSpec((1, tm, w), lambda b, i: (b, i, 0))
    fspec = lambda r: pl.BlockSpec((1, r, tm), lambda b, i: (b, 0, i))
    return pl.pallas_call(
        _inproj_kernel,
        grid=(B, S // tm),
        in_specs=[tspec(D),
                  pl.BlockSpec((1, D), lambda b, i: (0, 0)),
                  pl.BlockSpec((D, _T_COLS), lambda b, i: (0, 0)),
                  pl.BlockSpec((_F_ROWS, D), lambda b, i: (0, 0))],
        out_specs=[tspec(1024), tspec(128),
                   pl.BlockSpec((1, DA_HEADS, tm, 2 * DA_QK_DIM), lambda b, i: (b, 0, i, 0)), tspec(256),
                   pl.BlockSpec((1, NSA_GROUPS, tm, LANES), lambda b, i: (b, 0, i, 0)),
                   pl.BlockSpec((1, 2 * NSA_GROUPS, tm // CMP_STRIDE, CMP_STRIDE * NSA_DIM), lambda b, i: (b, 0, i, 0)),
                   fspec(256), fspec(4 * V_EXT), fspec(512), fspec(4 * V_EXT), fspec(NSA_GROUPS * GATE_PAD)],
        out_shape=[tok(1024, F32), tok(128, F32),
                   jax.ShapeDtypeStruct((B, DA_HEADS, S, 2 * DA_QK_DIM), BF16), tok(256, BF16),
                   jax.ShapeDtypeStruct((B, NSA_GROUPS, S, LANES), BF16),
                   jax.ShapeDtypeStruct((B, 2 * NSA_GROUPS, S // CMP_STRIDE, CMP_STRIDE * NSA_DIM), F32),
                   feat(256, BF16), feat(4 * V_EXT, BF16), feat(512, BF16), feat(4 * V_EXT, BF16),
                   feat(NSA_GROUPS * GATE_PAD, F32)],
        scratch_shapes=[pltpu.VMEM((2, tm, LANES), F32)],
        compiler_params=_cparams(("parallel", "parallel")),
        name="inproj",
    )(x, g, wt, wfT)


def _log_sigmoid(x):
    return jnp.minimum(x, 0.0) - jnp.log1p(jnp.exp(-jnp.abs(x)))


def _sigmoid(x):
    return 1.0 / (1.0 + jnp.exp(-x))


def _mlstm_kernel(zml_ref, zg_ref, cw_ref, gb_ref, nrm_ref, y_ref,
                  pad_ref, c_ref, n_ref, m_ref, *, T, L):
    d = ML_DIM
    W = max(L, d)
    t = pl.program_id(1)

    @pl.when(t == 0)
    def _():
        pad_ref[0:8, :] = jnp.zeros((8, 2 * ML_WIDTH), F32)
        c_ref[...] = jnp.zeros_like(c_ref)
        n_ref[...] = jnp.zeros_like(n_ref)
        m_ref[...] = jnp.zeros_like(m_ref)

    @pl.when(t > 0)
    def _():
        pad_ref[0:8, :] = pad_ref[T:T + 8, :]

    pad_ref[8:8 + T, :] = zml_ref[0, :, 0:2 * ML_WIDTH]
    conv = cw_ref[0:1, :] * pad_ref[5:5 + T, :]
    for j in range(1, CONV_W):
        conv = conv + cw_ref[j:j + 1, :] * pad_ref[5 + j:5 + j + T, :]
    qk = conv * _sigmoid(conv)
    gates = zg_ref[0] + gb_ref[...]
    logf = _log_sigmoid(gates)

    row = lax.broadcasted_iota(jnp.int32, (L, L), 0)
    col = lax.broadcasted_iota(jnp.int32, (L, L), 1)
    causal = col <= row
    tril = causal.astype(F32)
    triu = (row <= col).astype(F32)

    lane_of = lax.broadcasted_iota(jnp.int32, (LANES, W), 0)
    pick = [(lane_of == g).astype(F32) for g in range(2 * ML_HEADS)]

    H = range(ML_HEADS)

    def local_part(c):
        r0 = c * L
        g_c = gates[r0:r0 + L, :]
        lf_c = logf[r0:r0 + L, :]
        b_cols = jnp.dot(tril, lf_c, preferred_element_type=F32,
                         precision=lax.Precision.HIGHEST)
        b_rows = jnp.dot(lf_c.T, triu, preferred_element_type=F32,
                         precision=lax.Precision.HIGHEST)
        g_rows = g_c.T
        b_col = [jnp.dot(b_cols, pick[ML_HEADS + h], preferred_element_type=F32,
                         precision=lax.Precision.HIGHEST) for h in H]
        ig_col = [jnp.dot(g_c, pick[h], preferred_element_type=F32,
                          precision=lax.Precision.HIGHEST) for h in H]
        b_row = [b_rows[ML_HEADS + h:ML_HEADS + h + 1, :] for h in H]
        ig_row = [g_rows[h:h + 1, :] for h in H]
        qh = [qk[r0:r0 + L, h * d:(h + 1) * d] for h in H]
        kh = [qk[r0:r0 + L, ML_WIDTH + h * d:ML_WIDTH + (h + 1) * d] * (d ** -0.5) for h in H]
        vh = [zml_ref[0, r0:r0 + L, 2 * ML_WIDTH + h * d:2 * ML_WIDTH + (h + 1) * d] for h in H]
        qkt = [lax.dot_general(qh[h], kh[h], _NT, preferred_element_type=F32) for h in H]
        g_tot = [b_col[h][L - 1:L, :] for h in H]
        a_max = [jnp.max(g_tot[h][:, 0:L] - b_row[h] + ig_row[h], axis=1, keepdims=True) for h in H]
        w_col = [jnp.exp((g_tot[h] - b_col[h] + ig_col[h] - a_max[h])[:, 0:d]) for h in H]
        c_loc = [lax.dot_general(vh[h] * w_col[h], kh[h], _TN, preferred_element_type=F32) for h in H]
        n_loc = [jnp.sum(kh[h] * w_col[h], axis=0, keepdims=True) for h in H]
        dmat = [jnp.where(causal, b_col[h][:, 0:L] - b_row[h] + ig_row[h], NEG) for h in H]
        d_max = [jnp.max(dmat[h], axis=1, keepdims=True) for h in H]
        return dict(b_col=b_col, qh=qh, vh=vh, qkt=qkt, g_tot=g_tot, a_max=a_max, c_loc=c_loc, n_loc=n_loc,
                    dmat=dmat, d_max=d_max)

    def carried_part(c, lp):
        r0 = c * L
        b_col, qh, vh, qkt, g_tot, a_max = lp["b_col"], lp["qh"], lp["vh"], lp["qkt"], lp["g_tot"], lp["a_max"]
        oh = [zml_ref[0, r0:r0 + L, 3 * ML_WIDTH + h * d:3 * ML_WIDTH + (h + 1) * d] for h in H]
        c_prev = [c_ref[h] for h in H]
        n_prev = [n_ref[h, 0:1, :] for h in H]
        m_prev = [m_ref[h, 0:1, :] for h in H]
        q_c = [lax.dot_general(qh[h], c_prev[h], _NT, preferred_element_type=F32) for h in H]
        inter_log = [b_col[h] + m_prev[h] for h in H]
        m_t = [jnp.maximum(inter_log[h], lp["d_max"][h]) for h in H]
        wts = [jnp.exp(lp["dmat"][h] - m_t[h][:, 0:L]) * qkt[h] for h in H]
        s_inter = [jnp.exp((inter_log[h] - m_t[h])[:, 0:d]) for h in H]
        num = [jnp.dot(wts[h], vh[h], preferred_element_type=F32) + s_inter[h] * q_c[h] for h in H]
        den = [jnp.sum(wts[h], axis=1, keepdims=True)
               + s_inter[h] * jnp.sum(qh[h] * n_prev[h], axis=1, keepdims=True) for h in H]
        hh = [num[h] / jnp.maximum(jnp.abs(den[h]), jnp.exp(-m_t[h][:, 0:d])) for h in H]
        m_new = [jnp.maximum(g_tot[h] + m_prev[h], a_max[h]) for h in H]
        for h in H:
            s_prev = jnp.exp((g_tot[h] + m_prev[h] - m_new[h])[:, 0:d])
            s_loc = jnp.exp((a_max[h] - m_new[h])[:, 0:d])
            c_ref[h] = s_prev * c_prev[h] + s_loc * lp["c_loc"][h]
            n_ref[h] = jnp.broadcast_to(s_prev * n_prev[h] + s_loc * lp["n_loc"][h], (8, d))
            m_ref[h] = jnp.broadcast_to(m_new[h], (8, W))
        for h in H:
            yh = _sigmoid(oh[h]) * hh[h]
            y_ref[0, r0:r0 + L, h * d:(h + 1) * d] = _rms(yh, nrm_ref[0:1, h * d:(h + 1) * d])

    for c in range(T // L):
        carried_part(c, local_part(c))


def _mlstm(zml, zg, conv_w, gate_bias, norm_g, T=256, chunk=ML_TILE_CHUNK):
    B, S, _ = zml.shape
    return pl.pallas_call(
        functools.partial(_mlstm_kernel, T=T, L=chunk),
        grid=(B, S // T),
        in_specs=[pl.BlockSpec((1, T, 1024), lambda b, t: (b, t, 0)),
                  pl.BlockSpec((1, T, 128), lambda b, t: (b, t, 0)),
                  pl.BlockSpec((CONV_W, 2 * ML_WIDTH), lambda b, t: (0, 0)),
                  pl.BlockSpec((1, 128), lambda b, t: (0, 0)),
                  pl.BlockSpec((1, ML_WIDTH), lambda b, t: (0, 0))],
        out_specs=pl.BlockSpec((1, T, ML_WIDTH), lambda b, t: (b, t, 0)),
        out_shape=jax.ShapeDtypeStruct((B, S, ML_WIDTH), F32),
        scratch_shapes=[pltpu.VMEM((T + 8, 2 * ML_WIDTH), F32),
                        pltpu.VMEM((ML_HEADS, ML_DIM, ML_DIM), F32),
                        pltpu.VMEM((ML_HEADS, 8, ML_DIM), F32),
                        pltpu.VMEM((ML_HEADS, 8, max(chunk, ML_DIM)), F32)],
        compiler_params=_cparams(("parallel", "arbitrary")),
        name="mlstm",
    )(zml, zg, conv_w, gate_bias, norm_g)


def _softmax_cols(s, m_ref, p_ref, cols):
    m_old = m_ref[:, cols]
    m_new = jnp.maximum(m_old, jnp.max(s, axis=0, keepdims=True))
    m_ref[:, cols] = m_new
    p_ref[:, cols] = jnp.exp2(s - m_new).astype(BF16)
    return jnp.exp2(m_old - m_new)


def _pipelined_attention(scores, vT_tile, s_refs, p_refs, m_ref, acc_ref, softmax,
                         first, n_plain_pairs, n_tail, max_tile, col_groups, tail_is_short=None):
    sa, sb = s_refs
    pa, pb = p_refs

    def load(j):
        return jnp.clip(j, 0, max_tile)

    def half(j, s_cur, s_nxt, p_cur, p_prev, tail):
        s_nxt[...] = scores(load(j + 1))
        pv = jnp.dot(vT_tile(load(j - 1)), p_prev[...], preferred_element_type=F32)
        alphas = softmax(s_cur, p_cur, j, tail)
        for cols, alpha in zip(col_groups, alphas):
            acc_ref[:, cols] = alpha * (acc_ref[:, cols] + pv[:, cols])

    def pair(j, tail):
        half(j, sa, sb, pa, pb, tail)
        half(j + 1, sb, sa, pb, pa, tail)

    m_ref[...] = jnp.full_like(m_ref, M_INIT)
    acc_ref[...] = jnp.zeros_like(acc_ref)
    pb[...] = jnp.zeros_like(pb)
    sa[...] = scores(load(first))

    if not (isinstance(n_plain_pairs, int) and n_plain_pairs == 0):
        n_trips = n_plain_pairs // LOOP_PAIRS

        def body(i, carry):
            for u in range(LOOP_PAIRS):
                pair(first + 2 * (LOOP_PAIRS * i + u), False)
            return carry

        lax.fori_loop(0, n_trips, body, 0)
        rest = n_plain_pairs - n_trips * LOOP_PAIRS
        for u in range(LOOP_PAIRS - 1):
            @pl.when(u < rest)
            def _():
                pair(first + 2 * (LOOP_PAIRS * n_trips + u), False)
    j = first + 2 * n_plain_pairs
    bufs = ((sa, sb, pa, pb), (sb, sa, pb, pa))

    def tail(n):
        for t in range(n):
            half(j + t, *bufs[t % 2], True)
        p_last = bufs[(n - 1) % 2][2]
        acc_ref[...] += jnp.dot(vT_tile(load(j + n - 1)), p_last[...], preferred_element_type=F32)

    if tail_is_short is None:
        tail(n_tail)
    else:
        @pl.when(tail_is_short)
        def _():
            tail(n_tail - 1)

        @pl.when(jnp.logical_not(tail_is_short))
        def _():
            tail(n_tail)


def _da_kernel(lam_ref, qT_ref, k_ref, vT_ref, gain_ref, o_ref,
               qz_ref, sa_ref, sb_ref, pa_ref, pb_ref, m_ref, acc_ref, *, tq, tk, qb, lam_init, S):
    d = DA_QK_DIM
    col_groups = [slice(mp * tq, (mp + 1) * tq) for mp in range(2)]

    def scores(j):
        return jnp.dot(k_ref[0, 0, pl.ds(pl.multiple_of(j * tk, tk), tk), :], qz_ref[...],
                       preferred_element_type=F32)

    def vT_tile(j):
        return vT_ref[0, :, pl.ds(pl.multiple_of(j * tk, tk), tk)]

    lp = lam_ref[...]
    lam = (jnp.exp(jnp.sum(lp[0:1] * lp[1:2], axis=1, keepdims=True))
           - jnp.exp(jnp.sum(lp[2:3] * lp[3:4], axis=1, keepdims=True)) + lam_init)

    def query_block(qq, carry):
        qi = pl.program_id(2) * qb + qq
        jd = (qi * tq) // tk
        cols_q = pl.ds(pl.multiple_of(qq * tq, tq), tq)

        qz_ref[...] = jnp.zeros_like(qz_ref)
        qz_ref[0:d, 0:tq] = qT_ref[0, 0:d, cols_q]
        qz_ref[d:2 * d, tq:2 * tq] = qT_ref[0, d:2 * d, cols_q]

        def softmax(s_ref, p_ref, j, tail):
            alphas = []
            for cols in col_groups:
                s = s_ref[:, cols]
                if tail:
                    kpos = j * tk + lax.broadcasted_iota(jnp.int32, (tk, tq), 0)
                    qpos = qi * tq + lax.broadcasted_iota(jnp.int32, (tk, tq), 1)
                    s = jnp.where(kpos <= qpos, s, NEG)
                alphas.append(_softmax_cols(s, m_ref, p_ref, cols))
            return alphas

        _pipelined_attention(scores, vT_tile, (sa_ref, sb_ref), (pa_ref, pb_ref), m_ref, acc_ref, softmax,
                             first=0, n_plain_pairs=jd // 2, n_tail=2, max_tile=S // tk - 1,
                             col_groups=col_groups, tail_is_short=(2 * (jd // 2) + 1) * tk > qi * tq + tq - 1)

        o1 = acc_ref[0:V_DIM, 0:tq] / acc_ref[V_DIM:V_DIM + 1, 0:tq]
        o2 = acc_ref[0:V_DIM, tq:2 * tq] / acc_ref[V_DIM:V_DIM + 1, tq:2 * tq]
        o = o1 - lam * o2
        y = o * lax.rsqrt(jnp.mean(o * o, axis=0, keepdims=True) + EPS) * gain_ref[...]
        o_ref[0, :, cols_q] = (y * (1.0 - lam_init)).astype(o_ref.dtype)
        return carry

    lax.fori_loop(0, qb, query_block, 0)


def _diff_attn(da_lambda, daqT, dak, davT, gain_col, lam_init, tq=256, tk=256, qb=4):
    B, _, S = daqT.shape
    return pl.pallas_call(
        functools.partial(_da_kernel, tq=tq, tk=tk, qb=qb, lam_init=lam_init, S=S),
        grid=(B, DA_HEADS, S // (qb * tq)),
        in_specs=[pl.BlockSpec((4, DA_QK_DIM), lambda b, h, i: (0, 0)),
                  pl.BlockSpec((1, DA_V_DIM, qb * tq), lambda b, h, i: (b, h, i)),
                  pl.BlockSpec((1, 1, S, 2 * DA_QK_DIM), lambda b, h, i: (b, h, 0, 0)),
                  pl.BlockSpec((1, V_EXT, S), lambda b, h, i: (b, h, 0)),
                  pl.BlockSpec((DA_V_DIM, 1), lambda b, h, i: (h, 0))],
        out_specs=pl.BlockSpec((1, DA_V_DIM, qb * tq), lambda b, h, i: (b, h, i)),
        out_shape=jax.ShapeDtypeStruct((B, DA_WIDTH, S), BF16),
        scratch_shapes=[pltpu.VMEM((2 * DA_QK_DIM, 2 * tq), BF16),
                        pltpu.VMEM((tk, 2 * tq), F32),
                        pltpu.VMEM((tk, 2 * tq), F32),
                        pltpu.VMEM((tk, 2 * tq), BF16),
                        pltpu.VMEM((tk, 2 * tq), BF16),
                        pltpu.VMEM((1, 2 * tq), F32),
                        pltpu.VMEM((V_EXT, 2 * tq), F32)],
        compiler_params=_cparams(("parallel", "parallel", "arbitrary")),
        name="diff_attn",
    )(da_lambda, daqT, dak, davT, gain_col)


def _gelu_tanh(x):
    return x * (0.5 * (1.0 + jnp.tanh(math.sqrt(2.0 / math.pi) * (x + 0.044715 * (x * x * x)))))


def _compress_kernel(x_ref, pe_ref, w1_ref, w2_ref, o_ref, b_ref, *, feature_major):
    n = x_ref.shape[2]
    half = CMP_STRIDE * NSA_DIM
    x = x_ref[0, 0]
    a = jnp.dot((x + pe_ref[0, 0:1, :]).astype(BF16), w1_ref[0, 0:half, :], preferred_element_type=F32)
    b_ref[0:n, :] = jnp.dot((x + pe_ref[0, 1:2, :]).astype(BF16), w1_ref[0, half:2 * half, :],
                            preferred_element_type=F32)
    b_ref[n:n + 8, :] = jnp.zeros((8, CMP_HIDDEN), F32)
    hid = _gelu_tanh(a + b_ref[1:n + 1, :]).astype(BF16)
    if feature_major:
        o_ref[0, 0] = lax.dot_general(w2_ref[0], hid, _NT, preferred_element_type=F32).astype(o_ref.dtype)
    else:
        o_ref[0, 0] = jnp.dot(hid, w2_ref[0], preferred_element_type=F32).astype(o_ref.dtype)


def _compress(x16, pe2, w1, w2, which, feature_major):
    B, _, n, half = x16.shape
    G = NSA_GROUPS
    if feature_major:
        out_shape, out_block = (B, G, NSA_DIM, n), (1, 1, NSA_DIM, n)
        w2_arr, w2_block = jnp.swapaxes(w2, 1, 2), (1, NSA_DIM, CMP_HIDDEN)
    else:
        out_shape, out_block = (B, G, n, NSA_DIM), (1, 1, n, NSA_DIM)
        w2_arr, w2_block = w2, (1, CMP_HIDDEN, NSA_DIM)
    return pl.pallas_call(
        functools.partial(_compress_kernel, feature_major=feature_major),
        grid=(B, G),
        in_specs=[pl.BlockSpec((1, 1, n, half), lambda b, g: (b, which * G + g, 0, 0)),
                  pl.BlockSpec((1, 2, half), lambda b, g: (which, 0, 0)),
                  pl.BlockSpec((1, 2 * half, CMP_HIDDEN), lambda b, g: (which, 0, 0)),
                  pl.BlockSpec(w2_block, lambda b, g: (which, 0, 0))],
        out_specs=pl.BlockSpec(out_block, lambda b, g: (b, g, 0, 0)),
        out_shape=jax.ShapeDtypeStruct(out_shape, BF16),
        scratch_shapes=[pltpu.VMEM((n + 8, CMP_HIDDEN), F32)],
        compiler_params=_cparams(("parallel", "parallel")),
        name="compress_v" if feature_major else "compress_k",
    )(x16, pe2, w1, w2_arr.astype(BF16))


def _nsa_kernel(qT_ref, gT_ref, kc_ref, vcT_ref, ks_ref, vsT_ref, kw_ref, vwT_ref, o_ref,
                qg_ref, qz_ref, qs_ref, imp_ref, sel_ref, pc_ref, sa_ref, sb_ref, pa_ref, pb_ref, m_ref, acc_ref,
                wsa_ref, wsb_ref, wpa_ref, wpb_ref, wm_ref, wacc_ref, out_ref,
                *, qi, tq, tk, S):
    g = pl.program_id(1)
    d = NSA_DIM
    R = NSA_REP
    ncb = S // CMP_STRIDE
    nsb = S // SEL_BLOCK
    q0 = qi * tq
    jd = q0 // tk
    ratio = SEL_BLOCK // CMP_STRIDE
    per_tile = tk // SEL_BLOCK

    for r in range(R):
        qg_ref[:, r * tq:(r + 1) * tq] = qT_ref[0, r * d:(r + 1) * d, :]
    qz_ref[...] = jnp.zeros_like(qz_ref)
    qs_ref[...] = jnp.zeros_like(qs_ref)
    qs_ref[0:d, :] = qg_ref[...]
    for gg in range(NSA_GROUPS):
        @pl.when(g == gg)
        def _():
            qz_ref[gg * d:(gg + 1) * d, :] = qg_ref[...]

    def gate(r, br):
        return _sigmoid(gT_ref[0, r * 3 + br:r * 3 + br + 1, :])

    qpos = q0 + lax.broadcasted_iota(jnp.int32, (1, tq), 1)

    def compress_and_select(n_c):
        n_s = n_c // ratio
        s_all = jnp.dot(kc_ref[0, 0, 0:n_c, :], qg_ref[...], preferred_element_type=F32)
        cend = lax.broadcasted_iota(jnp.int32, (n_c, tq), 0) * CMP_STRIDE + (CMP_BLOCK - 1)
        cbias = jnp.where(cend <= qpos, 0.0, NEG)
        hs = range(R)
        sc = [s_all[:, r * tq:(r + 1) * tq] + cbias for r in hs]
        mx = [jnp.maximum(jnp.max(sc[r], axis=0, keepdims=True), M_INIT) for r in hs]
        pu = [jnp.exp2(sc[r] - mx[r]) for r in hs]
        den = [jnp.sum(pu[r], axis=0, keepdims=True) for r in hs]
        pn = [pu[r] / jnp.where(den[r] > 0, den[r], 1.0) for r in hs]
        imp = pn[0]
        for r in range(1, R):
            imp = imp + pn[r]
        for r in hs:
            pc_ref[0:n_c, r * tq:(r + 1) * tq] = pn[r].astype(BF16)
        o_cmp = jnp.dot(vcT_ref[0, 0, :, 0:n_c], pc_ref[0:n_c, :], preferred_element_type=F32)
        for r in range(R):
            cols = slice(r * tq, (r + 1) * tq)
            out_ref[:, cols] = gate(r, 0) * o_cmp[:, cols]

        slabs = []
        for c in range(tq // LANES):
            imp_ref[c, 0:8, :] = jnp.zeros((8, LANES), F32)
            imp_ref[c, 8:8 + n_c, :] = imp[:, c * LANES:(c + 1) * LANES]
            imp_ref[c, 8 + n_c:16 + n_c, :] = jnp.zeros((8, LANES), F32)
            slab = jnp.zeros((n_s, LANES), F32)
            for o in range(-1, ratio):
                slab = slab + imp_ref[c, pl.ds(8 + o, n_s, stride=ratio), :]
            slabs.append(slab)
        p_slc = slabs[0] if len(slabs) == 1 else jnp.concatenate(slabs, axis=1)
        blk = lax.broadcasted_iota(jnp.int32, (n_s, tq), 0)
        cur = lax.shift_right_logical(qpos, int(math.log2(SEL_BLOCK)))
        forced = (blk == 0) | (blk == cur) | (blk == cur - 1)
        causal_blk = blk * SEL_BLOCK <= qpos
        score = jnp.where(forced, FORCE_SCORE, jnp.where(causal_blk, p_slc, -1.0))
        blk_f = blk.astype(F32)
        for _ in range(SEL_TOPK):
            mx_s = jnp.max(score, axis=0, keepdims=True)
            first = jnp.min(jnp.where(score == mx_s, blk_f, float(nsb)), axis=0, keepdims=True)
            score = jnp.where(blk_f == first, TAKEN, score)
        sel_ref[0:n_s, :] = jnp.where(score == TAKEN, 0.0, NEG)
        if n_s < nsb:
            sel_ref[n_s:nsb, :] = jnp.full((nsb - n_s, tq), NEG, F32)

    n_classes = 4
    visible_class = (q0 + tq - 1) // (S // n_classes)
    for k in range(n_classes):
        @pl.when(visible_class == k)
        def _():
            compress_and_select((k + 1) * ncb // n_classes)

    head_cols = [slice(r * tq, (r + 1) * tq) for r in range(R)]
    col_groups = [slice(0, R * tq)]
    max_tile = S // tk - 1

    def k_tile(k_ref_, j):
        return k_ref_[0, pl.ds(pl.multiple_of(j * tk, tk), tk), :]

    def vT_tile_of(vT_ref_):
        def vT_tile(j):
            return vT_ref_[0, :, pl.ds(pl.multiple_of(j * tk, tk), tk)]
        return vT_tile

    def softmax_with(bias_of, m_ref_):
        def softmax(s_ref, p_ref, j, tail):
            bias = bias_of(j, tail)
            s = s_ref[...]
            if bias is not None:
                s = s + jnp.concatenate([bias] * R, axis=1)
            return [_softmax_cols(s, m_ref_, p_ref, col_groups[0])]
        return softmax

    def flush(br, acc_ref_):
        for r, cols in enumerate(head_cols):
            den = acc_ref_[V_DIM:V_DIM + 1, cols]
            out_ref[:, cols] += gate(r, br) * (acc_ref_[0:V_DIM, cols] / jnp.where(den > 0, den, 1.0))

    def kpos_of(j):
        return j * tk + lax.broadcasted_iota(jnp.int32, (tk, tq), 0)

    def win_bias(j, tail):
        kpos = kpos_of(j)
        rel = qpos - kpos
        return jnp.where((rel >= 0) & (rel < WINDOW) & (kpos >= 0), 0.0, NEG)

    def win_scores(j):
        return jnp.dot(k_tile(kw_ref, j), qz_ref[...], preferred_element_type=F32)

    n_win = WINDOW // tk + 1
    _pipelined_attention(win_scores, vT_tile_of(vwT_ref), (wsa_ref, wsb_ref), (wpa_ref, wpb_ref), wm_ref, wacc_ref,
                         softmax_with(win_bias, wm_ref), first=jd - (n_win - 1), n_plain_pairs=0, n_tail=n_win,
                         max_tile=max_tile, col_groups=col_groups)
    flush(2, wacc_ref)

    pad_rows = jnp.zeros((BF16_ROWS - per_tile, tq), F32)

    def sel_scores(j):
        rows = [sel_ref[pl.ds(j * per_tile + i, 1), :] for i in range(per_tile)]
        bias = jnp.concatenate(rows + [pad_rows], axis=0).astype(BF16)
        qs_ref[d:d + BF16_ROWS, :] = jnp.concatenate([bias] * R, axis=1)
        return jnp.dot(ks_ref[0, 0, pl.ds(pl.multiple_of(j * tk, tk), tk), :], qs_ref[...],
                       preferred_element_type=F32)

    def sel_bias(j, tail):
        return jnp.where(kpos_of(j) <= qpos, 0.0, NEG) if tail else None

    _pipelined_attention(sel_scores, vT_tile_of(vsT_ref), (sa_ref, sb_ref), (pa_ref, pb_ref), m_ref, acc_ref,
                         softmax_with(sel_bias, m_ref), first=0, n_plain_pairs=jd // 2, n_tail=2,
                         max_tile=max_tile, col_groups=col_groups,
                         tail_is_short=(2 * (jd // 2) + 1) * tk > q0 + tq - 1)
    flush(1, acc_ref)

    for r in range(R):
        o_ref[0, r * d:(r + 1) * d, :] = out_ref[:, r * tq:(r + 1) * tq].astype(o_ref.dtype)


def _nsa_step_kernel(qT_ref, gT_ref, kc_ref, vcT_ref, ks_ref, vsT_ref, kw_ref, vwT_ref, o_ref, *scratch,
                     tq, tk, S, qb):
    def query_block(qq, carry):
        cols = pl.ds(pl.multiple_of(qq * tq, tq), tq)
        _nsa_kernel(qT_ref.at[:, :, cols], gT_ref.at[:, :, cols], kc_ref, vcT_ref, ks_ref, vsT_ref, kw_ref, vwT_ref,
                    o_ref.at[:, :, cols], *scratch, qi=pl.program_id(2) * qb + qq, tq=tq, tk=tk, S=S)
        return carry

    lax.fori_loop(0, qb, query_block, 0)


def _nsa(nsqT, nsgT, kcmp, vcmpT, nsks, nsk, nsvT, tq=128, tk=NSA_TK, qb=4):
    B, _, S = nsqT.shape
    G, R, d = NSA_GROUPS, NSA_REP, NSA_DIM
    ncb = S // CMP_STRIDE
    nsb = S // SEL_BLOCK
    return pl.pallas_call(
        functools.partial(_nsa_step_kernel, tq=tq, tk=tk, S=S, qb=qb),
        grid=(B, G, S // (qb * tq)),
        in_specs=[pl.BlockSpec((1, R * d, qb * tq), lambda b, g, i: (b, g, i)),
                  pl.BlockSpec((1, GATE_PAD, qb * tq), lambda b, g, i: (b, g, i)),
                  pl.BlockSpec((1, 1, ncb, d), lambda b, g, i: (b, g, 0, 0)),
                  pl.BlockSpec((1, 1, d, ncb), lambda b, g, i: (b, g, 0, 0)),
                  pl.BlockSpec((1, 1, S, LANES), lambda b, g, i: (b, g, 0, 0)),
                  pl.BlockSpec((1, V_EXT, S), lambda b, g, i: (b, g, 0)),
                  pl.BlockSpec((1, S, LANES), lambda b, g, i: (b, 0, 1)),
                  pl.BlockSpec((1, V_EXT, S), lambda b, g, i: (b, G + g, 0))],
        out_specs=pl.BlockSpec((1, R * d, qb * tq), lambda b, g, i: (b, g, i)),
        out_shape=jax.ShapeDtypeStruct((B, NSA_WIDTH, S), BF16),
        scratch_shapes=[pltpu.VMEM((d, R * tq), BF16),
                        pltpu.VMEM((LANES, R * tq), BF16),
                        pltpu.VMEM((LANES, R * tq), BF16),
                        pltpu.VMEM((tq // LANES, ncb + 16, LANES), F32),
                        pltpu.VMEM((nsb, tq), F32),
                        pltpu.VMEM((ncb, R * tq), BF16),
                        pltpu.VMEM((tk, R * tq), F32),
                        pltpu.VMEM((tk, R * tq), F32),
                        pltpu.VMEM((tk, R * tq), BF16),
                        pltpu.VMEM((tk, R * tq), BF16),
                        pltpu.VMEM((1, R * tq), F32),
                        pltpu.VMEM((V_EXT, R * tq), F32),
                        pltpu.VMEM((tk, R * tq), F32),
                        pltpu.VMEM((tk, R * tq), F32),
                        pltpu.VMEM((tk, R * tq), BF16),
                        pltpu.VMEM((tk, R * tq), BF16),
                        pltpu.VMEM((1, R * tq), F32),
                        pltpu.VMEM((V_EXT, R * tq), F32),
                        pltpu.VMEM((d, R * tq), F32)],
        compiler_params=_cparams(("parallel", "parallel", "arbitrary")),
        name="nsa",
    )(nsqT, nsgT, kcmp, vcmpT, nsks, nsvT, nsk, nsvT)


def _mix_ffn_kernel(x_ref, yml_ref, ydaT_ref, ynsT_ref, wo_ref, g_ref, w1_ref, w2_ref, gf_ref, o_ref,
                    hb_ref, acc_ref, *, final):
    j = pl.program_id(2)

    @pl.when(j == 0)
    def _():
        x1 = x_ref[0] + jnp.dot(yml_ref[0].astype(BF16), wo_ref[0:ML_WIDTH, :], preferred_element_type=F32)
        x1 = x1 + lax.dot_general(ydaT_ref[0], wo_ref[ML_WIDTH:ML_WIDTH + DA_WIDTH, :], _TN,
                                  preferred_element_type=F32)
        x1 = x1 + lax.dot_general(ynsT_ref[0], wo_ref[ML_WIDTH + DA_WIDTH:, :], _TN,
                                  preferred_element_type=F32)
        hb_ref[...] = _rms(x1, g_ref[...]).astype(BF16)
        acc_ref[...] = x1

    u = jnp.dot(hb_ref[...], w1_ref[...], preferred_element_type=F32)
    a = jnp.square(jnp.maximum(u, 0.0)).astype(BF16)
    acc_ref[...] += jnp.dot(a, w2_ref[...], preferred_element_type=F32)

    @pl.when(j == pl.num_programs(2) - 1)
    def _():
        y = acc_ref[...]
        if final:
            y = _rms(y, gf_ref[...])
        o_ref[0] = y


def _mix_ffn(x, yml, ydaT, ynsT, wo, g, w1, w2, gf, final, tm=1024, tf=1024):
    B, S, D = x.shape
    return pl.pallas_call(
        functools.partial(_mix_ffn_kernel, final=final),
        grid=(B, S // tm, D_FF // tf),
        in_specs=[pl.BlockSpec((1, tm, D), lambda b, i, j: (b, i, 0)),
                  pl.BlockSpec((1, tm, ML_WIDTH), lambda b, i, j: (b, i, 0)),
                  pl.BlockSpec((1, DA_WIDTH, tm), lambda b, i, j: (b, 0, i)),
                  pl.BlockSpec((1, NSA_WIDTH, tm), lambda b, i, j: (b, 0, i)),
                  pl.BlockSpec((D, D), lambda b, i, j: (0, 0)),
                  pl.BlockSpec((1, D), lambda b, i, j: (0, 0)),
                  pl.BlockSpec((D, tf), lambda b, i, j: (0, j)),
                  pl.BlockSpec((tf, D), lambda b, i, j: (j, 0)),
                  pl.BlockSpec((1, D), lambda b, i, j: (0, 0))],
        out_specs=pl.BlockSpec((1, tm, D), lambda b, i, j: (b, i, 0)),
        out_shape=jax.ShapeDtypeStruct((B, S, D), F32),
        scratch_shapes=[pltpu.VMEM((tm, D), BF16), pltpu.VMEM((tm, D), F32)],
        compiler_params=_cparams(("parallel", "parallel", "arbitrary")),
        name="mix_ffn",
    )(x, yml, ydaT, ynsT, wo, g, w1, w2, gf)


def _split_w_in(w_in_l):
    edges = np.concatenate([[0], np.cumsum(IN_SIZES)])
    return [w_in_l[:, int(edges[i]):int(edges[i + 1])] for i in range(len(IN_SIZES))]


def _pad_cols(w, n):
    return jnp.pad(w, ((0, 0), (0, n - w.shape[1])))


def _inproj_weights(w_in_l):
    (ml_q, ml_k, ml_v, ml_o, ml_i, ml_f, da_q, da_k, da_v,
     ns_q, ns_kc, ns_vc, ns_ks, ns_vs, ns_kw, ns_vw, ns_g) = _split_w_in(w_in_l)
    wt = jnp.concatenate([ml_q, ml_k, ml_v, ml_o, _pad_cols(jnp.concatenate([ml_i, ml_f], 1), LANES),
                          da_k, ns_ks, ns_kw, ns_kc, ns_vc], axis=1)
    per_group = NSA_REP * 3
    ns_g_pad = jnp.concatenate(
        [_pad_cols(ns_g[:, gi * per_group:(gi + 1) * per_group], GATE_PAD) for gi in range(NSA_GROUPS)], axis=1)
    wf = jnp.concatenate([da_q, da_v, ns_q, ns_vs, ns_vw, ns_g_pad], axis=1)
    return wt.astype(BF16), wf.T.astype(BF16)


def kernel(x, norm1, w_in, ml_conv, ml_gate_bias, ml_norm, da_lambda, da_norm, nsa_pe,
           nsa_w1, nsa_w2, w_out, norm2, w_ff1, w_ff2, final_norm):
    B, S, D = x.shape
    depth = norm1.shape[0]
    for l in range(depth):
        wt, wfT = _inproj_weights(w_in[l])
        (zml, zg, dak, nsk, nsks, x16, daqT, davT, nsqT, nsvT, nsgT) = _inproj(x, norm1[l][None, :], wt, wfT)

        yml = _mlstm(zml, zg, ml_conv[l], _pad_cols(ml_gate_bias[l][None, :], LANES), ml_norm[l][None, :])

        lam_init = 0.8 - 0.6 * math.exp(-0.3 * l)
        ydaT = _diff_attn(da_lambda[l], daqT, dak, davT, da_norm[l][:, None], lam_init)

        pe2 = nsa_pe[l].reshape(2, 2, CMP_STRIDE * NSA_DIM)
        w1b = nsa_w1[l].astype(BF16)
        kcmp = _compress(x16, pe2, w1b, nsa_w2[l], 0, False)
        vcmpT = _compress(x16, pe2, w1b, nsa_w2[l], 1, True)
        ynsT = _nsa(nsqT, nsgT, kcmp, vcmpT, nsks, nsk, nsvT)

        x = _mix_ffn(x, yml, ydaT, ynsT, w_out[l].astype(BF16), norm2[l][None, :], w_ff1[l].astype(BF16),
                     w_ff2[l].astype(BF16), final_norm[None, :], final=(l == depth - 1))
    return x
```

```python
import functools
import math

import numpy as np
import jax
import jax.numpy as jnp
from jax import lax
from jax.experimental import pallas as pl
from jax.experimental.pallas import tpu as pltpu

F32 = jnp.float32
BF16 = jnp.bfloat16

D_MODEL = 1024
ML_HEADS = 4
ML_DIM = 64
ML_WIDTH = ML_HEADS * ML_DIM
ML_TILE_CHUNK = 128
CONV_W = 4
DA_HEADS = 4
DA_QK_DIM = 32
DA_V_DIM = 64
DA_WIDTH = DA_HEADS * DA_V_DIM
NSA_HEADS = 8
NSA_GROUPS = 2
NSA_REP = NSA_HEADS // NSA_GROUPS
NSA_DIM = 64
NSA_WIDTH = NSA_HEADS * NSA_DIM
NSA_KV = NSA_GROUPS * NSA_DIM
CMP_BLOCK = 32
CMP_STRIDE = 16
CMP_HIDDEN = 4 * NSA_DIM
SEL_BLOCK = 64
SEL_TOPK = 16
WINDOW = 512
D_FF = 4 * D_MODEL
EPS = 1e-6
FORCE_SCORE = 1e4
IN_SIZES = (ML_WIDTH, ML_WIDTH, ML_WIDTH, ML_WIDTH, ML_HEADS, ML_HEADS,
            2 * DA_HEADS * DA_QK_DIM, 2 * DA_HEADS * DA_QK_DIM, DA_WIDTH,
            NSA_WIDTH, NSA_KV, NSA_KV, NSA_KV, NSA_KV, NSA_KV, NSA_KV, 3 * NSA_HEADS)

LANES = 128
GATE_PAD = 16
NEG = -1e30
M_INIT = -1e29
LOOP_PAIRS = 4
TAKEN = -3e38
LOG2E = 1.4426950408889634
V_DIM = 64
BF16_ROWS = 16
NSA_TK = 256
V_EXT = 80
ACC_ROWS = V_DIM + 8
VMEM_LIMIT = 56 * 1024 * 1024

_T_ML = (0, 1024)
_T_MLG = (1024, 1152)
_T_DAK = (1152, 1408)
_T_NSK = (1408, 1664)
_T_NSC = (1664, 1920)
_T_COLS = 1920
_F_DAQ = (0, 256)
_F_DAV = (256, 512)
_F_NSQ = (512, 1024)
_F_NSV = (1024, 1280)
_F_NSG = (1280, 1280 + NSA_GROUPS * GATE_PAD)
_F_ROWS = _F_NSG[1]

_NT = (((1,), (1,)), ((), ()))
_TN = (((0,), (0,)), ((), ()))


def _cparams(sem):
    return pltpu.CompilerParams(dimension_semantics=sem, vmem_limit_bytes=VMEM_LIMIT)


def _rms(x, g):
    return x * lax.rsqrt(jnp.mean(x * x, axis=-1, keepdims=True) + EPS) * g


def _inproj_kernel(x_ref, g_ref, wt_ref, wf_ref,
                   zml_ref, zg_ref, dak_ref, nsk_ref, nsks_ref, x16_ref,
                   daqT_ref, davT_ref, nsqT_ref, nsvT_ref, nsgT_ref, nsc_ref):
    hb = _rms(x_ref[0], g_ref[...]).astype(BF16)

    def tdot(span):
        return jnp.dot(hb, wt_ref[:, span[0]:span[1]], preferred_element_type=F32)

    def fdot(span):
        return lax.dot_general(wf_ref[span[0]:span[1], :], hb, _NT, preferred_element_type=F32)

    zml_ref[0] = tdot(_T_ML)
    zg_ref[0] = tdot(_T_MLG)
    dak = tdot(_T_DAK).astype(BF16)
    for h in range(DA_HEADS):
        dak_ref[0, h] = dak[:, h * 2 * DA_QK_DIM:(h + 1) * 2 * DA_QK_DIM]
    nsk = tdot(_T_NSK).astype(BF16)
    nsk_ref[0] = nsk
    tm_ = nsk.shape[0]
    row = lax.broadcasted_iota(jnp.int32, (tm_, NSA_DIM), 0)
    lane = lax.broadcasted_iota(jnp.int32, (tm_, NSA_DIM), 1)
    blk_in_tile = lax.shift_right_logical(row & (NSA_TK - 1), int(math.log2(SEL_BLOCK)))
    onehot = jnp.where(lane == blk_in_tile, 1.0, 0.0).astype(BF16)
    for gi in range(NSA_GROUPS):
        nsks_ref[0, gi] = jnp.concatenate([nsk[:, gi * NSA_DIM:(gi + 1) * NSA_DIM], onehot], axis=1)
    nsc = tdot(_T_NSC)
    for half in range(2):
        nsc_ref[half] = nsc[:, half * LANES:(half + 1) * LANES]
    for i in range(CMP_STRIDE):
        for half in range(2):
            rows = nsc_ref[half, pl.ds(i, tm_ // CMP_STRIDE, stride=CMP_STRIDE), :]
            for w in range(2):
                x16_ref[0, 2 * half + w, :, i * NSA_DIM:(i + 1) * NSA_DIM] = rows[:, w * NSA_DIM:(w + 1) * NSA_DIM]
    daqT_ref[0] = (fdot(_F_DAQ) * (DA_QK_DIM ** -0.5 * LOG2E)).astype(BF16)
    nsqT_ref[0] = (fdot(_F_NSQ) * (NSA_DIM ** -0.5 * LOG2E)).astype(BF16)
    nsgT_ref[0] = fdot(_F_NSG)
    tm = hb.shape[0]
    for v_ref, span in ((davT_ref, _F_DAV), (nsvT_ref, _F_NSV)):
        v = fdot(span).astype(BF16)
        for h in range((span[1] - span[0]) // V_DIM):
            v_ref[0, h * V_EXT:h * V_EXT + V_DIM, :] = v[h * V_DIM:(h + 1) * V_DIM, :]
            v_ref[0, h * V_EXT + V_DIM:(h + 1) * V_EXT, :] = jnp.ones((V_EXT - V_DIM, tm), BF16)


def _inproj(x, g, wt, wfT, tm=512):
    B, S, D = x.shape
    tok = lambda w, dt: jax.ShapeDtypeStruct((B, S, w), dt)
    feat = lambda r, dt: jax.ShapeDtypeStruct((B, r, S), dt)
    tspec = lambda w: pl.BlockSpec((1, tm, w), lambda b, i: (b, i, 0))
    fspec = lambda r: pl.BlockSpec((1, r, tm), lambda b, i: (b, 0, i))
    return pl.pallas_call(
        _inproj_kernel,
        grid=(B, S // tm),
        in_specs=[tspec(D),
                  pl.BlockSpec((1, D), lambda b, i: (0, 0)),
                  pl.BlockSpec((D, _T_COLS), lambda b, i: (0, 0)),
                  pl.BlockSpec((_F_ROWS, D), lambda b, i: (0, 0))],
        out_specs=[tspec(1024), tspec(128),
                   pl.BlockSpec((1, DA_HEADS, tm, 2 * DA_QK_DIM), lambda b, i: (b, 0, i, 0)), tspec(256),
                   pl.BlockSpec((1, NSA_GROUPS, tm, LANES), lambda b, i: (b, 0, i, 0)),
                   pl.BlockSpec((1, 2 * NSA_GROUPS, tm // CMP_STRIDE, CMP_STRIDE * NSA_DIM), lambda b, i: (b, 0, i, 0)),
                   fspec(256), fspec(4 * V_EXT), fspec(512), fspec(4 * V_EXT), fspec(NSA_GROUPS * GATE_PAD)],
        out_shape=[tok(1024, F32), tok(128, F32),
                   jax.ShapeDtypeStruct((B, DA_HEADS, S, 2 * DA_QK_DIM), BF16), tok(256, BF16),
                   jax.ShapeDtypeStruct((B, NSA_GROUPS, S, LANES), BF16),
                   jax.ShapeDtypeStruct((B, 2 * NSA_GROUPS, S // CMP_STRIDE, CMP_STRIDE * NSA_DIM), F32),
                   feat(256, BF16), feat(4 * V_EXT, BF16), feat(512, BF16), feat(4 * V_EXT, BF16),
                   feat(NSA_GROUPS * GATE_PAD, F32)],
        scratch_shapes=[pltpu.VMEM((2, tm, LANES), F32)],
        compiler_params=_cparams(("parallel", "parallel")),
        name="inproj",
    )(x, g, wt, wfT)


def _log_sigmoid(x):
    return jnp.minimum(x, 0.0) - jnp.log1p(jnp.exp(-jnp.abs(x)))


def _sigmoid(x):
    return 1.0 / (1.0 + jnp.exp(-x))


def _mlstm_kernel(zml_ref, zg_ref, cw_ref, gb_ref, nrm_ref, y_ref,
                  pad_ref, c_ref, n_ref, m_ref, *, T, L):
    d = ML_DIM
    W = max(L, d)
    t = pl.program_id(1)

    @pl.when(t == 0)
    def _():
        pad_ref[0:8, :] = jnp.zeros((8, 2 * ML_WIDTH), F32)
        c_ref[...] = jnp.zeros_like(c_ref)
        n_ref[...] = jnp.zeros_like(n_ref)
        m_ref[...] = jnp.zeros_like(m_ref)

    @pl.when(t > 0)
    def _():
        pad_ref[0:8, :] = pad_ref[T:T + 8, :]

    pad_ref[8:8 + T, :] = zml_ref[0, :, 0:2 * ML_WIDTH]
    conv = cw_ref[0:1, :] * pad_ref[5:5 + T, :]
    for j in range(1, CONV_W):
        conv = conv + cw_ref[j:j + 1, :] * pad_ref[5 + j:5 + j + T, :]
    qk = conv * _sigmoid(conv)
    gates = zg_ref[0] + gb_ref[...]
    logf = _log_sigmoid(gates)

    row = lax.broadcasted_iota(jnp.int32, (L, L), 0)
    col = lax.broadcasted_iota(jnp.int32, (L, L), 1)
    causal = col <= row
    tril = causal.astype(F32)
    triu = (row <= col).astype(F32)

    lane_of = lax.broadcasted_iota(jnp.int32, (LANES, W), 0)
    pick = [(lane_of == g).astype(F32) for g in range(2 * ML_HEADS)]

    H = range(ML_HEADS)

    def local_part(c):
        r0 = c * L
        g_c = gates[r0:r0 + L, :]
        lf_c = logf[r0:r0 + L, :]
        b_cols = jnp.dot(tril, lf_c, preferred_element_type=F32,
                         precision=lax.Precision.HIGHEST)
        b_rows = jnp.dot(lf_c.T, triu, preferred_element_type=F32,
                         precision=lax.Precision.HIGHEST)
        g_rows = g_c.T
        b_col = [jnp.dot(b_cols, pick[ML_HEADS + h], preferred_element_type=F32,
                         precision=lax.Precision.HIGHEST) for h in H]
        ig_col = [jnp.dot(g_c, pick[h], preferred_element_type=F32,
                          precision=lax.Precision.HIGHEST) for h in H]
        b_row = [b_rows[ML_HEADS + h:ML_HEADS + h + 1, :] for h in H]
        ig_row = [g_rows[h:h + 1, :] for h in H]
        qh = [qk[r0:r0 + L, h * d:(h + 1) * d] for h in H]
        kh = [qk[r0:r0 + L, ML_WIDTH + h * d:ML_WIDTH + (h + 1) * d] * (d ** -0.5) for h in H]
        vh = [zml_ref[0, r0:r0 + L, 2 * ML_WIDTH + h * d:2 * ML_WIDTH + (h + 1) * d] for h in H]
        qkt = [lax.dot_general(qh[h], kh[h], _NT, preferred_element_type=F32) for h in H]
        g_tot = [b_col[h][L - 1:L, :] for h in H]
        a_max = [jnp.max(g_tot[h][:, 0:L] - b_row[h] + ig_row[h], axis=1, keepdims=True) for h in H]
        w_col = [jnp.exp((g_tot[h] - b_col[h] + ig_col[h] - a_max[h])[:, 0:d]) for h in H]
        c_loc = [lax.dot_general(vh[h] * w_col[h], kh[h], _TN, preferred_element_type=F32) for h in H]
        n_loc = [jnp.sum(kh[h] * w_col[h], axis=0, keepdims=True) for h in H]
        dmat = [jnp.where(causal, b_col[h][:, 0:L] - b_row[h] + ig_row[h], NEG) for h in H]
        d_max = [jnp.max(dmat[h], axis=1, keepdims=True) for h in H]
        return dict(b_col=b_col, qh=qh, vh=vh, qkt=qkt, g_tot=g_tot, a_max=a_max, c_loc=c_loc, n_loc=n_loc,
                    dmat=dmat, d_max=d_max)

    def carried_part(c, lp):
        r0 = c * L
        b_col, qh, vh, qkt, g_tot, a_max = lp["b_col"], lp["qh"], lp["vh"], lp["qkt"], lp["g_tot"], lp["a_max"]
        oh = [zml_ref[0, r0:r0 + L, 3 * ML_WIDTH + h * d:3 * ML_WIDTH + (h + 1) * d] for h in H]
        c_prev = [c_ref[h] for h in H]
        n_prev = [n_ref[h, 0:1, :] for h in H]
        m_prev = [m_ref[h, 0:1, :] for h in H]
        q_c = [lax.dot_general(qh[h], c_prev[h], _NT, preferred_element_type=F32) for h in H]
        inter_log = [b_col[h] + m_prev[h] for h in H]
        m_t = [jnp.maximum(inter_log[h], lp["d_max"][h]) for h in H]
        wts = [jnp.exp(lp["dmat"][h] - m_t[h][:, 0:L]) * qkt[h] for h in H]
        s_inter = [jnp.exp((inter_log[h] - m_t[h])[:, 0:d]) for h in H]
        num = [jnp.dot(wts[h], vh[h], preferred_element_type=F32) + s_inter[h] * q_c[h] for h in H]
        den = [jnp.sum(wts[h], axis=1, keepdims=True)
               + s_inter[h] * jnp.sum(qh[h] * n_prev[h], axis=1, keepdims=True) for h in H]
        hh = [num[h] / jnp.maximum(jnp.abs(den[h]), jnp.exp(-m_t[h][:, 0:d])) for h in H]
        m_new = [jnp.maximum(g_tot[h] + m_prev[h], a_max[h]) for h in H]
        for h in H:
            s_prev = jnp.exp((g_tot[h] + m_prev[h] - m_new[h])[:, 0:d])
            s_loc = jnp.exp((a_max[h] - m_new[h])[:, 0:d])
            c_ref[h] = s_prev * c_prev[h] + s_loc * lp["c_loc"][h]
            n_ref[h] = jnp.broadcast_to(s_prev * n_prev[h] + s_loc * lp["n_loc"][h], (8, d))
            m_ref[h] = jnp.broadcast_to(m_new[h], (8, W))
        for h in H:
            yh = _sigmoid(oh[h]) * hh[h]
            y_ref[0, r0:r0 + L, h * d:(h + 1) * d] = _rms(yh, nrm_ref[0:1, h * d:(h + 1) * d])

    for c in range(T // L):
        carried_part(c, local_part(c))


def _mlstm(zml, zg, conv_w, gate_bias, norm_g, T=256, chunk=ML_TILE_CHUNK):
    B, S, _ = zml.shape
    return pl.pallas_call(
        functools.partial(_mlstm_kernel, T=T, L=chunk),
        grid=(B, S // T),
        in_specs=[pl.BlockSpec((1, T, 1024), lambda b, t: (b, t, 0)),
                  pl.BlockSpec((1, T, 128), lambda b, t: (b, t, 0)),
                  pl.BlockSpec((CONV_W, 2 * ML_WIDTH), lambda b, t: (0, 0)),
                  pl.BlockSpec((1, 128), lambda b, t: (0, 0)),
                  pl.BlockSpec((1, ML_WIDTH), lambda b, t: (0, 0))],
        out_specs=pl.BlockSpec((1, T, ML_WIDTH), lambda b, t: (b, t, 0)),
        out_shape=jax.ShapeDtypeStruct((B, S, ML_WIDTH), F32),
        scratch_shapes=[pltpu.VMEM((T + 8, 2 * ML_WIDTH), F32),
                        pltpu.VMEM((ML_HEADS, ML_DIM, ML_DIM), F32),
                        pltpu.VMEM((ML_HEADS, 8, ML_DIM), F32),
                        pltpu.VMEM((ML_HEADS, 8, max(chunk, ML_DIM)), F32)],
        compiler_params=_cparams(("parallel", "arbitrary")),
        name="mlstm",
    )(zml, zg, conv_w, gate_bias, norm_g)


def _softmax_cols(s, m_ref, p_ref, cols):
    m_old = m_ref[:, cols]
    m_new = jnp.maximum(m_old, jnp.max(s, axis=0, keepdims=True))
    m_ref[:, cols] = m_new
    p_ref[:, cols] = jnp.exp2(s - m_new).astype(BF16)
    return jnp.exp2(m_old - m_new)


def _pipelined_attention(scores, vT_tile, s_refs, p_refs, m_ref, acc_ref, softmax,
                         first, n_plain_pairs, n_tail, max_tile, col_groups, tail_is_short=None):
    sa, sb = s_refs
    pa, pb = p_refs

    def load(j):
        return jnp.clip(j, 0, max_tile)

    def half(j, s_cur, s_nxt, p_cur, p_prev, tail):
        s_nxt[...] = scores(load(j + 1))
        pv = jnp.dot(vT_tile(load(j - 1)), p_prev[...], preferred_element_type=F32)
        alphas = softmax(s_cur, p_cur, j, tail)
        for cols, alpha in zip(col_groups, alphas):
            acc_ref[:, cols] = alpha * (acc_ref[:, cols] + pv[0:ACC_ROWS, cols])

    def pair(j, tail):
        half(j, sa, sb, pa, pb, tail)
        half(j + 1, sb, sa, pb, pa, tail)

    m_ref[...] = jnp.full_like(m_ref, M_INIT)
    acc_ref[...] = jnp.zeros_like(acc_ref)
    pb[...] = jnp.zeros_like(pb)
    sa[...] = scores(load(first))

    if not (isinstance(n_plain_pairs, int) and n_plain_pairs == 0):
        n_trips = n_plain_pairs // LOOP_PAIRS

        def body(i, carry):
            for u in range(LOOP_PAIRS):
                pair(first + 2 * (LOOP_PAIRS * i + u), False)
            return carry

        lax.fori_loop(0, n_trips, body, 0)
        rest = n_plain_pairs - n_trips * LOOP_PAIRS
        for u in range(LOOP_PAIRS - 1):
            @pl.when(u < rest)
            def _():
                pair(first + 2 * (LOOP_PAIRS * n_trips + u), False)
    j = first + 2 * n_plain_pairs
    bufs = ((sa, sb, pa, pb), (sb, sa, pb, pa))

    def tail(n):
        for t in range(n):
            half(j + t, *bufs[t % 2], True)
        p_last = bufs[(n - 1) % 2][2]
        acc_ref[...] += jnp.dot(vT_tile(load(j + n - 1)), p_last[...], preferred_element_type=F32)[0:ACC_ROWS]

    if tail_is_short is None:
        tail(n_tail)
    else:
        @pl.when(tail_is_short)
        def _():
            tail(n_tail - 1)

        @pl.when(jnp.logical_not(tail_is_short))
        def _():
            tail(n_tail)


def _da_kernel(lam_ref, qT_ref, k_ref, vT_ref, gain_ref, o_ref,
               qz_ref, sa_ref, sb_ref, pa_ref, pb_ref, m_ref, acc_ref, *, tq, tk, qb, lam_init, S):
    d = DA_QK_DIM
    col_groups = [slice(mp * tq, (mp + 1) * tq) for mp in range(2)]

    def scores(j):
        return jnp.dot(k_ref[0, 0, pl.ds(pl.multiple_of(j * tk, tk), tk), :], qz_ref[...],
                       preferred_element_type=F32)

    def vT_tile(j):
        return vT_ref[0, :, pl.ds(pl.multiple_of(j * tk, tk), tk)]

    lp = lam_ref[...]
    lam = (jnp.exp(jnp.sum(lp[0:1] * lp[1:2], axis=1, keepdims=True))
           - jnp.exp(jnp.sum(lp[2:3] * lp[3:4], axis=1, keepdims=True)) + lam_init)

    def query_block(qq, carry):
        qi = pl.program_id(2) * qb + qq
        jd = (qi * tq) // tk
        cols_q = pl.ds(pl.multiple_of(qq * tq, tq), tq)

        qz_ref[...] = jnp.zeros_like(qz_ref)
        qz_ref[0:d, 0:tq] = qT_ref[0, 0:d, cols_q]
        qz_ref[d:2 * d, tq:2 * tq] = qT_ref[0, d:2 * d, cols_q]

        def softmax(s_ref, p_ref, j, tail):
            alphas = []
            for cols in col_groups:
                s = s_ref[:, cols]
                if tail:
                    kpos = j * tk + lax.broadcasted_iota(jnp.int32, (tk, tq), 0)
                    qpos = qi * tq + lax.broadcasted_iota(jnp.int32, (tk, tq), 1)
                    s = jnp.where(kpos <= qpos, s, NEG)
                alphas.append(_softmax_cols(s, m_ref, p_ref, cols))
            return alphas

        _pipelined_attention(scores, vT_tile, (sa_ref, sb_ref), (pa_ref, pb_ref), m_ref, acc_ref, softmax,
                             first=0, n_plain_pairs=jd // 2, n_tail=2, max_tile=S // tk - 1,
                             col_groups=col_groups, tail_is_short=(2 * (jd // 2) + 1) * tk > qi * tq + tq - 1)

        o1 = acc_ref[0:V_DIM, 0:tq] / acc_ref[V_DIM:V_DIM + 1, 0:tq]
        o2 = acc_ref[0:V_DIM, tq:2 * tq] / acc_ref[V_DIM:V_DIM + 1, tq:2 * tq]
        o = o1 - lam * o2
        y = o * lax.rsqrt(jnp.mean(o * o, axis=0, keepdims=True) + EPS) * gain_ref[...]
        o_ref[0, :, cols_q] = (y * (1.0 - lam_init)).astype(o_ref.dtype)
        return carry

    lax.fori_loop(0, qb, query_block, 0)


def _diff_attn(da_lambda, daqT, dak, davT, gain_col, lam_init, tq=256, tk=256, qb=4):
    B, _, S = daqT.shape
    return pl.pallas_call(
        functools.partial(_da_kernel, tq=tq, tk=tk, qb=qb, lam_init=lam_init, S=S),
        grid=(B, DA_HEADS, S // (qb * tq)),
        in_specs=[pl.BlockSpec((4, DA_QK_DIM), lambda b, h, i: (0, 0)),
                  pl.BlockSpec((1, DA_V_DIM, qb * tq), lambda b, h, i: (b, h, i)),
                  pl.BlockSpec((1, 1, S, 2 * DA_QK_DIM), lambda b, h, i: (b, h, 0, 0)),
                  pl.BlockSpec((1, V_EXT, S), lambda b, h, i: (b, h, 0)),
                  pl.BlockSpec((DA_V_DIM, 1), lambda b, h, i: (h, 0))],
        out_specs=pl.BlockSpec((1, DA_V_DIM, qb * tq), lambda b, h, i: (b, h, i)),
        out_shape=jax.ShapeDtypeStruct((B, DA_WIDTH, S), BF16),
        scratch_shapes=[pltpu.VMEM((2 * DA_QK_DIM, 2 * tq), BF16),
                        pltpu.VMEM((tk, 2 * tq), F32),
                        pltpu.VMEM((tk, 2 * tq), F32),
                        pltpu.VMEM((tk, 2 * tq), BF16),
                        pltpu.VMEM((tk, 2 * tq), BF16),
                        pltpu.VMEM((1, 2 * tq), F32),
                        pltpu.VMEM((ACC_ROWS, 2 * tq), F32)],
        compiler_params=_cparams(("parallel", "parallel", "arbitrary")),
        name="diff_attn",
    )(da_lambda, daqT, dak, davT, gain_col)


def _gelu_tanh(x):
    return x * (0.5 * (1.0 + jnp.tanh(math.sqrt(2.0 / math.pi) * (x + 0.044715 * (x * x * x)))))


def _compress_kernel(x_ref, pe_ref, w1_ref, w2_ref, o_ref, b_ref, *, feature_major):
    n = x_ref.shape[2]
    half = CMP_STRIDE * NSA_DIM
    x = x_ref[0, 0]
    a = jnp.dot((x + pe_ref[0, 0:1, :]).astype(BF16), w1_ref[0, 0:half, :], preferred_element_type=F32)
    b_ref[0:n, :] = jnp.dot((x + pe_ref[0, 1:2, :]).astype(BF16), w1_ref[0, half:2 * half, :],
                            preferred_element_type=F32)
    b_ref[n:n + 8, :] = jnp.zeros((8, CMP_HIDDEN), F32)
    hid = _gelu_tanh(a + b_ref[1:n + 1, :]).astype(BF16)
    if feature_major:
        o_ref[0, 0] = lax.dot_general(w2_ref[0], hid, _NT, preferred_element_type=F32).astype(o_ref.dtype)
    else:
        o_ref[0, 0] = jnp.dot(hid, w2_ref[0], preferred_element_type=F32).astype(o_ref.dtype)


def _compress(x16, pe2, w1, w2, which, feature_major):
    B, _, n, half = x16.shape
    G = NSA_GROUPS
    if feature_major:
        out_shape, out_block = (B, G, NSA_DIM, n), (1, 1, NSA_DIM, n)
        w2_arr, w2_block = jnp.swapaxes(w2, 1, 2), (1, NSA_DIM, CMP_HIDDEN)
    else:
        out_shape, out_block = (B, G, n, NSA_DIM), (1, 1, n, NSA_DIM)
        w2_arr, w2_block = w2, (1, CMP_HIDDEN, NSA_DIM)
    return pl.pallas_call(
        functools.partial(_compress_kernel, feature_major=feature_major),
        grid=(B, G),
        in_specs=[pl.BlockSpec((1, 1, n, half), lambda b, g: (b, which * G + g, 0, 0)),
                  pl.BlockSpec((1, 2, half), lambda b, g: (which, 0, 0)),
                  pl.BlockSpec((1, 2 * half, CMP_HIDDEN), lambda b, g: (which, 0, 0)),
                  pl.BlockSpec(w2_block, lambda b, g: (which, 0, 0))],
        out_specs=pl.BlockSpec(out_block, lambda b, g: (b, g, 0, 0)),
        out_shape=jax.ShapeDtypeStruct(out_shape, BF16),
        scratch_shapes=[pltpu.VMEM((n + 8, CMP_HIDDEN), F32)],
        compiler_params=_cparams(("parallel", "parallel")),
        name="compress_v" if feature_major else "compress_k",
    )(x16, pe2, w1, w2_arr.astype(BF16))


def _nsa_kernel(qT_ref, gT_ref, kc_ref, vcT_ref, ks_ref, vsT_ref, kw_ref, vwT_ref, o_ref,
                qg_ref, qz_ref, qs_ref, imp_ref, sel_ref, pc_ref, sa_ref, sb_ref, pa_ref, pb_ref, m_ref, acc_ref,
                wsa_ref, wsb_ref, wpa_ref, wpb_ref, wm_ref, wacc_ref, out_ref,
                *, qi, tq, tk, S):
    g = pl.program_id(1)
    d = NSA_DIM
    R = NSA_REP
    ncb = S // CMP_STRIDE
    nsb = S // SEL_BLOCK
    q0 = qi * tq
    jd = q0 // tk
    ratio = SEL_BLOCK // CMP_STRIDE
    per_tile = tk // SEL_BLOCK

    for r in range(R):
        qg_ref[:, r * tq:(r + 1) * tq] = qT_ref[0, r * d:(r + 1) * d, :]
    qz_ref[...] = jnp.zeros_like(qz_ref)
    qs_ref[...] = jnp.zeros_like(qs_ref)
    qs_ref[0:d, :] = qg_ref[...]
    for gg in range(NSA_GROUPS):
        @pl.when(g == gg)
        def _():
            qz_ref[gg * d:(gg + 1) * d, :] = qg_ref[...]

    def gate(r, br):
        return _sigmoid(gT_ref[0, r * 3 + br:r * 3 + br + 1, :])

    qpos = q0 + lax.broadcasted_iota(jnp.int32, (1, tq), 1)

    def compress_and_select(n_c):
        n_s = n_c // ratio
        s_all = jnp.dot(kc_ref[0, 0, 0:n_c, :], qg_ref[...], preferred_element_type=F32)
        cend = lax.broadcasted_iota(jnp.int32, (n_c, tq), 0) * CMP_STRIDE + (CMP_BLOCK - 1)
        cbias = jnp.where(cend <= qpos, 0.0, NEG)
        hs = range(R)
        sc = [s_all[:, r * tq:(r + 1) * tq] + cbias for r in hs]
        mx = [jnp.maximum(jnp.max(sc[r], axis=0, keepdims=True), M_INIT) for r in hs]
        pu = [jnp.exp2(sc[r] - mx[r]) for r in hs]
        den = [jnp.sum(pu[r], axis=0, keepdims=True) for r in hs]
        pn = [pu[r] / jnp.where(den[r] > 0, den[r], 1.0) for r in hs]
        imp = pn[0]
        for r in range(1, R):
            imp = imp + pn[r]
        for r in hs:
            pc_ref[0:n_c, r * tq:(r + 1) * tq] = pn[r].astype(BF16)
        o_cmp = jnp.dot(vcT_ref[0, 0, :, 0:n_c], pc_ref[0:n_c, :], preferred_element_type=F32)
        for r in range(R):
            cols = slice(r * tq, (r + 1) * tq)
            out_ref[:, cols] = gate(r, 0) * o_cmp[:, cols]

        slabs = []
        for c in range(tq // LANES):
            imp_ref[c, 0:8, :] = jnp.zeros((8, LANES), F32)
            imp_ref[c, 8:8 + n_c, :] = imp[:, c * LANES:(c + 1) * LANES]
            imp_ref[c, 8 + n_c:16 + n_c, :] = jnp.zeros((8, LANES), F32)
            slab = jnp.zeros((n_s, LANES), F32)
            for o in range(-1, ratio):
                slab = slab + imp_ref[c, pl.ds(8 + o, n_s, stride=ratio), :]
            slabs.append(slab)
        p_slc = slabs[0] if len(slabs) == 1 else jnp.concatenate(slabs, axis=1)
        blk = lax.broadcasted_iota(jnp.int32, (n_s, tq), 0)
        cur = lax.shift_right_logical(qpos, int(math.log2(SEL_BLOCK)))
        forced = (blk == 0) | (blk == cur) | (blk == cur - 1)
        causal_blk = blk * SEL_BLOCK <= qpos
        score = jnp.where(forced, FORCE_SCORE, jnp.where(causal_blk, p_slc, -1.0))
        blk_f = blk.astype(F32)
        for _ in range(SEL_TOPK):
            mx_s = jnp.max(score, axis=0, keepdims=True)
            first = jnp.min(jnp.where(score == mx_s, blk_f, float(nsb)), axis=0, keepdims=True)
            score = jnp.where(blk_f == first, TAKEN, score)
        sel_ref[0:n_s, :] = jnp.where(score == TAKEN, 0.0, NEG)
        if n_s < nsb:
            sel_ref[n_s:nsb, :] = jnp.full((nsb - n_s, tq), NEG, F32)

    n_classes = 4
    visible_class = (q0 + tq - 1) // (S // n_classes)
    for k in range(n_classes):
        @pl.when(visible_class == k)
        def _():
            compress_and_select((k + 1) * ncb // n_classes)

    head_cols = [slice(r * tq, (r + 1) * tq) for r in range(R)]
    col_groups = [slice(0, R * tq)]
    max_tile = S // tk - 1

    def k_tile(k_ref_, j):
        return k_ref_[0, pl.ds(pl.multiple_of(j * tk, tk), tk), :]

    def vT_tile_of(vT_ref_):
        def vT_tile(j):
            return vT_ref_[0, :, pl.ds(pl.multiple_of(j * tk, tk), tk)]
        return vT_tile

    def softmax_with(bias_of, m_ref_):
        def softmax(s_ref, p_ref, j, tail):
            bias = bias_of(j, tail)
            s = s_ref[...]
            if bias is not None:
                s = s + jnp.concatenate([bias] * R, axis=1)
            return [_softmax_cols(s, m_ref_, p_ref, col_groups[0])]
        return softmax

    def flush(br, acc_ref_):
        for r, cols in enumerate(head_cols):
            den = acc_ref_[V_DIM:V_DIM + 1, cols]
            out_ref[:, cols] += gate(r, br) * (acc_ref_[0:V_DIM, cols] / jnp.where(den > 0, den, 1.0))

    def kpos_of(j):
        return j * tk + lax.broadcasted_iota(jnp.int32, (tk, tq), 0)

    def win_bias(j, tail):
        kpos = kpos_of(j)
        rel = qpos - kpos
        return jnp.where((rel >= 0) & (rel < WINDOW) & (kpos >= 0), 0.0, NEG)

    def win_scores(j):
        return jnp.dot(k_tile(kw_ref, j), qz_ref[...], preferred_element_type=F32)

    n_win = WINDOW // tk + 1
    _pipelined_attention(win_scores, vT_tile_of(vwT_ref), (wsa_ref, wsb_ref), (wpa_ref, wpb_ref), wm_ref, wacc_ref,
                         softmax_with(win_bias, wm_ref), first=jd - (n_win - 1), n_plain_pairs=0, n_tail=n_win,
                         max_tile=max_tile, col_groups=col_groups)
    flush(2, wacc_ref)

    pad_rows = jnp.zeros((BF16_ROWS - per_tile, tq), F32)

    def sel_scores(j):
        rows = [sel_ref[pl.ds(j * per_tile + i, 1), :] for i in range(per_tile)]
        bias = jnp.concatenate(rows + [pad_rows], axis=0).astype(BF16)
        qs_ref[d:d + BF16_ROWS, :] = jnp.concatenate([bias] * R, axis=1)
        return jnp.dot(ks_ref[0, 0, pl.ds(pl.multiple_of(j * tk, tk), tk), :], qs_ref[...],
                       preferred_element_type=F32)

    def sel_bias(j, tail):
        return jnp.where(kpos_of(j) <= qpos, 0.0, NEG) if tail else None

    _pipelined_attention(sel_scores, vT_tile_of(vsT_ref), (sa_ref, sb_ref), (pa_ref, pb_ref), m_ref, acc_ref,
                         softmax_with(sel_bias, m_ref), first=0, n_plain_pairs=jd // 2, n_tail=2,
                         max_tile=max_tile, col_groups=col_groups,
                         tail_is_short=(2 * (jd // 2) + 1) * tk > q0 + tq - 1)
    flush(1, acc_ref)

    for r in range(R):
        o_ref[0, r * d:(r + 1) * d, :] = out_ref[:, r * tq:(r + 1) * tq].astype(o_ref.dtype)


def _nsa_step_kernel(qT_ref, gT_ref, kc_ref, vcT_ref, ks_ref, vsT_ref, kw_ref, vwT_ref, o_ref, *scratch,
                     tq, tk, S, qb):
    def query_block(qq, carry):
        cols = pl.ds(pl.multiple_of(qq * tq, tq), tq)
        _nsa_kernel(qT_ref.at[:, :, cols], gT_ref.at[:, :, cols], kc_ref, vcT_ref, ks_ref, vsT_ref, kw_ref, vwT_ref,
                    o_ref.at[:, :, cols], *scratch, qi=pl.program_id(2) * qb + qq, tq=tq, tk=tk, S=S)
        return carry

    lax.fori_loop(0, qb, query_block, 0)


def _nsa(nsqT, nsgT, kcmp, vcmpT, nsks, nsk, nsvT, tq=128, tk=NSA_TK, qb=4):
    B, _, S = nsqT.shape
    G, R, d = NSA_GROUPS, NSA_REP, NSA_DIM
    ncb = S // CMP_STRIDE
    nsb = S // SEL_BLOCK
    return pl.pallas_call(
        functools.partial(_nsa_step_kernel, tq=tq, tk=tk, S=S, qb=qb),
        grid=(B, G, S // (qb * tq)),
        in_specs=[pl.BlockSpec((1, R * d, qb * tq), lambda b, g, i: (b, g, i)),
                  pl.BlockSpec((1, GATE_PAD, qb * tq), lambda b, g, i: (b, g, i)),
                  pl.BlockSpec((1, 1, ncb, d), lambda b, g, i: (b, g, 0, 0)),
                  pl.BlockSpec((1, 1, d, ncb), lambda b, g, i: (b, g, 0, 0)),
                  pl.BlockSpec((1, 1, S, LANES), lambda b, g, i: (b, g, 0, 0)),
                  pl.BlockSpec((1, V_EXT, S), lambda b, g, i: (b, g, 0)),
                  pl.BlockSpec((1, S, LANES), lambda b, g, i: (b, 0, 1)),
                  pl.BlockSpec((1, V_EXT, S), lambda b, g, i: (b, G + g, 0))],
        out_specs=pl.BlockSpec((1, R * d, qb * tq), lambda b, g, i: (b, g, i)),
        out_shape=jax.ShapeDtypeStruct((B, NSA_WIDTH, S), BF16),
        scratch_shapes=[pltpu.VMEM((d, R * tq), BF16),
                        pltpu.VMEM((LANES, R * tq), BF16),
                        pltpu.VMEM((LANES, R * tq), BF16),
                        pltpu.VMEM((tq // LANES, ncb + 16, LANES), F32),
                        pltpu.VMEM((nsb, tq), F32),
                        pltpu.VMEM((ncb, R * tq), BF16),
                        pltpu.VMEM((tk, R * tq), F32),
                        pltpu.VMEM((tk, R * tq), F32),
                        pltpu.VMEM((tk, R * tq), BF16),
                        pltpu.VMEM((tk, R * tq), BF16),
                        pltpu.VMEM((1, R * tq), F32),
                        pltpu.VMEM((ACC_ROWS, R * tq), F32),
                        pltpu.VMEM((tk, R * tq), F32),
                        pltpu.VMEM((tk, R * tq), F32),
                        pltpu.VMEM((tk, R * tq), BF16),
                        pltpu.VMEM((tk, R * tq), BF16),
                        pltpu.VMEM((1, R * tq), F32),
                        pltpu.VMEM((ACC_ROWS, R * tq), F32),
                        pltpu.VMEM((d, R * tq), F32)],
        compiler_params=_cparams(("parallel", "parallel", "arbitrary")),
        name="nsa",
    )(nsqT, nsgT, kcmp, vcmpT, nsks, nsvT, nsk, nsvT)


def _mix_ffn_kernel(x_ref, yml_ref, ydaT_ref, ynsT_ref, wo_ref, g_ref, w1_ref, w2_ref, gf_ref, o_ref,
                    hb_ref, acc_ref, *, final):
    j = pl.program_id(2)

    @pl.when(j == 0)
    def _():
        x1 = x_ref[0] + jnp.dot(yml_ref[0].astype(BF16), wo_ref[0:ML_WIDTH, :], preferred_element_type=F32)
        x1 = x1 + lax.dot_general(ydaT_ref[0], wo_ref[ML_WIDTH:ML_WIDTH + DA_WIDTH, :], _TN,
                                  preferred_element_type=F32)
        x1 = x1 + lax.dot_general(ynsT_ref[0], wo_ref[ML_WIDTH + DA_WIDTH:, :], _TN,
                                  preferred_element_type=F32)
        hb_ref[...] = _rms(x1, g_ref[...]).astype(BF16)
        acc_ref[...] = x1

    u = jnp.dot(hb_ref[...], w1_ref[...], preferred_element_type=F32)
    a = jnp.square(jnp.maximum(u, 0.0)).astype(BF16)
    acc_ref[...] += jnp.dot(a, w2_ref[...], preferred_element_type=F32)

    @pl.when(j == pl.num_programs(2) - 1)
    def _():
        y = acc_ref[...]
        if final:
            y = _rms(y, gf_ref[...])
        o_ref[0] = y


def _mix_ffn(x, yml, ydaT, ynsT, wo, g, w1, w2, gf, final, tm=1024, tf=1024):
    B, S, D = x.shape
    return pl.pallas_call(
        functools.partial(_mix_ffn_kernel, final=final),
        grid=(B, S // tm, D_FF // tf),
        in_specs=[pl.BlockSpec((1, tm, D), lambda b, i, j: (b, i, 0)),
                  pl.BlockSpec((1, tm, ML_WIDTH), lambda b, i, j: (b, i, 0)),
                  pl.BlockSpec((1, DA_WIDTH, tm), lambda b, i, j: (b, 0, i)),
                  pl.BlockSpec((1, NSA_WIDTH, tm), lambda b, i, j: (b, 0, i)),
                  pl.BlockSpec((D, D), lambda b, i, j: (0, 0)),
                  pl.BlockSpec((1, D), lambda b, i, j: (0, 0)),
                  pl.BlockSpec((D, tf), lambda b, i, j: (0, j)),
                  pl.BlockSpec((tf, D), lambda b, i, j: (j, 0)),
                  pl.BlockSpec((1, D), lambda b, i, j: (0, 0))],
        out_specs=pl.BlockSpec((1, tm, D), lambda b, i, j: (b, i, 0)),
        out_shape=jax.ShapeDtypeStruct((B, S, D), F32),
        scratch_shapes=[pltpu.VMEM((tm, D), BF16), pltpu.VMEM((tm, D), F32)],
        compiler_params=_cparams(("parallel", "parallel", "arbitrary")),
        name="mix_ffn",
    )(x, yml, ydaT, ynsT, wo, g, w1, w2, gf)


def _split_w_in(w_in_l):
    edges = np.concatenate([[0], np.cumsum(IN_SIZES)])
    return [w_in_l[:, int(edges[i]):int(edges[i + 1])] for i in range(len(IN_SIZES))]


def _pad_cols(w, n):
    return jnp.pad(w, ((0, 0), (0, n - w.shape[1])))


def _inproj_weights(w_in_l):
    (ml_q, ml_k, ml_v, ml_o, ml_i, ml_f, da_q, da_k, da_v,
     ns_q, ns_kc, ns_vc, ns_ks, ns_vs, ns_kw, ns_vw, ns_g) = _split_w_in(w_in_l)
    wt = jnp.concatenate([ml_q, ml_k, ml_v, ml_o, _pad_cols(jnp.concatenate([ml_i, ml_f], 1), LANES),
                          da_k, ns_ks, ns_kw, ns_kc, ns_vc], axis=1)
    per_group = NSA_REP * 3
    ns_g_pad = jnp.concatenate(
        [_pad_cols(ns_g[:, gi * per_group:(gi + 1) * per_group], GATE_PAD) for gi in range(NSA_GROUPS)], axis=1)
    wf = jnp.concatenate([da_q, da_v, ns_q, ns_vs, ns_vw, ns_g_pad], axis=1)
    return wt.astype(BF16), wf.T.astype(BF16)


def kernel(x, norm1, w_in, ml_conv, ml_gate_bias, ml_norm, da_lambda, da_norm, nsa_pe,
           nsa_w1, nsa_w2, w_out, norm2, w_ff1, w_ff2, final_norm):
    B, S, D = x.shape
    depth = norm1.shape[0]
    for l in range(depth):
        wt, wfT = _inproj_weights(w_in[l])
        (zml, zg, dak, nsk, nsks, x16, daqT, davT, nsqT, nsvT, nsgT) = _inproj(x, norm1[l][None, :], wt, wfT)

        yml = _mlstm(zml, zg, ml_conv[l], _pad_cols(ml_gate_bias[l][None, :], LANES), ml_norm[l][None, :])

        lam_init = 0.8 - 0.6 * math.exp(-0.3 * l)
        ydaT = _diff_attn(da_lambda[l], daqT, dak, davT, da_norm[l][:, None], lam_init)

        pe2 = nsa_pe[l].reshape(2, 2, CMP_STRIDE * NSA_DIM)
        w1b = nsa_w1[l].astype(BF16)
        kcmp = _compress(x16, pe2, w1b, nsa_w2[l], 0, False)
        vcmpT = _compress(x16, pe2, w1b, nsa_w2[l], 1, True)
        ynsT = _nsa(nsqT, nsgT, kcmp, vcmpT, nsks, nsk, nsvT)

        x = _mix_ffn(x, yml, ydaT, ynsT, w_out[l].astype(BF16), norm2[l][None, :], w_ff1[l].astype(BF16),
                     w_ff2[l].astype(BF16), final_norm[None, :], final=(l == depth - 1))
    return x
```

```python
import functools
import math

import numpy as np
import jax
import jax.numpy as jnp
from jax import lax
from jax.experimental import pallas as pl
from jax.experimental.pallas import tpu as pltpu

F32 = jnp.float32
BF16 = jnp.bfloat16

D_MODEL = 1024
ML_HEADS = 4
ML_DIM = 64
ML_WIDTH = ML_HEADS * ML_DIM
ML_TILE_CHUNK = 128
CONV_W = 4
DA_HEADS = 4
DA_QK_DIM = 32
DA_V_DIM = 64
DA_WIDTH = DA_HEADS * DA_V_DIM
NSA_HEADS = 8
NSA_GROUPS = 2
NSA_REP = NSA_HEADS // NSA_GROUPS
NSA_DIM = 64
NSA_WIDTH = NSA_HEADS * NSA_DIM
NSA_KV = NSA_GROUPS * NSA_DIM
CMP_BLOCK = 32
CMP_STRIDE = 16
CMP_HIDDEN = 4 * NSA_DIM
SEL_BLOCK = 64
SEL_TOPK = 16
WINDOW = 512
D_FF = 4 * D_MODEL
EPS = 1e-6
FORCE_SCORE = 1e4
IN_SIZES = (ML_WIDTH, ML_WIDTH, ML_WIDTH, ML_WIDTH, ML_HEADS, ML_HEADS,
            2 * DA_HEADS * DA_QK_DIM, 2 * DA_HEADS * DA_QK_DIM, DA_WIDTH,
            NSA_WIDTH, NSA_KV, NSA_KV, NSA_KV, NSA_KV, NSA_KV, NSA_KV, 3 * NSA_HEADS)

LANES = 128
GATE_PAD = 16
NEG = -1e30
M_INIT = -1e29
LOOP_PAIRS = 4
TAKEN = -3e38
LOG2E = 1.4426950408889634
V_DIM = 64
BF16_ROWS = 16
NSA_TK = 256
V_EXT = 80
ACC_ROWS = V_DIM + 8
VMEM_LIMIT = 56 * 1024 * 1024

_T_ML = (0, 1024)
_T_MLG = (1024, 1152)
_T_DAK = (1152, 1408)
_T_NSK = (1408, 1664)
_T_NSC = (1664, 1920)
_T_COLS = 1920
_F_DAQ = (0, 256)
_F_DAV = (256, 512)
_F_NSQ = (512, 1024)
_F_NSV = (1024, 1280)
_F_NSG = (1280, 1280 + NSA_GROUPS * GATE_PAD)
_F_ROWS = _F_NSG[1]

_NT = (((1,), (1,)), ((), ()))
_TN = (((0,), (0,)), ((), ()))


def _cparams(sem):
    return pltpu.CompilerParams(dimension_semantics=sem, vmem_limit_bytes=VMEM_LIMIT)


def _rms(x, g):
    return x * lax.rsqrt(jnp.mean(x * x, axis=-1, keepdims=True) + EPS) * g


def _inproj_kernel(x_ref, g_ref, wt_ref, wf_ref,
                   zml_ref, zg_ref, dak_ref, nsk_ref, nsks_ref, x16_ref,
                   daqT_ref, davT_ref, nsqT_ref, nsvT_ref, nsgT_ref, nsc_ref):
    hb = _rms(x_ref[0], g_ref[...]).astype(BF16)

    def tdot(span):
        return jnp.dot(hb, wt_ref[:, span[0]:span[1]], preferred_element_type=F32)

    def fdot(span):
        return lax.dot_general(wf_ref[span[0]:span[1], :], hb, _NT, preferred_element_type=F32)

    zml_ref[0] = tdot(_T_ML)
    zg_ref[0] = tdot(_T_MLG)
    dak = tdot(_T_DAK).astype(BF16)
    for h in range(DA_HEADS):
        dak_ref[0, h] = dak[:, h * 2 * DA_QK_DIM:(h + 1) * 2 * DA_QK_DIM]
    nsk = tdot(_T_NSK).astype(BF16)
    nsk_ref[0] = nsk
    tm_ = nsk.shape[0]
    row = lax.broadcasted_iota(jnp.int32, (tm_, NSA_DIM), 0)
    lane = lax.broadcasted_iota(jnp.int32, (tm_, NSA_DIM), 1)
    blk_in_tile = lax.shift_right_logical(row & (NSA_TK - 1), int(math.log2(SEL_BLOCK)))
    onehot = jnp.where(lane == blk_in_tile, 1.0, 0.0).astype(BF16)
    for gi in range(NSA_GROUPS):
        nsks_ref[0, gi] = jnp.concatenate([nsk[:, gi * NSA_DIM:(gi + 1) * NSA_DIM], onehot], axis=1)
    nsc = tdot(_T_NSC)
    for half in range(2):
        nsc_ref[half] = nsc[:, half * LANES:(half + 1) * LANES]
    for i in range(CMP_STRIDE):
        for half in range(2):
            rows = nsc_ref[half, pl.ds(i, tm_ // CMP_STRIDE, stride=CMP_STRIDE), :]
            for w in range(2):
                x16_ref[0, 2 * half + w, :, i * NSA_DIM:(i + 1) * NSA_DIM] = rows[:, w * NSA_DIM:(w + 1) * NSA_DIM]
    daqT_ref[0] = (fdot(_F_DAQ) * (DA_QK_DIM ** -0.5 * LOG2E)).astype(BF16)
    nsqT_ref[0] = (fdot(_F_NSQ) * (NSA_DIM ** -0.5 * LOG2E)).astype(BF16)
    nsgT_ref[0] = fdot(_F_NSG)
    tm = hb.shape[0]
    for v_ref, span in ((davT_ref, _F_DAV), (nsvT_ref, _F_NSV)):
        v = fdot(span).astype(BF16)
        for h in range((span[1] - span[0]) // V_DIM):
            v_ref[0, h * V_EXT:h * V_EXT + V_DIM, :] = v[h * V_DIM:(h + 1) * V_DIM, :]
            v_ref[0, h * V_EXT + V_DIM:(h + 1) * V_EXT, :] = jnp.ones((V_EXT - V_DIM, tm), BF16)


def _inproj(x, g, wt, wfT, tm=512):
    B, S, D = x.shape
    tok = lambda w, dt: jax.ShapeDtypeStruct((B, S, w), dt)
    feat = lambda r, dt: jax.ShapeDtypeStruct((B, r, S), dt)
    tspec = lambda w: pl.BlockSpec((1, tm, w), lambda b, i: (b, i, 0))
    fspec = lambda r: pl.BlockSpec((1, r, tm), lambda b, i: (b, 0, i))
    return pl.pallas_call(
        _inproj_kernel,
        grid=(B, S // tm),
        in_specs=[tspec(D),
                  pl.BlockSpec((1, D), lambda b, i: (0, 0)),
                  pl.BlockSpec((D, _T_COLS), lambda b, i: (0, 0)),
                  pl.BlockSpec((_F_ROWS, D), lambda b, i: (0, 0))],
        out_specs=[tspec(1024), tspec(128),
                   pl.BlockSpec((1, DA_HEADS, tm, 2 * DA_QK_DIM), lambda b, i: (b, 0, i, 0)), tspec(256),
                   pl.BlockSpec((1, NSA_GROUPS, tm, LANES), lambda b, i: (b, 0, i, 0)),
                   pl.BlockSpec((1, 2 * NSA_GROUPS, tm // CMP_STRIDE, CMP_STRIDE * NSA_DIM), lambda b, i: (b, 0, i, 0)),
                   fspec(256), fspec(4 * V_EXT), fspec(512), fspec(4 * V_EXT), fspec(NSA_GROUPS * GATE_PAD)],
        out_shape=[tok(1024, F32), tok(128, F32),
                   jax.ShapeDtypeStruct((B, DA_HEADS, S, 2 * DA_QK_DIM), BF16), tok(256, BF16),
                   jax.ShapeDtypeStruct((B, NSA_GROUPS, S, LANES), BF16),
                   jax.ShapeDtypeStruct((B, 2 * NSA_GROUPS, S // CMP_STRIDE, CMP_STRIDE * NSA_DIM), F32),
                   feat(256, BF16), feat(4 * V_EXT, BF16), feat(512, BF16), feat(4 * V_EXT, BF16),
                   feat(NSA_GROUPS * GATE_PAD, F32)],
        scratch_shapes=[pltpu.VMEM((2, tm, LANES), F32)],
        compiler_params=_cparams(("parallel", "parallel")),
        name="inproj",
    )(x, g, wt, wfT)


def _log_sigmoid(x):
    return jnp.minimum(x, 0.0) - jnp.log1p(jnp.exp(-jnp.abs(x)))


def _sigmoid(x):
    return 1.0 / (1.0 + jnp.exp(-x))


def _mlstm_kernel(zml_ref, zg_ref, cw_ref, gb_ref, nrm_ref, y_ref,
                  pad_ref, c_ref, n_ref, m_ref, *, T, L):
    d = ML_DIM
    W = max(L, d)
    t = pl.program_id(1)

    @pl.when(t == 0)
    def _():
        pad_ref[0:8, :] = jnp.zeros((8, 2 * ML_WIDTH), F32)
        c_ref[...] = jnp.zeros_like(c_ref)
        n_ref[...] = jnp.zeros_like(n_ref)
        m_ref[...] = jnp.zeros_like(m_ref)

    @pl.when(t > 0)
    def _():
        pad_ref[0:8, :] = pad_ref[T:T + 8, :]

    pad_ref[8:8 + T, :] = zml_ref[0, :, 0:2 * ML_WIDTH]
    conv = cw_ref[0:1, :] * pad_ref[5:5 + T, :]
    for j in range(1, CONV_W):
        conv = conv + cw_ref[j:j + 1, :] * pad_ref[5 + j:5 + j + T, :]
    qk = conv * _sigmoid(conv)
    gates = zg_ref[0] + gb_ref[...]
    logf = _log_sigmoid(gates)

    row = lax.broadcasted_iota(jnp.int32, (L, L), 0)
    col = lax.broadcasted_iota(jnp.int32, (L, L), 1)
    causal = col <= row
    tril = causal.astype(F32)
    triu = (row <= col).astype(F32)

    lane_of = lax.broadcasted_iota(jnp.int32, (LANES, W), 0)
    pick = [(lane_of == g).astype(F32) for g in range(2 * ML_HEADS)]

    H = range(ML_HEADS)

    def local_part(c):
        r0 = c * L
        g_c = gates[r0:r0 + L, :]
        lf_c = logf[r0:r0 + L, :]
        b_cols = jnp.dot(tril, lf_c, preferred_element_type=F32,
                         precision=lax.Precision.HIGHEST)
        b_rows = jnp.dot(lf_c.T, triu, preferred_element_type=F32,
                         precision=lax.Precision.HIGHEST)
        g_rows = g_c.T
        b_col = [jnp.dot(b_cols, pick[ML_HEADS + h], preferred_element_type=F32,
                         precision=lax.Precision.HIGHEST) for h in H]
        ig_col = [jnp.dot(g_c, pick[h], preferred_element_type=F32,
                          precision=lax.Precision.HIGHEST) for h in H]
        b_row = [b_rows[ML_HEADS + h:ML_HEADS + h + 1, :] for h in H]
        ig_row = [g_rows[h:h + 1, :] for h in H]
        qh = [qk[r0:r0 + L, h * d:(h + 1) * d] for h in H]
        kh = [qk[r0:r0 + L, ML_WIDTH + h * d:ML_WIDTH + (h + 1) * d] * (d ** -0.5) for h in H]
        vh = [zml_ref[0, r0:r0 + L, 2 * ML_WIDTH + h * d:2 * ML_WIDTH + (h + 1) * d] for h in H]
        qkt = [lax.dot_general(qh[h], kh[h], _NT, preferred_element_type=F32) for h in H]
        g_tot = [b_col[h][L - 1:L, :] for h in H]
        a_max = [jnp.max(g_tot[h][:, 0:L] - b_row[h] + ig_row[h], axis=1, keepdims=True) for h in H]
        w_col = [jnp.exp((g_tot[h] - b_col[h] + ig_col[h] - a_max[h])[:, 0:d]) for h in H]
        c_loc = [lax.dot_general(vh[h] * w_col[h], kh[h], _TN, preferred_element_type=F32) for h in H]
        n_loc = [jnp.sum(kh[h] * w_col[h], axis=0, keepdims=True) for h in H]
        dmat = [jnp.where(causal, b_col[h][:, 0:L] - b_row[h] + ig_row[h], NEG) for h in H]
        d_max = [jnp.max(dmat[h], axis=1, keepdims=True) for h in H]
        return dict(b_col=b_col, qh=qh, vh=vh, qkt=qkt, g_tot=g_tot, a_max=a_max, c_loc=c_loc, n_loc=n_loc,
                    dmat=dmat, d_max=d_max)

    def carried_part(c, lp):
        r0 = c * L
        b_col, qh, vh, qkt, g_tot, a_max = lp["b_col"], lp["qh"], lp["vh"], lp["qkt"], lp["g_tot"], lp["a_max"]
        oh = [zml_ref[0, r0:r0 + L, 3 * ML_WIDTH + h * d:3 * ML_WIDTH + (h + 1) * d] for h in H]
        c_prev = [c_ref[h] for h in H]
        n_prev = [n_ref[h, 0:1, :] for h in H]
        m_prev = [m_ref[h, 0:1, :] for h in H]
        q_c = [lax.dot_general(qh[h], c_prev[h], _NT, preferred_element_type=F32) for h in H]
        inter_log = [b_col[h] + m_prev[h] for h in H]
        m_t = [jnp.maximum(inter_log[h], lp["d_max"][h]) for h in H]
        wts = [jnp.exp(lp["dmat"][h] - m_t[h][:, 0:L]) * qkt[h] for h in H]
        s_inter = [jnp.exp((inter_log[h] - m_t[h])[:, 0:d]) for h in H]
        num = [jnp.dot(wts[h], vh[h], preferred_element_type=F32) + s_inter[h] * q_c[h] for h in H]
        den = [jnp.sum(wts[h], axis=1, keepdims=True)
               + s_inter[h] * jnp.sum(qh[h] * n_prev[h], axis=1, keepdims=True) for h in H]
        hh = [num[h] / jnp.maximum(jnp.abs(den[h]), jnp.exp(-m_t[h][:, 0:d])) for h in H]
        m_new = [jnp.maximum(g_tot[h] + m_prev[h], a_max[h]) for h in H]
        for h in H:
            s_prev = jnp.exp((g_tot[h] + m_prev[h] - m_new[h])[:, 0:d])
            s_loc = jnp.exp((a_max[h] - m_new[h])[:, 0:d])
            c_ref[h] = s_prev * c_prev[h] + s_loc * lp["c_loc"][h]
            n_ref[h] = jnp.broadcast_to(s_prev * n_prev[h] + s_loc * lp["n_loc"][h], (8, d))
            m_ref[h] = jnp.broadcast_to(m_new[h], (8, W))
        for h in H:
            yh = _sigmoid(oh[h]) * hh[h]
            y_ref[0, r0:r0 + L, h * d:(h + 1) * d] = _rms(yh, nrm_ref[0:1, h * d:(h + 1) * d])

    for c in range(T // L):
        carried_part(c, local_part(c))


def _mlstm(zml, zg, conv_w, gate_bias, norm_g, T=256, chunk=ML_TILE_CHUNK):
    B, S, _ = zml.shape
    return pl.pallas_call(
        functools.partial(_mlstm_kernel, T=T, L=chunk),
        grid=(B, S // T),
        in_specs=[pl.BlockSpec((1, T, 1024), lambda b, t: (b, t, 0)),
                  pl.BlockSpec((1, T, 128), lambda b, t: (b, t, 0)),
                  pl.BlockSpec((CONV_W, 2 * ML_WIDTH), lambda b, t: (0, 0)),
                  pl.BlockSpec((1, 128), lambda b, t: (0, 0)),
                  pl.BlockSpec((1, ML_WIDTH), lambda b, t: (0, 0))],
        out_specs=pl.BlockSpec((1, T, ML_WIDTH), lambda b, t: (b, t, 0)),
        out_shape=jax.ShapeDtypeStruct((B, S, ML_WIDTH), F32),
        scratch_shapes=[pltpu.VMEM((T + 8, 2 * ML_WIDTH), F32),
                        pltpu.VMEM((ML_HEADS, ML_DIM, ML_DIM), F32),
                        pltpu.VMEM((ML_HEADS, 8, ML_DIM), F32),
                        pltpu.VMEM((ML_HEADS, 8, max(chunk, ML_DIM)), F32)],
        compiler_params=_cparams(("parallel", "arbitrary")),
        name="mlstm",
    )(zml, zg, conv_w, gate_bias, norm_g)


def _softmax_cols(s, m_ref, p_ref, cols):
    m_old = m_ref[:, cols]
    m_new = jnp.maximum(m_old, jnp.max(s, axis=0, keepdims=True))
    m_ref[:, cols] = m_new
    p_ref[:, cols] = jnp.exp2(s - m_new).astype(BF16)
    return jnp.exp2(m_old - m_new)


def _pipelined_attention(scores, vT_tile, s_refs, p_refs, m_ref, acc_ref, softmax,
                         first, n_plain_pairs, n_tail, max_tile, col_groups, tail_is_short=None):
    sa, sb = s_refs
    pa, pb = p_refs

    def load(j):
        return jnp.clip(j, 0, max_tile)

    def half(j, s_cur, s_nxt, p_cur, p_prev, tail):
        s_nxt[...] = scores(load(j + 1))
        pv = jnp.dot(vT_tile(load(j - 1)), p_prev[...], preferred_element_type=F32)
        alphas = softmax(s_cur, p_cur, j, tail)
        for cols, alpha in zip(col_groups, alphas):
            acc_ref[:, cols] = alpha * (acc_ref[:, cols] + pv[0:ACC_ROWS, cols])

    def pair(j, tail):
        half(j, sa, sb, pa, pb, tail)
        half(j + 1, sb, sa, pb, pa, tail)

    m_ref[...] = jnp.full_like(m_ref, M_INIT)
    acc_ref[...] = jnp.zeros_like(acc_ref)
    pb[...] = jnp.zeros_like(pb)
    sa[...] = scores(load(first))

    if not (isinstance(n_plain_pairs, int) and n_plain_pairs == 0):
        n_trips = n_plain_pairs // LOOP_PAIRS

        def body(i, carry):
            for u in range(LOOP_PAIRS):
                pair(first + 2 * (LOOP_PAIRS * i + u), False)
            return carry

        lax.fori_loop(0, n_trips, body, 0)
        rest = n_plain_pairs - n_trips * LOOP_PAIRS
        for u in range(LOOP_PAIRS - 1):
            @pl.when(u < rest)
            def _():
                pair(first + 2 * (LOOP_PAIRS * n_trips + u), False)
    j = first + 2 * n_plain_pairs
    bufs = ((sa, sb, pa, pb), (sb, sa, pb, pa))

    def tail(n):
        for t in range(n):
            half(j + t, *bufs[t % 2], True)
        p_last = bufs[(n - 1) % 2][2]
        acc_ref[...] += jnp.dot(vT_tile(load(j + n - 1)), p_last[...], preferred_element_type=F32)[0:ACC_ROWS]

    if tail_is_short is None:
        tail(n_tail)
    else:
        @pl.when(tail_is_short)
        def _():
            tail(n_tail - 1)

        @pl.when(jnp.logical_not(tail_is_short))
        def _():
            tail(n_tail)


def _da_kernel(lam_ref, qT_ref, k_ref, vT_ref, gain_ref, o_ref,
               qz_ref, sa_ref, sb_ref, pa_ref, pb_ref, m_ref, acc_ref, *, tq, tk, qb, lam_init, S):
    d = DA_QK_DIM
    col_groups = [slice(mp * tq, (mp + 1) * tq) for mp in range(2)]

    def scores(j):
        return jnp.dot(k_ref[0, 0, pl.ds(pl.multiple_of(j * tk, tk), tk), :], qz_ref[...],
                       preferred_element_type=F32)

    def vT_tile(j):
        return vT_ref[0, :, pl.ds(pl.multiple_of(j * tk, tk), tk)]

    lp = lam_ref[...]
    lam = (jnp.exp(jnp.sum(lp[0:1] * lp[1:2], axis=1, keepdims=True))
           - jnp.exp(jnp.sum(lp[2:3] * lp[3:4], axis=1, keepdims=True)) + lam_init)

    def query_block(qq, carry):
        qi = pl.program_id(2) * qb + qq
        jd = (qi * tq) // tk
        cols_q = pl.ds(pl.multiple_of(qq * tq, tq), tq)

        qz_ref[...] = jnp.zeros_like(qz_ref)
        qz_ref[0:d, 0:tq] = qT_ref[0, 0:d, cols_q]
        qz_ref[d:2 * d, tq:2 * tq] = qT_ref[0, d:2 * d, cols_q]

        def softmax(s_ref, p_ref, j, tail):
            alphas = []
            for cols in col_groups:
                s = s_ref[:, cols]
                if tail:
                    kpos = j * tk + lax.broadcasted_iota(jnp.int32, (tk, tq), 0)
                    qpos = qi * tq + lax.broadcasted_iota(jnp.int32, (tk, tq), 1)
                    s = jnp.where(kpos <= qpos, s, NEG)
                alphas.append(_softmax_cols(s, m_ref, p_ref, cols))
            return alphas

        _pipelined_attention(scores, vT_tile, (sa_ref, sb_ref), (pa_ref, pb_ref), m_ref, acc_ref, softmax,
                             first=0, n_plain_pairs=jd // 2, n_tail=2, max_tile=S // tk - 1,
                             col_groups=col_groups, tail_is_short=(2 * (jd // 2) + 1) * tk > qi * tq + tq - 1)

        o1 = acc_ref[0:V_DIM, 0:tq] / acc_ref[V_DIM:V_DIM + 1, 0:tq]
        o2 = acc_ref[0:V_DIM, tq:2 * tq] / acc_ref[V_DIM:V_DIM + 1, tq:2 * tq]
        o = o1 - lam * o2
        y = o * lax.rsqrt(jnp.mean(o * o, axis=0, keepdims=True) + EPS) * gain_ref[...]
        o_ref[0, :, cols_q] = (y * (1.0 - lam_init)).astype(o_ref.dtype)
        return carry

    lax.fori_loop(0, qb, query_block, 0)


def _diff_attn(da_lambda, daqT, dak, davT, gain_col, lam_init, tq=256, tk=256, qb=4):
    B, _, S = daqT.shape
    return pl.pallas_call(
        functools.partial(_da_kernel, tq=tq, tk=tk, qb=qb, lam_init=lam_init, S=S),
        grid=(B, DA_HEADS, S // (qb * tq)),
        in_specs=[pl.BlockSpec((4, DA_QK_DIM), lambda b, h, i: (0, 0)),
                  pl.BlockSpec((1, DA_V_DIM, qb * tq), lambda b, h, i: (b, h, i)),
                  pl.BlockSpec((1, 1, S, 2 * DA_QK_DIM), lambda b, h, i: (b, h, 0, 0)),
                  pl.BlockSpec((1, V_EXT, S), lambda b, h, i: (b, h, 0)),
                  pl.BlockSpec((DA_V_DIM, 1), lambda b, h, i: (h, 0))],
        out_specs=pl.BlockSpec((1, DA_V_DIM, qb * tq), lambda b, h, i: (b, h, i)),
        out_shape=jax.ShapeDtypeStruct((B, DA_WIDTH, S), BF16),
        scratch_shapes=[pltpu.VMEM((2 * DA_QK_DIM, 2 * tq), BF16),
                        pltpu.VMEM((tk, 2 * tq), F32),
                        pltpu.VMEM((tk, 2 * tq), F32),
                        pltpu.VMEM((tk, 2 * tq), BF16),
                        pltpu.VMEM((tk, 2 * tq), BF16),
                        pltpu.VMEM((1, 2 * tq), F32),
                        pltpu.VMEM((ACC_ROWS, 2 * tq), F32)],
        compiler_params=_cparams(("parallel", "parallel", "arbitrary")),
        name="diff_attn",
    )(da_lambda, daqT, dak, davT, gain_col)


def _gelu_tanh(x):
    return x * (0.5 * (1.0 + jnp.tanh(math.sqrt(2.0 / math.pi) * (x + 0.044715 * (x * x * x)))))


def _compress_kernel(x_ref, pe_ref, w1_ref, w2_ref, o_ref, b_ref, *, feature_major):
    n = x_ref.shape[2]
    half = CMP_STRIDE * NSA_DIM
    x = x_ref[0, 0]
    a = jnp.dot((x + pe_ref[0, 0:1, :]).astype(BF16), w1_ref[0, 0:half, :], preferred_element_type=F32)
    b_ref[0:n, :] = jnp.dot((x + pe_ref[0, 1:2, :]).astype(BF16), w1_ref[0, half:2 * half, :],
                            preferred_element_type=F32)
    b_ref[n:n + 8, :] = jnp.zeros((8, CMP_HIDDEN), F32)
    hid = _gelu_tanh(a + b_ref[1:n + 1, :]).astype(BF16)
    if feature_major:
        o_ref[0, 0] = lax.dot_general(w2_ref[0], hid, _NT, preferred_element_type=F32).astype(o_ref.dtype)
    else:
        o_ref[0, 0] = jnp.dot(hid, w2_ref[0], preferred_element_type=F32).astype(o_ref.dtype)


def _compress(x16, pe2, w1, w2, which, feature_major):
    B, _, n, half = x16.shape
    G = NSA_GROUPS
    if feature_major:
        out_shape, out_block = (B, G, NSA_DIM, n), (1, 1, NSA_DIM, n)
        w2_arr, w2_block = jnp.swapaxes(w2, 1, 2), (1, NSA_DIM, CMP_HIDDEN)
    else:
        out_shape, out_block = (B, G, n, NSA_DIM), (1, 1, n, NSA_DIM)
        w2_arr, w2_block = w2, (1, CMP_HIDDEN, NSA_DIM)
    return pl.pallas_call(
        functools.partial(_compress_kernel, feature_major=feature_major),
        grid=(B, G),
        in_specs=[pl.BlockSpec((1, 1, n, half), lambda b, g: (b, which * G + g, 0, 0)),
                  pl.BlockSpec((1, 2, half), lambda b, g: (which, 0, 0)),
                  pl.BlockSpec((1, 2 * half, CMP_HIDDEN), lambda b, g: (which, 0, 0)),
                  pl.BlockSpec(w2_block, lambda b, g: (which, 0, 0))],
        out_specs=pl.BlockSpec(out_block, lambda b, g: (b, g, 0, 0)),
        out_shape=jax.ShapeDtypeStruct(out_shape, BF16),
        scratch_shapes=[pltpu.VMEM((n + 8, CMP_HIDDEN), F32)],
        compiler_params=_cparams(("parallel", "parallel")),
        name="compress_v" if feature_major else "compress_k",
    )(x16, pe2, w1, w2_arr.astype(BF16))


def _nsa_kernel(qT_ref, gT_ref, kc_ref, vcT_ref, ks_ref, vsT_ref, kw_ref, vwT_ref, o_ref,
                qg_ref, qz_ref, qs_ref, imp_ref, sel_ref, pc_ref, sa_ref, sb_ref, pa_ref, pb_ref, m_ref, acc_ref,
                wsa_ref, wsb_ref, wpa_ref, wpb_ref, wm_ref, wacc_ref, out_ref,
                *, qi, tq, tk, S):
    g = pl.program_id(1)
    d = NSA_DIM
    R = NSA_REP
    ncb = S // CMP_STRIDE
    nsb = S // SEL_BLOCK
    q0 = qi * tq
    jd = q0 // tk
    ratio = SEL_BLOCK // CMP_STRIDE
    per_tile = tk // SEL_BLOCK

    for r in range(R):
        qg_ref[:, r * tq:(r + 1) * tq] = qT_ref[0, r * d:(r + 1) * d, :]
    qz_ref[...] = jnp.zeros_like(qz_ref)
    qs_ref[...] = jnp.zeros_like(qs_ref)
    qs_ref[0:d, :] = qg_ref[...]
    for gg in range(NSA_GROUPS):
        @pl.when(g == gg)
        def _():
            qz_ref[gg * d:(gg + 1) * d, :] = qg_ref[...]

    def gate(r, br):
        return _sigmoid(gT_ref[0, r * 3 + br:r * 3 + br + 1, :])

    qpos = q0 + lax.broadcasted_iota(jnp.int32, (1, tq), 1)

    def compress_and_select(n_c):
        n_s = n_c // ratio
        s_all = jnp.dot(kc_ref[0, 0, 0:n_c, :], qg_ref[...], preferred_element_type=F32)
        cend = lax.broadcasted_iota(jnp.int32, (n_c, tq), 0) * CMP_STRIDE + (CMP_BLOCK - 1)
        cbias = jnp.where(cend <= qpos, 0.0, NEG)
        hs = range(R)
        sc = [s_all[:, r * tq:(r + 1) * tq] + cbias for r in hs]
        mx = [jnp.maximum(jnp.max(sc[r], axis=0, keepdims=True), M_INIT) for r in hs]
        pu = [jnp.exp2(sc[r] - mx[r]) for r in hs]
        den = [jnp.sum(pu[r], axis=0, keepdims=True) for r in hs]
        pn = [pu[r] / jnp.where(den[r] > 0, den[r], 1.0) for r in hs]
        imp = pn[0]
        for r in range(1, R):
            imp = imp + pn[r]
        for r in hs:
            pc_ref[0:n_c, r * tq:(r + 1) * tq] = pn[r].astype(BF16)
        o_cmp = jnp.dot(vcT_ref[0, 0, :, 0:n_c], pc_ref[0:n_c, :], preferred_element_type=F32)
        for r in range(R):
            cols = slice(r * tq, (r + 1) * tq)
            out_ref[:, cols] = gate(r, 0) * o_cmp[:, cols]

        slabs = []
        for c in range(tq // LANES):
            imp_ref[c, 0:8, :] = jnp.zeros((8, LANES), F32)
            imp_ref[c, 8:8 + n_c, :] = imp[:, c * LANES:(c + 1) * LANES]
            imp_ref[c, 8 + n_c:16 + n_c, :] = jnp.zeros((8, LANES), F32)
            slab = jnp.zeros((n_s, LANES), F32)
            for o in range(-1, ratio):
                slab = slab + imp_ref[c, pl.ds(8 + o, n_s, stride=ratio), :]
            slabs.append(slab)
        p_slc = slabs[0] if len(slabs) == 1 else jnp.concatenate(slabs, axis=1)
        blk = lax.broadcasted_iota(jnp.int32, (n_s, tq), 0)
        cur = lax.shift_right_logical(qpos, int(math.log2(SEL_BLOCK)))
        forced = (blk == 0) | (blk == cur) | (blk == cur - 1)
        causal_blk = blk * SEL_BLOCK <= qpos
        score = jnp.where(forced, FORCE_SCORE, jnp.where(causal_blk, p_slc, -1.0))
        blk_f = blk.astype(F32)
        for _ in range(SEL_TOPK):
            mx_s = jnp.max(score, axis=0, keepdims=True)
            first = jnp.min(jnp.where(score == mx_s, blk_f, float(nsb)), axis=0, keepdims=True)
            score = jnp.where(blk_f == first, TAKEN, score)
        sel_ref[0:n_s, :] = jnp.where(score == TAKEN, 0.0, NEG)
        if n_s < nsb:
            sel_ref[n_s:nsb, :] = jnp.full((nsb - n_s, tq), NEG, F32)

    n_classes = 8
    visible_class = (q0 + tq - 1) // (S // n_classes)
    for k in range(n_classes):
        @pl.when(visible_class == k)
        def _():
            compress_and_select((k + 1) * ncb // n_classes)

    head_cols = [slice(r * tq, (r + 1) * tq) for r in range(R)]
    col_groups = [slice(0, R * tq)]
    max_tile = S // tk - 1

    def k_tile(k_ref_, j):
        return k_ref_[0, pl.ds(pl.multiple_of(j * tk, tk), tk), :]

    def vT_tile_of(vT_ref_):
        def vT_tile(j):
            return vT_ref_[0, :, pl.ds(pl.multiple_of(j * tk, tk), tk)]
        return vT_tile

    def softmax_with(bias_of, m_ref_):
        def softmax(s_ref, p_ref, j, tail):
            bias = bias_of(j, tail)
            s = s_ref[...]
            if bias is not None:
                s = s + jnp.concatenate([bias] * R, axis=1)
            return [_softmax_cols(s, m_ref_, p_ref, col_groups[0])]
        return softmax

    def flush(br, acc_ref_):
        for r, cols in enumerate(head_cols):
            den = acc_ref_[V_DIM:V_DIM + 1, cols]
            out_ref[:, cols] += gate(r, br) * (acc_ref_[0:V_DIM, cols] / jnp.where(den > 0, den, 1.0))

    def kpos_of(j):
        return j * tk + lax.broadcasted_iota(jnp.int32, (tk, tq), 0)

    def win_bias(j, tail):
        kpos = kpos_of(j)
        rel = qpos - kpos
        return jnp.where((rel >= 0) & (rel < WINDOW) & (kpos >= 0), 0.0, NEG)

    def win_scores(j):
        return jnp.dot(k_tile(kw_ref, j), qz_ref[...], preferred_element_type=F32)

    n_win = WINDOW // tk + 1
    _pipelined_attention(win_scores, vT_tile_of(vwT_ref), (wsa_ref, wsb_ref), (wpa_ref, wpb_ref), wm_ref, wacc_ref,
                         softmax_with(win_bias, wm_ref), first=jd - (n_win - 1), n_plain_pairs=0, n_tail=n_win,
                         max_tile=max_tile, col_groups=col_groups)
    flush(2, wacc_ref)

    pad_rows = jnp.zeros((BF16_ROWS - per_tile, tq), F32)

    def sel_scores(j):
        rows = [sel_ref[pl.ds(j * per_tile + i, 1), :] for i in range(per_tile)]
        bias = jnp.concatenate(rows + [pad_rows], axis=0).astype(BF16)
        qs_ref[d:d + BF16_ROWS, :] = jnp.concatenate([bias] * R, axis=1)
        return jnp.dot(ks_ref[0, 0, pl.ds(pl.multiple_of(j * tk, tk), tk), :], qs_ref[...],
                       preferred_element_type=F32)

    def sel_bias(j, tail):
        return jnp.where(kpos_of(j) <= qpos, 0.0, NEG) if tail else None

    _pipelined_attention(sel_scores, vT_tile_of(vsT_ref), (sa_ref, sb_ref), (pa_ref, pb_ref), m_ref, acc_ref,
                         softmax_with(sel_bias, m_ref), first=0, n_plain_pairs=jd // 2, n_tail=2,
                         max_tile=max_tile, col_groups=col_groups,
                         tail_is_short=(2 * (jd // 2) + 1) * tk > q0 + tq - 1)
    flush(1, acc_ref)

    for r in range(R):
        o_ref[0, r * d:(r + 1) * d, :] = out_ref[:, r * tq:(r + 1) * tq].astype(o_ref.dtype)


def _nsa_step_kernel(qT_ref, gT_ref, kc_ref, vcT_ref, ks_ref, vsT_ref, kw_ref, vwT_ref, o_ref, *scratch,
                     tq, tk, S, qb):
    def query_block(qq, carry):
        cols = pl.ds(pl.multiple_of(qq * tq, tq), tq)
        _nsa_kernel(qT_ref.at[:, :, cols], gT_ref.at[:, :, cols], kc_ref, vcT_ref, ks_ref, vsT_ref, kw_ref, vwT_ref,
                    o_ref.at[:, :, cols], *scratch, qi=pl.program_id(2) * qb + qq, tq=tq, tk=tk, S=S)
        return carry

    lax.fori_loop(0, qb, query_block, 0)


def _nsa(nsqT, nsgT, kcmp, vcmpT, nsks, nsk, nsvT, tq=128, tk=NSA_TK, qb=4):
    B, _, S = nsqT.shape
    G, R, d = NSA_GROUPS, NSA_REP, NSA_DIM
    ncb = S // CMP_STRIDE
    nsb = S // SEL_BLOCK
    return pl.pallas_call(
        functools.partial(_nsa_step_kernel, tq=tq, tk=tk, S=S, qb=qb),
        grid=(B, G, S // (qb * tq)),
        in_specs=[pl.BlockSpec((1, R * d, qb * tq), lambda b, g, i: (b, g, i)),
                  pl.BlockSpec((1, GATE_PAD, qb * tq), lambda b, g, i: (b, g, i)),
                  pl.BlockSpec((1, 1, ncb, d), lambda b, g, i: (b, g, 0, 0)),
                  pl.BlockSpec((1, 1, d, ncb), lambda b, g, i: (b, g, 0, 0)),
                  pl.BlockSpec((1, 1, S, LANES), lambda b, g, i: (b, g, 0, 0)),
                  pl.BlockSpec((1, V_EXT, S), lambda b, g, i: (b, g, 0)),
                  pl.BlockSpec((1, S, LANES), lambda b, g, i: (b, 0, 1)),
                  pl.BlockSpec((1, V_EXT, S), lambda b, g, i: (b, G + g, 0))],
        out_specs=pl.BlockSpec((1, R * d, qb * tq), lambda b, g, i: (b, g, i)),
        out_shape=jax.ShapeDtypeStruct((B, NSA_WIDTH, S), BF16),
        scratch_shapes=[pltpu.VMEM((d, R * tq), BF16),
                        pltpu.VMEM((LANES, R * tq), BF16),
                        pltpu.VMEM((LANES, R * tq), BF16),
                        pltpu.VMEM((tq // LANES, ncb + 16, LANES), F32),
                        pltpu.VMEM((nsb, tq), F32),
                        pltpu.VMEM((ncb, R * tq), BF16),
                        pltpu.VMEM((tk, R * tq), F32),
                        pltpu.VMEM((tk, R * tq), F32),
                        pltpu.VMEM((tk, R * tq), BF16),
                        pltpu.VMEM((tk, R * tq), BF16),
                        pltpu.VMEM((1, R * tq), F32),
                        pltpu.VMEM((ACC_ROWS, R * tq), F32),
                        pltpu.VMEM((tk, R * tq), F32),
                        pltpu.VMEM((tk, R * tq), F32),
                        pltpu.VMEM((tk, R * tq), BF16),
                        pltpu.VMEM((tk, R * tq), BF16),
                        pltpu.VMEM((1, R * tq), F32),
                        pltpu.VMEM((ACC_ROWS, R * tq), F32),
                        pltpu.VMEM((d, R * tq), F32)],
        compiler_params=_cparams(("parallel", "parallel", "arbitrary")),
        name="nsa",
    )(nsqT, nsgT, kcmp, vcmpT, nsks, nsvT, nsk, nsvT)


def _mix_ffn_kernel(x_ref, yml_ref, ydaT_ref, ynsT_ref, wo_ref, g_ref, w1_ref, w2_ref, gf_ref, o_ref,
                    hb_ref, acc_ref, *, final):
    j = pl.program_id(2)

    @pl.when(j == 0)
    def _():
        x1 = x_ref[0] + jnp.dot(yml_ref[0].astype(BF16), wo_ref[0:ML_WIDTH, :], preferred_element_type=F32)
        x1 = x1 + lax.dot_general(ydaT_ref[0], wo_ref[ML_WIDTH:ML_WIDTH + DA_WIDTH, :], _TN,
                                  preferred_element_type=F32)
        x1 = x1 + lax.dot_general(ynsT_ref[0], wo_ref[ML_WIDTH + DA_WIDTH:, :], _TN,
                                  preferred_element_type=F32)
        hb_ref[...] = _rms(x1, g_ref[...]).astype(BF16)
        acc_ref[...] = x1

    u = jnp.dot(hb_ref[...], w1_ref[...], preferred_element_type=F32)
    a = jnp.square(jnp.maximum(u, 0.0)).astype(BF16)
    acc_ref[...] += jnp.dot(a, w2_ref[...], preferred_element_type=F32)

    @pl.when(j == pl.num_programs(2) - 1)
    def _():
        y = acc_ref[...]
        if final:
            y = _rms(y, gf_ref[...])
        o_ref[0] = y


def _mix_ffn(x, yml, ydaT, ynsT, wo, g, w1, w2, gf, final, tm=1024, tf=1024):
    B, S, D = x.shape
    return pl.pallas_call(
        functools.partial(_mix_ffn_kernel, final=final),
        grid=(B, S // tm, D_FF // tf),
        in_specs=[pl.BlockSpec((1, tm, D), lambda b, i, j: (b, i, 0)),
                  pl.BlockSpec((1, tm, ML_WIDTH), lambda b, i, j: (b, i, 0)),
                  pl.BlockSpec((1, DA_WIDTH, tm), lambda b, i, j: (b, 0, i)),
                  pl.BlockSpec((1, NSA_WIDTH, tm), lambda b, i, j: (b, 0, i)),
                  pl.BlockSpec((D, D), lambda b, i, j: (0, 0)),
                  pl.BlockSpec((1, D), lambda b, i, j: (0, 0)),
                  pl.BlockSpec((D, tf), lambda b, i, j: (0, j)),
                  pl.BlockSpec((tf, D), lambda b, i, j: (j, 0)),
                  pl.BlockSpec((1, D), lambda b, i, j: (0, 0))],
        out_specs=pl.BlockSpec((1, tm, D), lambda b, i, j: (b, i, 0)),
        out_shape=jax.ShapeDtypeStruct((B, S, D), F32),
        scratch_shapes=[pltpu.VMEM((tm, D), BF16), pltpu.VMEM((tm, D), F32)],
        compiler_params=_cparams(("parallel", "parallel", "arbitrary")),
        name="mix_ffn",
    )(x, yml, ydaT, ynsT, wo, g, w1, w2, gf)


def _split_w_in(w_in_l):
    edges = np.concatenate([[0], np.cumsum(IN_SIZES)])
    return [w_in_l[:, int(edges[i]):int(edges[i + 1])] for i in range(len(IN_SIZES))]


def _pad_cols(w, n):
    return jnp.pad(w, ((0, 0), (0, n - w.shape[1])))


def _inproj_weights(w_in_l):
    (ml_q, ml_k, ml_v, ml_o, ml_i, ml_f, da_q, da_k, da_v,
     ns_q, ns_kc, ns_vc, ns_ks, ns_vs, ns_kw, ns_vw, ns_g) = _split_w_in(w_in_l)
    wt = jnp.concatenate([ml_q, ml_k, ml_v, ml_o, _pad_cols(jnp.concatenate([ml_i, ml_f], 1), LANES),
                          da_k, ns_ks, ns_kw, ns_kc, ns_vc], axis=1)
    per_group = NSA_REP * 3
    ns_g_pad = jnp.concatenate(
        [_pad_cols(ns_g[:, gi * per_group:(gi + 1) * per_group], GATE_PAD) for gi in range(NSA_GROUPS)], axis=1)
    wf = jnp.concatenate([da_q, da_v, ns_q, ns_vs, ns_vw, ns_g_pad], axis=1)
    return wt.astype(BF16), wf.T.astype(BF16)


def kernel(x, norm1, w_in, ml_conv, ml_gate_bias, ml_norm, da_lambda, da_norm, nsa_pe,
           nsa_w1, nsa_w2, w_out, norm2, w_ff1, w_ff2, final_norm):
    B, S, D = x.shape
    depth = norm1.shape[0]
    for l in range(depth):
        wt, wfT = _inproj_weights(w_in[l])
        (zml, zg, dak, nsk, nsks, x16, daqT, davT, nsqT, nsvT, nsgT) = _inproj(x, norm1[l][None, :], wt, wfT)

        yml = _mlstm(zml, zg, ml_conv[l], _pad_cols(ml_gate_bias[l][None, :], LANES), ml_norm[l][None, :])

        lam_init = 0.8 - 0.6 * math.exp(-0.3 * l)
        ydaT = _diff_attn(da_lambda[l], daqT, dak, davT, da_norm[l][:, None], lam_init)

        pe2 = nsa_pe[l].reshape(2, 2, CMP_STRIDE * NSA_DIM)
        w1b = nsa_w1[l].astype(BF16)
        kcmp = _compress(x16, pe2, w1b, nsa_w2[l], 0, False)
        vcmpT = _compress(x16, pe2, w1b, nsa_w2[l], 1, True)
        ynsT = _nsa(nsqT, nsgT, kcmp, vcmpT, nsks, nsk, nsvT)

        x = _mix_ffn(x, yml, ydaT, ynsT, w_out[l].astype(BF16), norm2[l][None, :], w_ff1[l].astype(BF16),
                     w_ff2[l].astype(BF16), final_norm[None, :], final=(l == depth - 1))
    return x
```

```python
import functools
import math

import numpy as np
import jax
import jax.numpy as jnp
from jax import lax
from jax.experimental import pallas as pl
from jax.experimental.pallas import tpu as pltpu

F32 = jnp.float32
BF16 = jnp.bfloat16

D_MODEL = 1024
ML_HEADS = 4
ML_DIM = 64
ML_WIDTH = ML_HEADS * ML_DIM
ML_TILE_CHUNK = 128
CONV_W = 4
DA_HEADS = 4
DA_QK_DIM = 32
DA_V_DIM = 64
DA_WIDTH = DA_HEADS * DA_V_DIM
NSA_HEADS = 8
NSA_GROUPS = 2
NSA_REP = NSA_HEADS // NSA_GROUPS
NSA_DIM = 64
NSA_WIDTH = NSA_HEADS * NSA_DIM
NSA_KV = NSA_GROUPS * NSA_DIM
CMP_BLOCK = 32
CMP_STRIDE = 16
CMP_HIDDEN = 4 * NSA_DIM
SEL_BLOCK = 64
SEL_TOPK = 16
WINDOW = 512
D_FF = 4 * D_MODEL
EPS = 1e-6
FORCE_SCORE = 1e4
IN_SIZES = (ML_WIDTH, ML_WIDTH, ML_WIDTH, ML_WIDTH, ML_HEADS, ML_HEADS,
            2 * DA_HEADS * DA_QK_DIM, 2 * DA_HEADS * DA_QK_DIM, DA_WIDTH,
            NSA_WIDTH, NSA_KV, NSA_KV, NSA_KV, NSA_KV, NSA_KV, NSA_KV, 3 * NSA_HEADS)

LANES = 128
GATE_PAD = 16
NEG = -1e30
M_INIT = -1e29
LOOP_PAIRS = 4
TAKEN = -3e38
LOG2E = 1.4426950408889634
V_DIM = 64
BF16_ROWS = 16
NSA_TK = 256
V_EXT = 80
ACC_ROWS = V_DIM + 8
VMEM_LIMIT = 56 * 1024 * 1024

_T_ML = (0, 1024)
_T_MLG = (1024, 1152)
_T_DAK = (1152, 1408)
_T_NSK = (1408, 1664)
_T_NSC = (1664, 1920)
_T_COLS = 1920
_F_DAQ = (0, 256)
_F_DAV = (256, 512)
_F_NSQ = (512, 1024)
_F_NSV = (1024, 1280)
_F_NSG = (1280, 1280 + NSA_GROUPS * GATE_PAD)
_F_ROWS = _F_NSG[1]

_NT = (((1,), (1,)), ((), ()))
_TN = (((0,), (0,)), ((), ()))


def _cparams(sem):
    return pltpu.CompilerParams(dimension_semantics=sem, vmem_limit_bytes=VMEM_LIMIT)


def _rms(x, g):
    return x * lax.rsqrt(jnp.mean(x * x, axis=-1, keepdims=True) + EPS) * g


def _inproj_kernel(x_ref, g_ref, wt_ref, wf_ref,
                   zml_ref, zg_ref, dak_ref, nsk_ref, nsks_ref, x16_ref,
                   daqT_ref, davT_ref, nsqT_ref, nsvT_ref, nsgT_ref, nsc_ref):
    hb = _rms(x_ref[0], g_ref[...]).astype(BF16)

    def tdot(span):
        return jnp.dot(hb, wt_ref[:, span[0]:span[1]], preferred_element_type=F32)

    def fdot(span):
        return lax.dot_general(wf_ref[span[0]:span[1], :], hb, _NT, preferred_element_type=F32)

    zml_ref[0] = tdot(_T_ML)
    zg_ref[0] = tdot(_T_MLG)
    dak = tdot(_T_DAK).astype(BF16)
    for h in range(DA_HEADS):
        dak_ref[0, h] = dak[:, h * 2 * DA_QK_DIM:(h + 1) * 2 * DA_QK_DIM]
    nsk = tdot(_T_NSK).astype(BF16)
    nsk_ref[0] = nsk
    tm_ = nsk.shape[0]
    row = lax.broadcasted_iota(jnp.int32, (tm_, NSA_DIM), 0)
    lane = lax.broadcasted_iota(jnp.int32, (tm_, NSA_DIM), 1)
    blk_in_tile = lax.shift_right_logical(row & (NSA_TK - 1), int(math.log2(SEL_BLOCK)))
    onehot = jnp.where(lane == blk_in_tile, 1.0, 0.0).astype(BF16)
    for gi in range(NSA_GROUPS):
        nsks_ref[0, gi] = jnp.concatenate([nsk[:, gi * NSA_DIM:(gi + 1) * NSA_DIM], onehot], axis=1)
    nsc = tdot(_T_NSC)
    for half in range(2):
        nsc_ref[half] = nsc[:, half * LANES:(half + 1) * LANES]
    for i in range(CMP_STRIDE):
        for half in range(2):
            rows = nsc_ref[half, pl.ds(i, tm_ // CMP_STRIDE, stride=CMP_STRIDE), :]
            for w in range(2):
                x16_ref[0, 2 * half + w, :, i * NSA_DIM:(i + 1) * NSA_DIM] = rows[:, w * NSA_DIM:(w + 1) * NSA_DIM]
    daqT_ref[0] = (fdot(_F_DAQ) * (DA_QK_DIM ** -0.5 * LOG2E)).astype(BF16)
    nsqT_ref[0] = (fdot(_F_NSQ) * (NSA_DIM ** -0.5 * LOG2E)).astype(BF16)
    nsgT_ref[0] = fdot(_F_NSG)
    tm = hb.shape[0]
    for v_ref, span in ((davT_ref, _F_DAV), (nsvT_ref, _F_NSV)):
        v = fdot(span).astype(BF16)
        for h in range((span[1] - span[0]) // V_DIM):
            v_ref[0, h * V_EXT:h * V_EXT + V_DIM, :] = v[h * V_DIM:(h + 1) * V_DIM, :]
            v_ref[0, h * V_EXT + V_DIM:(h + 1) * V_EXT, :] = jnp.ones((V_EXT - V_DIM, tm), BF16)


def _inproj(x, g, wt, wfT, tm=512):
    B, S, D = x.shape
    tok = lambda w, dt: jax.ShapeDtypeStruct((B, S, w), dt)
    feat = lambda r, dt: jax.ShapeDtypeStruct((B, r, S), dt)
    tspec = lambda w: pl.BlockSpec((1, tm, w), lambda b, i: (b, i, 0))
    fspec = lambda r: pl.BlockSpec((1, r, tm), lambda b, i: (b, 0, i))
    return pl.pallas_call(
        _inproj_kernel,
        grid=(B, S // tm),
        in_specs=[tspec(D),
                  pl.BlockSpec((1, D), lambda b, i: (0, 0)),
                  pl.BlockSpec((D, _T_COLS), lambda b, i: (0, 0)),
                  pl.BlockSpec((_F_ROWS, D), lambda b, i: (0, 0))],
        out_specs=[tspec(1024), tspec(128),
                   pl.BlockSpec((1, DA_HEADS, tm, 2 * DA_QK_DIM), lambda b, i: (b, 0, i, 0)), tspec(256),
                   pl.BlockSpec((1, NSA_GROUPS, tm, LANES), lambda b, i: (b, 0, i, 0)),
                   pl.BlockSpec((1, 2 * NSA_GROUPS, tm // CMP_STRIDE, CMP_STRIDE * NSA_DIM), lambda b, i: (b, 0, i, 0)),
                   fspec(256), fspec(4 * V_EXT), fspec(512), fspec(4 * V_EXT), fspec(NSA_GROUPS * GATE_PAD)],
        out_shape=[tok(1024, F32), tok(128, F32),
                   jax.ShapeDtypeStruct((B, DA_HEADS, S, 2 * DA_QK_DIM), BF16), tok(256, BF16),
                   jax.ShapeDtypeStruct((B, NSA_GROUPS, S, LANES), BF16),
                   jax.ShapeDtypeStruct((B, 2 * NSA_GROUPS, S // CMP_STRIDE, CMP_STRIDE * NSA_DIM), F32),
                   feat(256, BF16), feat(4 * V_EXT, BF16), feat(512, BF16), feat(4 * V_EXT, BF16),
                   feat(NSA_GROUPS * GATE_PAD, F32)],
        scratch_shapes=[pltpu.VMEM((2, tm, LANES), F32)],
        compiler_params=_cparams(("parallel", "parallel")),
        name="inproj",
    )(x, g, wt, wfT)


def _log_sigmoid(x):
    return jnp.minimum(x, 0.0) - jnp.log1p(jnp.exp(-jnp.abs(x)))


def _sigmoid(x):
    return 1.0 / (1.0 + jnp.exp(-x))


def _mlstm_kernel(zml_ref, zg_ref, cw_ref, gb_ref, nrm_ref, y_ref,
                  pad_ref, c_ref, n_ref, m_ref, *, T, L):
    d = ML_DIM
    W = max(L, d)
    t = pl.program_id(1)

    @pl.when(t == 0)
    def _():
        pad_ref[0:8, :] = jnp.zeros((8, 2 * ML_WIDTH), F32)
        c_ref[...] = jnp.zeros_like(c_ref)
        n_ref[...] = jnp.zeros_like(n_ref)
        m_ref[...] = jnp.zeros_like(m_ref)

    @pl.when(t > 0)
    def _():
        pad_ref[0:8, :] = pad_ref[T:T + 8, :]

    pad_ref[8:8 + T, :] = zml_ref[0, :, 0:2 * ML_WIDTH]
    conv = cw_ref[0:1, :] * pad_ref[5:5 + T, :]
    for j in range(1, CONV_W):
        conv = conv + cw_ref[j:j + 1, :] * pad_ref[5 + j:5 + j + T, :]
    qk = conv * _sigmoid(conv)
    gates = zg_ref[0] + gb_ref[...]
    logf = _log_sigmoid(gates)

    row = lax.broadcasted_iota(jnp.int32, (L, L), 0)
    col = lax.broadcasted_iota(jnp.int32, (L, L), 1)
    causal = col <= row
    tril = causal.astype(F32)
    triu = (row <= col).astype(F32)

    lane_of = lax.broadcasted_iota(jnp.int32, (LANES, W), 0)
    pick = [(lane_of == g).astype(F32) for g in range(2 * ML_HEADS)]

    H = range(ML_HEADS)

    def local_part(c):
        r0 = c * L
        g_c = gates[r0:r0 + L, :]
        lf_c = logf[r0:r0 + L, :]
        b_cols = jnp.dot(tril, lf_c, preferred_element_type=F32,
                         precision=lax.Precision.HIGHEST)
        b_rows = jnp.dot(lf_c.T, triu, preferred_element_type=F32,
                         precision=lax.Precision.HIGHEST)
        g_rows = g_c.T
        b_col = [jnp.dot(b_cols, pick[ML_HEADS + h], preferred_element_type=F32,
                         precision=lax.Precision.HIGHEST) for h in H]
        ig_col = [jnp.dot(g_c, pick[h], preferred_element_type=F32,
                          precision=lax.Precision.HIGHEST) for h in H]
        b_row = [b_rows[ML_HEADS + h:ML_HEADS + h + 1, :] for h in H]
        ig_row = [g_rows[h:h + 1, :] for h in H]
        qh = [qk[r0:r0 + L, h * d:(h + 1) * d] for h in H]
        kh = [qk[r0:r0 + L, ML_WIDTH + h * d:ML_WIDTH + (h + 1) * d] * (d ** -0.5) for h in H]
        vh = [zml_ref[0, r0:r0 + L, 2 * ML_WIDTH + h * d:2 * ML_WIDTH + (h + 1) * d] for h in H]
        qkt = [lax.dot_general(qh[h], kh[h], _NT, preferred_element_type=F32) for h in H]
        g_tot = [b_col[h][L - 1:L, :] for h in H]
        a_max = [jnp.max(g_tot[h][:, 0:L] - b_row[h] + ig_row[h], axis=1, keepdims=True) for h in H]
        w_col = [jnp.exp((g_tot[h] - b_col[h] + ig_col[h] - a_max[h])[:, 0:d]) for h in H]
        c_loc = [lax.dot_general(vh[h] * w_col[h], kh[h], _TN, preferred_element_type=F32) for h in H]
        n_loc = [jnp.sum(kh[h] * w_col[h], axis=0, keepdims=True) for h in H]
        dmat = [jnp.where(causal, b_col[h][:, 0:L] - b_row[h] + ig_row[h], NEG) for h in H]
        d_max = [jnp.max(dmat[h], axis=1, keepdims=True) for h in H]
        return dict(b_col=b_col, qh=qh, vh=vh, qkt=qkt, g_tot=g_tot, a_max=a_max, c_loc=c_loc, n_loc=n_loc,
                    dmat=dmat, d_max=d_max)

    def carried_part(c, lp):
        r0 = c * L
        b_col, qh, vh, qkt, g_tot, a_max = lp["b_col"], lp["qh"], lp["vh"], lp["qkt"], lp["g_tot"], lp["a_max"]
        oh = [zml_ref[0, r0:r0 + L, 3 * ML_WIDTH + h * d:3 * ML_WIDTH + (h + 1) * d] for h in H]
        c_prev = [c_ref[h] for h in H]
        n_prev = [n_ref[h, 0:1, :] for h in H]
        m_prev = [m_ref[h, 0:1, :] for h in H]
        q_c = [lax.dot_general(qh[h], c_prev[h], _NT, preferred_element_type=F32) for h in H]
        inter_log = [b_col[h] + m_prev[h] for h in H]
        m_t = [jnp.maximum(inter_log[h], lp["d_max"][h]) for h in H]
        wts = [jnp.exp(lp["dmat"][h] - m_t[h][:, 0:L]) * qkt[h] for h in H]
        s_inter = [jnp.exp((inter_log[h] - m_t[h])[:, 0:d]) for h in H]
        num = [jnp.dot(wts[h], vh[h], preferred_element_type=F32) + s_inter[h] * q_c[h] for h in H]
        den = [jnp.sum(wts[h], axis=1, keepdims=True)
               + s_inter[h] * jnp.sum(qh[h] * n_prev[h], axis=1, keepdims=True) for h in H]
        hh = [num[h] / jnp.maximum(jnp.abs(den[h]), jnp.exp(-m_t[h][:, 0:d])) for h in H]
        m_new = [jnp.maximum(g_tot[h] + m_prev[h], a_max[h]) for h in H]
        for h in H:
            s_prev = jnp.exp((g_tot[h] + m_prev[h] - m_new[h])[:, 0:d])
            s_loc = jnp.exp((a_max[h] - m_new[h])[:, 0:d])
            c_ref[h] = s_prev * c_prev[h] + s_loc * lp["c_loc"][h]
            n_ref[h] = jnp.broadcast_to(s_prev * n_prev[h] + s_loc * lp["n_loc"][h], (8, d))
            m_ref[h] = jnp.broadcast_to(m_new[h], (8, W))
        for h in H:
            yh = _sigmoid(oh[h]) * hh[h]
            y_ref[0, r0:r0 + L, h * d:(h + 1) * d] = _rms(yh, nrm_ref[0:1, h * d:(h + 1) * d])

    for c in range(T // L):
        carried_part(c, local_part(c))


def _mlstm(zml, zg, conv_w, gate_bias, norm_g, T=256, chunk=ML_TILE_CHUNK):
    B, S, _ = zml.shape
    return pl.pallas_call(
        functools.partial(_mlstm_kernel, T=T, L=chunk),
        grid=(B, S // T),
        in_specs=[pl.BlockSpec((1, T, 1024), lambda b, t: (b, t, 0)),
                  pl.BlockSpec((1, T, 128), lambda b, t: (b, t, 0)),
                  pl.BlockSpec((CONV_W, 2 * ML_WIDTH), lambda b, t: (0, 0)),
                  pl.BlockSpec((1, 128), lambda b, t: (0, 0)),
                  pl.BlockSpec((1, ML_WIDTH), lambda b, t: (0, 0))],
        out_specs=pl.BlockSpec((1, T, ML_WIDTH), lambda b, t: (b, t, 0)),
        out_shape=jax.ShapeDtypeStruct((B, S, ML_WIDTH), F32),
        scratch_shapes=[pltpu.VMEM((T + 8, 2 * ML_WIDTH), F32),
                        pltpu.VMEM((ML_HEADS, ML_DIM, ML_DIM), F32),
                        pltpu.VMEM((ML_HEADS, 8, ML_DIM), F32),
                        pltpu.VMEM((ML_HEADS, 8, max(chunk, ML_DIM)), F32)],
        compiler_params=_cparams(("parallel", "arbitrary")),
        name="mlstm",
    )(zml, zg, conv_w, gate_bias, norm_g)


def _softmax_cols(s, m_ref, p_ref, cols):
    m_old = m_ref[:, cols]
    m_new = jnp.maximum(m_old, jnp.max(s, axis=0, keepdims=True))
    m_ref[:, cols] = m_new
    p_ref[:, cols] = jnp.exp2(s - m_new).astype(BF16)
    return jnp.exp2(m_old - m_new)


def _pipelined_attention(scores, vT_tile, s_refs, p_refs, m_ref, acc_ref, softmax,
                         first, n_plain_pairs, n_tail, max_tile, col_groups, tail_is_short=None):
    sa, sb = s_refs
    pa, pb = p_refs

    def load(j):
        return jnp.clip(j, 0, max_tile)

    def half(j, s_cur, s_nxt, p_cur, p_prev, tail):
        s_nxt[...] = scores(load(j + 1))
        pv = jnp.dot(vT_tile(load(j - 1)), p_prev[...], preferred_element_type=F32)
        alphas = softmax(s_cur, p_cur, j, tail)
        for cols, alpha in zip(col_groups, alphas):
            acc_ref[:, cols] = alpha * (acc_ref[:, cols] + pv[0:ACC_ROWS, cols])

    def pair(j, tail):
        half(j, sa, sb, pa, pb, tail)
        half(j + 1, sb, sa, pb, pa, tail)

    m_ref[...] = jnp.full_like(m_ref, M_INIT)
    acc_ref[...] = jnp.zeros_like(acc_ref)
    pb[...] = jnp.zeros_like(pb)
    sa[...] = scores(load(first))

    if not (isinstance(n_plain_pairs, int) and n_plain_pairs == 0):
        n_trips = n_plain_pairs // LOOP_PAIRS

        def body(i, carry):
            for u in range(LOOP_PAIRS):
                pair(first + 2 * (LOOP_PAIRS * i + u), False)
            return carry

        lax.fori_loop(0, n_trips, body, 0)
        rest = n_plain_pairs - n_trips * LOOP_PAIRS
        for u in range(LOOP_PAIRS - 1):
            @pl.when(u < rest)
            def _():
                pair(first + 2 * (LOOP_PAIRS * n_trips + u), False)
    j = first + 2 * n_plain_pairs
    bufs = ((sa, sb, pa, pb), (sb, sa, pb, pa))

    def tail(n):
        for t in range(n):
            half(j + t, *bufs[t % 2], True)
        p_last = bufs[(n - 1) % 2][2]
        acc_ref[...] += jnp.dot(vT_tile(load(j + n - 1)), p_last[...], preferred_element_type=F32)[0:ACC_ROWS]

    if tail_is_short is None:
        tail(n_tail)
    else:
        @pl.when(tail_is_short)
        def _():
            tail(n_tail - 1)

        @pl.when(jnp.logical_not(tail_is_short))
        def _():
            tail(n_tail)


def _da_kernel(lam_ref, qT_ref, k_ref, vT_ref, gain_ref, o_ref,
               qz_ref, sa_ref, sb_ref, pa_ref, pb_ref, m_ref, acc_ref, *, tq, tk, qb, lam_init, S):
    d = DA_QK_DIM
    col_groups = [slice(mp * tq, (mp + 1) * tq) for mp in range(2)]

    def scores(j):
        return jnp.dot(k_ref[0, 0, pl.ds(pl.multiple_of(j * tk, tk), tk), :], qz_ref[...],
                       preferred_element_type=F32)

    def vT_tile(j):
        return vT_ref[0, :, pl.ds(pl.multiple_of(j * tk, tk), tk)]

    lp = lam_ref[...]
    lam = (jnp.exp(jnp.sum(lp[0:1] * lp[1:2], axis=1, keepdims=True))
           - jnp.exp(jnp.sum(lp[2:3] * lp[3:4], axis=1, keepdims=True)) + lam_init)

    def query_block(qq, carry):
        qi = pl.program_id(2) * qb + qq
        jd = (qi * tq) // tk
        cols_q = pl.ds(pl.multiple_of(qq * tq, tq), tq)

        qz_ref[...] = jnp.zeros_like(qz_ref)
        qz_ref[0:d, 0:tq] = qT_ref[0, 0:d, cols_q]
        qz_ref[d:2 * d, tq:2 * tq] = qT_ref[0, d:2 * d, cols_q]

        def softmax(s_ref, p_ref, j, tail):
            alphas = []
            for cols in col_groups:
                s = s_ref[:, cols]
                if tail:
                    kpos = j * tk + lax.broadcasted_iota(jnp.int32, (tk, tq), 0)
                    qpos = qi * tq + lax.broadcasted_iota(jnp.int32, (tk, tq), 1)
                    s = jnp.where(kpos <= qpos, s, NEG)
                alphas.append(_softmax_cols(s, m_ref, p_ref, cols))
            return alphas

        _pipelined_attention(scores, vT_tile, (sa_ref, sb_ref), (pa_ref, pb_ref), m_ref, acc_ref, softmax,
                             first=0, n_plain_pairs=jd // 2, n_tail=2, max_tile=S // tk - 1,
                             col_groups=col_groups, tail_is_short=(2 * (jd // 2) + 1) * tk > qi * tq + tq - 1)

        o1 = acc_ref[0:V_DIM, 0:tq] * (1.0 / acc_ref[V_DIM:V_DIM + 1, 0:tq])
        o2 = acc_ref[0:V_DIM, tq:2 * tq] * (1.0 / acc_ref[V_DIM:V_DIM + 1, tq:2 * tq])
        o = o1 - lam * o2
        y = o * lax.rsqrt(jnp.mean(o * o, axis=0, keepdims=True) + EPS) * gain_ref[...]
        o_ref[0, :, cols_q] = (y * (1.0 - lam_init)).astype(o_ref.dtype)
        return carry

    lax.fori_loop(0, qb, query_block, 0)


def _diff_attn(da_lambda, daqT, dak, davT, gain_col, lam_init, tq=256, tk=256, qb=4):
    B, _, S = daqT.shape
    return pl.pallas_call(
        functools.partial(_da_kernel, tq=tq, tk=tk, qb=qb, lam_init=lam_init, S=S),
        grid=(B, DA_HEADS, S // (qb * tq)),
        in_specs=[pl.BlockSpec((4, DA_QK_DIM), lambda b, h, i: (0, 0)),
                  pl.BlockSpec((1, DA_V_DIM, qb * tq), lambda b, h, i: (b, h, i)),
                  pl.BlockSpec((1, 1, S, 2 * DA_QK_DIM), lambda b, h, i: (b, h, 0, 0)),
                  pl.BlockSpec((1, V_EXT, S), lambda b, h, i: (b, h, 0)),
                  pl.BlockSpec((DA_V_DIM, 1), lambda b, h, i: (h, 0))],
        out_specs=pl.BlockSpec((1, DA_V_DIM, qb * tq), lambda b, h, i: (b, h, i)),
        out_shape=jax.ShapeDtypeStruct((B, DA_WIDTH, S), BF16),
        scratch_shapes=[pltpu.VMEM((2 * DA_QK_DIM, 2 * tq), BF16),
                        pltpu.VMEM((tk, 2 * tq), F32),
                        pltpu.VMEM((tk, 2 * tq), F32),
                        pltpu.VMEM((tk, 2 * tq), BF16),
                        pltpu.VMEM((tk, 2 * tq), BF16),
                        pltpu.VMEM((1, 2 * tq), F32),
                        pltpu.VMEM((ACC_ROWS, 2 * tq), F32)],
        compiler_params=_cparams(("parallel", "parallel", "arbitrary")),
        name="diff_attn",
    )(da_lambda, daqT, dak, davT, gain_col)


def _gelu_tanh(x):
    return x * (0.5 * (1.0 + jnp.tanh(math.sqrt(2.0 / math.pi) * (x + 0.044715 * (x * x * x)))))


def _compress_kernel(x_ref, pe_ref, w1_ref, w2_ref, o_ref, b_ref, *, feature_major):
    n = x_ref.shape[2]
    half = CMP_STRIDE * NSA_DIM
    x = x_ref[0, 0]
    a = jnp.dot((x + pe_ref[0, 0:1, :]).astype(BF16), w1_ref[0, 0:half, :], preferred_element_type=F32)
    b_ref[0:n, :] = jnp.dot((x + pe_ref[0, 1:2, :]).astype(BF16), w1_ref[0, half:2 * half, :],
                            preferred_element_type=F32)
    b_ref[n:n + 8, :] = jnp.zeros((8, CMP_HIDDEN), F32)
    hid = _gelu_tanh(a + b_ref[1:n + 1, :]).astype(BF16)
    if feature_major:
        o_ref[0, 0] = lax.dot_general(w2_ref[0], hid, _NT, preferred_element_type=F32).astype(o_ref.dtype)
    else:
        o_ref[0, 0] = jnp.dot(hid, w2_ref[0], preferred_element_type=F32).astype(o_ref.dtype)


def _compress(x16, pe2, w1, w2, which, feature_major):
    B, _, n, half = x16.shape
    G = NSA_GROUPS
    if feature_major:
        out_shape, out_block = (B, G, NSA_DIM, n), (1, 1, NSA_DIM, n)
        w2_arr, w2_block = jnp.swapaxes(w2, 1, 2), (1, NSA_DIM, CMP_HIDDEN)
    else:
        out_shape, out_block = (B, G, n, NSA_DIM), (1, 1, n, NSA_DIM)
        w2_arr, w2_block = w2, (1, CMP_HIDDEN, NSA_DIM)
    return pl.pallas_call(
        functools.partial(_compress_kernel, feature_major=feature_major),
        grid=(B, G),
        in_specs=[pl.BlockSpec((1, 1, n, half), lambda b, g: (b, which * G + g, 0, 0)),
                  pl.BlockSpec((1, 2, half), lambda b, g: (which, 0, 0)),
                  pl.BlockSpec((1, 2 * half, CMP_HIDDEN), lambda b, g: (which, 0, 0)),
                  pl.BlockSpec(w2_block, lambda b, g: (which, 0, 0))],
        out_specs=pl.BlockSpec(out_block, lambda b, g: (b, g, 0, 0)),
        out_shape=jax.ShapeDtypeStruct(out_shape, BF16),
        scratch_shapes=[pltpu.VMEM((n + 8, CMP_HIDDEN), F32)],
        compiler_params=_cparams(("parallel", "parallel")),
        name="compress_v" if feature_major else "compress_k",
    )(x16, pe2, w1, w2_arr.astype(BF16))


def _nsa_kernel(qT_ref, gT_ref, kc_ref, vcT_ref, ks_ref, vsT_ref, kw_ref, vwT_ref, o_ref,
                qg_ref, qz_ref, qs_ref, imp_ref, sel_ref, pc_ref, sa_ref, sb_ref, pa_ref, pb_ref, m_ref, acc_ref,
                wsa_ref, wsb_ref, wpa_ref, wpb_ref, wm_ref, wacc_ref, out_ref,
                *, qi, tq, tk, S):
    g = pl.program_id(1)
    d = NSA_DIM
    R = NSA_REP
    ncb = S // CMP_STRIDE
    nsb = S // SEL_BLOCK
    q0 = qi * tq
    jd = q0 // tk
    ratio = SEL_BLOCK // CMP_STRIDE
    per_tile = tk // SEL_BLOCK

    for r in range(R):
        qg_ref[:, r * tq:(r + 1) * tq] = qT_ref[0, r * d:(r + 1) * d, :]
    qz_ref[...] = jnp.zeros_like(qz_ref)
    qs_ref[...] = jnp.zeros_like(qs_ref)
    qs_ref[0:d, :] = qg_ref[...]
    for gg in range(NSA_GROUPS):
        @pl.when(g == gg)
        def _():
            qz_ref[gg * d:(gg + 1) * d, :] = qg_ref[...]

    def gate(r, br):
        return _sigmoid(gT_ref[0, r * 3 + br:r * 3 + br + 1, :])

    qpos = q0 + lax.broadcasted_iota(jnp.int32, (1, tq), 1)

    def compress_and_select(n_c):
        n_s = n_c // ratio
        s_all = jnp.dot(kc_ref[0, 0, 0:n_c, :], qg_ref[...], preferred_element_type=F32)
        cend = lax.broadcasted_iota(jnp.int32, (n_c, tq), 0) * CMP_STRIDE + (CMP_BLOCK - 1)
        cbias = jnp.where(cend <= qpos, 0.0, NEG)
        hs = range(R)
        sc = [s_all[:, r * tq:(r + 1) * tq] + cbias for r in hs]
        mx = [jnp.maximum(jnp.max(sc[r], axis=0, keepdims=True), M_INIT) for r in hs]
        pu = [jnp.exp2(sc[r] - mx[r]) for r in hs]
        den = [jnp.sum(pu[r], axis=0, keepdims=True) for r in hs]
        pn = [pu[r] * (1.0 / jnp.where(den[r] > 0, den[r], 1.0)) for r in hs]
        imp = pn[0]
        for r in range(1, R):
            imp = imp + pn[r]
        for r in hs:
            pc_ref[0:n_c, r * tq:(r + 1) * tq] = pn[r].astype(BF16)
        o_cmp = jnp.dot(vcT_ref[0, 0, :, 0:n_c], pc_ref[0:n_c, :], preferred_element_type=F32)
        for r in range(R):
            cols = slice(r * tq, (r + 1) * tq)
            out_ref[:, cols] = gate(r, 0) * o_cmp[:, cols]

        slabs = []
        for c in range(tq // LANES):
            imp_ref[c, 0:8, :] = jnp.zeros((8, LANES), F32)
            imp_ref[c, 8:8 + n_c, :] = imp[:, c * LANES:(c + 1) * LANES]
            imp_ref[c, 8 + n_c:16 + n_c, :] = jnp.zeros((8, LANES), F32)
            slab = jnp.zeros((n_s, LANES), F32)
            for o in range(-1, ratio):
                slab = slab + imp_ref[c, pl.ds(8 + o, n_s, stride=ratio), :]
            slabs.append(slab)
        p_slc = slabs[0] if len(slabs) == 1 else jnp.concatenate(slabs, axis=1)
        blk = lax.broadcasted_iota(jnp.int32, (n_s, tq), 0)
        cur = lax.shift_right_logical(qpos, int(math.log2(SEL_BLOCK)))
        forced = (blk == 0) | (blk == cur) | (blk == cur - 1)
        causal_blk = blk * SEL_BLOCK <= qpos
        score = jnp.where(forced, FORCE_SCORE, jnp.where(causal_blk, p_slc, -1.0))
        blk_f = blk.astype(F32)
        for _ in range(SEL_TOPK):
            mx_s = jnp.max(score, axis=0, keepdims=True)
            first = jnp.min(jnp.where(score == mx_s, blk_f, float(nsb)), axis=0, keepdims=True)
            score = jnp.where(blk_f == first, TAKEN, score)
        sel_ref[0:n_s, :] = jnp.where(score == TAKEN, 0.0, NEG)
        if n_s < nsb:
            sel_ref[n_s:nsb, :] = jnp.full((nsb - n_s, tq), NEG, F32)

    n_classes = 8
    visible_class = (q0 + tq - 1) // (S // n_classes)
    for k in range(n_classes):
        @pl.when(visible_class == k)
        def _():
            compress_and_select((k + 1) * ncb // n_classes)

    head_cols = [slice(r * tq, (r + 1) * tq) for r in range(R)]
    col_groups = [slice(0, R * tq)]
    max_tile = S // tk - 1

    def k_tile(k_ref_, j):
        return k_ref_[0, pl.ds(pl.multiple_of(j * tk, tk), tk), :]

    def vT_tile_of(vT_ref_):
        def vT_tile(j):
            return vT_ref_[0, :, pl.ds(pl.multiple_of(j * tk, tk), tk)]
        return vT_tile

    def softmax_with(bias_of, m_ref_):
        def softmax(s_ref, p_ref, j, tail):
            bias = bias_of(j, tail)
            s = s_ref[...]
            if bias is not None:
                s = s + jnp.concatenate([bias] * R, axis=1)
            return [_softmax_cols(s, m_ref_, p_ref, col_groups[0])]
        return softmax

    def flush(br, acc_ref_):
        for r, cols in enumerate(head_cols):
            den = acc_ref_[V_DIM:V_DIM + 1, cols]
            out_ref[:, cols] += (gate(r, br) / jnp.where(den > 0, den, 1.0)) * acc_ref_[0:V_DIM, cols]

    def kpos_of(j):
        return j * tk + lax.broadcasted_iota(jnp.int32, (tk, tq), 0)

    def win_bias(j, tail):
        kpos = kpos_of(j)
        rel = qpos - kpos
        return jnp.where((rel >= 0) & (rel < WINDOW) & (kpos >= 0), 0.0, NEG)

    def win_scores(j):
        return jnp.dot(k_tile(kw_ref, j), qz_ref[...], preferred_element_type=F32)

    n_win = WINDOW // tk + 1
    _pipelined_attention(win_scores, vT_tile_of(vwT_ref), (wsa_ref, wsb_ref), (wpa_ref, wpb_ref), wm_ref, wacc_ref,
                         softmax_with(win_bias, wm_ref), first=jd - (n_win - 1), n_plain_pairs=0, n_tail=n_win,
                         max_tile=max_tile, col_groups=col_groups)
    flush(2, wacc_ref)

    pad_rows = jnp.zeros((BF16_ROWS - per_tile, tq), F32)

    def sel_scores(j):
        rows = [sel_ref[pl.ds(j * per_tile + i, 1), :] for i in range(per_tile)]
        bias = jnp.concatenate(rows + [pad_rows], axis=0).astype(BF16)
        qs_ref[d:d + BF16_ROWS, :] = jnp.concatenate([bias] * R, axis=1)
        return jnp.dot(ks_ref[0, 0, pl.ds(pl.multiple_of(j * tk, tk), tk), :], qs_ref[...],
                       preferred_element_type=F32)

    def sel_bias(j, tail):
        return jnp.where(kpos_of(j) <= qpos, 0.0, NEG) if tail else None

    _pipelined_attention(sel_scores, vT_tile_of(vsT_ref), (sa_ref, sb_ref), (pa_ref, pb_ref), m_ref, acc_ref,
                         softmax_with(sel_bias, m_ref), first=0, n_plain_pairs=jd // 2, n_tail=2,
                         max_tile=max_tile, col_groups=col_groups,
                         tail_is_short=(2 * (jd // 2) + 1) * tk > q0 + tq - 1)
    flush(1, acc_ref)

    for r in range(R):
        o_ref[0, r * d:(r + 1) * d, :] = out_ref[:, r * tq:(r + 1) * tq].astype(o_ref.dtype)


def _nsa_step_kernel(qT_ref, gT_ref, kc_ref, vcT_ref, ks_ref, vsT_ref, kw_ref, vwT_ref, o_ref, *scratch,
                     tq, tk, S, qb):
    def query_block(qq, carry):
        cols = pl.ds(pl.multiple_of(qq * tq, tq), tq)
        _nsa_kernel(qT_ref.at[:, :, cols], gT_ref.at[:, :, cols], kc_ref, vcT_ref, ks_ref, vsT_ref, kw_ref, vwT_ref,
                    o_ref.at[:, :, cols], *scratch, qi=pl.program_id(2) * qb + qq, tq=tq, tk=tk, S=S)
        return carry

    lax.fori_loop(0, qb, query_block, 0)


def _nsa(nsqT, nsgT, kcmp, vcmpT, nsks, nsk, nsvT, tq=128, tk=NSA_TK, qb=4):
    B, _, S = nsqT.shape
    G, R, d = NSA_GROUPS, NSA_REP, NSA_DIM
    ncb = S // CMP_STRIDE
    nsb = S // SEL_BLOCK
    return pl.pallas_call(
        functools.partial(_nsa_step_kernel, tq=tq, tk=tk, S=S, qb=qb),
        grid=(B, G, S // (qb * tq)),
        in_specs=[pl.BlockSpec((1, R * d, qb * tq), lambda b, g, i: (b, g, i)),
                  pl.BlockSpec((1, GATE_PAD, qb * tq), lambda b, g, i: (b, g, i)),
                  pl.BlockSpec((1, 1, ncb, d), lambda b, g, i: (b, g, 0, 0)),
                  pl.BlockSpec((1, 1, d, ncb), lambda b, g, i: (b, g, 0, 0)),
                  pl.BlockSpec((1, 1, S, LANES), lambda b, g, i: (b, g, 0, 0)),
                  pl.BlockSpec((1, V_EXT, S), lambda b, g, i: (b, g, 0)),
                  pl.BlockSpec((1, S, LANES), lambda b, g, i: (b, 0, 1)),
                  pl.BlockSpec((1, V_EXT, S), lambda b, g, i: (b, G + g, 0))],
        out_specs=pl.BlockSpec((1, R * d, qb * tq), lambda b, g, i: (b, g, i)),
        out_shape=jax.ShapeDtypeStruct((B, NSA_WIDTH, S), BF16),
        scratch_shapes=[pltpu.VMEM((d, R * tq), BF16),
                        pltpu.VMEM((LANES, R * tq), BF16),
                        pltpu.VMEM((LANES, R * tq), BF16),
                        pltpu.VMEM((tq // LANES, ncb + 16, LANES), F32),
                        pltpu.VMEM((nsb, tq), F32),
                        pltpu.VMEM((ncb, R * tq), BF16),
                        pltpu.VMEM((tk, R * tq), F32),
                        pltpu.VMEM((tk, R * tq), F32),
                        pltpu.VMEM((tk, R * tq), BF16),
                        pltpu.VMEM((tk, R * tq), BF16),
                        pltpu.VMEM((1, R * tq), F32),
                        pltpu.VMEM((ACC_ROWS, R * tq), F32),
                        pltpu.VMEM((tk, R * tq), F32),
                        pltpu.VMEM((tk, R * tq), F32),
                        pltpu.VMEM((tk, R * tq), BF16),
                        pltpu.VMEM((tk, R * tq), BF16),
                        pltpu.VMEM((1, R * tq), F32),
                        pltpu.VMEM((ACC_ROWS, R * tq), F32),
                        pltpu.VMEM((d, R * tq), F32)],
        compiler_params=_cparams(("parallel", "parallel", "arbitrary")),
        name="nsa",
    )(nsqT, nsgT, kcmp, vcmpT, nsks, nsvT, nsk, nsvT)


def _mix_ffn_kernel(x_ref, yml_ref, ydaT_ref, ynsT_ref, wo_ref, g_ref, w1_ref, w2_ref, gf_ref, o_ref,
                    hb_ref, acc_ref, *, final):
    j = pl.program_id(2)

    @pl.when(j == 0)
    def _():
        x1 = x_ref[0] + jnp.dot(yml_ref[0].astype(BF16), wo_ref[0:ML_WIDTH, :], preferred_element_type=F32)
        x1 = x1 + lax.dot_general(ydaT_ref[0], wo_ref[ML_WIDTH:ML_WIDTH + DA_WIDTH, :], _TN,
                                  preferred_element_type=F32)
        x1 = x1 + lax.dot_general(ynsT_ref[0], wo_ref[ML_WIDTH + DA_WIDTH:, :], _TN,
                                  preferred_element_type=F32)
        hb_ref[...] = _rms(x1, g_ref[...]).astype(BF16)
        acc_ref[...] = x1

    u = jnp.dot(hb_ref[...], w1_ref[...], preferred_element_type=F32)
    a = jnp.square(jnp.maximum(u, 0.0)).astype(BF16)
    acc_ref[...] += jnp.dot(a, w2_ref[...], preferred_element_type=F32)

    @pl.when(j == pl.num_programs(2) - 1)
    def _():
        y = acc_ref[...]
        if final:
            y = _rms(y, gf_ref[...])
        o_ref[0] = y


def _mix_ffn(x, yml, ydaT, ynsT, wo, g, w1, w2, gf, final, tm=1024, tf=1024):
    B, S, D = x.shape
    return pl.pallas_call(
        functools.partial(_mix_ffn_kernel, final=final),
        grid=(B, S // tm, D_FF // tf),
        in_specs=[pl.BlockSpec((1, tm, D), lambda b, i, j: (b, i, 0)),
                  pl.BlockSpec((1, tm, ML_WIDTH), lambda b, i, j: (b, i, 0)),
                  pl.BlockSpec((1, DA_WIDTH, tm), lambda b, i, j: (b, 0, i)),
                  pl.BlockSpec((1, NSA_WIDTH, tm), lambda b, i, j: (b, 0, i)),
                  pl.BlockSpec((D, D), lambda b, i, j: (0, 0)),
                  pl.BlockSpec((1, D), lambda b, i, j: (0, 0)),
                  pl.BlockSpec((D, tf), lambda b, i, j: (0, j)),
                  pl.BlockSpec((tf, D), lambda b, i, j: (j, 0)),
                  pl.BlockSpec((1, D), lambda b, i, j: (0, 0))],
        out_specs=pl.BlockSpec((1, tm, D), lambda b, i, j: (b, i, 0)),
        out_shape=jax.ShapeDtypeStruct((B, S, D), F32),
        scratch_shapes=[pltpu.VMEM((tm, D), BF16), pltpu.VMEM((tm, D), F32)],
        compiler_params=_cparams(("parallel", "parallel", "arbitrary")),
        name="mix_ffn",
    )(x, yml, ydaT, ynsT, wo, g, w1, w2, gf)


def _split_w_in(w_in_l):
    edges = np.concatenate([[0], np.cumsum(IN_SIZES)])
    return [w_in_l[:, int(edges[i]):int(edges[i + 1])] for i in range(len(IN_SIZES))]


def _pad_cols(w, n):
    return jnp.pad(w, ((0, 0), (0, n - w.shape[1])))


def _inproj_weights(w_in_l):
    (ml_q, ml_k, ml_v, ml_o, ml_i, ml_f, da_q, da_k, da_v,
     ns_q, ns_kc, ns_vc, ns_ks, ns_vs, ns_kw, ns_vw, ns_g) = _split_w_in(w_in_l)
    wt = jnp.concatenate([ml_q, ml_k, ml_v, ml_o, _pad_cols(jnp.concatenate([ml_i, ml_f], 1), LANES),
                          da_k, ns_ks, ns_kw, ns_kc, ns_vc], axis=1)
    per_group = NSA_REP * 3
    ns_g_pad = jnp.concatenate(
        [_pad_cols(ns_g[:, gi * per_group:(gi + 1) * per_group], GATE_PAD) for gi in range(NSA_GROUPS)], axis=1)
    wf = jnp.concatenate([da_q, da_v, ns_q, ns_vs, ns_vw, ns_g_pad], axis=1)
    return wt.astype(BF16), wf.T.astype(BF16)


def kernel(x, norm1, w_in, ml_conv, ml_gate_bias, ml_norm, da_lambda, da_norm, nsa_pe,
           nsa_w1, nsa_w2, w_out, norm2, w_ff1, w_ff2, final_norm):
    B, S, D = x.shape
    depth = norm1.shape[0]
    for l in range(depth):
        wt, wfT = _inproj_weights(w_in[l])
        (zml, zg, dak, nsk, nsks, x16, daqT, davT, nsqT, nsvT, nsgT) = _inproj(x, norm1[l][None, :], wt, wfT)

        yml = _mlstm(zml, zg, ml_conv[l], _pad_cols(ml_gate_bias[l][None, :], LANES), ml_norm[l][None, :])

        lam_init = 0.8 - 0.6 * math.exp(-0.3 * l)
        ydaT = _diff_attn(da_lambda[l], daqT, dak, davT, da_norm[l][:, None], lam_init)

        pe2 = nsa_pe[l].reshape(2, 2, CMP_STRIDE * NSA_DIM)
        w1b = nsa_w1[l].astype(BF16)
        kcmp = _compress(x16, pe2, w1b, nsa_w2[l], 0, False)
        vcmpT = _compress(x16, pe2, w1b, nsa_w2[l], 1, True)
        ynsT = _nsa(nsqT, nsgT, kcmp, vcmpT, nsks, nsk, nsvT)

        x = _mix_ffn(x, yml, ydaT, ynsT, w_out[l].astype(BF16), norm2[l][None, :], w_ff1[l].astype(BF16),
                     w_ff2[l].astype(BF16), final_norm[None, :], final=(l == depth - 1))
    return x
```
